```python
import jax, jax.numpy as jnp
from jax import lax
import numpy as np

D_MODEL = 1024
BATCH = 1
SEQ = 16384
DEPTH = 1
DEC_BATCH = 32
DEC_SEQ = 1
PAST_LEN = 16384
PAGE_SIZE = 128

HEAD_DIM_A = 64
HEADS_PER_GROUP_A = 4
DILATED_GROUPS = ((128, 1), (512, 4), (2048, 16))
N_GROUPS_A = len(DILATED_GROUPS)
N_HEADS_A = HEADS_PER_GROUP_A * N_GROUPS_A
WIDTH_A = N_HEADS_A * HEAD_DIM_A
COMB_WIDTH_A = HEADS_PER_GROUP_A * HEAD_DIM_A
CHUNK = 128
N_GROUPS_B = 4
WIDTH_B = 768
GROUP_DIM_B = WIDTH_B // N_GROUPS_B
N_MEM = 256
N_HEADS_M = 4
HEAD_DIM_M = 128
WIDTH_M = N_HEADS_M * HEAD_DIM_M
N_BRANCH = 3
IN_WIDTH = 3 * WIDTH_A + 2 * WIDTH_B + WIDTH_M + N_BRANCH * D_MODEL
D_FF = 2816
CONV_W = 3
LN_EPS = 1e-5
ALPHA = (2.0 * DEPTH) ** 0.25
BETA = (8.0 * DEPTH) ** -0.25
NEG = -1e30

kernel_name = "hybrid_dilated_gmlp_memory_decoder_step"


def layer_norm(x, g, b):
    xf = x.astype(jnp.float32)
    mu = jnp.mean(xf, axis=-1, keepdims=True)
    var = jnp.mean(jnp.square(xf - mu), axis=-1, keepdims=True)
    return ((xf - mu) * lax.rsqrt(var + LN_EPS) * g + b).astype(x.dtype)


def split_in(h):
    bounds = np.cumsum([WIDTH_A, WIDTH_A, WIDTH_A, WIDTH_B, WIDTH_B, WIDTH_M]).tolist()
    return jnp.split(h, bounds, axis=-1)


def heads(t, n_heads, head_dim):
    return t.reshape(t.shape[:-1] + (n_heads, head_dim))


def group_slice(t, g):
    return t[:, :, g * HEADS_PER_GROUP_A:(g + 1) * HEADS_PER_GROUP_A]


def dilated_attn_prompt(q, k, v, window, dilation):
    B, S, H, E = q.shape
    nk = window // dilation
    unit = dilation * nk
    Sp = -(-S // unit) * unit
    nb = Sp // unit
    pad = ((0, 0), (0, Sp - S), (0, 0), (0, 0))

    def to_sub(t):
        return jnp.pad(t, pad).reshape(B, nb, nk, dilation, H, E)

    def with_prev(t):
        prev = jnp.pad(t, ((0, 0), (1, 0), (0, 0), (0, 0), (0, 0), (0, 0)))[:, :-1]
        return jnp.concatenate([prev, t], axis=2)

    qs = to_sub(q)
    kk = with_prev(to_sub(k))
    vv = with_prev(to_sub(v))
    s = jnp.einsum('bnidhe,bnjdhe->bndhij', qs, kk,
                   preferred_element_type=jnp.float32) * (E ** -0.5)
    i = jnp.arange(nk)[:, None]
    j = jnp.arange(2 * nk)[None, :]
    rel = i + nk - j
    blk = jnp.arange(nb)[:, None, None]
    valid = (rel >= 0) & (rel <= nk) & (blk * nk + j - nk >= 0)
    s = jnp.where(valid[None, :, None, None], s, NEG)
    lse = jax.nn.logsumexp(s, axis=-1)
    p = jnp.exp(s - lse[..., None]).astype(v.dtype)
    o = jnp.einsum('bndhij,bnjdhe->bnidhe', p, vv)
    o = o.reshape(B, Sp, H, E)[:, :S]
    lse = lse.transpose(0, 1, 4, 2, 3).reshape(B, Sp, H)[:, :S]
    return o, lse


def dilated_attn_sample(q, k_new, v_new, k_buf, v_buf, window, dilation):
    T, E = q.shape[1], q.shape[-1]
    Wb = k_buf.shape[1]
    nk = window // dilation
    keys = jnp.concatenate([k_buf, k_new], axis=1)
    vals = jnp.concatenate([v_buf, v_new], axis=1)
    i = jnp.arange(T)[:, None]
    kk = jnp.arange(nk + 1)[None, :]
    pos = PAST_LEN + i - kk * dilation
    idx = pos - (PAST_LEN - Wb)
    valid = (pos >= 0) & (idx >= 0)
    idx = jnp.clip(idx, 0, Wb + T - 1)
    kg = jnp.take(keys, idx, axis=1)
    vg = jnp.take(vals, idx, axis=1)
    s = jnp.einsum('bthe,btkhe->bthk', q, kg,
                   preferred_element_type=jnp.float32) * (E ** -0.5)
    s = jnp.where(valid[None, :, None, :], s, NEG)
    lse = jax.nn.logsumexp(s, axis=-1)
    p = jnp.exp(s - lse[..., None]).astype(v_new.dtype)
    o = jnp.einsum('bthk,btkhe->bthe', p, vg)
    return o, lse


def combine_dilations(outs, lses):
    wts = jax.nn.softmax(jnp.stack(lses, axis=0), axis=0)
    o = jnp.einsum('gbsh,gbshe->bshe', wts, jnp.stack(outs, axis=0).astype(jnp.float32))
    B, S = o.shape[:2]
    return o.reshape(B, S, COMB_WIDTH_A).astype(outs[0].dtype)


def gmlp_branch(u, v, ln_g, ln_b, w_s, b_s, rows):
    u = jax.nn.gelu(u)
    v = layer_norm(jax.nn.gelu(v), ln_g, ln_b)
    B, S, W = u.shape
    n = S // rows
    mask = jnp.tril(jnp.ones((rows, rows), dtype=bool))
    ws = jnp.where(mask, w_s[:, :rows, :rows], 0.0).astype(v.dtype)
    vs = v.reshape(B, n, rows, N_GROUPS_B, GROUP_DIM_B)
    mixed = jnp.einsum('gij,bnjgc->bnigc', ws, vs) + b_s[:, :rows].T[:, :, None]
    return u * mixed.reshape(B, S, W), v


def memory_attn(q, mk, mv):
    B, S, H, E = q.shape
    s = jnp.einsum('bshe,bmhe->bshm', q, mk, preferred_element_type=jnp.float32) * (E ** -0.5)
    p = jax.nn.softmax(s, axis=-1).astype(mv.dtype)
    return jnp.einsum('bshm,bmhe->bshe', p, mv).reshape(B, S, WIDTH_M)


def merge_and_ffn(x, o_a, o_b, o_m, gates, conv_state, b_gate, w_ba, w_bb, w_bm, w_out,
                  ln1_g, ln1_b, w_up, conv_w, conv_b, w_down, ln2_g, ln2_b):
    g = jax.nn.sigmoid(gates.reshape(gates.shape[:-1] + (N_BRANCH, D_MODEL)) + b_gate)
    mixed = g[..., 0, :] * (o_a @ w_ba) + g[..., 1, :] * (o_b @ w_bb) + g[..., 2, :] * (o_m @ w_bm)
    x1 = layer_norm(ALPHA * x + mixed @ w_out, ln1_g, ln1_b)
    a, val = jnp.split(x1 @ w_up, 2, axis=-1)
    a_ext = jnp.concatenate([conv_state.astype(a.dtype), a], axis=1)
    S = a.shape[1]
    conv = conv_b + sum(conv_w[k] * a_ext[:, k:k + S] for k in range(CONV_W))
    h = jax.nn.gelu(conv) * val
    y = layer_norm(ALPHA * x1 + h @ w_down, ln2_g, ln2_b)
    return y, a_ext[:, a_ext.shape[1] - (CONV_W - 1):]


def setup_inputs(seed: int = 0) -> dict:
    key = jax.random.key(seed)
    ks = jax.random.split(key, 32)
    f32 = jnp.float32
    nrm = lambda k, shape, scale: jax.random.normal(k, shape, f32) * scale
    win_lens = [min(w, PAST_LEN) for (w, _) in DILATED_GROUPS]
    return {
        "x_prompt": nrm(ks[0], (BATCH, SEQ, D_MODEL), 1.0),
        "x_sample": nrm(ks[1], (DEC_BATCH, DEC_SEQ, D_MODEL), 1.0),
        "mem_prompt": nrm(ks[2], (BATCH, N_MEM, D_MODEL), 1.0),
        "cache_win128_kv": nrm(ks[3], (DEPTH, DEC_BATCH, win_lens[0], 2, HEADS_PER_GROUP_A, HEAD_DIM_A), 1.0),
        "cache_win512_kv": nrm(ks[4], (DEPTH, DEC_BATCH, win_lens[1], 2, HEADS_PER_GROUP_A, HEAD_DIM_A), 1.0),
        "cache_win2048_kv": nrm(ks[5], (DEPTH, DEC_BATCH, win_lens[2], 2, HEADS_PER_GROUP_A, HEAD_DIM_A), 1.0),
        "cache_mem_kv": nrm(ks[6], (DEPTH, DEC_BATCH, N_MEM, 2, N_HEADS_M, HEAD_DIM_M), 1.0),
        "state_ffn_conv": nrm(ks[7], (DEPTH, DEC_BATCH, CONV_W - 1, D_FF), 1.0),
        "w_in": nrm(ks[8], (DEPTH, D_MODEL, IN_WIDTH), D_MODEL ** -0.5),
        "b_gate": nrm(ks[9], (DEPTH, N_BRANCH, D_MODEL), 0.02),
        "ln_v_g": 1.0 + nrm(ks[10], (DEPTH, WIDTH_B), 0.02),
        "ln_v_b": nrm(ks[11], (DEPTH, WIDTH_B), 0.02),
        "w_spatial": nrm(ks[12], (DEPTH, N_GROUPS_B, CHUNK, CHUNK), CHUNK ** -0.5),
        "b_spatial": 1.0 + nrm(ks[13], (DEPTH, N_GROUPS_B, CHUNK), 0.02),
        "w_mem_kv": nrm(ks[14], (DEPTH, D_MODEL, 2 * WIDTH_M), D_MODEL ** -0.5),
        "w_branch_a": nrm(ks[15], (DEPTH, COMB_WIDTH_A, D_MODEL), COMB_WIDTH_A ** -0.5),
        "w_branch_b": nrm(ks[16], (DEPTH, WIDTH_B, D_MODEL), WIDTH_B ** -0.5),
        "w_branch_m": nrm(ks[17], (DEPTH, WIDTH_M, D_MODEL), WIDTH_M ** -0.5),
        "w_out": nrm(ks[18], (DEPTH, D_MODEL, D_MODEL), BETA * D_MODEL ** -0.5),
        "ln1_g": 1.0 + nrm(ks[19], (DEPTH, D_MODEL), 0.02),
        "ln1_b": nrm(ks[20], (DEPTH, D_MODEL), 0.02),
        "w_up": nrm(ks[21], (DEPTH, D_MODEL, 2 * D_FF), D_MODEL ** -0.5),
        "conv_w": nrm(ks[22], (DEPTH, CONV_W, D_FF), CONV_W ** -0.5),
        "conv_b": nrm(ks[23], (DEPTH, D_FF), 0.02),
        "w_down": nrm(ks[24], (DEPTH, D_FF, D_MODEL), BETA * D_FF ** -0.5),
        "ln2_g": 1.0 + nrm(ks[25], (DEPTH, D_MODEL), 0.02),
        "ln2_b": nrm(ks[26], (DEPTH, D_MODEL), 0.02),
    }


def reference(x_prompt, x_sample, mem_prompt, cache_win128_kv, cache_win512_kv, cache_win2048_kv,
              cache_mem_kv, state_ffn_conv, w_in, b_gate, ln_v_g, ln_v_b, w_spatial, b_spatial,
              w_mem_kv, w_branch_a, w_branch_b, w_branch_m, w_out, ln1_g, ln1_b, w_up, conv_w,
              conv_b, w_down, ln2_g, ln2_b):
    win_caches = (cache_win128_kv, cache_win512_kv, cache_win2048_kv)
    yp, ys = x_prompt, x_sample
    win_p = [[] for _ in range(N_GROUPS_A)]
    win_s = [[] for _ in range(N_GROUPS_A)]
    mem_p, conv_p_list, gmlp_s, conv_s_list = [], [], [], []
    for l in range(DEPTH):
        B, S = yp.shape[:2]
        qa, ka, va, ub, vb, qm, gp = split_in(yp @ w_in[l])
        qa, ka, va = (heads(t, N_HEADS_A, HEAD_DIM_A) for t in (qa, ka, va))
        outs, lses = [], []
        for g, (win, dil) in enumerate(DILATED_GROUPS):
            qg, kg, vg = group_slice(qa, g), group_slice(ka, g), group_slice(va, g)
            o, lse = dilated_attn_prompt(qg, kg, vg, win, dil)
            outs.append(o)
            lses.append(lse)
            keep = min(win, S)
            win_p[g].append(jnp.stack([kg[:, S - keep:], vg[:, S - keep:]], axis=2))
        o_a = combine_dilations(outs, lses)
        o_b, _ = gmlp_branch(ub, vb, ln_v_g[l], ln_v_b[l], w_spatial[l], b_spatial[l], CHUNK)
        mkv = (mem_prompt @ w_mem_kv[l]).reshape(B, N_MEM, 2, N_HEADS_M, HEAD_DIM_M)
        mem_p.append(mkv)
        o_m = memory_attn(heads(qm, N_HEADS_M, HEAD_DIM_M), mkv[:, :, 0], mkv[:, :, 1])
        conv0 = jnp.zeros((B, CONV_W - 1, D_FF), yp.dtype)
        yp_next, conv_p = merge_and_ffn(yp, o_a, o_b, o_m, gp, conv0, b_gate[l], w_branch_a[l],
                                        w_branch_b[l], w_branch_m[l], w_out[l], ln1_g[l], ln1_b[l],
                                        w_up[l], conv_w[l], conv_b[l], w_down[l], ln2_g[l], ln2_b[l])
        conv_p_list.append(conv_p)
        T = ys.shape[1]
        qa, ka, va, ub, vb, qm, gs = split_in(ys @ w_in[l])
        qa, ka, va = (heads(t, N_HEADS_A, HEAD_DIM_A) for t in (qa, ka, va))
        outs, lses = [], []
        for g, (win, dil) in enumerate(DILATED_GROUPS):
            qg, kg, vg = group_slice(qa, g), group_slice(ka, g), group_slice(va, g)
            buf = win_caches[g][l]
            o, lse = dilated_attn_sample(qg, kg, vg, buf[:, :, 0], buf[:, :, 1], win, dil)
            outs.append(o)
            lses.append(lse)
            win_s[g].append(jnp.stack([kg, vg], axis=2))
        o_a = combine_dilations(outs, lses)
        o_b, v_rows = gmlp_branch(ub, vb, ln_v_g[l], ln_v_b[l], w_spatial[l], b_spatial[l], T)
        gmlp_s.append(v_rows)
        mkv = cache_mem_kv[l]
        o_m = memory_attn(heads(qm, N_HEADS_M, HEAD_DIM_M), mkv[:, :, 0], mkv[:, :, 1])
        ys_next, conv_s = merge_and_ffn(ys, o_a, o_b, o_m, gs, state_ffn_conv[l], b_gate[l], w_branch_a[l],
                                        w_branch_b[l], w_branch_m[l], w_out[l], ln1_g[l], ln1_b[l],
                                        w_up[l], conv_w[l], conv_b[l], w_down[l], ln2_g[l], ln2_b[l])
        conv_s_list.append(conv_s)
        yp, ys = yp_next, ys_next
    new_win128_kv_prompt = jnp.stack(win_p[0])
    new_win512_kv_prompt = jnp.stack(win_p[1])
    new_win2048_kv_prompt = jnp.stack(win_p[2])
    new_mem_kv_prompt = jnp.stack(mem_p)
    new_ffn_conv_prompt = jnp.stack(conv_p_list)
    new_win128_kv_sample = jnp.stack(win_s[0])
    new_win512_kv_sample = jnp.stack(win_s[1])
    new_win2048_kv_sample = jnp.stack(win_s[2])
    new_gmlp_v_sample = jnp.stack(gmlp_s)
    new_ffn_conv_sample = jnp.stack(conv_s_list)
    return (yp, ys, new_win128_kv_prompt, new_win512_kv_prompt, new_win2048_kv_prompt,
            new_mem_kv_prompt, new_ffn_conv_prompt, new_win128_kv_sample, new_win512_kv_sample,
            new_win2048_kv_sample, new_gmlp_v_sample, new_ffn_conv_sample)
```

```python
import functools

import jax
import jax.numpy as jnp
from jax import lax
from jax.experimental import pallas as pl
from jax.experimental.pallas import tpu as pltpu

BF16 = jnp.bfloat16
F32 = jnp.float32

HEAD_DIM_A = 64
HEADS_PER_GROUP_A = 4
DILATED_GROUPS = ((128, 1), (512, 4), (2048, 16))
N_GROUPS_A = len(DILATED_GROUPS)
GROUP_WIDTH_A = HEADS_PER_GROUP_A * HEAD_DIM_A
WIDTH_A = N_GROUPS_A * GROUP_WIDTH_A
NK = 128
CHUNK = 128
N_GROUPS_B = 4
WIDTH_B = 768
GROUP_DIM_B = WIDTH_B // N_GROUPS_B
N_HEADS_M = 4
HEAD_DIM_M = 128
WIDTH_M = N_HEADS_M * HEAD_DIM_M
N_BRANCH = 3
CONV_W = 3
LN_EPS = 1e-5
NEG = -1e30
PAST_LEN = 16384

LANES = 128
SUBLANES = 8
VMEM_LIMIT_BYTES = 56 * 1024 * 1024

ATTN_BLOCK = 2048
MIX_ROWS = 512
FFN_ROWS = 512
FFN_COLS = 1536
SAMPLE_BATCH_BLOCK = 4

_NT = (((1,), (1,)), ((), ()))


def _dot(a, b):
    return jnp.dot(a, b, preferred_element_type=F32)


def _dot_nt(a, b):
    return lax.dot_general(a, b, _NT, preferred_element_type=F32)


def _layer_norm(x, g, b):
    mu = jnp.mean(x, axis=-1, keepdims=True)
    var = jnp.mean(jnp.square(x - mu), axis=-1, keepdims=True)
    return (x - mu) * lax.rsqrt(var + LN_EPS) * g + b


def _params(*semantics):
    return pltpu.CompilerParams(dimension_semantics=semantics, vmem_limit_bytes=VMEM_LIMIT_BYTES)


def _resident(shape):
    zeros = (0,) * len(shape)
    return pl.BlockSpec(shape, lambda *_: zeros, pipeline_mode=pl.Buffered(1))


def _mem_kv_kernel(mem_ref, w_ref, o_ref):
    o_ref[...] = _dot(mem_ref[...].astype(BF16), w_ref[...])


def _mem_kv_proj(mem, w_bf16):
    n, d = mem.shape
    return pl.pallas_call(
        _mem_kv_kernel,
        out_shape=jax.ShapeDtypeStruct((n, w_bf16.shape[1]), F32),
        name="mem_kv_proj",
    )(mem, w_bf16)


def _qkv_proj_kernel(x_ref, w_ref, o_ref, stage_ref):
    xb = x_ref[...].astype(BF16)
    g_id = pl.program_id(1)
    for g, (win, dil) in enumerate(DILATED_GROUPS):

        @pl.when(g_id == g)
        def _(win=win, dil=dil):
            for c in range(3 * GROUP_WIDTH_A // 256):
                res = _dot(xb, w_ref[:, c * 256:(c + 1) * 256])
                if dil == 1:
                    o_ref[:, c * 256:(c + 1) * 256] = res.astype(BF16)
                    continue
                stage_ref[0] = res[:, :LANES]
                stage_ref[1] = res[:, LANES:]
                for s in range(2):
                    col = c * 256 + s * LANES
                    for blk in range(ATTN_BLOCK // win):
                        for r in range(dil):
                            rows = stage_ref[s, pl.ds(blk * win + r, NK, stride=dil), :]
                            dst = blk * win + r * NK
                            o_ref[dst:dst + NK, col:col + LANES] = rows.astype(BF16)


def _qkv_proj(x, w_qkv):
    s, d = x.shape
    n_blk = s // ATTN_BLOCK
    return pl.pallas_call(
        _qkv_proj_kernel,
        grid=(n_blk, N_GROUPS_A),
        in_specs=[
            pl.BlockSpec((ATTN_BLOCK, d), lambda n, g: (n, 0)),
            pl.BlockSpec((None, d, 3 * GROUP_WIDTH_A), lambda n, g: (g, 0, 0)),
        ],
        out_specs=pl.BlockSpec((None, ATTN_BLOCK, 3 * GROUP_WIDTH_A), lambda n, g: (g, n, 0)),
        out_shape=jax.ShapeDtypeStruct((N_GROUPS_A, s, 3 * GROUP_WIDTH_A), BF16),
        scratch_shapes=[pltpu.VMEM((2, ATTN_BLOCK, LANES), F32)],
        compiler_params=_params("arbitrary", "arbitrary"),
        name="qkv_proj",
    )(x, w_qkv)


def _attn_unit(q2, kp2, kc2, vp2, vc2, prev_shift):
    lane = lax.broadcasted_iota(jnp.int32, (NK, LANES), 1)
    row = lax.broadcasted_iota(jnp.int32, (NK, NK), 0)
    col = lax.broadcasted_iota(jnp.int32, (NK, NK), 1)
    cur_ok = col <= row
    prev_ok = col >= row + prev_shift
    out = None
    lse = None
    for hh in range(2):
        in_head = (lane >= hh * HEAD_DIM_A) & (lane < (hh + 1) * HEAD_DIM_A)
        qh = jnp.where(in_head, q2, jnp.zeros_like(q2))
        s_c = jnp.where(cur_ok, _dot_nt(qh, kc2), NEG)
        s_p = jnp.where(prev_ok, _dot_nt(qh, kp2), NEG)
        m = jnp.maximum(jnp.max(s_c, axis=-1, keepdims=True), jnp.max(s_p, axis=-1, keepdims=True))
        p_c = jnp.exp(s_c - m)
        p_p = jnp.exp(s_p - m)
        den = jnp.sum(p_c, axis=-1, keepdims=True) + jnp.sum(p_p, axis=-1, keepdims=True)
        o_h = (_dot(p_c.astype(BF16), vc2) + _dot(p_p.astype(BF16), vp2)) / den
        lse_h = jnp.broadcast_to(m + jnp.log(den), (NK, LANES))
        out = o_h if hh == 0 else jnp.where(in_head, o_h, out)
        lse = lse_h if hh == 0 else jnp.where(in_head, lse_h, lse)
    return out, lse


def _dilated_attn_kernel(qkv0_ref, qkv1_ref, qkv2_ref, o_ref, kv0_ref, kv1_ref, kv2_ref, og_ref, lg_ref):
    n = pl.program_id(0)
    qkv_refs = (qkv0_ref, qkv1_ref, qkv2_ref)
    kv_refs = (kv0_ref, kv1_ref, kv2_ref)
    n_sub = ATTN_BLOCK // NK

    for g, (win, dil) in enumerate(DILATED_GROUPS):
        qkv_ref, kv_ref = qkv_refs[g], kv_refs[g]

        @pl.when(n == 0)
        def _(kv_ref=kv_ref, win=win):
            kv_ref[0:win, :] = jnp.zeros((win, 2 * GROUP_WIDTH_A), BF16)

        kv_ref[win:win + ATTN_BLOCK, :] = qkv_ref[:, GROUP_WIDTH_A:]

        def sub_block(t, carry, g=g, win=win, dil=dil, qkv_ref=qkv_ref, kv_ref=kv_ref):
            off = pl.multiple_of(t * NK, NK)
            win_blk = t // dil
            res = t - win_blk * dil
            has_prev = (n * (ATTN_BLOCK // win) + win_blk) > 0
            prev_shift = jnp.where(has_prev, 0, 2 * NK)
            nat = win_blk * win + res
            for hp in range(GROUP_WIDTH_A // LANES):
                cs = slice(hp * LANES, (hp + 1) * LANES)
                vs = slice(GROUP_WIDTH_A + hp * LANES, GROUP_WIDTH_A + (hp + 1) * LANES)
                q2 = qkv_ref[pl.ds(off, NK), cs]
                kp2 = kv_ref[pl.ds(off, NK), cs]
                vp2 = kv_ref[pl.ds(off, NK), vs]
                kc2 = kv_ref[pl.ds(off + win, NK), cs]
                vc2 = kv_ref[pl.ds(off + win, NK), vs]
                out, lse = _attn_unit(q2, kp2, kc2, vp2, vc2, prev_shift)
                if dil == 1:
                    og_ref[g, hp, pl.ds(off, NK), :] = out
                    lg_ref[g, hp, pl.ds(off, NK), :] = lse
                else:
                    og_ref[g, hp, pl.ds(nat, NK, stride=dil), :] = out
                    lg_ref[g, hp, pl.ds(nat, NK, stride=dil), :] = lse
            return carry

        lax.fori_loop(0, n_sub, sub_block, 0)
        kv_ref[0:win, :] = kv_ref[ATTN_BLOCK:ATTN_BLOCK + win, :]

    def merge(t, carry):
        off = pl.multiple_of(t * NK, NK)
        for hp in range(GROUP_WIDTH_A // LANES):
            lses = [lg_ref[g, hp, pl.ds(off, NK), :] for g in range(N_GROUPS_A)]
            m = jnp.maximum(jnp.maximum(lses[0], lses[1]), lses[2])
            es = [jnp.exp(l - m) for l in lses]
            num = sum(es[g] * og_ref[g, hp, pl.ds(off, NK), :] for g in range(N_GROUPS_A))
            o_ref[pl.ds(off, NK), hp * LANES:(hp + 1) * LANES] = (num / (es[0] + es[1] + es[2])).astype(BF16)
        return carry

    lax.fori_loop(0, n_sub, merge, 0)


def _dilated_attn(qkv):
    _, s, _ = qkv.shape
    n_blk = s // ATTN_BLOCK
    n_hp = GROUP_WIDTH_A // LANES
    in_specs = [pl.BlockSpec((None, ATTN_BLOCK, 3 * GROUP_WIDTH_A), functools.partial(lambda n, g: (g, n, 0), g=g))
                for g in range(N_GROUPS_A)]
    return pl.pallas_call(
        _dilated_attn_kernel,
        grid=(n_blk,),
        in_specs=in_specs,
        out_specs=pl.BlockSpec((ATTN_BLOCK, GROUP_WIDTH_A), lambda n: (n, 0)),
        out_shape=jax.ShapeDtypeStruct((s, GROUP_WIDTH_A), BF16),
        scratch_shapes=[pltpu.VMEM((win + ATTN_BLOCK, 2 * GROUP_WIDTH_A), BF16) for win, _ in DILATED_GROUPS]
        + [pltpu.VMEM((N_GROUPS_A, n_hp, ATTN_BLOCK, LANES), F32),
           pltpu.VMEM((N_GROUPS_A, n_hp, ATTN_BLOCK, LANES), F32)],
        compiler_params=_params("arbitrary"),
        name="dilated_attn",
    )(qkv, qkv, qkv)


def _head_rows(vec, n_heads, head_dim):
    width = n_heads * head_dim
    row = lax.broadcasted_iota(jnp.int32, (SUBLANES, width), 0)
    lane = lax.broadcasted_iota(jnp.int32, (SUBLANES, width), 1)
    own = (lane >= row * head_dim) & (lane < (row + 1) * head_dim)
    return jnp.where(own, jnp.broadcast_to(vec, (SUBLANES, width)), 0.0), own


def _sample_attn_kernel(x_ref, wqkv_ref, wqm_ref, c0_ref, c1_ref, c2_ref, cm_ref,
                        oa_ref, om_ref, kvn_ref, proj_ref):
    step = pl.program_id(0)
    caches = (c0_ref, c1_ref, c2_ref)
    n_batch = x_ref.shape[0]

    @pl.when(step == 0)
    def _():
        xb = x_ref[...].astype(BF16)
        for g in range(N_GROUPS_A):
            h = _dot(xb, wqkv_ref[g])
            proj_ref[:, g * 3 * GROUP_WIDTH_A:(g + 1) * 3 * GROUP_WIDTH_A] = h
            kvn_ref[g] = h[:, GROUP_WIDTH_A:]
        proj_ref[:, N_GROUPS_A * 3 * GROUP_WIDTH_A:] = _dot(xb, wqm_ref[...])

    def one_row(j, carry):
        b = step * SAMPLE_BATCH_BLOCK + j
        row = proj_ref[pl.ds(b, 1), :]
        lses, outs = [], []
        for g in range(N_GROUPS_A):
            base = g * 3 * GROUP_WIDTH_A
            q = row[:, base:base + GROUP_WIDTH_A]
            k_new = row[:, base + GROUP_WIDTH_A:base + 2 * GROUP_WIDTH_A]
            v_new = row[:, base + 2 * GROUP_WIDTH_A:base + 3 * GROUP_WIDTH_A]
            q_rows, own = _head_rows(q, HEADS_PER_GROUP_A, HEAD_DIM_A)
            q_rows = q_rows.astype(BF16)
            kv = caches[g][j]
            k_c = kv[:, :GROUP_WIDTH_A].astype(BF16)
            v_c = kv[:, GROUP_WIDTH_A:].astype(BF16)
            s_c = _dot_nt(q_rows, k_c)
            k_new8 = jnp.broadcast_to(k_new, (SUBLANES, GROUP_WIDTH_A)).astype(BF16)
            s_n = _dot_nt(q_rows, k_new8)[:, 0:1]
            m = jnp.maximum(jnp.max(s_c, axis=-1, keepdims=True), s_n)
            p_c = jnp.exp(s_c - m)
            p_n = jnp.exp(s_n - m)
            den = jnp.sum(p_c, axis=-1, keepdims=True) + p_n
            v_new_r = v_new.astype(BF16).astype(F32)
            o8 = (_dot(p_c.astype(BF16), v_c) + p_n.astype(BF16).astype(F32) * v_new_r) / den
            outs.append(jnp.sum(jnp.where(own, o8, 0.0), axis=0, keepdims=True))
            lse8 = jnp.broadcast_to(m + jnp.log(den), (SUBLANES, GROUP_WIDTH_A))
            lses.append(jnp.sum(jnp.where(own, lse8, 0.0), axis=0, keepdims=True))
        m3 = jnp.maximum(jnp.maximum(lses[0], lses[1]), lses[2])
        es = [jnp.exp(l - m3) for l in lses]
        oa = (es[0] * outs[0] + es[1] * outs[1] + es[2] * outs[2]) / (es[0] + es[1] + es[2])
        oa_ref[pl.ds(b, 1), :] = oa

        qm = row[:, N_GROUPS_A * 3 * GROUP_WIDTH_A:]
        qm_rows, own_m = _head_rows(qm, N_HEADS_M, HEAD_DIM_M)
        mkv = cm_ref[j]
        mk = mkv[:, :WIDTH_M].astype(BF16)
        mv = mkv[:, WIDTH_M:].astype(BF16)
        s = _dot_nt(qm_rows.astype(BF16), mk) * (HEAD_DIM_M ** -0.5)
        m = jnp.max(s, axis=-1, keepdims=True)
        p = jnp.exp(s - m)
        o8 = _dot(p.astype(BF16), mv) / jnp.sum(p, axis=-1, keepdims=True)
        om_ref[pl.ds(b, 1), :] = jnp.sum(jnp.where(own_m, o8, 0.0), axis=0, keepdims=True)
        return carry

    lax.fori_loop(0, SAMPLE_BATCH_BLOCK, one_row, 0)


def _sample_attn(xs, w_qkv, w_qm, caches, cache_mem):
    n_batch, d = xs.shape
    bb = SAMPLE_BATCH_BLOCK
    proj_w = N_GROUPS_A * 3 * GROUP_WIDTH_A + WIDTH_M
    cache_specs = [pl.BlockSpec((bb, NK, 2 * GROUP_WIDTH_A), lambda i: (i, 0, 0)) for _ in caches]
    whole = lambda shape: pl.BlockSpec(shape, lambda i: (0,) * len(shape))
    return pl.pallas_call(
        _sample_attn_kernel,
        grid=(n_batch // bb,),
        in_specs=[whole(xs.shape), whole(w_qkv.shape), whole(w_qm.shape)] + cache_specs
        + [pl.BlockSpec((bb,) + cache_mem.shape[1:], lambda i: (i, 0, 0))],
        out_specs=[whole((n_batch, GROUP_WIDTH_A)), whole((n_batch, WIDTH_M)),
                   whole((N_GROUPS_A, n_batch, 2 * GROUP_WIDTH_A))],
        out_shape=[jax.ShapeDtypeStruct((n_batch, GROUP_WIDTH_A), F32),
                   jax.ShapeDtypeStruct((n_batch, WIDTH_M), F32),
                   jax.ShapeDtypeStruct((N_GROUPS_A, n_batch, 2 * GROUP_WIDTH_A), F32)],
        scratch_shapes=[pltpu.VMEM((n_batch, proj_w), F32)],
        compiler_params=_params("arbitrary"),
        name="sample_attn",
    )(xs, w_qkv, w_qm, *caches, cache_mem)


def _mix_kernel(*refs, sample, alpha):
    if sample:
        (x_ref, oa_ref, om_ref, w_ref, bgate_ref, lnvg_ref, lnvb_ref, ws_ref, bs_ref,
         wba_ref, wbb_ref, wbm_ref, wout_ref, ln1g_ref, ln1b_ref, x1_ref, vrows_ref) = refs
    else:
        (x_ref, oa_ref, mk_ref, mv_ref, w_ref, bgate_ref, lnvg_ref, lnvb_ref, ws_ref, bs_ref,
         wba_ref, wbb_ref, wbm_ref, wout_ref, ln1g_ref, ln1b_ref, x1_ref) = refs
    d = x_ref.shape[1]
    rows = x_ref.shape[0]
    x = x_ref[...]
    xb = x.astype(BF16)
    col_u, col_v, col_qm, col_gate = 0, WIDTH_B, 2 * WIDTH_B, 2 * WIDTH_B + WIDTH_M

    def gate(k):
        z = _dot(xb, w_ref[:, col_gate + k * d:col_gate + (k + 1) * d])
        return jax.nn.sigmoid(z + bgate_ref[k:k + 1, :])

    mixed = gate(0) * _dot(oa_ref[...].astype(BF16), wba_ref[...])

    u = jax.nn.gelu(_dot(xb, w_ref[:, col_u:col_u + WIDTH_B]))
    v = _layer_norm(jax.nn.gelu(_dot(xb, w_ref[:, col_v:col_v + WIDTH_B])), lnvg_ref[...], lnvb_ref[...])
    if sample:
        vrows_ref[...] = v
        spatial = v.astype(BF16).astype(F32) * ws_ref[...].astype(BF16).astype(F32) + bs_ref[...]
    else:
        r_i = lax.broadcasted_iota(jnp.int32, (N_GROUPS_B * CHUNK, CHUNK), 0)
        c_i = lax.broadcasted_iota(jnp.int32, (N_GROUPS_B * CHUNK, CHUNK), 1)
        w_s = jnp.where((r_i % CHUNK) >= c_i, ws_ref[...], 0.0).astype(BF16)
        lane = lax.broadcasted_iota(jnp.int32, (CHUNK, WIDTH_B), 1)
        in_group = [(lane >= gb * GROUP_DIM_B) & (lane < (gb + 1) * GROUP_DIM_B) for gb in range(N_GROUPS_B)]
        vb = v.astype(BF16)
        parts = []
        for ch in range(rows // CHUNK):
            allg = _dot(w_s, vb[ch * CHUNK:(ch + 1) * CHUNK, :])
            sp = bs_ref[...]
            for gb in range(N_GROUPS_B):
                sp = sp + jnp.where(in_group[gb], allg[gb * CHUNK:(gb + 1) * CHUNK, :], 0.0)
            parts.append(sp)
        spatial = jnp.concatenate(parts, axis=0)
    o_b = u * spatial
    mixed = mixed + gate(1) * _dot(o_b.astype(BF16), wbb_ref[...])

    if sample:
        o_m = om_ref[...]
    else:
        qm = _dot(xb, w_ref[:, col_qm:col_qm + WIDTH_M]).astype(BF16)
        heads = []
        for h in range(N_HEADS_M):
            hs = slice(h * HEAD_DIM_M, (h + 1) * HEAD_DIM_M)
            s = _dot_nt(qm[:, hs], mk_ref[:, hs]) * (HEAD_DIM_M ** -0.5)
            p = jnp.exp(s - jnp.max(s, axis=-1, keepdims=True))
            heads.append(_dot(p.astype(BF16), mv_ref[:, hs]) / jnp.sum(p, axis=-1, keepdims=True))
        o_m = jnp.concatenate(heads, axis=-1)
    mixed = mixed + gate(2) * _dot(o_m.astype(BF16), wbm_ref[...])

    x1_ref[...] = _layer_norm(alpha * x + _dot(mixed.astype(BF16), wout_ref[...]), ln1g_ref[...], ln1b_ref[...])


def _mix(x, o_a, mem_or_om, w_rest, b_gate, ln_v_g, ln_v_b, w_s, b_s, w_ba, w_bb, w_bm, w_out, ln1_g, ln1_b,
         *, sample, alpha):
    s, d = x.shape
    rows = s if sample else MIX_ROWS
    tile = lambda width: pl.BlockSpec((rows, width), lambda i: (i, 0))
    consts = (w_rest, b_gate, ln_v_g, ln_v_b, w_s, b_s, w_ba, w_bb, w_bm, w_out, ln1_g, ln1_b)
    const_specs = [_resident(c.shape) for c in consts]
    if sample:
        inputs = (x, o_a, mem_or_om)
        in_specs = [tile(d), tile(GROUP_WIDTH_A), tile(WIDTH_M)]
        out_specs = [tile(d), tile(WIDTH_B)]
        out_shape = [jax.ShapeDtypeStruct((s, d), F32), jax.ShapeDtypeStruct((s, WIDTH_B), F32)]
    else:
        mk, mv = mem_or_om
        inputs = (x, o_a, mk, mv)
        in_specs = [tile(d), tile(GROUP_WIDTH_A), _resident(mk.shape), _resident(mv.shape)]
        out_specs = tile(d)
        out_shape = jax.ShapeDtypeStruct((s, d), F32)
    return pl.pallas_call(
        functools.partial(_mix_kernel, sample=sample, alpha=alpha),
        grid=(s // rows,),
        in_specs=in_specs + const_specs,
        out_specs=out_specs,
        out_shape=out_shape,
        compiler_params=_params("arbitrary"),
        name="mix_sample" if sample else "mix_prompt",
    )(*inputs, *consts)


def _ffn_chunks(d_ff):
    bounds = list(range(0, d_ff, FFN_COLS)) + [d_ff]
    return list(zip(bounds[:-1], bounds[1:]))


def _ffn_kernel(*refs, sample, alpha):
    if sample:
        (x1_ref, st0_ref, st1_ref, wup_ref, cw_ref, cb_ref, wdown_ref, g_ref, b_ref, y_ref, a_ref) = refs
    else:
        (x1_ref, wup_ref, cw_ref, cb_ref, wdown_ref, g_ref, b_ref, y_ref, a_ref, abuf_ref) = refs
    rows = x1_ref.shape[0]
    d_ff = wdown_ref.shape[0]
    x1 = x1_ref[...]
    xb = x1.astype(BF16)
    acc = jnp.zeros(x1.shape, F32)
    if not sample:
        @pl.when(pl.program_id(0) == 0)
        def _():
            abuf_ref[0:SUBLANES, :] = jnp.zeros((SUBLANES, d_ff), F32)

    for lo, hi in _ffn_chunks(d_ff):
        cs = slice(lo, hi)
        a = _dot(xb, wup_ref[:, cs])
        val = _dot(xb, wup_ref[:, d_ff + lo:d_ff + hi])
        if sample:
            a_ref[:, cs] = a
            a_m2, a_m1 = st0_ref[:, cs], st1_ref[:, cs]
        else:
            abuf_ref[SUBLANES:SUBLANES + rows, cs] = a
            a_m1 = abuf_ref[SUBLANES - 1:SUBLANES - 1 + rows, cs]
            a_m2 = abuf_ref[SUBLANES - 2:SUBLANES - 2 + rows, cs]
            tail = abuf_ref[rows:rows + SUBLANES, cs]
            abuf_ref[0:SUBLANES, cs] = tail
            a_ref[:, cs] = tail
        conv = cb_ref[:, cs] + cw_ref[0:1, cs] * a_m2 + cw_ref[1:2, cs] * a_m1 + cw_ref[2:3, cs] * a
        h = jax.nn.gelu(conv) * val
        acc = acc + _dot(h.astype(BF16), wdown_ref[cs, :])
    y_ref[...] = _layer_norm(alpha * x1 + acc, g_ref[...], b_ref[...])


def _ffn(x1, state, w_up, conv_w, conv_b, w_down, ln2_g, ln2_b, *, sample, alpha):
    s, d = x1.shape
    d_ff = w_down.shape[0]
    rows = s if sample else FFN_ROWS
    tile = lambda width: pl.BlockSpec((rows, width), lambda i: (i, 0))
    consts = (w_up, conv_w, conv_b, w_down, ln2_g, ln2_b)
    const_specs = [_resident(c.shape) for c in consts]
    if sample:
        inputs = (x1, state[:, 0], state[:, 1])
        in_specs = [tile(d), tile(d_ff), tile(d_ff)]
        out_specs = [tile(d), tile(d_ff)]
        out_shape = [jax.ShapeDtypeStruct((s, d), F32), jax.ShapeDtypeStruct((s, d_ff), F32)]
        scratch = []
    else:
        inputs = (x1,)
        in_specs = [tile(d)]
        out_specs = [tile(d), pl.BlockSpec((SUBLANES, d_ff), lambda i: (0, 0))]
        out_shape = [jax.ShapeDtypeStruct((s, d), F32), jax.ShapeDtypeStruct((SUBLANES, d_ff), F32)]
        scratch = [pltpu.VMEM((rows + SUBLANES, d_ff), F32)]
    return pl.pallas_call(
        functools.partial(_ffn_kernel, sample=sample, alpha=alpha),
        grid=(s // rows,),
        in_specs=in_specs + const_specs,
        out_specs=out_specs,
        out_shape=out_shape,
        scratch_shapes=scratch,
        compiler_params=_params("arbitrary"),
        name="ffn_sample" if sample else "ffn_prompt",
    )(*inputs, *consts)


def _natural_order(perm_rows, win, dil):
    t = perm_rows.reshape(dil, NK, 2, HEADS_PER_GROUP_A, HEAD_DIM_A)
    return jnp.swapaxes(t, 0, 1).reshape(win, 2, HEADS_PER_GROUP_A, HEAD_DIM_A)


def kernel(x_prompt, x_sample, mem_prompt, cache_win128_kv, cache_win512_kv, cache_win2048_kv, cache_mem_kv, state_ffn_conv, w_in, b_gate, ln_v_g, ln_v_b, w_spatial, b_spatial, w_mem_kv, w_branch_a, w_branch_b, w_branch_m, w_out, ln1_g, ln1_b, w_up, conv_w, conv_b, w_down, ln2_g, ln2_b):
    depth = w_in.shape[0]
    batch, seq, d_model = x_prompt.shape
    dec_batch, dec_seq, _ = x_sample.shape
    assert batch == 1 and dec_seq == 1 and seq % ATTN_BLOCK == 0
    alpha = (2.0 * depth) ** 0.25
    win_caches = (cache_win128_kv, cache_win512_kv, cache_win2048_kv)
    for cache, (win, _) in zip(win_caches, DILATED_GROUPS):
        assert cache.shape[2] == win and PAST_LEN >= win

    yp = x_prompt.reshape(seq, d_model)
    ys = x_sample.reshape(dec_batch, d_model)
    win_p = [[] for _ in range(N_GROUPS_A)]
    win_s = [[] for _ in range(N_GROUPS_A)]
    mem_p, conv_p, gmlp_s, conv_s = [], [], [], []
    for l in range(depth):
        w_l = w_in[l]
        q_scale = HEAD_DIM_A ** -0.5
        w_qkv = jnp.stack([
            jnp.concatenate([w_l[:, g * GROUP_WIDTH_A:(g + 1) * GROUP_WIDTH_A] * q_scale,
                             w_l[:, WIDTH_A + g * GROUP_WIDTH_A:WIDTH_A + (g + 1) * GROUP_WIDTH_A],
                             w_l[:, 2 * WIDTH_A + g * GROUP_WIDTH_A:2 * WIDTH_A + (g + 1) * GROUP_WIDTH_A]], axis=1)
            for g in range(N_GROUPS_A)]).astype(BF16)
        w_rest = w_l[:, 3 * WIDTH_A:].astype(BF16)
        w_qm = w_rest[:, 2 * WIDTH_B:2 * WIDTH_B + WIDTH_M]
        row2 = lambda t: t.reshape(1, -1)
        per_lane = lambda t: jnp.repeat(t, GROUP_DIM_B, axis=-1)
        consts_tail = (w_branch_a[l].astype(BF16), w_branch_b[l].astype(BF16), w_branch_m[l].astype(BF16),
                       w_out[l].astype(BF16), row2(ln1_g[l]), row2(ln1_b[l]))
        ffn_consts = (w_up[l].astype(BF16), conv_w[l], row2(conv_b[l]), w_down[l].astype(BF16),
                      row2(ln2_g[l]), row2(ln2_b[l]))

        mkv = _mem_kv_proj(mem_prompt[0], w_mem_kv[l].astype(BF16))
        mem_p.append(mkv.reshape(1, -1, 2, N_HEADS_M, HEAD_DIM_M))
        qkv = _qkv_proj(yp, w_qkv)
        for g, (win, dil) in enumerate(DILATED_GROUPS):
            keep = min(win, seq)
            win_p[g].append(_natural_order(qkv[g, seq - win:, GROUP_WIDTH_A:].astype(F32), win, dil)[win - keep:][None])
        o_a = _dilated_attn(qkv)
        x1 = _mix(yp, o_a, (mkv[:, :WIDTH_M].astype(BF16), mkv[:, WIDTH_M:].astype(BF16)), w_rest, b_gate[l],
                  row2(ln_v_g[l]), row2(ln_v_b[l]), w_spatial[l].reshape(N_GROUPS_B * CHUNK, CHUNK),
                  per_lane(b_spatial[l].T), *consts_tail, sample=False, alpha=alpha)
        yp_next, a_tail = _ffn(x1, None, *ffn_consts, sample=False, alpha=alpha)
        conv_p.append(a_tail[SUBLANES - (CONV_W - 1):][None])

        caches = [c[l].reshape(dec_batch, NK, dil * 2 * GROUP_WIDTH_A) for c, (_, dil) in zip(win_caches, DILATED_GROUPS)]
        cache_mem = cache_mem_kv[l].reshape(dec_batch, -1, 2 * WIDTH_M)
        oa_s, om_s, kv_new = _sample_attn(ys, w_qkv, w_qm, caches, cache_mem)
        for g in range(N_GROUPS_A):
            win_s[g].append(kv_new[g].reshape(dec_batch, 1, 2, HEADS_PER_GROUP_A, HEAD_DIM_A))
        x1_s, v_rows = _mix(ys, oa_s, om_s, w_rest, b_gate[l], row2(ln_v_g[l]), row2(ln_v_b[l]),
                            row2(per_lane(w_spatial[l][:, 0, 0])), row2(per_lane(b_spatial[l][:, 0])),
                            *consts_tail, sample=True, alpha=alpha)
        gmlp_s.append(v_rows.reshape(dec_batch, 1, WIDTH_B))
        ys_next, a_s = _ffn(x1_s, state_ffn_conv[l], *ffn_consts, sample=True, alpha=alpha)
        conv_s.append(jnp.stack([state_ffn_conv[l][:, 1], a_s], axis=1))
        yp, ys = yp_next, ys_next

    return (yp.reshape(batch, seq, d_model), ys.reshape(dec_batch, dec_seq, d_model),
            jnp.stack(win_p[0]), jnp.stack(win_p[1]), jnp.stack(win_p[2]),
            jnp.stack(mem_p), jnp.stack(conv_p),
            jnp.stack(win_s[0]), jnp.stack(win_s[1]), jnp.stack(win_s[2]),
            jnp.stack(gmlp_s), jnp.stack(conv_s))
```

```python
import functools

import jax
import jax.numpy as jnp
from jax import lax
from jax.experimental import pallas as pl
from jax.experimental.pallas import tpu as pltpu

BF16 = jnp.bfloat16
F32 = jnp.float32

HEAD_DIM_A = 64
HEADS_PER_GROUP_A = 4
DILATED_GROUPS = ((128, 1), (512, 4), (2048, 16))
N_GROUPS_A = len(DILATED_GROUPS)
GROUP_WIDTH_A = HEADS_PER_GROUP_A * HEAD_DIM_A
WIDTH_A = N_GROUPS_A * GROUP_WIDTH_A
NK = 128
CHUNK = 128
N_GROUPS_B = 4
WIDTH_B = 768
GROUP_DIM_B = WIDTH_B // N_GROUPS_B
N_HEADS_M = 4
HEAD_DIM_M = 128
WIDTH_M = N_HEADS_M * HEAD_DIM_M
N_BRANCH = 3
CONV_W = 3
LN_EPS = 1e-5
NEG = -1e30
PAST_LEN = 16384

LANES = 128
SUBLANES = 8
VMEM_LIMIT_BYTES = 56 * 1024 * 1024

ATTN_BLOCK = 2048
MIX_ROWS = 512
FFN_ROWS = 512
FFN_COLS = 1536
SAMPLE_BATCH_BLOCK = 2
SAMPLE_LANE_CHUNK = 512
SAMPLE_MEM_CHUNK = 32

_NT = (((1,), (1,)), ((), ()))


def _dot(a, b):
    return jnp.dot(a, b, preferred_element_type=F32)


def _dot_nt(a, b):
    return lax.dot_general(a, b, _NT, preferred_element_type=F32)


def _layer_norm(x, g, b):
    mu = jnp.mean(x, axis=-1, keepdims=True)
    var = jnp.mean(jnp.square(x - mu), axis=-1, keepdims=True)
    return (x - mu) * lax.rsqrt(var + LN_EPS) * g + b


def _params(*semantics):
    return pltpu.CompilerParams(dimension_semantics=semantics, vmem_limit_bytes=VMEM_LIMIT_BYTES)


def _resident(shape):
    zeros = (0,) * len(shape)
    return pl.BlockSpec(shape, lambda *_: zeros, pipeline_mode=pl.Buffered(1))


def _mem_kv_kernel(mem_ref, w_ref, o_ref):
    o_ref[...] = _dot(mem_ref[...].astype(BF16), w_ref[...])


def _mem_kv_proj(mem, w_bf16):
    n, d = mem.shape
    return pl.pallas_call(
        _mem_kv_kernel,
        out_shape=jax.ShapeDtypeStruct((n, w_bf16.shape[1]), F32),
        name="mem_kv_proj",
    )(mem, w_bf16)


def _qkv_proj_kernel(x_ref, w_ref, o_ref, stage_ref):
    xb = x_ref[...].astype(BF16)
    g_id = pl.program_id(1)
    for g, (win, dil) in enumerate(DILATED_GROUPS):

        @pl.when(g_id == g)
        def _(win=win, dil=dil):
            for c in range(3 * GROUP_WIDTH_A // 256):
                res = _dot(xb, w_ref[:, c * 256:(c + 1) * 256])
                if dil == 1:
                    o_ref[:, c * 256:(c + 1) * 256] = res.astype(BF16)
                    continue
                stage_ref[0] = res[:, :LANES]
                stage_ref[1] = res[:, LANES:]
                for s in range(2):
                    col = c * 256 + s * LANES
                    for blk in range(ATTN_BLOCK // win):
                        for r in range(dil):
                            rows = stage_ref[s, pl.ds(blk * win + r, NK, stride=dil), :]
                            dst = blk * win + r * NK
                            o_ref[dst:dst + NK, col:col + LANES] = rows.astype(BF16)


def _qkv_proj(x, w_qkv):
    s, d = x.shape
    n_blk = s // ATTN_BLOCK
    return pl.pallas_call(
        _qkv_proj_kernel,
        grid=(n_blk, N_GROUPS_A),
        in_specs=[
            pl.BlockSpec((ATTN_BLOCK, d), lambda n, g: (n, 0)),
            pl.BlockSpec((None, d, 3 * GROUP_WIDTH_A), lambda n, g: (g, 0, 0)),
        ],
        out_specs=pl.BlockSpec((None, ATTN_BLOCK, 3 * GROUP_WIDTH_A), lambda n, g: (g, n, 0)),
        out_shape=jax.ShapeDtypeStruct((N_GROUPS_A, s, 3 * GROUP_WIDTH_A), BF16),
        scratch_shapes=[pltpu.VMEM((2, ATTN_BLOCK, LANES), F32)],
        compiler_params=_params("arbitrary", "arbitrary"),
        name="qkv_proj",
    )(x, w_qkv)


def _attn_unit(q2, kp2, kc2, vp2, vc2, prev_shift):
    lane = lax.broadcasted_iota(jnp.int32, (NK, LANES), 1)
    row = lax.broadcasted_iota(jnp.int32, (NK, NK), 0)
    col = lax.broadcasted_iota(jnp.int32, (NK, NK), 1)
    cur_ok = col <= row
    prev_ok = col >= row + prev_shift
    out = None
    lse = None
    for hh in range(2):
        in_head = (lane >= hh * HEAD_DIM_A) & (lane < (hh + 1) * HEAD_DIM_A)
        qh = jnp.where(in_head, q2, jnp.zeros_like(q2))
        s_c = jnp.where(cur_ok, _dot_nt(qh, kc2), NEG)
        s_p = jnp.where(prev_ok, _dot_nt(qh, kp2), NEG)
        m = jnp.maximum(jnp.max(s_c, axis=-1, keepdims=True), jnp.max(s_p, axis=-1, keepdims=True))
        p_c = jnp.exp(s_c - m)
        p_p = jnp.exp(s_p - m)
        den = jnp.sum(p_c, axis=-1, keepdims=True) + jnp.sum(p_p, axis=-1, keepdims=True)
        o_h = (_dot(p_c.astype(BF16), vc2) + _dot(p_p.astype(BF16), vp2)) / den
        lse_h = jnp.broadcast_to(m + jnp.log(den), (NK, LANES))
        out = o_h if hh == 0 else jnp.where(in_head, o_h, out)
        lse = lse_h if hh == 0 else jnp.where(in_head, lse_h, lse)
    return out, lse


def _dilated_attn_kernel(qkv0_ref, qkv1_ref, qkv2_ref, o_ref, kv0_ref, kv1_ref, kv2_ref, og_ref, lg_ref):
    n = pl.program_id(0)
    qkv_refs = (qkv0_ref, qkv1_ref, qkv2_ref)
    kv_refs = (kv0_ref, kv1_ref, kv2_ref)
    n_sub = ATTN_BLOCK // NK

    for g, (win, dil) in enumerate(DILATED_GROUPS):
        qkv_ref, kv_ref = qkv_refs[g], kv_refs[g]

        @pl.when(n == 0)
        def _(kv_ref=kv_ref, win=win):
            kv_ref[0:win, :] = jnp.zeros((win, 2 * GROUP_WIDTH_A), BF16)

        kv_ref[win:win + ATTN_BLOCK, :] = qkv_ref[:, GROUP_WIDTH_A:]

        def sub_block(t, carry, g=g, win=win, dil=dil, qkv_ref=qkv_ref, kv_ref=kv_ref):
            off = pl.multiple_of(t * NK, NK)
            win_blk = t // dil
            res = t - win_blk * dil
            has_prev = (n * (ATTN_BLOCK // win) + win_blk) > 0
            prev_shift = jnp.where(has_prev, 0, 2 * NK)
            nat = win_blk * win + res
            for hp in range(GROUP_WIDTH_A // LANES):
                cs = slice(hp * LANES, (hp + 1) * LANES)
                vs = slice(GROUP_WIDTH_A + hp * LANES, GROUP_WIDTH_A + (hp + 1) * LANES)
                q2 = qkv_ref[pl.ds(off, NK), cs]
                kp2 = kv_ref[pl.ds(off, NK), cs]
                vp2 = kv_ref[pl.ds(off, NK), vs]
                kc2 = kv_ref[pl.ds(off + win, NK), cs]
                vc2 = kv_ref[pl.ds(off + win, NK), vs]
                out, lse = _attn_unit(q2, kp2, kc2, vp2, vc2, prev_shift)
                if dil == 1:
                    og_ref[g, hp, pl.ds(off, NK), :] = out
                    lg_ref[g, hp, pl.ds(off, NK), :] = lse
                else:
                    og_ref[g, hp, pl.ds(nat, NK, stride=dil), :] = out
                    lg_ref[g, hp, pl.ds(nat, NK, stride=dil), :] = lse
            return carry

        lax.fori_loop(0, n_sub, sub_block, 0)
        kv_ref[0:win, :] = kv_ref[ATTN_BLOCK:ATTN_BLOCK + win, :]

    def merge(t, carry):
        off = pl.multiple_of(t * NK, NK)
        for hp in range(GROUP_WIDTH_A // LANES):
            lses = [lg_ref[g, hp, pl.ds(off, NK), :] for g in range(N_GROUPS_A)]
            m = jnp.maximum(jnp.maximum(lses[0], lses[1]), lses[2])
            es = [jnp.exp(l - m) for l in lses]
            num = sum(es[g] * og_ref[g, hp, pl.ds(off, NK), :] for g in range(N_GROUPS_A))
            o_ref[pl.ds(off, NK), hp * LANES:(hp + 1) * LANES] = (num / (es[0] + es[1] + es[2])).astype(BF16)
        return carry

    lax.fori_loop(0, n_sub, merge, 0)


def _dilated_attn(qkv):
    _, s, _ = qkv.shape
    n_blk = s // ATTN_BLOCK
    n_hp = GROUP_WIDTH_A // LANES
    in_specs = [pl.BlockSpec((None, ATTN_BLOCK, 3 * GROUP_WIDTH_A), functools.partial(lambda n, g: (g, n, 0), g=g))
                for g in range(N_GROUPS_A)]
    return pl.pallas_call(
        _dilated_attn_kernel,
        grid=(n_blk,),
        in_specs=in_specs,
        out_specs=pl.BlockSpec((ATTN_BLOCK, GROUP_WIDTH_A), lambda n: (n, 0)),
        out_shape=jax.ShapeDtypeStruct((s, GROUP_WIDTH_A), BF16),
        scratch_shapes=[pltpu.VMEM((win + ATTN_BLOCK, 2 * GROUP_WIDTH_A), BF16) for win, _ in DILATED_GROUPS]
        + [pltpu.VMEM((N_GROUPS_A, n_hp, ATTN_BLOCK, LANES), F32),
           pltpu.VMEM((N_GROUPS_A, n_hp, ATTN_BLOCK, LANES), F32)],
        compiler_params=_params("arbitrary"),
        name="dilated_attn",
    )(qkv, qkv, qkv)


def _to_heads(row, n_heads, head_dim):
    return jnp.concatenate([row[:, h * head_dim:(h + 1) * head_dim] for h in range(n_heads)], axis=0)


def _from_heads(t):
    return jnp.concatenate([t[h:h + 1] for h in range(t.shape[0])], axis=1)


def _window_cache_attention(c_ref, j, cols_ref, base, dil):
    win = c_ref.shape[-1]
    lc = min(win, SAMPLE_LANE_CHUNK)
    n_chunks = win // lc
    lane = lax.broadcasted_iota(jnp.int32, (1, win), 1)
    wanted = (lane % dil) == 0
    out = []
    for h in range(HEADS_PER_GROUP_A):
        lo = base + h * HEAD_DIM_A
        q, k_new, v_new = (cols_ref[lo + c * GROUP_WIDTH_A:lo + c * GROUP_WIDTH_A + HEAD_DIM_A, :] for c in range(3))
        q_wide = jnp.concatenate([q] * (lc // LANES), axis=1)
        parts = [jnp.sum(c_ref[j, 0, h, :, i * lc:(i + 1) * lc] * q_wide, axis=0, keepdims=True)
                 for i in range(n_chunks)]
        s = jnp.where(wanted, jnp.concatenate(parts, axis=1), NEG)
        s_n = jnp.sum(k_new[:, 0:1] * q[:, 0:1], axis=0, keepdims=True)
        m = jnp.maximum(jnp.max(s, axis=-1, keepdims=True), s_n)
        p = jnp.exp(s - m)
        p_n = jnp.exp(s_n - m)
        den = jnp.sum(p, axis=-1, keepdims=True) + p_n
        acc = p_n * v_new[:, 0:1]
        for i in range(n_chunks):
            acc = acc + jnp.sum(c_ref[j, 1, h, :, i * lc:(i + 1) * lc] * p[:, i * lc:(i + 1) * lc],
                                axis=-1, keepdims=True)
        out.append((acc / den, m + jnp.log(den)))
    return out


def _memory_cache_attention(cm_ref, j, q):
    n_heads, head_dim = q.shape
    chunk = SAMPLE_MEM_CHUNK

    def body(c, carry):
        m, den, acc = carry
        pos = pl.ds(pl.multiple_of(c * chunk, chunk), chunk)
        s = jnp.sum(cm_ref[j, pos, 0] * q[None], axis=-1, keepdims=True) * (head_dim ** -0.5)
        m_new = jnp.maximum(m, jnp.max(s, axis=0))
        rescale = jnp.exp(m - m_new)
        p = jnp.exp(s - m_new[None])
        return m_new, rescale * den + jnp.sum(p, axis=0), rescale * acc + jnp.sum(p * cm_ref[j, pos, 1], axis=0)

    init = (jnp.full((n_heads, 1), NEG, F32), jnp.zeros((n_heads, 1), F32), jnp.zeros((n_heads, head_dim), F32))
    _, den, acc = lax.fori_loop(0, cm_ref.shape[1] // chunk, body, init)
    return acc / den


def _sample_attn_kernel(x_ref, wqkv_ref, wqm_ref, c0_ref, c1_ref, c2_ref, cm_ref,
                        oa_ref, om_ref, kvn_ref, proj_ref, projt_ref, cols_ref, oat_ref):
    step = pl.program_id(0)
    caches = (c0_ref, c1_ref, c2_ref)
    n_batch, d = x_ref.shape
    qkv_w = N_GROUPS_A * 3 * GROUP_WIDTH_A

    @pl.when(step == 0)
    def _():
        xb = jnp.concatenate([x_ref[...], jnp.zeros((LANES - n_batch, d), F32)], axis=0).astype(BF16)
        for g in range(N_GROUPS_A):
            h = _dot(xb, wqkv_ref[g])
            proj_ref[:, g * 3 * GROUP_WIDTH_A:(g + 1) * 3 * GROUP_WIDTH_A] = h
            kvn_ref[g] = h[:n_batch, GROUP_WIDTH_A:]
        proj_ref[:, qkv_w:] = _dot(xb, wqm_ref[...])
        for t in range(qkv_w // LANES):
            projt_ref[t * LANES:(t + 1) * LANES, :] = proj_ref[:, t * LANES:(t + 1) * LANES].T.astype(BF16)
        oat_ref[...] = jnp.zeros(oat_ref.shape, F32)

    def one_row(j, carry):
        b = step * SAMPLE_BATCH_BLOCK + j
        pick_row = (lax.broadcasted_iota(jnp.int32, (LANES, LANES), 0) == b).astype(BF16)
        cols_ref[...] = _dot(projt_ref[...], pick_row)
        per_group = [_window_cache_attention(caches[g], j, cols_ref, g * 3 * GROUP_WIDTH_A, dil)
                     for g, (_, dil) in enumerate(DILATED_GROUPS)]
        this_lane = lax.broadcasted_iota(jnp.int32, (1, LANES), 1) == b
        for h in range(HEADS_PER_GROUP_A):
            outs = [per_group[g][h][0] for g in range(N_GROUPS_A)]
            lses = [per_group[g][h][1] for g in range(N_GROUPS_A)]
            m3 = jnp.maximum(jnp.maximum(lses[0], lses[1]), lses[2])
            es = [jnp.exp(l - m3) for l in lses]
            oa = (es[0] * outs[0] + es[1] * outs[1] + es[2] * outs[2]) / (es[0] + es[1] + es[2])
            rows = slice(h * HEAD_DIM_A, (h + 1) * HEAD_DIM_A)
            oat_ref[rows, :] = jnp.where(this_lane, oa, oat_ref[rows, :])

        qm = _to_heads(proj_ref[pl.ds(b, 1), qkv_w:], N_HEADS_M, HEAD_DIM_M)
        om_ref[pl.ds(b, 1), :] = _from_heads(_memory_cache_attention(cm_ref, j, qm))
        return carry

    lax.fori_loop(0, SAMPLE_BATCH_BLOCK, one_row, 0)

    @pl.when(step == pl.num_programs(0) - 1)
    def _():
        oa_ref[...] = oat_ref[...].T[:n_batch, :]


def _sample_attn(xs, w_qkv, w_qm, caches, cache_mem):
    n_batch, d = xs.shape
    assert n_batch <= LANES
    bb = SAMPLE_BATCH_BLOCK
    qkv_w = N_GROUPS_A * 3 * GROUP_WIDTH_A
    cache_specs = [pl.BlockSpec((bb,) + c.shape[1:], lambda i: (i, 0, 0, 0, 0)) for c in caches]
    whole = lambda shape: pl.BlockSpec(shape, lambda i: (0,) * len(shape))
    return pl.pallas_call(
        _sample_attn_kernel,
        grid=(n_batch // bb,),
        in_specs=[whole(xs.shape), _resident(w_qkv.shape), _resident(w_qm.shape)] + cache_specs
        + [pl.BlockSpec((bb,) + cache_mem.shape[1:], lambda i: (i, 0, 0, 0, 0))],
        out_specs=[whole((n_batch, GROUP_WIDTH_A)), whole((n_batch, WIDTH_M)),
                   whole((N_GROUPS_A, n_batch, 2 * GROUP_WIDTH_A))],
        out_shape=[jax.ShapeDtypeStruct((n_batch, GROUP_WIDTH_A), F32),
                   jax.ShapeDtypeStruct((n_batch, WIDTH_M), F32),
                   jax.ShapeDtypeStruct((N_GROUPS_A, n_batch, 2 * GROUP_WIDTH_A), F32)],
        scratch_shapes=[pltpu.VMEM((LANES, qkv_w + WIDTH_M), F32),
                        pltpu.VMEM((qkv_w, LANES), BF16),
                        pltpu.VMEM((qkv_w, LANES), F32),
                        pltpu.VMEM((GROUP_WIDTH_A, LANES), F32)],
        compiler_params=_params("arbitrary"),
        name="sample_attn",
    )(xs, w_qkv, w_qm, *caches, cache_mem)


def _mix_kernel(*refs, sample, alpha):
    if sample:
        (x_ref, oa_ref, om_ref, w_ref, bgate_ref, lnvg_ref, lnvb_ref, ws_ref, bs_ref,
         wba_ref, wbb_ref, wbm_ref, wout_ref, ln1g_ref, ln1b_ref, x1_ref, vrows_ref) = refs
    else:
        (x_ref, oa_ref, mk_ref, mv_ref, w_ref, bgate_ref, lnvg_ref, lnvb_ref, ws_ref, bs_ref,
         wba_ref, wbb_ref, wbm_ref, wout_ref, ln1g_ref, ln1b_ref, x1_ref) = refs
    d = x_ref.shape[1]
    rows = x_ref.shape[0]
    x = x_ref[...]
    xb = x.astype(BF16)
    col_u, col_v, col_qm, col_gate = 0, WIDTH_B, 2 * WIDTH_B, 2 * WIDTH_B + WIDTH_M

    def gate(k):
        z = _dot(xb, w_ref[:, col_gate + k * d:col_gate + (k + 1) * d])
        return jax.nn.sigmoid(z + bgate_ref[k:k + 1, :])

    mixed = gate(0) * _dot(oa_ref[...].astype(BF16), wba_ref[...])

    u = jax.nn.gelu(_dot(xb, w_ref[:, col_u:col_u + WIDTH_B]))
    v = _layer_norm(jax.nn.gelu(_dot(xb, w_ref[:, col_v:col_v + WIDTH_B])), lnvg_ref[...], lnvb_ref[...])
    if sample:
        vrows_ref[...] = v
        spatial = v.astype(BF16).astype(F32) * ws_ref[...].astype(BF16).astype(F32) + bs_ref[...]
    else:
        r_i = lax.broadcasted_iota(jnp.int32, (N_GROUPS_B * CHUNK, CHUNK), 0)
        c_i = lax.broadcasted_iota(jnp.int32, (N_GROUPS_B * CHUNK, CHUNK), 1)
        w_s = jnp.where((r_i % CHUNK) >= c_i, ws_ref[...], 0.0).astype(BF16)
        lane = lax.broadcasted_iota(jnp.int32, (CHUNK, WIDTH_B), 1)
        in_group = [(lane >= gb * GROUP_DIM_B) & (lane < (gb + 1) * GROUP_DIM_B) for gb in range(N_GROUPS_B)]
        vb = v.astype(BF16)
        parts = []
        for ch in range(rows // CHUNK):
            allg = _dot(w_s, vb[ch * CHUNK:(ch + 1) * CHUNK, :])
            sp = bs_ref[...]
            for gb in range(N_GROUPS_B):
                sp = sp + jnp.where(in_group[gb], allg[gb * CHUNK:(gb + 1) * CHUNK, :], 0.0)
            parts.append(sp)
        spatial = jnp.concatenate(parts, axis=0)
    o_b = u * spatial
    mixed = mixed + gate(1) * _dot(o_b.astype(BF16), wbb_ref[...])

    if sample:
        o_m = om_ref[...]
    else:
        qm = _dot(xb, w_ref[:, col_qm:col_qm + WIDTH_M]).astype(BF16)
        heads = []
        for h in range(N_HEADS_M):
            hs = slice(h * HEAD_DIM_M, (h + 1) * HEAD_DIM_M)
            s = _dot_nt(qm[:, hs], mk_ref[:, hs]) * (HEAD_DIM_M ** -0.5)
            p = jnp.exp(s - jnp.max(s, axis=-1, keepdims=True))
            heads.append(_dot(p.astype(BF16), mv_ref[:, hs]) / jnp.sum(p, axis=-1, keepdims=True))
        o_m = jnp.concatenate(heads, axis=-1)
    mixed = mixed + gate(2) * _dot(o_m.astype(BF16), wbm_ref[...])

    x1_ref[...] = _layer_norm(alpha * x + _dot(mixed.astype(BF16), wout_ref[...]), ln1g_ref[...], ln1b_ref[...])


def _mix(x, o_a, mem_or_om, w_rest, b_gate, ln_v_g, ln_v_b, w_s, b_s, w_ba, w_bb, w_bm, w_out, ln1_g, ln1_b,
         *, sample, alpha):
    s, d = x.shape
    rows = s if sample else MIX_ROWS
    tile = lambda width: pl.BlockSpec((rows, width), lambda i: (i, 0))
    consts = (w_rest, b_gate, ln_v_g, ln_v_b, w_s, b_s, w_ba, w_bb, w_bm, w_out, ln1_g, ln1_b)
    const_specs = [_resident(c.shape) for c in consts]
    if sample:
        inputs = (x, o_a, mem_or_om)
        in_specs = [tile(d), tile(GROUP_WIDTH_A), tile(WIDTH_M)]
        out_specs = [tile(d), tile(WIDTH_B)]
        out_shape = [jax.ShapeDtypeStruct((s, d), F32), jax.ShapeDtypeStruct((s, WIDTH_B), F32)]
    else:
        mk, mv = mem_or_om
        inputs = (x, o_a, mk, mv)
        in_specs = [tile(d), tile(GROUP_WIDTH_A), _resident(mk.shape), _resident(mv.shape)]
        out_specs = tile(d)
        out_shape = jax.ShapeDtypeStruct((s, d), F32)
    return pl.pallas_call(
        functools.partial(_mix_kernel, sample=sample, alpha=alpha),
        grid=(s // rows,),
        in_specs=in_specs + const_specs,
        out_specs=out_specs,
        out_shape=out_shape,
        compiler_params=_params("arbitrary"),
        name="mix_sample" if sample else "mix_prompt",
    )(*inputs, *consts)


def _ffn_chunks(d_ff):
    bounds = list(range(0, d_ff, FFN_COLS)) + [d_ff]
    return list(zip(bounds[:-1], bounds[1:]))


def _ffn_kernel(*refs, sample, alpha):
    if sample:
        (x1_ref, st0_ref, st1_ref, wup_ref, cw_ref, cb_ref, wdown_ref, g_ref, b_ref, y_ref, a_ref) = refs
    else:
        (x1_ref, wup_ref, cw_ref, cb_ref, wdown_ref, g_ref, b_ref, y_ref, a_ref, abuf_ref) = refs
    rows = x1_ref.shape[0]
    d_ff = wdown_ref.shape[0]
    x1 = x1_ref[...]
    xb = x1.astype(BF16)
    acc = jnp.zeros(x1.shape, F32)
    if not sample:
        @pl.when(pl.program_id(0) == 0)
        def _():
            abuf_ref[0:SUBLANES, :] = jnp.zeros((SUBLANES, d_ff), F32)

    for lo, hi in _ffn_chunks(d_ff):
        cs = slice(lo, hi)
        a = _dot(xb, wup_ref[:, cs])
        val = _dot(xb, wup_ref[:, d_ff + lo:d_ff + hi])
        if sample:
            a_ref[:, cs] = a
            a_m2, a_m1 = st0_ref[:, cs], st1_ref[:, cs]
        else:
            abuf_ref[SUBLANES:SUBLANES + rows, cs] = a
            a_m1 = abuf_ref[SUBLANES - 1:SUBLANES - 1 + rows, cs]
            a_m2 = abuf_ref[SUBLANES - 2:SUBLANES - 2 + rows, cs]
            tail = abuf_ref[rows:rows + SUBLANES, cs]
            abuf_ref[0:SUBLANES, cs] = tail
            a_ref[:, cs] = tail
        conv = cb_ref[:, cs] + cw_ref[0:1, cs] * a_m2 + cw_ref[1:2, cs] * a_m1 + cw_ref[2:3, cs] * a
        h = jax.nn.gelu(conv) * val
        acc = acc + _dot(h.astype(BF16), wdown_ref[cs, :])
    y_ref[...] = _layer_norm(alpha * x1 + acc, g_ref[...], b_ref[...])


def _ffn(x1, state, w_up, conv_w, conv_b, w_down, ln2_g, ln2_b, *, sample, alpha):
    s, d = x1.shape
    d_ff = w_down.shape[0]
    rows = s if sample else FFN_ROWS
    tile = lambda width: pl.BlockSpec((rows, width), lambda i: (i, 0))
    consts = (w_up, conv_w, conv_b, w_down, ln2_g, ln2_b)
    const_specs = [_resident(c.shape) for c in consts]
    if sample:
        inputs = (x1, state[:, 0], state[:, 1])
        in_specs = [tile(d), tile(d_ff), tile(d_ff)]
        out_specs = [tile(d), tile(d_ff)]
        out_shape = [jax.ShapeDtypeStruct((s, d), F32), jax.ShapeDtypeStruct((s, d_ff), F32)]
        scratch = []
    else:
        inputs = (x1,)
        in_specs = [tile(d)]
        out_specs = [tile(d), pl.BlockSpec((SUBLANES, d_ff), lambda i: (0, 0))]
        out_shape = [jax.ShapeDtypeStruct((s, d), F32), jax.ShapeDtypeStruct((SUBLANES, d_ff), F32)]
        scratch = [pltpu.VMEM((rows + SUBLANES, d_ff), F32)]
    return pl.pallas_call(
        functools.partial(_ffn_kernel, sample=sample, alpha=alpha),
        grid=(s // rows,),
        in_specs=in_specs + const_specs,
        out_specs=out_specs,
        out_shape=out_shape,
        scratch_shapes=scratch,
        compiler_params=_params("arbitrary"),
        name="ffn_sample" if sample else "ffn_prompt",
    )(*inputs, *consts)


def _natural_order(perm_rows, win, dil):
    t = perm_rows.reshape(dil, NK, 2, HEADS_PER_GROUP_A, HEAD_DIM_A)
    return jnp.swapaxes(t, 0, 1).reshape(win, 2, HEADS_PER_GROUP_A, HEAD_DIM_A)


def kernel(x_prompt, x_sample, mem_prompt, cache_win128_kv, cache_win512_kv, cache_win2048_kv, cache_mem_kv, state_ffn_conv, w_in, b_gate, ln_v_g, ln_v_b, w_spatial, b_spatial, w_mem_kv, w_branch_a, w_branch_b, w_branch_m, w_out, ln1_g, ln1_b, w_up, conv_w, conv_b, w_down, ln2_g, ln2_b):
    depth = w_in.shape[0]
    batch, seq, d_model = x_prompt.shape
    dec_batch, dec_seq, _ = x_sample.shape
    assert batch == 1 and dec_seq == 1 and seq % ATTN_BLOCK == 0
    alpha = (2.0 * depth) ** 0.25
    win_caches = (cache_win128_kv, cache_win512_kv, cache_win2048_kv)
    for cache, (win, _) in zip(win_caches, DILATED_GROUPS):
        assert cache.shape[2] == win and PAST_LEN >= win

    yp = x_prompt.reshape(seq, d_model)
    ys = x_sample.reshape(dec_batch, d_model)
    win_p = [[] for _ in range(N_GROUPS_A)]
    win_s = [[] for _ in range(N_GROUPS_A)]
    mem_p, conv_p, gmlp_s, conv_s = [], [], [], []
    for l in range(depth):
        w_l = w_in[l]
        q_scale = HEAD_DIM_A ** -0.5
        w_qkv = jnp.stack([
            jnp.concatenate([w_l[:, g * GROUP_WIDTH_A:(g + 1) * GROUP_WIDTH_A] * q_scale,
                             w_l[:, WIDTH_A + g * GROUP_WIDTH_A:WIDTH_A + (g + 1) * GROUP_WIDTH_A],
                             w_l[:, 2 * WIDTH_A + g * GROUP_WIDTH_A:2 * WIDTH_A + (g + 1) * GROUP_WIDTH_A]], axis=1)
            for g in range(N_GROUPS_A)]).astype(BF16)
        w_rest = w_l[:, 3 * WIDTH_A:].astype(BF16)
        w_qm = w_rest[:, 2 * WIDTH_B:2 * WIDTH_B + WIDTH_M]
        row2 = lambda t: t.reshape(1, -1)
        per_lane = lambda t: jnp.repeat(t, GROUP_DIM_B, axis=-1)
        consts_tail = (w_branch_a[l].astype(BF16), w_branch_b[l].astype(BF16), w_branch_m[l].astype(BF16),
                       w_out[l].astype(BF16), row2(ln1_g[l]), row2(ln1_b[l]))
        ffn_consts = (w_up[l].astype(BF16), conv_w[l], row2(conv_b[l]), w_down[l].astype(BF16),
                      row2(ln2_g[l]), row2(ln2_b[l]))

        mkv = _mem_kv_proj(mem_prompt[0], w_mem_kv[l].astype(BF16))
        mem_p.append(mkv.reshape(1, -1, 2, N_HEADS_M, HEAD_DIM_M))
        qkv = _qkv_proj(yp, w_qkv)
        for g, (win, dil) in enumerate(DILATED_GROUPS):
            keep = min(win, seq)
            win_p[g].append(_natural_order(qkv[g, seq - win:, GROUP_WIDTH_A:].astype(F32), win, dil)[win - keep:][None])
        o_a = _dilated_attn(qkv)
        x1 = _mix(yp, o_a, (mkv[:, :WIDTH_M].astype(BF16), mkv[:, WIDTH_M:].astype(BF16)), w_rest, b_gate[l],
                  row2(ln_v_g[l]), row2(ln_v_b[l]), w_spatial[l].reshape(N_GROUPS_B * CHUNK, CHUNK),
                  per_lane(b_spatial[l].T), *consts_tail, sample=False, alpha=alpha)
        yp_next, a_tail = _ffn(x1, None, *ffn_consts, sample=False, alpha=alpha)
        conv_p.append(a_tail[SUBLANES - (CONV_W - 1):][None])

        caches = [jnp.transpose(c[l], (0, 2, 3, 4, 1)) for c in win_caches]
        cache_mem = cache_mem_kv[l]
        oa_s, om_s, kv_new = _sample_attn(ys, w_qkv, w_qm, caches, cache_mem)
        for g in range(N_GROUPS_A):
            win_s[g].append(kv_new[g].reshape(dec_batch, 1, 2, HEADS_PER_GROUP_A, HEAD_DIM_A))
        x1_s, v_rows = _mix(ys, oa_s, om_s, w_rest, b_gate[l], row2(ln_v_g[l]), row2(ln_v_b[l]),
                            row2(per_lane(w_spatial[l][:, 0, 0])), row2(per_lane(b_spatial[l][:, 0])),
                            *consts_tail, sample=True, alpha=alpha)
        gmlp_s.append(v_rows.reshape(dec_batch, 1, WIDTH_B))
        ys_next, a_s = _ffn(x1_s, state_ffn_conv[l], *ffn_consts, sample=True, alpha=alpha)
        conv_s.append(jnp.stack([state_ffn_conv[l][:, 1], a_s], axis=1))
        yp, ys = yp_next, ys_next

    return (yp.reshape(batch, seq, d_model), ys.reshape(dec_batch, dec_seq, d_model),
            jnp.stack(win_p[0]), jnp.stack(win_p[1]), jnp.stack(win_p[2]),
            jnp.stack(mem_p), jnp.stack(conv_p),
            jnp.stack(win_s[0]), jnp.stack(win_s[1]), jnp.stack(win_s[2]),
            jnp.stack(gmlp_s), jnp.stack(conv_s))
```

```python
import functools

import jax
import jax.numpy as jnp
from jax import lax
from jax.experimental import pallas as pl
from jax.experimental.pallas import tpu as pltpu

BF16 = jnp.bfloat16
F32 = jnp.float32

HEAD_DIM_A = 64
HEADS_PER_GROUP_A = 4
DILATED_GROUPS = ((128, 1), (512, 4), (2048, 16))
N_GROUPS_A = len(DILATED_GROUPS)
GROUP_WIDTH_A = HEADS_PER_GROUP_A * HEAD_DIM_A
WIDTH_A = N_GROUPS_A * GROUP_WIDTH_A
NK = 128
CHUNK = 128
N_GROUPS_B = 4
WIDTH_B = 768
GROUP_DIM_B = WIDTH_B // N_GROUPS_B
N_HEADS_M = 4
HEAD_DIM_M = 128
WIDTH_M = N_HEADS_M * HEAD_DIM_M
N_BRANCH = 3
CONV_W = 3
LN_EPS = 1e-5
NEG = -1e30
PAST_LEN = 16384

LANES = 128
SUBLANES = 8
VMEM_LIMIT_BYTES = 56 * 1024 * 1024

ATTN_BLOCK = 2048
MIX_ROWS = 512
FFN_ROWS = 512
FFN_COLS = 1536
SAMPLE_BATCH_BLOCK = 2
SAMPLE_LANE_CHUNK = 512
SAMPLE_MEM_CHUNK = 32

_NT = (((1,), (1,)), ((), ()))


def _dot(a, b):
    return jnp.dot(a, b, preferred_element_type=F32)


def _dot_nt(a, b):
    return lax.dot_general(a, b, _NT, preferred_element_type=F32)


def _layer_norm(x, g, b):
    mu = jnp.mean(x, axis=-1, keepdims=True)
    var = jnp.mean(jnp.square(x - mu), axis=-1, keepdims=True)
    return (x - mu) * lax.rsqrt(var + LN_EPS) * g + b


def _params(*semantics):
    return pltpu.CompilerParams(dimension_semantics=semantics, vmem_limit_bytes=VMEM_LIMIT_BYTES)


def _resident(shape):
    zeros = (0,) * len(shape)
    return pl.BlockSpec(shape, lambda *_: zeros, pipeline_mode=pl.Buffered(1))


def _mem_kv_kernel(mem_ref, w_ref, o_ref):
    o_ref[...] = _dot(mem_ref[...].astype(BF16), w_ref[...])


def _mem_kv_proj(mem, w_bf16):
    n, d = mem.shape
    return pl.pallas_call(
        _mem_kv_kernel,
        out_shape=jax.ShapeDtypeStruct((n, w_bf16.shape[1]), F32),
        name="mem_kv_proj",
    )(mem, w_bf16)


def _qkv_proj_kernel(x_ref, w_ref, o_ref, stage_ref):
    xb = x_ref[...].astype(BF16)
    g_id = pl.program_id(1)
    for g, (win, dil) in enumerate(DILATED_GROUPS):

        @pl.when(g_id == g)
        def _(win=win, dil=dil):
            for c in range(3 * GROUP_WIDTH_A // 256):
                res = _dot(xb, w_ref[:, c * 256:(c + 1) * 256])
                if dil == 1:
                    o_ref[:, c * 256:(c + 1) * 256] = res.astype(BF16)
                    continue
                stage_ref[0] = res[:, :LANES]
                stage_ref[1] = res[:, LANES:]
                for s in range(2):
                    col = c * 256 + s * LANES
                    for blk in range(ATTN_BLOCK // win):
                        for r in range(dil):
                            rows = stage_ref[s, pl.ds(blk * win + r, NK, stride=dil), :]
                            dst = blk * win + r * NK
                            o_ref[dst:dst + NK, col:col + LANES] = rows.astype(BF16)


def _qkv_proj(x, w_qkv):
    s, d = x.shape
    n_blk = s // ATTN_BLOCK
    return pl.pallas_call(
        _qkv_proj_kernel,
        grid=(n_blk, N_GROUPS_A),
        in_specs=[
            pl.BlockSpec((ATTN_BLOCK, d), lambda n, g: (n, 0)),
            pl.BlockSpec((None, d, 3 * GROUP_WIDTH_A), lambda n, g: (g, 0, 0)),
        ],
        out_specs=pl.BlockSpec((None, ATTN_BLOCK, 3 * GROUP_WIDTH_A), lambda n, g: (g, n, 0)),
        out_shape=jax.ShapeDtypeStruct((N_GROUPS_A, s, 3 * GROUP_WIDTH_A), BF16),
        scratch_shapes=[pltpu.VMEM((2, ATTN_BLOCK, LANES), F32)],
        compiler_params=_params("arbitrary", "arbitrary"),
        name="qkv_proj",
    )(x, w_qkv)


def _attn_units(units, prev_shifts):
    first_head = lax.broadcasted_iota(jnp.int32, (NK, LANES), 1) < HEAD_DIM_A
    row = lax.broadcasted_iota(jnp.int32, (2 * NK, NK), 0) % NK
    col = lax.broadcasted_iota(jnp.int32, (2 * NK, NK), 1)
    cur_ok = col <= row
    scores = []
    for (q, k_p, k_c, _, _), shift in zip(units, prev_shifts):
        zero = jnp.zeros_like(q)
        q_st = jnp.concatenate([jnp.where(first_head, q, zero), jnp.where(first_head, zero, q)], axis=0)
        scores.append((jnp.where(cur_ok, _dot_nt(q_st, k_c), NEG),
                       jnp.where(col >= row + shift, _dot_nt(q_st, k_p), NEG)))
    maxes = [jnp.max(jnp.maximum(s_c, s_p), axis=-1, keepdims=True) for s_c, s_p in scores]
    probs = [(jnp.exp(s_c - m), jnp.exp(s_p - m)) for (s_c, s_p), m in zip(scores, maxes)]
    dens = [jnp.sum(p_c + p_p, axis=-1, keepdims=True) for p_c, p_p in probs]
    results = []
    for (_, _, _, v_p, v_c), (p_c, p_p), m, den in zip(units, probs, maxes, dens):
        o_st = (_dot(p_c.astype(BF16), v_c) + _dot(p_p.astype(BF16), v_p)) / den
        lse_st = jnp.broadcast_to(m + jnp.log(den), (2 * NK, LANES))
        results.append((jnp.where(first_head, o_st[:NK], o_st[NK:]), jnp.where(first_head, lse_st[:NK], lse_st[NK:])))
    return results


def _dilated_attn_kernel(qkv0_ref, qkv1_ref, qkv2_ref, o_ref, kv0_ref, kv1_ref, kv2_ref, og_ref, lg_ref):
    n = pl.program_id(0)
    qkv_refs = (qkv0_ref, qkv1_ref, qkv2_ref)
    kv_refs = (kv0_ref, kv1_ref, kv2_ref)
    n_sub = ATTN_BLOCK // NK
    n_hp = GROUP_WIDTH_A // LANES

    for (win, _), qkv_ref, kv_ref in zip(DILATED_GROUPS, qkv_refs, kv_refs):
        @pl.when(n == 0)
        def _(kv_ref=kv_ref, win=win):
            kv_ref[0:win, :] = jnp.zeros((win, 2 * GROUP_WIDTH_A), BF16)

        kv_ref[win:win + ATTN_BLOCK, :] = qkv_ref[:, GROUP_WIDTH_A:]

    def sub_block(t, carry):
        off = pl.multiple_of(t * NK, NK)
        units, shifts, dsts = [], [], []
        for g, (win, dil) in enumerate(DILATED_GROUPS):
            win_blk = t // dil
            res = t - win_blk * dil
            has_prev = (n * (ATTN_BLOCK // win) + win_blk) > 0
            nat = win_blk * win + res
            for hp in range(n_hp):
                ks = slice(hp * LANES, (hp + 1) * LANES)
                vs = slice(GROUP_WIDTH_A + hp * LANES, GROUP_WIDTH_A + (hp + 1) * LANES)
                units.append((qkv_refs[g][pl.ds(off, NK), ks],
                              kv_refs[g][pl.ds(off, NK), ks], kv_refs[g][pl.ds(off + win, NK), ks],
                              kv_refs[g][pl.ds(off, NK), vs], kv_refs[g][pl.ds(off + win, NK), vs]))
                shifts.append(jnp.where(has_prev, 0, 2 * NK))
                dsts.append((g, hp, pl.ds(off, NK) if dil == 1 else pl.ds(nat, NK, stride=dil)))
        for (g, hp, dst), (out, lse) in zip(dsts, _attn_units(units, shifts)):
            og_ref[g, hp, dst, :] = out
            lg_ref[g, hp, dst, :] = lse
        return carry

    lax.fori_loop(0, n_sub, sub_block, 0)
    for (win, _), kv_ref in zip(DILATED_GROUPS, kv_refs):
        kv_ref[0:win, :] = kv_ref[ATTN_BLOCK:ATTN_BLOCK + win, :]

    def merge(t, carry):
        off = pl.multiple_of(t * NK, NK)
        for hp in range(GROUP_WIDTH_A // LANES):
            lses = [lg_ref[g, hp, pl.ds(off, NK), :] for g in range(N_GROUPS_A)]
            m = jnp.maximum(jnp.maximum(lses[0], lses[1]), lses[2])
            es = [jnp.exp(l - m) for l in lses]
            num = sum(es[g] * og_ref[g, hp, pl.ds(off, NK), :] for g in range(N_GROUPS_A))
            o_ref[pl.ds(off, NK), hp * LANES:(hp + 1) * LANES] = (num / (es[0] + es[1] + es[2])).astype(BF16)
        return carry

    lax.fori_loop(0, n_sub, merge, 0)


def _dilated_attn(qkv):
    _, s, _ = qkv.shape
    n_blk = s // ATTN_BLOCK
    n_hp = GROUP_WIDTH_A // LANES
    in_specs = [pl.BlockSpec((None, ATTN_BLOCK, 3 * GROUP_WIDTH_A), functools.partial(lambda n, g: (g, n, 0), g=g))
                for g in range(N_GROUPS_A)]
    return pl.pallas_call(
        _dilated_attn_kernel,
        grid=(n_blk,),
        in_specs=in_specs,
        out_specs=pl.BlockSpec((ATTN_BLOCK, GROUP_WIDTH_A), lambda n: (n, 0)),
        out_shape=jax.ShapeDtypeStruct((s, GROUP_WIDTH_A), BF16),
        scratch_shapes=[pltpu.VMEM((win + ATTN_BLOCK, 2 * GROUP_WIDTH_A), BF16) for win, _ in DILATED_GROUPS]
        + [pltpu.VMEM((N_GROUPS_A, n_hp, ATTN_BLOCK, LANES), F32),
           pltpu.VMEM((N_GROUPS_A, n_hp, ATTN_BLOCK, LANES), F32)],
        compiler_params=_params("arbitrary"),
        name="dilated_attn",
    )(qkv, qkv, qkv)


def _to_heads(row, n_heads, head_dim):
    return jnp.concatenate([row[:, h * head_dim:(h + 1) * head_dim] for h in range(n_heads)], axis=0)


def _from_heads(t):
    return jnp.concatenate([t[h:h + 1] for h in range(t.shape[0])], axis=1)


def _window_cache_attention(c_ref, j, cols_ref, base, dil):
    win = c_ref.shape[-1]
    lc = min(win, SAMPLE_LANE_CHUNK)
    n_chunks = win // lc
    lane = lax.broadcasted_iota(jnp.int32, (1, win), 1)
    wanted = (lane % dil) == 0
    out = []
    for h in range(HEADS_PER_GROUP_A):
        lo = base + h * HEAD_DIM_A
        q, k_new, v_new = (cols_ref[lo + c * GROUP_WIDTH_A:lo + c * GROUP_WIDTH_A + HEAD_DIM_A, :] for c in range(3))
        q_wide = jnp.concatenate([q] * (lc // LANES), axis=1)
        parts = [jnp.sum(c_ref[j, 0, h, :, i * lc:(i + 1) * lc] * q_wide, axis=0, keepdims=True)
                 for i in range(n_chunks)]
        s = jnp.where(wanted, jnp.concatenate(parts, axis=1), NEG)
        s_n = jnp.sum(k_new[:, 0:1] * q[:, 0:1], axis=0, keepdims=True)
        m = jnp.maximum(jnp.max(s, axis=-1, keepdims=True), s_n)
        p = jnp.exp(s - m)
        p_n = jnp.exp(s_n - m)
        den = jnp.sum(p, axis=-1, keepdims=True) + p_n
        acc = p_n * v_new[:, 0:1]
        for i in range(n_chunks):
            acc = acc + jnp.sum(c_ref[j, 1, h, :, i * lc:(i + 1) * lc] * p[:, i * lc:(i + 1) * lc],
                                axis=-1, keepdims=True)
        out.append((acc / den, m + jnp.log(den)))
    return out


def _memory_cache_attention(cm_ref, j, q):
    n_heads, head_dim = q.shape
    chunk = SAMPLE_MEM_CHUNK

    def body(c, carry):
        m, den, acc = carry
        pos = pl.ds(pl.multiple_of(c * chunk, chunk), chunk)
        s = jnp.sum(cm_ref[j, pos, 0] * q[None], axis=-1, keepdims=True) * (head_dim ** -0.5)
        m_new = jnp.maximum(m, jnp.max(s, axis=0))
        rescale = jnp.exp(m - m_new)
        p = jnp.exp(s - m_new[None])
        return m_new, rescale * den + jnp.sum(p, axis=0), rescale * acc + jnp.sum(p * cm_ref[j, pos, 1], axis=0)

    init = (jnp.full((n_heads, 1), NEG, F32), jnp.zeros((n_heads, 1), F32), jnp.zeros((n_heads, head_dim), F32))
    _, den, acc = lax.fori_loop(0, cm_ref.shape[1] // chunk, body, init)
    return acc / den


def _sample_attn_kernel(x_ref, wqkv_ref, wqm_ref, c0_ref, c1_ref, c2_ref, cm_ref,
                        oa_ref, om_ref, kvn_ref, proj_ref, projt_ref, cols_ref, oat_ref):
    step = pl.program_id(0)
    caches = (c0_ref, c1_ref, c2_ref)
    n_batch, d = x_ref.shape
    qkv_w = N_GROUPS_A * 3 * GROUP_WIDTH_A

    @pl.when(step == 0)
    def _():
        xb = jnp.concatenate([x_ref[...], jnp.zeros((LANES - n_batch, d), F32)], axis=0).astype(BF16)
        for g in range(N_GROUPS_A):
            h = _dot(xb, wqkv_ref[g])
            proj_ref[:, g * 3 * GROUP_WIDTH_A:(g + 1) * 3 * GROUP_WIDTH_A] = h
            kvn_ref[g] = h[:n_batch, GROUP_WIDTH_A:]
        proj_ref[:, qkv_w:] = _dot(xb, wqm_ref[...])
        for t in range(qkv_w // LANES):
            projt_ref[t * LANES:(t + 1) * LANES, :] = proj_ref[:, t * LANES:(t + 1) * LANES].T.astype(BF16)
        oat_ref[...] = jnp.zeros(oat_ref.shape, F32)

    def one_row(j, carry):
        b = step * SAMPLE_BATCH_BLOCK + j
        pick_row = (lax.broadcasted_iota(jnp.int32, (LANES, LANES), 0) == b).astype(BF16)
        cols_ref[...] = _dot(projt_ref[...], pick_row)
        per_group = [_window_cache_attention(caches[g], j, cols_ref, g * 3 * GROUP_WIDTH_A, dil)
                     for g, (_, dil) in enumerate(DILATED_GROUPS)]
        this_lane = lax.broadcasted_iota(jnp.int32, (1, LANES), 1) == b
        for h in range(HEADS_PER_GROUP_A):
            outs = [per_group[g][h][0] for g in range(N_GROUPS_A)]
            lses = [per_group[g][h][1] for g in range(N_GROUPS_A)]
            m3 = jnp.maximum(jnp.maximum(lses[0], lses[1]), lses[2])
            es = [jnp.exp(l - m3) for l in lses]
            oa = (es[0] * outs[0] + es[1] * outs[1] + es[2] * outs[2]) / (es[0] + es[1] + es[2])
            rows = slice(h * HEAD_DIM_A, (h + 1) * HEAD_DIM_A)
            oat_ref[rows, :] = jnp.where(this_lane, oa, oat_ref[rows, :])

        qm = _to_heads(proj_ref[pl.ds(b, 1), qkv_w:], N_HEADS_M, HEAD_DIM_M)
        om_ref[pl.ds(b, 1), :] = _from_heads(_memory_cache_attention(cm_ref, j, qm))
        return carry

    lax.fori_loop(0, SAMPLE_BATCH_BLOCK, one_row, 0)

    @pl.when(step == pl.num_programs(0) - 1)
    def _():
        oa_ref[...] = oat_ref[...].T[:n_batch, :]


def _sample_attn(xs, w_qkv, w_qm, caches, cache_mem):
    n_batch, d = xs.shape
    assert n_batch <= LANES
    bb = SAMPLE_BATCH_BLOCK
    qkv_w = N_GROUPS_A * 3 * GROUP_WIDTH_A
    cache_specs = [pl.BlockSpec((bb,) + c.shape[1:], lambda i: (i, 0, 0, 0, 0)) for c in caches]
    whole = lambda shape: pl.BlockSpec(shape, lambda i: (0,) * len(shape))
    return pl.pallas_call(
        _sample_attn_kernel,
        grid=(n_batch // bb,),
        in_specs=[whole(xs.shape), _resident(w_qkv.shape), _resident(w_qm.shape)] + cache_specs
        + [pl.BlockSpec((bb,) + cache_mem.shape[1:], lambda i: (i, 0, 0, 0, 0))],
        out_specs=[whole((n_batch, GROUP_WIDTH_A)), whole((n_batch, WIDTH_M)),
                   whole((N_GROUPS_A, n_batch, 2 * GROUP_WIDTH_A))],
        out_shape=[jax.ShapeDtypeStruct((n_batch, GROUP_WIDTH_A), F32),
                   jax.ShapeDtypeStruct((n_batch, WIDTH_M), F32),
                   jax.ShapeDtypeStruct((N_GROUPS_A, n_batch, 2 * GROUP_WIDTH_A), F32)],
        scratch_shapes=[pltpu.VMEM((LANES, qkv_w + WIDTH_M), F32),
                        pltpu.VMEM((qkv_w, LANES), BF16),
                        pltpu.VMEM((qkv_w, LANES), F32),
                        pltpu.VMEM((GROUP_WIDTH_A, LANES), F32)],
        compiler_params=_params("arbitrary"),
        name="sample_attn",
    )(xs, w_qkv, w_qm, *caches, cache_mem)


def _mix_kernel(*refs, sample, alpha):
    if sample:
        (x_ref, oa_ref, om_ref, w_ref, bgate_ref, lnvg_ref, lnvb_ref, ws_ref, bs_ref,
         wba_ref, wbb_ref, wbm_ref, wout_ref, ln1g_ref, ln1b_ref, x1_ref, vrows_ref) = refs
    else:
        (x_ref, oa_ref, mk_ref, mv_ref, w_ref, bgate_ref, lnvg_ref, lnvb_ref, ws_ref, bs_ref,
         wba_ref, wbb_ref, wbm_ref, wout_ref, ln1g_ref, ln1b_ref, x1_ref) = refs
    d = x_ref.shape[1]
    rows = x_ref.shape[0]
    x = x_ref[...]
    xb = x.astype(BF16)
    col_u, col_v, col_qm, col_gate = 0, WIDTH_B, 2 * WIDTH_B, 2 * WIDTH_B + WIDTH_M

    def gate(k):
        z = _dot(xb, w_ref[:, col_gate + k * d:col_gate + (k + 1) * d])
        return jax.nn.sigmoid(z + bgate_ref[k:k + 1, :])

    mixed = gate(0) * _dot(oa_ref[...].astype(BF16), wba_ref[...])

    u = jax.nn.gelu(_dot(xb, w_ref[:, col_u:col_u + WIDTH_B]))
    v = _layer_norm(jax.nn.gelu(_dot(xb, w_ref[:, col_v:col_v + WIDTH_B])), lnvg_ref[...], lnvb_ref[...])
    if sample:
        vrows_ref[...] = v
        spatial = v.astype(BF16).astype(F32) * ws_ref[...].astype(BF16).astype(F32) + bs_ref[...]
    else:
        r_i = lax.broadcasted_iota(jnp.int32, (N_GROUPS_B * CHUNK, CHUNK), 0)
        c_i = lax.broadcasted_iota(jnp.int32, (N_GROUPS_B * CHUNK, CHUNK), 1)
        w_s = jnp.where((r_i % CHUNK) >= c_i, ws_ref[...], 0.0).astype(BF16)
        lane = lax.broadcasted_iota(jnp.int32, (CHUNK, WIDTH_B), 1)
        in_group = [(lane >= gb * GROUP_DIM_B) & (lane < (gb + 1) * GROUP_DIM_B) for gb in range(N_GROUPS_B)]
        vb = v.astype(BF16)
        parts = []
        for ch in range(rows // CHUNK):
            allg = _dot(w_s, vb[ch * CHUNK:(ch + 1) * CHUNK, :])
            sp = bs_ref[...]
            for gb in range(N_GROUPS_B):
                sp = sp + jnp.where(in_group[gb], allg[gb * CHUNK:(gb + 1) * CHUNK, :], 0.0)
            parts.append(sp)
        spatial = jnp.concatenate(parts, axis=0)
    o_b = u * spatial
    mixed = mixed + gate(1) * _dot(o_b.astype(BF16), wbb_ref[...])

    if sample:
        o_m = om_ref[...]
    else:
        qm = _dot(xb, w_ref[:, col_qm:col_qm + WIDTH_M]).astype(BF16)
        heads = []
        for h in range(N_HEADS_M):
            hs = slice(h * HEAD_DIM_M, (h + 1) * HEAD_DIM_M)
            s = _dot_nt(qm[:, hs], mk_ref[:, hs]) * (HEAD_DIM_M ** -0.5)
            p = jnp.exp(s - jnp.max(s, axis=-1, keepdims=True))
            heads.append(_dot(p.astype(BF16), mv_ref[:, hs]) / jnp.sum(p, axis=-1, keepdims=True))
        o_m = jnp.concatenate(heads, axis=-1)
    mixed = mixed + gate(2) * _dot(o_m.astype(BF16), wbm_ref[...])

    x1_ref[...] = _layer_norm(alpha * x + _dot(mixed.astype(BF16), wout_ref[...]), ln1g_ref[...], ln1b_ref[...])


def _mix(x, o_a, mem_or_om, w_rest, b_gate, ln_v_g, ln_v_b, w_s, b_s, w_ba, w_bb, w_bm, w_out, ln1_g, ln1_b,
         *, sample, alpha):
    s, d = x.shape
    rows = s if sample else MIX_ROWS
    tile = lambda width: pl.BlockSpec((rows, width), lambda i: (i, 0))
    consts = (w_rest, b_gate, ln_v_g, ln_v_b, w_s, b_s, w_ba, w_bb, w_bm, w_out, ln1_g, ln1_b)
    const_specs = [_resident(c.shape) for c in consts]
    if sample:
        inputs = (x, o_a, mem_or_om)
        in_specs = [tile(d), tile(GROUP_WIDTH_A), tile(WIDTH_M)]
        out_specs = [tile(d), tile(WIDTH_B)]
        out_shape = [jax.ShapeDtypeStruct((s, d), F32), jax.ShapeDtypeStruct((s, WIDTH_B), F32)]
    else:
        mk, mv = mem_or_om
        inputs = (x, o_a, mk, mv)
        in_specs = [tile(d), tile(GROUP_WIDTH_A), _resident(mk.shape), _resident(mv.shape)]
        out_specs = tile(d)
        out_shape = jax.ShapeDtypeStruct((s, d), F32)
    return pl.pallas_call(
        functools.partial(_mix_kernel, sample=sample, alpha=alpha),
        grid=(s // rows,),
        in_specs=in_specs + const_specs,
        out_specs=out_specs,
        out_shape=out_shape,
        compiler_params=_params("arbitrary"),
        name="mix_sample" if sample else "mix_prompt",
    )(*inputs, *consts)


def _ffn_chunks(d_ff):
    bounds = list(range(0, d_ff, FFN_COLS)) + [d_ff]
    return list(zip(bounds[:-1], bounds[1:]))


def _ffn_kernel(*refs, sample, alpha):
    if sample:
        (x1_ref, st0_ref, st1_ref, wup_ref, cw_ref, cb_ref, wdown_ref, g_ref, b_ref, y_ref, a_ref) = refs
    else:
        (x1_ref, wup_ref, cw_ref, cb_ref, wdown_ref, g_ref, b_ref, y_ref, a_ref, abuf_ref) = refs
    rows = x1_ref.shape[0]
    d_ff = wdown_ref.shape[0]
    x1 = x1_ref[...]
    xb = x1.astype(BF16)
    acc = jnp.zeros(x1.shape, F32)
    if not sample:
        @pl.when(pl.program_id(0) == 0)
        def _():
            abuf_ref[0:SUBLANES, :] = jnp.zeros((SUBLANES, d_ff), F32)

    for lo, hi in _ffn_chunks(d_ff):
        cs = slice(lo, hi)
        a = _dot(xb, wup_ref[:, cs])
        val = _dot(xb, wup_ref[:, d_ff + lo:d_ff + hi])
        if sample:
            a_ref[:, cs] = a
            a_m2, a_m1 = st0_ref[:, cs], st1_ref[:, cs]
        else:
            abuf_ref[SUBLANES:SUBLANES + rows, cs] = a
            a_m1 = abuf_ref[SUBLANES - 1:SUBLANES - 1 + rows, cs]
            a_m2 = abuf_ref[SUBLANES - 2:SUBLANES - 2 + rows, cs]
            tail = abuf_ref[rows:rows + SUBLANES, cs]
            abuf_ref[0:SUBLANES, cs] = tail
            a_ref[:, cs] = tail
        conv = cb_ref[:, cs] + cw_ref[0:1, cs] * a_m2 + cw_ref[1:2, cs] * a_m1 + cw_ref[2:3, cs] * a
        h = jax.nn.gelu(conv) * val
        acc = acc + _dot(h.astype(BF16), wdown_ref[cs, :])
    y_ref[...] = _layer_norm(alpha * x1 + acc, g_ref[...], b_ref[...])


def _ffn(x1, state, w_up, conv_w, conv_b, w_down, ln2_g, ln2_b, *, sample, alpha):
    s, d = x1.shape
    d_ff = w_down.shape[0]
    rows = s if sample else FFN_ROWS
    tile = lambda width: pl.BlockSpec((rows, width), lambda i: (i, 0))
    consts = (w_up, conv_w, conv_b, w_down, ln2_g, ln2_b)
    const_specs = [_resident(c.shape) for c in consts]
    if sample:
        inputs = (x1, state[:, 0], state[:, 1])
        in_specs = [tile(d), tile(d_ff), tile(d_ff)]
        out_specs = [tile(d), tile(d_ff)]
        out_shape = [jax.ShapeDtypeStruct((s, d), F32), jax.ShapeDtypeStruct((s, d_ff), F32)]
        scratch = []
    else:
        inputs = (x1,)
        in_specs = [tile(d)]
        out_specs = [tile(d), pl.BlockSpec((SUBLANES, d_ff), lambda i: (0, 0))]
        out_shape = [jax.ShapeDtypeStruct((s, d), F32), jax.ShapeDtypeStruct((SUBLANES, d_ff), F32)]
        scratch = [pltpu.VMEM((rows + SUBLANES, d_ff), F32)]
    return pl.pallas_call(
        functools.partial(_ffn_kernel, sample=sample, alpha=alpha),
        grid=(s // rows,),
        in_specs=in_specs + const_specs,
        out_specs=out_specs,
        out_shape=out_shape,
        scratch_shapes=scratch,
        compiler_params=_params("arbitrary"),
        name="ffn_sample" if sample else "ffn_prompt",
    )(*inputs, *consts)


def _natural_order(perm_rows, win, dil):
    t = perm_rows.reshape(dil, NK, 2, HEADS_PER_GROUP_A, HEAD_DIM_A)
    return jnp.swapaxes(t, 0, 1).reshape(win, 2, HEADS_PER_GROUP_A, HEAD_DIM_A)


def kernel(x_prompt, x_sample, mem_prompt, cache_win128_kv, cache_win512_kv, cache_win2048_kv, cache_mem_kv, state_ffn_conv, w_in, b_gate, ln_v_g, ln_v_b, w_spatial, b_spatial, w_mem_kv, w_branch_a, w_branch_b, w_branch_m, w_out, ln1_g, ln1_b, w_up, conv_w, conv_b, w_down, ln2_g, ln2_b):
    depth = w_in.shape[0]
    batch, seq, d_model = x_prompt.shape
    dec_batch, dec_seq, _ = x_sample.shape
    assert batch == 1 and dec_seq == 1 and seq % ATTN_BLOCK == 0
    alpha = (2.0 * depth) ** 0.25
    win_caches = (cache_win128_kv, cache_win512_kv, cache_win2048_kv)
    for cache, (win, _) in zip(win_caches, DILATED_GROUPS):
        assert cache.shape[2] == win and PAST_LEN >= win

    yp = x_prompt.reshape(seq, d_model)
    ys = x_sample.reshape(dec_batch, d_model)
    win_p = [[] for _ in range(N_GROUPS_A)]
    win_s = [[] for _ in range(N_GROUPS_A)]
    mem_p, conv_p, gmlp_s, conv_s = [], [], [], []
    for l in range(depth):
        w_l = w_in[l]
        q_scale = HEAD_DIM_A ** -0.5
        w_qkv = jnp.stack([
            jnp.concatenate([w_l[:, g * GROUP_WIDTH_A:(g + 1) * GROUP_WIDTH_A] * q_scale,
                             w_l[:, WIDTH_A + g * GROUP_WIDTH_A:WIDTH_A + (g + 1) * GROUP_WIDTH_A],
                             w_l[:, 2 * WIDTH_A + g * GROUP_WIDTH_A:2 * WIDTH_A + (g + 1) * GROUP_WIDTH_A]], axis=1)
            for g in range(N_GROUPS_A)]).astype(BF16)
        w_rest = w_l[:, 3 * WIDTH_A:].astype(BF16)
        w_qm = w_rest[:, 2 * WIDTH_B:2 * WIDTH_B + WIDTH_M]
        row2 = lambda t: t.reshape(1, -1)
        per_lane = lambda t: jnp.repeat(t, GROUP_DIM_B, axis=-1)
        consts_tail = (w_branch_a[l].astype(BF16), w_branch_b[l].astype(BF16), w_branch_m[l].astype(BF16),
                       w_out[l].astype(BF16), row2(ln1_g[l]), row2(ln1_b[l]))
        ffn_consts = (w_up[l].astype(BF16), conv_w[l], row2(conv_b[l]), w_down[l].astype(BF16),
                      row2(ln2_g[l]), row2(ln2_b[l]))

        mkv = _mem_kv_proj(mem_prompt[0], w_mem_kv[l].astype(BF16))
        mem_p.append(mkv.reshape(1, -1, 2, N_HEADS_M, HEAD_DIM_M))
        qkv = _qkv_proj(yp, w_qkv)
        for g, (win, dil) in enumerate(DILATED_GROUPS):
            keep = min(win, seq)
            win_p[g].append(_natural_order(qkv[g, seq - win:, GROUP_WIDTH_A:].astype(F32), win, dil)[win - keep:][None])
        o_a = _dilated_attn(qkv)
        x1 = _mix(yp, o_a, (mkv[:, :WIDTH_M].astype(BF16), mkv[:, WIDTH_M:].astype(BF16)), w_rest, b_gate[l],
                  row2(ln_v_g[l]), row2(ln_v_b[l]), w_spatial[l].reshape(N_GROUPS_B * CHUNK, CHUNK),
                  per_lane(b_spatial[l].T), *consts_tail, sample=False, alpha=alpha)
        yp_next, a_tail = _ffn(x1, None, *ffn_consts, sample=False, alpha=alpha)
        conv_p.append(a_tail[SUBLANES - (CONV_W - 1):][None])

        caches = [jnp.transpose(c[l], (0, 2, 3, 4, 1)) for c in win_caches]
        cache_mem = cache_mem_kv[l]
        oa_s, om_s, kv_new = _sample_attn(ys, w_qkv, w_qm, caches, cache_mem)
        for g in range(N_GROUPS_A):
            win_s[g].append(kv_new[g].reshape(dec_batch, 1, 2, HEADS_PER_GROUP_A, HEAD_DIM_A))
        x1_s, v_rows = _mix(ys, oa_s, om_s, w_rest, b_gate[l], row2(ln_v_g[l]), row2(ln_v_b[l]),
                            row2(per_lane(w_spatial[l][:, 0, 0])), row2(per_lane(b_spatial[l][:, 0])),
                            *consts_tail, sample=True, alpha=alpha)
        gmlp_s.append(v_rows.reshape(dec_batch, 1, WIDTH_B))
        ys_next, a_s = _ffn(x1_s, state_ffn_conv[l], *ffn_consts, sample=True, alpha=alpha)
        conv_s.append(jnp.stack([state_ffn_conv[l][:, 1], a_s], axis=1))
        yp, ys = yp_next, ys_next

    return (yp.reshape(batch, seq, d_model), ys.reshape(dec_batch, dec_seq, d_model),
            jnp.stack(win_p[0]), jnp.stack(win_p[1]), jnp.stack(win_p[2]),
            jnp.stack(mem_p), jnp.stack(conv_p),
            jnp.stack(win_s[0]), jnp.stack(win_s[1]), jnp.stack(win_s[2]),
            jnp.stack(gmlp_s), jnp.stack(conv_s))
```

```python
import functools

import jax
import jax.numpy as jnp
from jax import lax
from jax.experimental import pallas as pl
from jax.experimental.pallas import tpu as pltpu

BF16 = jnp.bfloat16
F32 = jnp.float32

HEAD_DIM_A = 64
HEADS_PER_GROUP_A = 4
DILATED_GROUPS = ((128, 1), (512, 4), (2048, 16))
N_GROUPS_A = len(DILATED_GROUPS)
GROUP_WIDTH_A = HEADS_PER_GROUP_A * HEAD_DIM_A
WIDTH_A = N_GROUPS_A * GROUP_WIDTH_A
NK = 128
CHUNK = 128
N_GROUPS_B = 4
WIDTH_B = 768
GROUP_DIM_B = WIDTH_B // N_GROUPS_B
N_HEADS_M = 4
HEAD_DIM_M = 128
WIDTH_M = N_HEADS_M * HEAD_DIM_M
N_BRANCH = 3
CONV_W = 3
LN_EPS = 1e-5
NEG = -1e30
PAST_LEN = 16384

LANES = 128
SUBLANES = 8
VMEM_LIMIT_BYTES = 56 * 1024 * 1024

ATTN_BLOCK = 2048
MIX_ROWS = 512
FFN_ROWS = 512
FFN_SUB_ROWS = 512
FFN_COLS = 2816
SAMPLE_BATCH_BLOCK = 2
SAMPLE_LANE_CHUNK = 512
SAMPLE_MEM_CHUNK = 32

_NT = (((1,), (1,)), ((), ()))


def _dot(a, b):
    return jnp.dot(a, b, preferred_element_type=F32)


def _dot_nt(a, b):
    return lax.dot_general(a, b, _NT, preferred_element_type=F32)


def _layer_norm(x, g, b):
    mu = jnp.mean(x, axis=-1, keepdims=True)
    var = jnp.mean(jnp.square(x - mu), axis=-1, keepdims=True)
    return (x - mu) * lax.rsqrt(var + LN_EPS) * g + b


def _params(*semantics):
    return pltpu.CompilerParams(dimension_semantics=semantics, vmem_limit_bytes=VMEM_LIMIT_BYTES)


def _resident(shape):
    zeros = (0,) * len(shape)
    return pl.BlockSpec(shape, lambda *_: zeros, pipeline_mode=pl.Buffered(1))


def _mem_kv_kernel(mem_ref, w_ref, o_ref):
    o_ref[...] = _dot(mem_ref[...].astype(BF16), w_ref[...])


def _mem_kv_proj(mem, w_bf16):
    n, d = mem.shape
    return pl.pallas_call(
        _mem_kv_kernel,
        out_shape=jax.ShapeDtypeStruct((n, w_bf16.shape[1]), F32),
        name="mem_kv_proj",
    )(mem, w_bf16)


def _qkv_proj_kernel(x_ref, w_ref, o_ref, stage_ref):
    xb = x_ref[...].astype(BF16)
    g_id = pl.program_id(1)
    for g, (win, dil) in enumerate(DILATED_GROUPS):

        @pl.when(g_id == g)
        def _(win=win, dil=dil):
            for c in range(3 * GROUP_WIDTH_A // 256):
                res = _dot(xb, w_ref[:, c * 256:(c + 1) * 256])
                if dil == 1:
                    o_ref[:, c * 256:(c + 1) * 256] = res.astype(BF16)
                    continue
                stage_ref[2 * c] = res[:, :LANES]
                stage_ref[2 * c + 1] = res[:, LANES:]
                for s in range(2):
                    col = c * 256 + s * LANES
                    for blk in range(ATTN_BLOCK // win):
                        for r in range(dil):
                            rows = stage_ref[2 * c + s, pl.ds(blk * win + r, NK, stride=dil), :]
                            dst = blk * win + r * NK
                            o_ref[dst:dst + NK, col:col + LANES] = rows.astype(BF16)


def _qkv_proj(x, w_qkv):
    s, d = x.shape
    n_blk = s // ATTN_BLOCK
    return pl.pallas_call(
        _qkv_proj_kernel,
        grid=(n_blk, N_GROUPS_A),
        in_specs=[
            pl.BlockSpec((ATTN_BLOCK, d), lambda n, g: (n, 0)),
            pl.BlockSpec((None, d, 3 * GROUP_WIDTH_A), lambda n, g: (g, 0, 0)),
        ],
        out_specs=pl.BlockSpec((None, ATTN_BLOCK, 3 * GROUP_WIDTH_A), lambda n, g: (g, n, 0)),
        out_shape=jax.ShapeDtypeStruct((N_GROUPS_A, s, 3 * GROUP_WIDTH_A), BF16),
        scratch_shapes=[pltpu.VMEM((3 * GROUP_WIDTH_A // LANES, ATTN_BLOCK, LANES), F32)],
        compiler_params=_params("arbitrary", "arbitrary"),
        name="qkv_proj",
    )(x, w_qkv)


def _attn_units(units, prev_shifts):
    first_head = lax.broadcasted_iota(jnp.int32, (NK, LANES), 1) < HEAD_DIM_A
    row = lax.broadcasted_iota(jnp.int32, (2 * NK, NK), 0) % NK
    col = lax.broadcasted_iota(jnp.int32, (2 * NK, NK), 1)
    cur_ok = col <= row
    scores = []
    for (q, k_p, k_c, _, _), shift in zip(units, prev_shifts):
        zero = jnp.zeros_like(q)
        q_st = jnp.concatenate([jnp.where(first_head, q, zero), jnp.where(first_head, zero, q)], axis=0)
        scores.append((jnp.where(cur_ok, _dot_nt(q_st, k_c), NEG),
                       jnp.where(col >= row + shift, _dot_nt(q_st, k_p), NEG)))
    maxes = [jnp.max(jnp.maximum(s_c, s_p), axis=-1, keepdims=True) for s_c, s_p in scores]
    probs = [(jnp.exp(s_c - m), jnp.exp(s_p - m)) for (s_c, s_p), m in zip(scores, maxes)]
    dens = [jnp.sum(p_c + p_p, axis=-1, keepdims=True) for p_c, p_p in probs]
    results = []
    for (_, _, _, v_p, v_c), (p_c, p_p), m, den in zip(units, probs, maxes, dens):
        o_st = (_dot(p_c.astype(BF16), v_c) + _dot(p_p.astype(BF16), v_p)) / den
        lse_st = jnp.broadcast_to(m + jnp.log(den), (2 * NK, LANES))
        results.append((jnp.where(first_head, o_st[:NK], o_st[NK:]), jnp.where(first_head, lse_st[:NK], lse_st[NK:])))
    return results


def _dilated_attn_kernel(qkv0_ref, qkv1_ref, qkv2_ref, o_ref, kv0_ref, kv1_ref, kv2_ref, og_ref, lg_ref):
    n = pl.program_id(0)
    qkv_refs = (qkv0_ref, qkv1_ref, qkv2_ref)
    kv_refs = (kv0_ref, kv1_ref, kv2_ref)
    n_sub = ATTN_BLOCK // NK
    n_hp = GROUP_WIDTH_A // LANES

    for (win, _), qkv_ref, kv_ref in zip(DILATED_GROUPS, qkv_refs, kv_refs):
        @pl.when(n == 0)
        def _(kv_ref=kv_ref, win=win):
            kv_ref[0:win, :] = jnp.zeros((win, 2 * GROUP_WIDTH_A), BF16)

        kv_ref[win:win + ATTN_BLOCK, :] = qkv_ref[:, GROUP_WIDTH_A:]

    def sub_block(t, carry):
        off = pl.multiple_of(t * NK, NK)
        units, shifts, dsts = [], [], []
        for g, (win, dil) in enumerate(DILATED_GROUPS):
            win_blk = t // dil
            res = t - win_blk * dil
            has_prev = (n * (ATTN_BLOCK // win) + win_blk) > 0
            nat = win_blk * win + res
            for hp in range(n_hp):
                ks = slice(hp * LANES, (hp + 1) * LANES)
                vs = slice(GROUP_WIDTH_A + hp * LANES, GROUP_WIDTH_A + (hp + 1) * LANES)
                units.append((qkv_refs[g][pl.ds(off, NK), ks],
                              kv_refs[g][pl.ds(off, NK), ks], kv_refs[g][pl.ds(off + win, NK), ks],
                              kv_refs[g][pl.ds(off, NK), vs], kv_refs[g][pl.ds(off + win, NK), vs]))
                shifts.append(jnp.where(has_prev, 0, 2 * NK))
                dsts.append((g, hp, pl.ds(off, NK) if dil == 1 else pl.ds(nat, NK, stride=dil)))
        for (g, hp, dst), (out, lse) in zip(dsts, _attn_units(units, shifts)):
            og_ref[g, hp, dst, :] = out
            lg_ref[g, hp, dst, :] = lse
        return carry

    lax.fori_loop(0, n_sub, sub_block, 0)
    for (win, _), kv_ref in zip(DILATED_GROUPS, kv_refs):
        kv_ref[0:win, :] = kv_ref[ATTN_BLOCK:ATTN_BLOCK + win, :]

    def merge(t, carry):
        off = pl.multiple_of(t * NK, NK)
        for hp in range(GROUP_WIDTH_A // LANES):
            lses = [lg_ref[g, hp, pl.ds(off, NK), :] for g in range(N_GROUPS_A)]
            m = jnp.maximum(jnp.maximum(lses[0], lses[1]), lses[2])
            es = [jnp.exp(l - m) for l in lses]
            num = sum(es[g] * og_ref[g, hp, pl.ds(off, NK), :] for g in range(N_GROUPS_A))
            o_ref[pl.ds(off, NK), hp * LANES:(hp + 1) * LANES] = (num / (es[0] + es[1] + es[2])).astype(BF16)
        return carry

    lax.fori_loop(0, n_sub, merge, 0)


def _dilated_attn(qkv):
    _, s, _ = qkv.shape
    n_blk = s // ATTN_BLOCK
    n_hp = GROUP_WIDTH_A // LANES
    in_specs = [pl.BlockSpec((None, ATTN_BLOCK, 3 * GROUP_WIDTH_A), functools.partial(lambda n, g: (g, n, 0), g=g))
                for g in range(N_GROUPS_A)]
    return pl.pallas_call(
        _dilated_attn_kernel,
        grid=(n_blk,),
        in_specs=in_specs,
        out_specs=pl.BlockSpec((ATTN_BLOCK, GROUP_WIDTH_A), lambda n: (n, 0)),
        out_shape=jax.ShapeDtypeStruct((s, GROUP_WIDTH_A), BF16),
        scratch_shapes=[pltpu.VMEM((win + ATTN_BLOCK, 2 * GROUP_WIDTH_A), BF16) for win, _ in DILATED_GROUPS]
        + [pltpu.VMEM((N_GROUPS_A, n_hp, ATTN_BLOCK, LANES), F32),
           pltpu.VMEM((N_GROUPS_A, n_hp, ATTN_BLOCK, LANES), F32)],
        compiler_params=_params("arbitrary"),
        name="dilated_attn",
    )(qkv, qkv, qkv)


def _to_heads(row, n_heads, head_dim):
    return jnp.concatenate([row[:, h * head_dim:(h + 1) * head_dim] for h in range(n_heads)], axis=0)


def _from_heads(t):
    return jnp.concatenate([t[h:h + 1] for h in range(t.shape[0])], axis=1)


def _window_cache_attention(c_ref, j, cols_ref, base, dil):
    win = c_ref.shape[-1]
    lc = min(win, SAMPLE_LANE_CHUNK)
    n_chunks = win // lc
    lane = lax.broadcasted_iota(jnp.int32, (1, win), 1)
    wanted = (lane % dil) == 0
    out = []
    for h in range(HEADS_PER_GROUP_A):
        lo = base + h * HEAD_DIM_A
        q, k_new, v_new = (cols_ref[lo + c * GROUP_WIDTH_A:lo + c * GROUP_WIDTH_A + HEAD_DIM_A, :] for c in range(3))
        q_wide = jnp.concatenate([q] * (lc // LANES), axis=1)
        parts = [jnp.sum(c_ref[j, 0, h, :, i * lc:(i + 1) * lc] * q_wide, axis=0, keepdims=True)
                 for i in range(n_chunks)]
        s = jnp.where(wanted, jnp.concatenate(parts, axis=1), NEG)
        s_n = jnp.sum(k_new[:, 0:1] * q[:, 0:1], axis=0, keepdims=True)
        m = jnp.maximum(jnp.max(s, axis=-1, keepdims=True), s_n)
        p = jnp.exp(s - m)
        p_n = jnp.exp(s_n - m)
        den = jnp.sum(p, axis=-1, keepdims=True) + p_n
        acc = p_n * v_new[:, 0:1]
        for i in range(n_chunks):
            acc = acc + jnp.sum(c_ref[j, 1, h, :, i * lc:(i + 1) * lc] * p[:, i * lc:(i + 1) * lc],
                                axis=-1, keepdims=True)
        out.append((acc / den, m + jnp.log(den)))
    return out


def _memory_cache_attention(cm_ref, j, q, s_ref):
    n_heads, head_dim = q.shape
    chunk = SAMPLE_MEM_CHUNK
    n_chunks = cm_ref.shape[1] // chunk
    m = jnp.full((n_heads, head_dim), NEG, F32)
    for c in range(n_chunks):
        pos = slice(c * chunk, (c + 1) * chunk)
        s = jnp.sum(cm_ref[j, pos, 0] * q[None], axis=-1, keepdims=True) * (head_dim ** -0.5)
        s = jnp.broadcast_to(s, (chunk, n_heads, head_dim))
        s_ref[pos] = s
        m = jnp.maximum(m, jnp.max(s, axis=0))
    den = jnp.zeros((n_heads, head_dim), F32)
    acc = jnp.zeros((n_heads, head_dim), F32)
    for c in range(n_chunks):
        pos = slice(c * chunk, (c + 1) * chunk)
        p = jnp.exp(s_ref[pos] - m[None])
        den = den + jnp.sum(p, axis=0)
        acc = acc + jnp.sum(p * cm_ref[j, pos, 1], axis=0)
    return acc / den


def _sample_attn_kernel(x_ref, wqkv_ref, wqm_ref, c0_ref, c1_ref, c2_ref, cm_ref,
                        oa_ref, om_ref, kvn_ref, proj_ref, projt_ref, cols_ref, oat_ref, ms_ref):
    step = pl.program_id(0)
    caches = (c0_ref, c1_ref, c2_ref)
    n_batch, d = x_ref.shape
    qkv_w = N_GROUPS_A * 3 * GROUP_WIDTH_A

    @pl.when(step == 0)
    def _():
        xb = jnp.concatenate([x_ref[...], jnp.zeros((LANES - n_batch, d), F32)], axis=0).astype(BF16)
        for g in range(N_GROUPS_A):
            h = _dot(xb, wqkv_ref[g])
            proj_ref[:, g * 3 * GROUP_WIDTH_A:(g + 1) * 3 * GROUP_WIDTH_A] = h
            kvn_ref[g] = h[:n_batch, GROUP_WIDTH_A:]
        proj_ref[:, qkv_w:] = _dot(xb, wqm_ref[...])
        for t in range(qkv_w // LANES):
            projt_ref[t * LANES:(t + 1) * LANES, :] = proj_ref[:, t * LANES:(t + 1) * LANES].T.astype(BF16)
        oat_ref[...] = jnp.zeros(oat_ref.shape, F32)

    def one_row(j, carry):
        b = step * SAMPLE_BATCH_BLOCK + j
        pick_row = (lax.broadcasted_iota(jnp.int32, (LANES, LANES), 0) == b).astype(BF16)
        cols_ref[...] = _dot(projt_ref[...], pick_row)
        per_group = [_window_cache_attention(caches[g], j, cols_ref, g * 3 * GROUP_WIDTH_A, dil)
                     for g, (_, dil) in enumerate(DILATED_GROUPS)]
        this_lane = lax.broadcasted_iota(jnp.int32, (1, LANES), 1) == b
        for h in range(HEADS_PER_GROUP_A):
            outs = [per_group[g][h][0] for g in range(N_GROUPS_A)]
            lses = [per_group[g][h][1] for g in range(N_GROUPS_A)]
            m3 = jnp.maximum(jnp.maximum(lses[0], lses[1]), lses[2])
            es = [jnp.exp(l - m3) for l in lses]
            oa = (es[0] * outs[0] + es[1] * outs[1] + es[2] * outs[2]) / (es[0] + es[1] + es[2])
            rows = slice(h * HEAD_DIM_A, (h + 1) * HEAD_DIM_A)
            oat_ref[rows, :] = jnp.where(this_lane, oa, oat_ref[rows, :])

        qm = _to_heads(proj_ref[pl.ds(b, 1), qkv_w:], N_HEADS_M, HEAD_DIM_M)
        om_ref[pl.ds(b, 1), :] = _from_heads(_memory_cache_attention(cm_ref, j, qm, ms_ref))
        return carry

    lax.fori_loop(0, SAMPLE_BATCH_BLOCK, one_row, 0)

    @pl.when(step == pl.num_programs(0) - 1)
    def _():
        oa_ref[...] = oat_ref[...].T[:n_batch, :]


def _sample_attn(xs, w_qkv, w_qm, caches, cache_mem):
    n_batch, d = xs.shape
    assert n_batch <= LANES
    bb = SAMPLE_BATCH_BLOCK
    qkv_w = N_GROUPS_A * 3 * GROUP_WIDTH_A
    cache_specs = [pl.BlockSpec((bb,) + c.shape[1:], lambda i: (i, 0, 0, 0, 0)) for c in caches]
    whole = lambda shape: pl.BlockSpec(shape, lambda i: (0,) * len(shape))
    return pl.pallas_call(
        _sample_attn_kernel,
        grid=(n_batch // bb,),
        in_specs=[whole(xs.shape), _resident(w_qkv.shape), _resident(w_qm.shape)] + cache_specs
        + [pl.BlockSpec((bb,) + cache_mem.shape[1:], lambda i: (i, 0, 0, 0, 0))],
        out_specs=[whole((n_batch, GROUP_WIDTH_A)), whole((n_batch, WIDTH_M)),
                   whole((N_GROUPS_A, n_batch, 2 * GROUP_WIDTH_A))],
        out_shape=[jax.ShapeDtypeStruct((n_batch, GROUP_WIDTH_A), F32),
                   jax.ShapeDtypeStruct((n_batch, WIDTH_M), F32),
                   jax.ShapeDtypeStruct((N_GROUPS_A, n_batch, 2 * GROUP_WIDTH_A), F32)],
        scratch_shapes=[pltpu.VMEM((LANES, qkv_w + WIDTH_M), F32),
                        pltpu.VMEM((qkv_w, LANES), BF16),
                        pltpu.VMEM((qkv_w, LANES), F32),
                        pltpu.VMEM((GROUP_WIDTH_A, LANES), F32),
                        pltpu.VMEM((cache_mem.shape[1], N_HEADS_M, HEAD_DIM_M), F32)],
        compiler_params=_params("arbitrary"),
        name="sample_attn",
    )(xs, w_qkv, w_qm, *caches, cache_mem)


def _mix_kernel(*refs, sample, alpha):
    if sample:
        (x_ref, oa_ref, om_ref, w_ref, bgate_ref, lnvg_ref, lnvb_ref, ws_ref, bs_ref,
         wba_ref, wbb_ref, wbm_ref, wout_ref, ln1g_ref, ln1b_ref, x1_ref, vrows_ref) = refs
    else:
        (x_ref, oa_ref, mk_ref, mv_ref, w_ref, bgate_ref, lnvg_ref, lnvb_ref, ws_ref, bs_ref,
         wba_ref, wbb_ref, wbm_ref, wout_ref, ln1g_ref, ln1b_ref, x1_ref) = refs
    d = x_ref.shape[1]
    rows = x_ref.shape[0]
    x = x_ref[...]
    xb = x.astype(BF16)
    col_u, col_v, col_qm, col_gate = 0, WIDTH_B, 2 * WIDTH_B, 2 * WIDTH_B + WIDTH_M

    h_all = _dot(xb, w_ref[...])

    def gate(k):
        z = h_all[:, col_gate + k * d:col_gate + (k + 1) * d]
        return jax.nn.sigmoid(z + bgate_ref[k:k + 1, :])

    mixed = gate(0) * _dot(oa_ref[...].astype(BF16), wba_ref[...])

    u = jax.nn.gelu(h_all[:, col_u:col_u + WIDTH_B])
    v = _layer_norm(jax.nn.gelu(h_all[:, col_v:col_v + WIDTH_B]), lnvg_ref[...], lnvb_ref[...])
    if sample:
        vrows_ref[...] = v
        spatial = v.astype(BF16).astype(F32) * ws_ref[...].astype(BF16).astype(F32) + bs_ref[...]
    else:
        r_i = lax.broadcasted_iota(jnp.int32, (N_GROUPS_B * CHUNK, CHUNK), 0)
        c_i = lax.broadcasted_iota(jnp.int32, (N_GROUPS_B * CHUNK, CHUNK), 1)
        w_s = jnp.where((r_i % CHUNK) >= c_i, ws_ref[...], 0.0).astype(BF16)
        lane = lax.broadcasted_iota(jnp.int32, (CHUNK, WIDTH_B), 1)
        in_group = [(lane >= gb * GROUP_DIM_B) & (lane < (gb + 1) * GROUP_DIM_B) for gb in range(N_GROUPS_B)]
        vb = v.astype(BF16)
        parts = []
        for ch in range(rows // CHUNK):
            allg = _dot(w_s, vb[ch * CHUNK:(ch + 1) * CHUNK, :])
            sp = bs_ref[...]
            for gb in range(N_GROUPS_B):
                sp = sp + jnp.where(in_group[gb], allg[gb * CHUNK:(gb + 1) * CHUNK, :], 0.0)
            parts.append(sp)
        spatial = jnp.concatenate(parts, axis=0)
    o_b = u * spatial
    mixed = mixed + gate(1) * _dot(o_b.astype(BF16), wbb_ref[...])

    if sample:
        o_m = om_ref[...]
    else:
        qm = h_all[:, col_qm:col_qm + WIDTH_M].astype(BF16)
        heads = []
        for h in range(N_HEADS_M):
            hs = slice(h * HEAD_DIM_M, (h + 1) * HEAD_DIM_M)
            s = _dot_nt(qm[:, hs], mk_ref[:, hs]) * (HEAD_DIM_M ** -0.5)
            p = jnp.exp(s - jnp.max(s, axis=-1, keepdims=True))
            heads.append(_dot(p.astype(BF16), mv_ref[:, hs]) / jnp.sum(p, axis=-1, keepdims=True))
        o_m = jnp.concatenate(heads, axis=-1)
    mixed = mixed + gate(2) * _dot(o_m.astype(BF16), wbm_ref[...])

    x1_ref[...] = _layer_norm(alpha * x + _dot(mixed.astype(BF16), wout_ref[...]), ln1g_ref[...], ln1b_ref[...])


def _mix(x, o_a, mem_or_om, w_rest, b_gate, ln_v_g, ln_v_b, w_s, b_s, w_ba, w_bb, w_bm, w_out, ln1_g, ln1_b,
         *, sample, alpha):
    s, d = x.shape
    rows = s if sample else MIX_ROWS
    tile = lambda width: pl.BlockSpec((rows, width), lambda i: (i, 0))
    consts = (w_rest, b_gate, ln_v_g, ln_v_b, w_s, b_s, w_ba, w_bb, w_bm, w_out, ln1_g, ln1_b)
    const_specs = [_resident(c.shape) for c in consts]
    if sample:
        inputs = (x, o_a, mem_or_om)
        in_specs = [tile(d), tile(GROUP_WIDTH_A), tile(WIDTH_M)]
        out_specs = [tile(d), tile(WIDTH_B)]
        out_shape = [jax.ShapeDtypeStruct((s, d), F32), jax.ShapeDtypeStruct((s, WIDTH_B), F32)]
    else:
        mk, mv = mem_or_om
        inputs = (x, o_a, mk, mv)
        in_specs = [tile(d), tile(GROUP_WIDTH_A), _resident(mk.shape), _resident(mv.shape)]
        out_specs = tile(d)
        out_shape = jax.ShapeDtypeStruct((s, d), F32)
    return pl.pallas_call(
        functools.partial(_mix_kernel, sample=sample, alpha=alpha),
        grid=(s // rows,),
        in_specs=in_specs + const_specs,
        out_specs=out_specs,
        out_shape=out_shape,
        compiler_params=_params("arbitrary"),
        name="mix_sample" if sample else "mix_prompt",
    )(*inputs, *consts)


def _ffn_chunks(d_ff):
    bounds = list(range(0, d_ff, FFN_COLS)) + [d_ff]
    return list(zip(bounds[:-1], bounds[1:]))


def _ffn_kernel(*refs, sample, alpha):
    if sample:
        (x1_ref, st0_ref, st1_ref, wup_ref, cw_ref, cb_ref, wdown_ref, g_ref, b_ref, y_ref, a_ref) = refs
    else:
        (x1_ref, wup_ref, cw_ref, cb_ref, wdown_ref, g_ref, b_ref, y_ref, a_ref, abuf_ref) = refs
    rows = x1_ref.shape[0]
    d_ff = wdown_ref.shape[0]
    if not sample:
        @pl.when(pl.program_id(0) == 0)
        def _():
            abuf_ref[0:SUBLANES, :] = jnp.zeros((SUBLANES, d_ff), F32)

    sub = rows if sample else FFN_SUB_ROWS
    for r0 in range(0, rows, sub):
        x1 = x1_ref[r0:r0 + sub, :]
        xb = x1.astype(BF16)
        acc = jnp.zeros(x1.shape, F32)
        for lo, hi in _ffn_chunks(d_ff):
            cs = slice(lo, hi)
            a = _dot(xb, wup_ref[:, cs])
            val = _dot(xb, wup_ref[:, d_ff + lo:d_ff + hi])
            if sample:
                a_ref[:, cs] = a
                a_m2, a_m1 = st0_ref[:, cs], st1_ref[:, cs]
            else:
                first = SUBLANES + r0
                abuf_ref[first:first + sub, cs] = a
                a_m1 = abuf_ref[first - 1:first - 1 + sub, cs]
                a_m2 = abuf_ref[first - 2:first - 2 + sub, cs]
            conv = cb_ref[:, cs] + cw_ref[0:1, cs] * a_m2 + cw_ref[1:2, cs] * a_m1 + cw_ref[2:3, cs] * a
            h = jax.nn.gelu(conv) * val
            acc = acc + _dot(h.astype(BF16), wdown_ref[cs, :])
        y_ref[r0:r0 + sub, :] = _layer_norm(alpha * x1 + acc, g_ref[...], b_ref[...])
    if not sample:
        tail = abuf_ref[rows:rows + SUBLANES, :]
        abuf_ref[0:SUBLANES, :] = tail
        a_ref[...] = tail


def _ffn(x1, state, w_up, conv_w, conv_b, w_down, ln2_g, ln2_b, *, sample, alpha):
    s, d = x1.shape
    d_ff = w_down.shape[0]
    rows = s if sample else FFN_ROWS
    tile = lambda width: pl.BlockSpec((rows, width), lambda i: (i, 0))
    consts = (w_up, conv_w, conv_b, w_down, ln2_g, ln2_b)
    const_specs = [_resident(c.shape) for c in consts]
    if sample:
        inputs = (x1, state[:, 0], state[:, 1])
        in_specs = [tile(d), tile(d_ff), tile(d_ff)]
        out_specs = [tile(d), tile(d_ff)]
        out_shape = [jax.ShapeDtypeStruct((s, d), F32), jax.ShapeDtypeStruct((s, d_ff), F32)]
        scratch = []
    else:
        inputs = (x1,)
        in_specs = [tile(d)]
        out_specs = [tile(d), pl.BlockSpec((SUBLANES, d_ff), lambda i: (0, 0))]
        out_shape = [jax.ShapeDtypeStruct((s, d), F32), jax.ShapeDtypeStruct((SUBLANES, d_ff), F32)]
        scratch = [pltpu.VMEM((rows + SUBLANES, d_ff), F32)]
    return pl.pallas_call(
        functools.partial(_ffn_kernel, sample=sample, alpha=alpha),
        grid=(s // rows,),
        in_specs=in_specs + const_specs,
        out_specs=out_specs,
        out_shape=out_shape,
        scratch_shapes=scratch,
        compiler_params=_params("arbitrary"),
        name="ffn_sample" if sample else "ffn_prompt",
    )(*inputs, *consts)


def _natural_order(perm_rows, win, dil):
    t = perm_rows.reshape(dil, NK, 2, HEADS_PER_GROUP_A, HEAD_DIM_A)
    return jnp.swapaxes(t, 0, 1).reshape(win, 2, HEADS_PER_GROUP_A, HEAD_DIM_A)


def kernel(x_prompt, x_sample, mem_prompt, cache_win128_kv, cache_win512_kv, cache_win2048_kv, cache_mem_kv, state_ffn_conv, w_in, b_gate, ln_v_g, ln_v_b, w_spatial, b_spatial, w_mem_kv, w_branch_a, w_branch_b, w_branch_m, w_out, ln1_g, ln1_b, w_up, conv_w, conv_b, w_down, ln2_g, ln2_b):
    depth = w_in.shape[0]
    batch, seq, d_model = x_prompt.shape
    dec_batch, dec_seq, _ = x_sample.shape
    assert batch == 1 and dec_seq == 1 and seq % ATTN_BLOCK == 0
    alpha = (2.0 * depth) ** 0.25
    win_caches = (cache_win128_kv, cache_win512_kv, cache_win2048_kv)
    for cache, (win, _) in zip(win_caches, DILATED_GROUPS):
        assert cache.shape[2] == win and PAST_LEN >= win

    yp = x_prompt.reshape(seq, d_model)
    ys = x_sample.reshape(dec_batch, d_model)
    win_p = [[] for _ in range(N_GROUPS_A)]
    win_s = [[] for _ in range(N_GROUPS_A)]
    mem_p, conv_p, gmlp_s, conv_s = [], [], [], []
    for l in range(depth):
        w_l = w_in[l]
        q_scale = HEAD_DIM_A ** -0.5
        w_qkv = jnp.stack([
            jnp.concatenate([w_l[:, g * GROUP_WIDTH_A:(g + 1) * GROUP_WIDTH_A] * q_scale,
                             w_l[:, WIDTH_A + g * GROUP_WIDTH_A:WIDTH_A + (g + 1) * GROUP_WIDTH_A],
                             w_l[:, 2 * WIDTH_A + g * GROUP_WIDTH_A:2 * WIDTH_A + (g + 1) * GROUP_WIDTH_A]], axis=1)
            for g in range(N_GROUPS_A)]).astype(BF16)
        w_rest = w_l[:, 3 * WIDTH_A:].astype(BF16)
        w_qm = w_rest[:, 2 * WIDTH_B:2 * WIDTH_B + WIDTH_M]
        row2 = lambda t: t.reshape(1, -1)
        per_lane = lambda t: jnp.repeat(t, GROUP_DIM_B, axis=-1)
        consts_tail = (w_branch_a[l].astype(BF16), w_branch_b[l].astype(BF16), w_branch_m[l].astype(BF16),
                       w_out[l].astype(BF16), row2(ln1_g[l]), row2(ln1_b[l]))
        ffn_consts = (w_up[l].astype(BF16), conv_w[l], row2(conv_b[l]), w_down[l].astype(BF16),
                      row2(ln2_g[l]), row2(ln2_b[l]))

        mkv = _mem_kv_proj(mem_prompt[0], w_mem_kv[l].astype(BF16))
        mem_p.append(mkv.reshape(1, -1, 2, N_HEADS_M, HEAD_DIM_M))
        qkv = _qkv_proj(yp, w_qkv)
        for g, (win, dil) in enumerate(DILATED_GROUPS):
            keep = min(win, seq)
            win_p[g].append(_natural_order(qkv[g, seq - win:, GROUP_WIDTH_A:].astype(F32), win, dil)[win - keep:][None])
        o_a = _dilated_attn(qkv)
        x1 = _mix(yp, o_a, (mkv[:, :WIDTH_M].astype(BF16), mkv[:, WIDTH_M:].astype(BF16)), w_rest, b_gate[l],
                  row2(ln_v_g[l]), row2(ln_v_b[l]), w_spatial[l].reshape(N_GROUPS_B * CHUNK, CHUNK),
                  per_lane(b_spatial[l].T), *consts_tail, sample=False, alpha=alpha)
        yp_next, a_tail = _ffn(x1, None, *ffn_consts, sample=False, alpha=alpha)
        conv_p.append(a_tail[SUBLANES - (CONV_W - 1):][None])

        caches = [jnp.transpose(c[l], (0, 2, 3, 4, 1)) for c in win_caches]
        cache_mem = cache_mem_kv[l]
        oa_s, om_s, kv_new = _sample_attn(ys, w_qkv, w_qm, caches, cache_mem)
        for g in range(N_GROUPS_A):
            win_s[g].append(kv_new[g].reshape(dec_batch, 1, 2, HEADS_PER_GROUP_A, HEAD_DIM_A))
        x1_s, v_rows = _mix(ys, oa_s, om_s, w_rest, b_gate[l], row2(ln_v_g[l]), row2(ln_v_b[l]),
                            row2(per_lane(w_spatial[l][:, 0, 0])), row2(per_lane(b_spatial[l][:, 0])),
                            *consts_tail, sample=True, alpha=alpha)
        gmlp_s.append(v_rows.reshape(dec_batch, 1, WIDTH_B))
        ys_next, a_s = _ffn(x1_s, state_ffn_conv[l], *ffn_consts, sample=True, alpha=alpha)
        conv_s.append(jnp.stack([state_ffn_conv[l][:, 1], a_s], axis=1))
        yp, ys = yp_next, ys_next

    return (yp.reshape(batch, seq, d_model), ys.reshape(dec_batch, dec_seq, d_model),
            jnp.stack(win_p[0]), jnp.stack(win_p[1]), jnp.stack(win_p[2]),
            jnp.stack(mem_p), jnp.stack(conv_p),
            jnp.stack(win_s[0]), jnp.stack(win_s[1]), jnp.stack(win_s[2]),
            jnp.stack(gmlp_s), jnp.stack(conv_s))
```

```python
import functools

import jax
import jax.numpy as jnp
from jax import lax
from jax.experimental import pallas as pl
from jax.experimental.pallas import tpu as pltpu

BF16 = jnp.bfloat16
F32 = jnp.float32

HEAD_DIM_A = 64
HEADS_PER_GROUP_A = 4
DILATED_GROUPS = ((128, 1), (512, 4), (2048, 16))
N_GROUPS_A = len(DILATED_GROUPS)
GROUP_WIDTH_A = HEADS_PER_GROUP_A * HEAD_DIM_A
WIDTH_A = N_GROUPS_A * GROUP_WIDTH_A
NK = 128
CHUNK = 128
N_GROUPS_B = 4
WIDTH_B = 768
GROUP_DIM_B = WIDTH_B // N_GROUPS_B
N_HEADS_M = 4
HEAD_DIM_M = 128
WIDTH_M = N_HEADS_M * HEAD_DIM_M
N_BRANCH = 3
CONV_W = 3
LN_EPS = 1e-5
NEG = -1e30
PAST_LEN = 16384

LANES = 128
SUBLANES = 8
VMEM_LIMIT_BYTES = 56 * 1024 * 1024

ATTN_BLOCK = 2048
MIX_ROWS = 512
FFN_ROWS = 512
FFN_SUB_ROWS = 512
FFN_COLS = 2816
SAMPLE_BATCH_BLOCK = 2
SAMPLE_LANE_CHUNK = 512
SAMPLE_MEM_CHUNK = 32

_NT = (((1,), (1,)), ((), ()))


def _dot(a, b):
    return jnp.dot(a, b, preferred_element_type=F32)


def _dot_nt(a, b):
    return lax.dot_general(a, b, _NT, preferred_element_type=F32)


def _layer_norm(x, g, b):
    mu = jnp.mean(x, axis=-1, keepdims=True)
    var = jnp.mean(jnp.square(x - mu), axis=-1, keepdims=True)
    return (x - mu) * lax.rsqrt(var + LN_EPS) * g + b


def _params(*semantics):
    return pltpu.CompilerParams(dimension_semantics=semantics, vmem_limit_bytes=VMEM_LIMIT_BYTES)


def _resident(shape):
    zeros = (0,) * len(shape)
    return pl.BlockSpec(shape, lambda *_: zeros, pipeline_mode=pl.Buffered(1))


def _mem_kv_kernel(mem_ref, w_ref, o_ref):
    o_ref[...] = _dot(mem_ref[...].astype(BF16), w_ref[...])


def _mem_kv_proj(mem, w_bf16):
    n, d = mem.shape
    return pl.pallas_call(
        _mem_kv_kernel,
        out_shape=jax.ShapeDtypeStruct((n, w_bf16.shape[1]), F32),
        name="mem_kv_proj",
    )(mem, w_bf16)


def _qkv_proj_kernel(x_ref, wq_ref, wk_ref, wv_ref, o_ref, kvt0_ref, kvt1_ref, kvt2_ref, stage_ref):
    xb = x_ref[...].astype(BF16)
    g_id = pl.program_id(1)
    last_block = pl.program_id(0) == pl.num_programs(0) - 1
    for g, (win, dil) in enumerate(DILATED_GROUPS):
        kvt_ref = (kvt0_ref, kvt1_ref, kvt2_ref)[g]

        @pl.when(g_id == g)
        def _(win=win, dil=dil, kvt_ref=kvt_ref):
            for c, w_ref in enumerate((wq_ref, wk_ref, wv_ref)):
                res = _dot(xb, w_ref[...])
                if c == 0:
                    res = res * (HEAD_DIM_A ** -0.5)
                else:
                    @pl.when(last_block)
                    def _(res=res, c=c):
                        kvt_ref[(c - 1) * GROUP_WIDTH_A:c * GROUP_WIDTH_A, :] = res[ATTN_BLOCK - win:, :].T

                if dil == 1:
                    o_ref[:, c * 256:(c + 1) * 256] = res.astype(BF16)
                    continue
                stage_ref[2 * c] = res[:, :LANES]
                stage_ref[2 * c + 1] = res[:, LANES:]
                for s in range(2):
                    col = c * 256 + s * LANES
                    for blk in range(ATTN_BLOCK // win):
                        for r in range(dil):
                            rows = stage_ref[2 * c + s, pl.ds(blk * win + r, NK, stride=dil), :]
                            dst = blk * win + r * NK
                            o_ref[dst:dst + NK, col:col + LANES] = rows.astype(BF16)


def _qkv_proj(x, w_in):
    s, d = x.shape
    n_blk = s // ATTN_BLOCK
    col_block = lambda part: pl.BlockSpec((d, GROUP_WIDTH_A), lambda n, g: (0, part * N_GROUPS_A + g))
    kvt_shapes = [(2 * GROUP_WIDTH_A, win) for win, _ in DILATED_GROUPS]
    return pl.pallas_call(
        _qkv_proj_kernel,
        grid=(n_blk, N_GROUPS_A),
        in_specs=[pl.BlockSpec((ATTN_BLOCK, d), lambda n, g: (n, 0)), col_block(0), col_block(1), col_block(2)],
        out_specs=[pl.BlockSpec((None, ATTN_BLOCK, 3 * GROUP_WIDTH_A), lambda n, g: (g, n, 0))]
        + [pl.BlockSpec(shape, lambda n, g: (0, 0)) for shape in kvt_shapes],
        out_shape=[jax.ShapeDtypeStruct((N_GROUPS_A, s, 3 * GROUP_WIDTH_A), BF16)]
        + [jax.ShapeDtypeStruct(shape, F32) for shape in kvt_shapes],
        scratch_shapes=[pltpu.VMEM((3 * GROUP_WIDTH_A // LANES, ATTN_BLOCK, LANES), F32)],
        compiler_params=_params("arbitrary", "arbitrary"),
        name="qkv_proj",
    )(x, w_in, w_in, w_in)


def _attn_units(units, prev_shifts):
    first_head = lax.broadcasted_iota(jnp.int32, (NK, LANES), 1) < HEAD_DIM_A
    row = lax.broadcasted_iota(jnp.int32, (2 * NK, NK), 0) % NK
    col = lax.broadcasted_iota(jnp.int32, (2 * NK, NK), 1)
    cur_ok = col <= row
    scores = []
    for (q, k_p, k_c, _, _), shift in zip(units, prev_shifts):
        zero = jnp.zeros_like(q)
        q_st = jnp.concatenate([jnp.where(first_head, q, zero), jnp.where(first_head, zero, q)], axis=0)
        scores.append((jnp.where(cur_ok, _dot_nt(q_st, k_c), NEG),
                       jnp.where(col >= row + shift, _dot_nt(q_st, k_p), NEG)))
    maxes = [jnp.max(jnp.maximum(s_c, s_p), axis=-1, keepdims=True) for s_c, s_p in scores]
    probs = [(jnp.exp(s_c - m), jnp.exp(s_p - m)) for (s_c, s_p), m in zip(scores, maxes)]
    dens = [jnp.sum(p_c + p_p, axis=-1, keepdims=True) for p_c, p_p in probs]
    results = []
    for (_, _, _, v_p, v_c), (p_c, p_p), m, den in zip(units, probs, maxes, dens):
        o_st = (_dot(p_c.astype(BF16), v_c) + _dot(p_p.astype(BF16), v_p)) / den
        lse_st = jnp.broadcast_to(m + jnp.log(den), (2 * NK, LANES))
        results.append((jnp.where(first_head, o_st[:NK], o_st[NK:]), jnp.where(first_head, lse_st[:NK], lse_st[NK:])))
    return results


def _dilated_attn_kernel(qkv0_ref, qkv1_ref, qkv2_ref, o_ref, kv0_ref, kv1_ref, kv2_ref, og_ref, lg_ref):
    n = pl.program_id(0)
    qkv_refs = (qkv0_ref, qkv1_ref, qkv2_ref)
    kv_refs = (kv0_ref, kv1_ref, kv2_ref)
    n_sub = ATTN_BLOCK // NK
    n_hp = GROUP_WIDTH_A // LANES

    for (win, _), qkv_ref, kv_ref in zip(DILATED_GROUPS, qkv_refs, kv_refs):
        @pl.when(n == 0)
        def _(kv_ref=kv_ref, win=win):
            kv_ref[0:win, :] = jnp.zeros((win, 2 * GROUP_WIDTH_A), BF16)

        kv_ref[win:win + ATTN_BLOCK, :] = qkv_ref[:, GROUP_WIDTH_A:]

    def sub_block(t, carry):
        off = pl.multiple_of(t * NK, NK)
        units, shifts, dsts = [], [], []
        for g, (win, dil) in enumerate(DILATED_GROUPS):
            win_blk = t // dil
            res = t - win_blk * dil
            has_prev = (n * (ATTN_BLOCK // win) + win_blk) > 0
            nat = win_blk * win + res
            for hp in range(n_hp):
                ks = slice(hp * LANES, (hp + 1) * LANES)
                vs = slice(GROUP_WIDTH_A + hp * LANES, GROUP_WIDTH_A + (hp + 1) * LANES)
                units.append((qkv_refs[g][pl.ds(off, NK), ks],
                              kv_refs[g][pl.ds(off, NK), ks], kv_refs[g][pl.ds(off + win, NK), ks],
                              kv_refs[g][pl.ds(off, NK), vs], kv_refs[g][pl.ds(off + win, NK), vs]))
                shifts.append(jnp.where(has_prev, 0, 2 * NK))
                dsts.append((g, hp, pl.ds(off, NK) if dil == 1 else pl.ds(nat, NK, stride=dil)))
        for (g, hp, dst), (out, lse) in zip(dsts, _attn_units(units, shifts)):
            og_ref[g, hp, dst, :] = out
            lg_ref[g, hp, dst, :] = lse
        return carry

    lax.fori_loop(0, n_sub, sub_block, 0)
    for (win, _), kv_ref in zip(DILATED_GROUPS, kv_refs):
        kv_ref[0:win, :] = kv_ref[ATTN_BLOCK:ATTN_BLOCK + win, :]

    def merge(t, carry):
        off = pl.multiple_of(t * NK, NK)
        for hp in range(GROUP_WIDTH_A // LANES):
            lses = [lg_ref[g, hp, pl.ds(off, NK), :] for g in range(N_GROUPS_A)]
            m = jnp.maximum(jnp.maximum(lses[0], lses[1]), lses[2])
            es = [jnp.exp(l - m) for l in lses]
            num = sum(es[g] * og_ref[g, hp, pl.ds(off, NK), :] for g in range(N_GROUPS_A))
            o_ref[pl.ds(off, NK), hp * LANES:(hp + 1) * LANES] = (num / (es[0] + es[1] + es[2])).astype(BF16)
        return carry

    lax.fori_loop(0, n_sub, merge, 0)


def _dilated_attn(qkv):
    _, s, _ = qkv.shape
    n_blk = s // ATTN_BLOCK
    n_hp = GROUP_WIDTH_A // LANES
    in_specs = [pl.BlockSpec((None, ATTN_BLOCK, 3 * GROUP_WIDTH_A), functools.partial(lambda n, g: (g, n, 0), g=g))
                for g in range(N_GROUPS_A)]
    return pl.pallas_call(
        _dilated_attn_kernel,
        grid=(n_blk,),
        in_specs=in_specs,
        out_specs=pl.BlockSpec((ATTN_BLOCK, GROUP_WIDTH_A), lambda n: (n, 0)),
        out_shape=jax.ShapeDtypeStruct((s, GROUP_WIDTH_A), BF16),
        scratch_shapes=[pltpu.VMEM((win + ATTN_BLOCK, 2 * GROUP_WIDTH_A), BF16) for win, _ in DILATED_GROUPS]
        + [pltpu.VMEM((N_GROUPS_A, n_hp, ATTN_BLOCK, LANES), F32),
           pltpu.VMEM((N_GROUPS_A, n_hp, ATTN_BLOCK, LANES), F32)],
        compiler_params=_params("arbitrary"),
        name="dilated_attn",
    )(qkv, qkv, qkv)


def _to_heads(row, n_heads, head_dim):
    return jnp.concatenate([row[:, h * head_dim:(h + 1) * head_dim] for h in range(n_heads)], axis=0)


def _from_heads(t):
    return jnp.concatenate([t[h:h + 1] for h in range(t.shape[0])], axis=1)


def _window_cache_attention(c_ref, j, cols_ref, base, dil):
    win = c_ref.shape[-1]
    lc = min(win, SAMPLE_LANE_CHUNK)
    n_chunks = win // lc
    lane = lax.broadcasted_iota(jnp.int32, (1, win), 1)
    wanted = (lane % dil) == 0
    out = []
    for h in range(HEADS_PER_GROUP_A):
        lo = base + h * HEAD_DIM_A
        q, k_new, v_new = (cols_ref[lo + c * GROUP_WIDTH_A:lo + c * GROUP_WIDTH_A + HEAD_DIM_A, :] for c in range(3))
        q_wide = jnp.concatenate([q] * (lc // LANES), axis=1)
        parts = [jnp.sum(c_ref[j, 0, h, :, i * lc:(i + 1) * lc] * q_wide, axis=0, keepdims=True)
                 for i in range(n_chunks)]
        s = jnp.where(wanted, jnp.concatenate(parts, axis=1), NEG)
        s_n = jnp.sum(k_new[:, 0:1] * q[:, 0:1], axis=0, keepdims=True)
        m = jnp.maximum(jnp.max(s, axis=-1, keepdims=True), s_n)
        p = jnp.exp(s - m)
        p_n = jnp.exp(s_n - m)
        den = jnp.sum(p, axis=-1, keepdims=True) + p_n
        acc = p_n * v_new[:, 0:1]
        for i in range(n_chunks):
            acc = acc + jnp.sum(c_ref[j, 1, h, :, i * lc:(i + 1) * lc] * p[:, i * lc:(i + 1) * lc],
                                axis=-1, keepdims=True)
        out.append((acc / den, m + jnp.log(den)))
    return out


def _memory_cache_attention(cm_ref, j, q, s_ref):
    n_heads, head_dim = q.shape
    chunk = SAMPLE_MEM_CHUNK
    n_chunks = cm_ref.shape[1] // chunk
    m = jnp.full((n_heads, head_dim), NEG, F32)
    for c in range(n_chunks):
        pos = slice(c * chunk, (c + 1) * chunk)
        s = jnp.sum(cm_ref[j, pos, 0] * q[None], axis=-1, keepdims=True) * (head_dim ** -0.5)
        s = jnp.broadcast_to(s, (chunk, n_heads, head_dim))
        s_ref[pos] = s
        m = jnp.maximum(m, jnp.max(s, axis=0))
    den = jnp.zeros((n_heads, head_dim), F32)
    acc = jnp.zeros((n_heads, head_dim), F32)
    for c in range(n_chunks):
        pos = slice(c * chunk, (c + 1) * chunk)
        p = jnp.exp(s_ref[pos] - m[None])
        den = den + jnp.sum(p, axis=0)
        acc = acc + jnp.sum(p * cm_ref[j, pos, 1], axis=0)
    return acc / den


def _sample_attn_kernel(x_ref, wqkv_ref, wqm_ref, c0_ref, c1_ref, c2_ref, cm_ref,
                        oa_ref, om_ref, kvn_ref, proj_ref, projt_ref, cols_ref, oat_ref, ms_ref):
    step = pl.program_id(0)
    caches = (c0_ref, c1_ref, c2_ref)
    n_batch, d = x_ref.shape
    qkv_w = N_GROUPS_A * 3 * GROUP_WIDTH_A

    @pl.when(step == 0)
    def _():
        xb = jnp.concatenate([x_ref[...], jnp.zeros((LANES - n_batch, d), F32)], axis=0).astype(BF16)
        for part in range(3):
            h = _dot(xb, wqkv_ref[:, part * WIDTH_A:(part + 1) * WIDTH_A])
            if part == 0:
                h = h * (HEAD_DIM_A ** -0.5)
            for g in range(N_GROUPS_A):
                h_g = h[:, g * GROUP_WIDTH_A:(g + 1) * GROUP_WIDTH_A]
                dst = (3 * g + part) * GROUP_WIDTH_A
                proj_ref[:, dst:dst + GROUP_WIDTH_A] = h_g
                if part > 0:
                    kvn_ref[g, :, (part - 1) * GROUP_WIDTH_A:part * GROUP_WIDTH_A] = h_g[:n_batch]
        proj_ref[:, qkv_w:] = _dot(xb, wqm_ref[:, :WIDTH_M])
        for t in range(qkv_w // LANES):
            projt_ref[t * LANES:(t + 1) * LANES, :] = proj_ref[:, t * LANES:(t + 1) * LANES].T.astype(BF16)
        oat_ref[...] = jnp.zeros(oat_ref.shape, F32)

    def one_row(j, carry):
        b = step * SAMPLE_BATCH_BLOCK + j
        pick_row = (lax.broadcasted_iota(jnp.int32, (LANES, LANES), 0) == b).astype(BF16)
        cols_ref[...] = _dot(projt_ref[...], pick_row)
        per_group = [_window_cache_attention(caches[g], j, cols_ref, g * 3 * GROUP_WIDTH_A, dil)
                     for g, (_, dil) in enumerate(DILATED_GROUPS)]
        this_lane = lax.broadcasted_iota(jnp.int32, (1, LANES), 1) == b
        for h in range(HEADS_PER_GROUP_A):
            outs = [per_group[g][h][0] for g in range(N_GROUPS_A)]
            lses = [per_group[g][h][1] for g in range(N_GROUPS_A)]
            m3 = jnp.maximum(jnp.maximum(lses[0], lses[1]), lses[2])
            es = [jnp.exp(l - m3) for l in lses]
            oa = (es[0] * outs[0] + es[1] * outs[1] + es[2] * outs[2]) / (es[0] + es[1] + es[2])
            rows = slice(h * HEAD_DIM_A, (h + 1) * HEAD_DIM_A)
            oat_ref[rows, :] = jnp.where(this_lane, oa, oat_ref[rows, :])

        qm = _to_heads(proj_ref[pl.ds(b, 1), qkv_w:], N_HEADS_M, HEAD_DIM_M)
        om_ref[pl.ds(b, 1), :] = _from_heads(_memory_cache_attention(cm_ref, j, qm, ms_ref))
        return carry

    lax.fori_loop(0, SAMPLE_BATCH_BLOCK, one_row, 0)

    @pl.when(step == pl.num_programs(0) - 1)
    def _():
        oa_ref[...] = oat_ref[...].T[:n_batch, :]


def _sample_attn(xs, w_in, caches, cache_mem):
    n_batch, d = xs.shape
    assert n_batch <= LANES
    bb = SAMPLE_BATCH_BLOCK
    qkv_w = N_GROUPS_A * 3 * GROUP_WIDTH_A
    qm_col = 3 * WIDTH_A + 2 * WIDTH_B
    assert qm_col % WIDTH_B == 0 and WIDTH_M <= WIDTH_B
    w_specs = [pl.BlockSpec((d, qkv_w), lambda i: (0, 0), pipeline_mode=pl.Buffered(1)),
               pl.BlockSpec((d, WIDTH_B), lambda i: (0, qm_col // WIDTH_B), pipeline_mode=pl.Buffered(1))]
    cache_specs = [pl.BlockSpec((bb,) + c.shape[1:], lambda i: (i, 0, 0, 0, 0)) for c in caches]
    whole = lambda shape: pl.BlockSpec(shape, lambda i: (0,) * len(shape))
    return pl.pallas_call(
        _sample_attn_kernel,
        grid=(n_batch // bb,),
        in_specs=[whole(xs.shape)] + w_specs + cache_specs
        + [pl.BlockSpec((bb,) + cache_mem.shape[1:], lambda i: (i, 0, 0, 0, 0))],
        out_specs=[whole((n_batch, GROUP_WIDTH_A)), whole((n_batch, WIDTH_M)),
                   whole((N_GROUPS_A, n_batch, 2 * GROUP_WIDTH_A))],
        out_shape=[jax.ShapeDtypeStruct((n_batch, GROUP_WIDTH_A), F32),
                   jax.ShapeDtypeStruct((n_batch, WIDTH_M), F32),
                   jax.ShapeDtypeStruct((N_GROUPS_A, n_batch, 2 * GROUP_WIDTH_A), F32)],
        scratch_shapes=[pltpu.VMEM((LANES, qkv_w + WIDTH_M), F32),
                        pltpu.VMEM((qkv_w, LANES), BF16),
                        pltpu.VMEM((qkv_w, LANES), F32),
                        pltpu.VMEM((GROUP_WIDTH_A, LANES), F32),
                        pltpu.VMEM((cache_mem.shape[1], N_HEADS_M, HEAD_DIM_M), F32)],
        compiler_params=_params("arbitrary"),
        name="sample_attn",
    )(xs, w_in, w_in, *caches, cache_mem)


def _mix_kernel(*refs, sample, alpha):
    if sample:
        (x_ref, oa_ref, om_ref, w_ref, bgate_ref, lnvg_ref, lnvb_ref, ws_ref, bs_ref,
         wba_ref, wbb_ref, wbm_ref, wout_ref, ln1g_ref, ln1b_ref, x1_ref, vrows_ref) = refs
    else:
        (x_ref, oa_ref, mk_ref, mv_ref, w_ref, bgate_ref, lnvg_ref, lnvb_ref, ws_ref, bs_ref,
         wba_ref, wbb_ref, wbm_ref, wout_ref, ln1g_ref, ln1b_ref, x1_ref) = refs
    d = x_ref.shape[1]
    rows = x_ref.shape[0]
    x = x_ref[...]
    xb = x.astype(BF16)
    col_u, col_v, col_qm, col_gate = 0, WIDTH_B, 2 * WIDTH_B, 2 * WIDTH_B + WIDTH_M

    h_all = _dot(xb, w_ref[...])

    def gate(k):
        z = h_all[:, col_gate + k * d:col_gate + (k + 1) * d]
        return jax.nn.sigmoid(z + bgate_ref[k:k + 1, :])

    mixed = gate(0) * _dot(oa_ref[...].astype(BF16), wba_ref[...])

    u = jax.nn.gelu(h_all[:, col_u:col_u + WIDTH_B])
    v = _layer_norm(jax.nn.gelu(h_all[:, col_v:col_v + WIDTH_B]), lnvg_ref[...], lnvb_ref[...])
    if sample:
        vrows_ref[...] = v
        spatial = v.astype(BF16).astype(F32) * ws_ref[...].astype(BF16).astype(F32) + bs_ref[...]
    else:
        r_i = lax.broadcasted_iota(jnp.int32, (N_GROUPS_B * CHUNK, CHUNK), 0)
        c_i = lax.broadcasted_iota(jnp.int32, (N_GROUPS_B * CHUNK, CHUNK), 1)
        w_s = jnp.where((r_i % CHUNK) >= c_i, ws_ref[...], 0.0).astype(BF16)
        lane = lax.broadcasted_iota(jnp.int32, (CHUNK, WIDTH_B), 1)
        in_group = [(lane >= gb * GROUP_DIM_B) & (lane < (gb + 1) * GROUP_DIM_B) for gb in range(N_GROUPS_B)]
        vb = v.astype(BF16)
        parts = []
        for ch in range(rows // CHUNK):
            allg = _dot(w_s, vb[ch * CHUNK:(ch + 1) * CHUNK, :])
            sp = bs_ref[...]
            for gb in range(N_GROUPS_B):
                sp = sp + jnp.where(in_group[gb], allg[gb * CHUNK:(gb + 1) * CHUNK, :], 0.0)
            parts.append(sp)
        spatial = jnp.concatenate(parts, axis=0)
    o_b = u * spatial
    mixed = mixed + gate(1) * _dot(o_b.astype(BF16), wbb_ref[...])

    if sample:
        o_m = om_ref[...]
    else:
        qm = h_all[:, col_qm:col_qm + WIDTH_M].astype(BF16)
        heads = []
        for h in range(N_HEADS_M):
            hs = slice(h * HEAD_DIM_M, (h + 1) * HEAD_DIM_M)
            s = _dot_nt(qm[:, hs], mk_ref[:, hs]) * (HEAD_DIM_M ** -0.5)
            p = jnp.exp(s - jnp.max(s, axis=-1, keepdims=True))
            heads.append(_dot(p.astype(BF16), mv_ref[:, hs]) / jnp.sum(p, axis=-1, keepdims=True))
        o_m = jnp.concatenate(heads, axis=-1)
    mixed = mixed + gate(2) * _dot(o_m.astype(BF16), wbm_ref[...])

    x1_ref[...] = _layer_norm(alpha * x + _dot(mixed.astype(BF16), wout_ref[...]), ln1g_ref[...], ln1b_ref[...])


def _mix(x, o_a, mem_or_om, w_rest, b_gate, ln_v_g, ln_v_b, w_s, b_s, w_ba, w_bb, w_bm, w_out, ln1_g, ln1_b,
         *, sample, alpha):
    s, d = x.shape
    rows = s if sample else MIX_ROWS
    tile = lambda width: pl.BlockSpec((rows, width), lambda i: (i, 0))
    consts = (w_rest, b_gate, ln_v_g, ln_v_b, w_s, b_s, w_ba, w_bb, w_bm, w_out, ln1_g, ln1_b)
    const_specs = [_resident(c.shape) for c in consts]
    if sample:
        inputs = (x, o_a, mem_or_om)
        in_specs = [tile(d), tile(GROUP_WIDTH_A), tile(WIDTH_M)]
        out_specs = [tile(d), tile(WIDTH_B)]
        out_shape = [jax.ShapeDtypeStruct((s, d), F32), jax.ShapeDtypeStruct((s, WIDTH_B), F32)]
    else:
        mk, mv = mem_or_om
        inputs = (x, o_a, mk, mv)
        in_specs = [tile(d), tile(GROUP_WIDTH_A), _resident(mk.shape), _resident(mv.shape)]
        out_specs = tile(d)
        out_shape = jax.ShapeDtypeStruct((s, d), F32)
    return pl.pallas_call(
        functools.partial(_mix_kernel, sample=sample, alpha=alpha),
        grid=(s // rows,),
        in_specs=in_specs + const_specs,
        out_specs=out_specs,
        out_shape=out_shape,
        compiler_params=_params("arbitrary"),
        name="mix_sample" if sample else "mix_prompt",
    )(*inputs, *consts)


def _ffn_chunks(d_ff):
    bounds = list(range(0, d_ff, FFN_COLS)) + [d_ff]
    return list(zip(bounds[:-1], bounds[1:]))


def _ffn_kernel(*refs, sample, alpha):
    if sample:
        (x1_ref, st0_ref, st1_ref, wup_ref, cw_ref, cb_ref, wdown_ref, g_ref, b_ref, y_ref, a_ref) = refs
    else:
        (x1_ref, wup_ref, cw_ref, cb_ref, wdown_ref, g_ref, b_ref, y_ref, a_ref, abuf_ref) = refs
    rows = x1_ref.shape[0]
    d_ff = wdown_ref.shape[0]
    if not sample:
        @pl.when(pl.program_id(0) == 0)
        def _():
            abuf_ref[0:SUBLANES, :] = jnp.zeros((SUBLANES, d_ff), F32)

    sub = rows if sample else FFN_SUB_ROWS
    for r0 in range(0, rows, sub):
        x1 = x1_ref[r0:r0 + sub, :]
        xb = x1.astype(BF16)
        acc = jnp.zeros(x1.shape, F32)
        for lo, hi in _ffn_chunks(d_ff):
            cs = slice(lo, hi)
            a = _dot(xb, wup_ref[:, cs])
            val = _dot(xb, wup_ref[:, d_ff + lo:d_ff + hi])
            if sample:
                a_ref[:, cs] = a
                a_m2, a_m1 = st0_ref[:, cs], st1_ref[:, cs]
            else:
                first = SUBLANES + r0
                abuf_ref[first:first + sub, cs] = a
                a_m1 = abuf_ref[first - 1:first - 1 + sub, cs]
                a_m2 = abuf_ref[first - 2:first - 2 + sub, cs]
            conv = cb_ref[:, cs] + cw_ref[0:1, cs] * a_m2 + cw_ref[1:2, cs] * a_m1 + cw_ref[2:3, cs] * a
            h = jax.nn.gelu(conv) * val
            acc = acc + _dot(h.astype(BF16), wdown_ref[cs, :])
        y_ref[r0:r0 + sub, :] = _layer_norm(alpha * x1 + acc, g_ref[...], b_ref[...])
    if not sample:
        tail = abuf_ref[rows:rows + SUBLANES, :]
        abuf_ref[0:SUBLANES, :] = tail
        a_ref[...] = tail


def _ffn(x1, state, w_up, conv_w, conv_b, w_down, ln2_g, ln2_b, *, sample, alpha):
    s, d = x1.shape
    d_ff = w_down.shape[0]
    rows = s if sample else FFN_ROWS
    tile = lambda width: pl.BlockSpec((rows, width), lambda i: (i, 0))
    consts = (w_up, conv_w, conv_b, w_down, ln2_g, ln2_b)
    const_specs = [_resident(c.shape) for c in consts]
    if sample:
        inputs = (x1, state[:, 0], state[:, 1])
        in_specs = [tile(d), tile(d_ff), tile(d_ff)]
        out_specs = [tile(d), tile(d_ff)]
        out_shape = [jax.ShapeDtypeStruct((s, d), F32), jax.ShapeDtypeStruct((s, d_ff), F32)]
        scratch = []
    else:
        inputs = (x1,)
        in_specs = [tile(d)]
        out_specs = [tile(d), pl.BlockSpec((SUBLANES, d_ff), lambda i: (0, 0))]
        out_shape = [jax.ShapeDtypeStruct((s, d), F32), jax.ShapeDtypeStruct((SUBLANES, d_ff), F32)]
        scratch = [pltpu.VMEM((rows + SUBLANES, d_ff), F32)]
    return pl.pallas_call(
        functools.partial(_ffn_kernel, sample=sample, alpha=alpha),
        grid=(s // rows,),
        in_specs=in_specs + const_specs,
        out_specs=out_specs,
        out_shape=out_shape,
        scratch_shapes=scratch,
        compiler_params=_params("arbitrary"),
        name="ffn_sample" if sample else "ffn_prompt",
    )(*inputs, *consts)


def kernel(x_prompt, x_sample, mem_prompt, cache_win128_kv, cache_win512_kv, cache_win2048_kv, cache_mem_kv, state_ffn_conv, w_in, b_gate, ln_v_g, ln_v_b, w_spatial, b_spatial, w_mem_kv, w_branch_a, w_branch_b, w_branch_m, w_out, ln1_g, ln1_b, w_up, conv_w, conv_b, w_down, ln2_g, ln2_b):
    depth = w_in.shape[0]
    batch, seq, d_model = x_prompt.shape
    dec_batch, dec_seq, _ = x_sample.shape
    assert batch == 1 and dec_seq == 1 and seq % ATTN_BLOCK == 0
    alpha = (2.0 * depth) ** 0.25
    win_caches = (cache_win128_kv, cache_win512_kv, cache_win2048_kv)
    for cache, (win, _) in zip(win_caches, DILATED_GROUPS):
        assert cache.shape[2] == win and PAST_LEN >= win

    yp = x_prompt.reshape(seq, d_model)
    ys = x_sample.reshape(dec_batch, d_model)
    win_p = [[] for _ in range(N_GROUPS_A)]
    win_s = [[] for _ in range(N_GROUPS_A)]
    mem_p, conv_p, gmlp_s, conv_s = [], [], [], []
    for l in range(depth):
        w_l = w_in[l].astype(BF16)
        w_rest = w_l[:, 3 * WIDTH_A:]
        row2 = lambda t: t.reshape(1, -1)
        per_lane = lambda t: jnp.repeat(t, GROUP_DIM_B, axis=-1)
        consts_tail = (w_branch_a[l].astype(BF16), w_branch_b[l].astype(BF16), w_branch_m[l].astype(BF16),
                       w_out[l].astype(BF16), row2(ln1_g[l]), row2(ln1_b[l]))
        ffn_consts = (w_up[l].astype(BF16), conv_w[l], row2(conv_b[l]), w_down[l].astype(BF16),
                      row2(ln2_g[l]), row2(ln2_b[l]))

        mkv = _mem_kv_proj(mem_prompt[0], w_mem_kv[l].astype(BF16))
        mem_p.append(mkv.reshape(1, -1, 2, N_HEADS_M, HEAD_DIM_M))
        qkv, *kv_tails = _qkv_proj(yp, w_l)
        for g, (win, _) in enumerate(DILATED_GROUPS):
            tail = kv_tails[g].reshape(2, HEADS_PER_GROUP_A, HEAD_DIM_A, win)
            win_p[g].append(jnp.transpose(tail, (3, 0, 1, 2))[None])
        o_a = _dilated_attn(qkv)
        x1 = _mix(yp, o_a, (mkv[:, :WIDTH_M].astype(BF16), mkv[:, WIDTH_M:].astype(BF16)), w_rest, b_gate[l],
                  row2(ln_v_g[l]), row2(ln_v_b[l]), w_spatial[l].reshape(N_GROUPS_B * CHUNK, CHUNK),
                  per_lane(b_spatial[l].T), *consts_tail, sample=False, alpha=alpha)
        yp_next, a_tail = _ffn(x1, None, *ffn_consts, sample=False, alpha=alpha)
        conv_p.append(a_tail[SUBLANES - (CONV_W - 1):][None])

        caches = [jnp.transpose(c[l], (0, 2, 3, 4, 1)) for c in win_caches]
        cache_mem = cache_mem_kv[l]
        oa_s, om_s, kv_new = _sample_attn(ys, w_l, caches, cache_mem)
        for g in range(N_GROUPS_A):
            win_s[g].append(kv_new[g].reshape(dec_batch, 1, 2, HEADS_PER_GROUP_A, HEAD_DIM_A))
        x1_s, v_rows = _mix(ys, oa_s, om_s, w_rest, b_gate[l], row2(ln_v_g[l]), row2(ln_v_b[l]),
                            row2(per_lane(w_spatial[l][:, 0, 0])), row2(per_lane(b_spatial[l][:, 0])),
                            *consts_tail, sample=True, alpha=alpha)
        gmlp_s.append(v_rows.reshape(dec_batch, 1, WIDTH_B))
        ys_next, a_s = _ffn(x1_s, state_ffn_conv[l], *ffn_consts, sample=True, alpha=alpha)
        conv_s.append(jnp.stack([state_ffn_conv[l][:, 1], a_s], axis=1))
        yp, ys = yp_next, ys_next

    return (yp.reshape(batch, seq, d_model), ys.reshape(dec_batch, dec_seq, d_model),
            jnp.stack(win_p[0]), jnp.stack(win_p[1]), jnp.stack(win_p[2]),
            jnp.stack(mem_p), jnp.stack(conv_p),
            jnp.stack(win_s[0]), jnp.stack(win_s[1]), jnp.stack(win_s[2]),
            jnp.stack(gmlp_s), jnp.stack(conv_s))
```

```python
import functools

import jax
import jax.numpy as jnp
from jax import lax
from jax.experimental import pallas as pl
from jax.experimental.pallas import tpu as pltpu

BF16 = jnp.bfloat16
F32 = jnp.float32

HEAD_DIM_A = 64
HEADS_PER_GROUP_A = 4
DILATED_GROUPS = ((128, 1), (512, 4), (2048, 16))
N_GROUPS_A = len(DILATED_GROUPS)
GROUP_WIDTH_A = HEADS_PER_GROUP_A * HEAD_DIM_A
WIDTH_A = N_GROUPS_A * GROUP_WIDTH_A
NK = 128
CHUNK = 128
N_GROUPS_B = 4
WIDTH_B = 768
GROUP_DIM_B = WIDTH_B // N_GROUPS_B
N_HEADS_M = 4
HEAD_DIM_M = 128
WIDTH_M = N_HEADS_M * HEAD_DIM_M
N_BRANCH = 3
CONV_W = 3
LN_EPS = 1e-5
NEG = -1e30
PAST_LEN = 16384

LANES = 128
SUBLANES = 8
VMEM_LIMIT_BYTES = 56 * 1024 * 1024

ATTN_BLOCK = 2048
MIX_ROWS = 512
FFN_ROWS = 512
FFN_SUB_ROWS = 512
FFN_COLS = 2816
SAMPLE_BATCH_BLOCK = 2
SAMPLE_LANE_CHUNK = 512
SAMPLE_MEM_CHUNK = 32

_NT = (((1,), (1,)), ((), ()))


def _dot(a, b):
    return jnp.dot(a, b, preferred_element_type=F32)


def _dot_nt(a, b):
    return lax.dot_general(a, b, _NT, preferred_element_type=F32)


def _layer_norm(x, g, b):
    mu = jnp.mean(x, axis=-1, keepdims=True)
    var = jnp.mean(jnp.square(x - mu), axis=-1, keepdims=True)
    return (x - mu) * lax.rsqrt(var + LN_EPS) * g + b


def _params(*semantics):
    return pltpu.CompilerParams(dimension_semantics=semantics, vmem_limit_bytes=VMEM_LIMIT_BYTES)


def _resident(shape):
    zeros = (0,) * len(shape)
    return pl.BlockSpec(shape, lambda *_: zeros, pipeline_mode=pl.Buffered(1))


def _mem_kv_kernel(mem_ref, w_ref, o_ref):
    o_ref[...] = _dot(mem_ref[...].astype(BF16), w_ref[...])


def _mem_kv_proj(mem, w):
    n, d = mem.shape
    return pl.pallas_call(
        _mem_kv_kernel,
        out_shape=jax.ShapeDtypeStruct((n, w.shape[1]), F32),
        name="mem_kv_proj",
    )(mem, w)


def _qkv_proj_kernel(x_ref, wq_ref, wk_ref, wv_ref, o_ref, kvt0_ref, kvt1_ref, kvt2_ref, stage_ref):
    xb = x_ref[...].astype(BF16)
    g_id = pl.program_id(1)
    last_block = pl.program_id(0) == pl.num_programs(0) - 1
    for g, (win, dil) in enumerate(DILATED_GROUPS):
        kvt_ref = (kvt0_ref, kvt1_ref, kvt2_ref)[g]

        @pl.when(g_id == g)
        def _(win=win, dil=dil, kvt_ref=kvt_ref):
            for c, w_ref in enumerate((wq_ref, wk_ref, wv_ref)):
                res = _dot(xb, w_ref[...].astype(BF16))
                if c == 0:
                    res = res * (HEAD_DIM_A ** -0.5)
                if dil == 1:
                    o_ref[:, c * 256:(c + 1) * 256] = res.astype(BF16)
                    continue
                stage_ref[2 * c] = res[:, :LANES]
                stage_ref[2 * c + 1] = res[:, LANES:]
                for s in range(2):
                    col = c * 256 + s * LANES
                    for blk in range(ATTN_BLOCK // win):
                        for r in range(dil):
                            rows = stage_ref[2 * c + s, pl.ds(blk * win + r, NK, stride=dil), :]
                            dst = blk * win + r * NK
                            o_ref[dst:dst + NK, col:col + LANES] = rows.astype(BF16)

            @pl.when(last_block)
            def _():
                tail = xb[ATTN_BLOCK - win:, :]
                kvt_ref[0:GROUP_WIDTH_A, :] = _dot(tail, wk_ref[...].astype(BF16)).T
                kvt_ref[GROUP_WIDTH_A:2 * GROUP_WIDTH_A, :] = _dot(tail, wv_ref[...].astype(BF16)).T


def _qkv_proj(x, w_in):
    s, d = x.shape
    n_blk = s // ATTN_BLOCK
    col_block = lambda part: pl.BlockSpec((d, GROUP_WIDTH_A), lambda n, g: (0, part * N_GROUPS_A + g))
    kvt_shapes = [(2 * GROUP_WIDTH_A, win) for win, _ in DILATED_GROUPS]
    return pl.pallas_call(
        _qkv_proj_kernel,
        grid=(n_blk, N_GROUPS_A),
        in_specs=[pl.BlockSpec((ATTN_BLOCK, d), lambda n, g: (n, 0)), col_block(0), col_block(1), col_block(2)],
        out_specs=[pl.BlockSpec((None, ATTN_BLOCK, 3 * GROUP_WIDTH_A), lambda n, g: (g, n, 0))]
        + [pl.BlockSpec(shape, lambda n, g: (0, 0)) for shape in kvt_shapes],
        out_shape=[jax.ShapeDtypeStruct((N_GROUPS_A, s, 3 * GROUP_WIDTH_A), BF16)]
        + [jax.ShapeDtypeStruct(shape, F32) for shape in kvt_shapes],
        scratch_shapes=[pltpu.VMEM((3 * GROUP_WIDTH_A // LANES, ATTN_BLOCK, LANES), F32)],
        compiler_params=_params("arbitrary", "arbitrary"),
        name="qkv_proj",
    )(x, w_in, w_in, w_in)


def _attn_units(units, prev_shifts):
    first_head = lax.broadcasted_iota(jnp.int32, (NK, LANES), 1) < HEAD_DIM_A
    row = lax.broadcasted_iota(jnp.int32, (2 * NK, NK), 0) % NK
    col = lax.broadcasted_iota(jnp.int32, (2 * NK, NK), 1)
    cur_ok = col <= row
    prev_ok = {}
    scores = []
    for (q, k_p, k_c, _, _), shift in zip(units, prev_shifts):
        zero = jnp.zeros_like(q)
        q_st = jnp.concatenate([jnp.where(first_head, q, zero), jnp.where(first_head, zero, q)], axis=0)
        if id(shift) not in prev_ok:
            prev_ok[id(shift)] = col >= row + shift
        scores.append((jnp.where(cur_ok, _dot_nt(q_st, k_c), NEG),
                       jnp.where(prev_ok[id(shift)], _dot_nt(q_st, k_p), NEG)))
    maxes = [jnp.max(jnp.maximum(s_c, s_p), axis=-1, keepdims=True) for s_c, s_p in scores]
    probs = [(jnp.exp(s_c - m), jnp.exp(s_p - m)) for (s_c, s_p), m in zip(scores, maxes)]
    dens = [jnp.sum(p_c + p_p, axis=-1, keepdims=True) for p_c, p_p in probs]
    results = []
    for (_, _, _, v_p, v_c), (p_c, p_p), m, den in zip(units, probs, maxes, dens):
        o_st = _dot(p_c.astype(BF16), v_c) + _dot(p_p.astype(BF16), v_p)
        pick = lambda t: jnp.where(first_head, jnp.broadcast_to(t[:NK], (NK, LANES)), jnp.broadcast_to(t[NK:], (NK, LANES)))
        den_sel = pick(den)
        results.append((pick(o_st) / den_sel, pick(m) + jnp.log(den_sel)))
    return results


def _dilated_attn_kernel(qkv0_ref, qkv1_ref, qkv2_ref, o_ref, kv0_ref, kv1_ref, kv2_ref, og_ref, lg_ref):
    n = pl.program_id(0)
    qkv_refs = (qkv0_ref, qkv1_ref, qkv2_ref)
    kv_refs = (kv0_ref, kv1_ref, kv2_ref)
    n_sub = ATTN_BLOCK // NK
    n_hp = GROUP_WIDTH_A // LANES

    for (win, _), qkv_ref, kv_ref in zip(DILATED_GROUPS, qkv_refs, kv_refs):
        @pl.when(n == 0)
        def _(kv_ref=kv_ref, win=win):
            kv_ref[0:win, :] = jnp.zeros((win, 2 * GROUP_WIDTH_A), BF16)

        kv_ref[win:win + ATTN_BLOCK, :] = qkv_ref[:, GROUP_WIDTH_A:]

    def sub_block(t, carry):
        off = pl.multiple_of(t * NK, NK)
        units, shifts, dsts = [], [], []
        for g, (win, dil) in enumerate(DILATED_GROUPS):
            win_blk = t // dil
            res = t - win_blk * dil
            has_prev = (n * (ATTN_BLOCK // win) + win_blk) > 0
            nat = win_blk * win + res
            shift = jnp.where(has_prev, 0, 2 * NK)
            for hp in range(n_hp):
                ks = slice(hp * LANES, (hp + 1) * LANES)
                vs = slice(GROUP_WIDTH_A + hp * LANES, GROUP_WIDTH_A + (hp + 1) * LANES)
                units.append((qkv_refs[g][pl.ds(off, NK), ks],
                              kv_refs[g][pl.ds(off, NK), ks], kv_refs[g][pl.ds(off + win, NK), ks],
                              kv_refs[g][pl.ds(off, NK), vs], kv_refs[g][pl.ds(off + win, NK), vs]))
                shifts.append(shift)
                dsts.append((g, hp, pl.ds(off, NK) if dil == 1 else pl.ds(nat, NK, stride=dil)))
        for (g, hp, dst), (out, lse) in zip(dsts, _attn_units(units, shifts)):
            og_ref[g, hp, dst, :] = out
            lg_ref[g, hp, dst, :] = lse
        return carry

    lax.fori_loop(0, n_sub, sub_block, 0)
    for (win, _), kv_ref in zip(DILATED_GROUPS, kv_refs):
        kv_ref[0:win, :] = kv_ref[ATTN_BLOCK:ATTN_BLOCK + win, :]

    def merge(t, carry):
        off = pl.multiple_of(t * NK, NK)
        for hp in range(GROUP_WIDTH_A // LANES):
            lses = [lg_ref[g, hp, pl.ds(off, NK), :] for g in range(N_GROUPS_A)]
            m = jnp.maximum(jnp.maximum(lses[0], lses[1]), lses[2])
            es = [jnp.exp(l - m) for l in lses]
            num = sum(es[g] * og_ref[g, hp, pl.ds(off, NK), :] for g in range(N_GROUPS_A))
            o_ref[pl.ds(off, NK), hp * LANES:(hp + 1) * LANES] = (num / (es[0] + es[1] + es[2])).astype(BF16)
        return carry

    lax.fori_loop(0, n_sub, merge, 0)


def _dilated_attn(qkv):
    _, s, _ = qkv.shape
    n_blk = s // ATTN_BLOCK
    n_hp = GROUP_WIDTH_A // LANES
    in_specs = [pl.BlockSpec((None, ATTN_BLOCK, 3 * GROUP_WIDTH_A), functools.partial(lambda n, g: (g, n, 0), g=g))
                for g in range(N_GROUPS_A)]
    return pl.pallas_call(
        _dilated_attn_kernel,
        grid=(n_blk,),
        in_specs=in_specs,
        out_specs=pl.BlockSpec((ATTN_BLOCK, GROUP_WIDTH_A), lambda n: (n, 0)),
        out_shape=jax.ShapeDtypeStruct((s, GROUP_WIDTH_A), BF16),
        scratch_shapes=[pltpu.VMEM((win + ATTN_BLOCK, 2 * GROUP_WIDTH_A), BF16) for win, _ in DILATED_GROUPS]
        + [pltpu.VMEM((N_GROUPS_A, n_hp, ATTN_BLOCK, LANES), F32),
           pltpu.VMEM((N_GROUPS_A, n_hp, ATTN_BLOCK, LANES), F32)],
        compiler_params=_params("arbitrary"),
        name="dilated_attn",
    )(qkv, qkv, qkv)


def _to_heads(row, n_heads, head_dim):
    return jnp.concatenate([row[:, h * head_dim:(h + 1) * head_dim] for h in range(n_heads)], axis=0)


def _from_heads(t):
    return jnp.concatenate([t[h:h + 1] for h in range(t.shape[0])], axis=1)


def _window_cache_attention(c_ref, j, cols_ref, base, dil):
    win = c_ref.shape[-1]
    lc = min(win, SAMPLE_LANE_CHUNK)
    n_chunks = win // lc
    lane = lax.broadcasted_iota(jnp.int32, (1, win), 1)
    wanted = (lane % dil) == 0
    out = []
    for h in range(HEADS_PER_GROUP_A):
        lo = base + h * HEAD_DIM_A
        q, k_new, v_new = (cols_ref[lo + c * GROUP_WIDTH_A:lo + c * GROUP_WIDTH_A + HEAD_DIM_A, :] for c in range(3))
        q_wide = jnp.concatenate([q] * (lc // LANES), axis=1)
        parts = [jnp.sum(c_ref[j, 0, h, :, i * lc:(i + 1) * lc] * q_wide, axis=0, keepdims=True)
                 for i in range(n_chunks)]
        s = jnp.where(wanted, jnp.concatenate(parts, axis=1), NEG)
        s_n = jnp.sum(k_new[:, 0:1] * q[:, 0:1], axis=0, keepdims=True)
        m = jnp.maximum(jnp.max(s, axis=-1, keepdims=True), s_n)
        p = jnp.exp(s - m)
        p_n = jnp.exp(s_n - m)
        den = jnp.sum(p, axis=-1, keepdims=True) + p_n
        acc = p_n * v_new[:, 0:1]
        for i in range(n_chunks):
            acc = acc + jnp.sum(c_ref[j, 1, h, :, i * lc:(i + 1) * lc] * p[:, i * lc:(i + 1) * lc],
                                axis=-1, keepdims=True)
        out.append((acc / den, m + jnp.log(den)))
    return out


def _memory_cache_attention(cm_ref, j, q, s_ref):
    n_heads, head_dim = q.shape
    chunk = SAMPLE_MEM_CHUNK
    n_chunks = cm_ref.shape[1] // chunk
    m = jnp.full((n_heads, head_dim), NEG, F32)
    for c in range(n_chunks):
        pos = slice(c * chunk, (c + 1) * chunk)
        s = jnp.sum(cm_ref[j, pos, 0] * q[None], axis=-1, keepdims=True) * (head_dim ** -0.5)
        s = jnp.broadcast_to(s, (chunk, n_heads, head_dim))
        s_ref[pos] = s
        m = jnp.maximum(m, jnp.max(s, axis=0))
    den = jnp.zeros((n_heads, head_dim), F32)
    acc = jnp.zeros((n_heads, head_dim), F32)
    for c in range(n_chunks):
        pos = slice(c * chunk, (c + 1) * chunk)
        p = jnp.exp(s_ref[pos] - m[None])
        den = den + jnp.sum(p, axis=0)
        acc = acc + jnp.sum(p * cm_ref[j, pos, 1], axis=0)
    return acc / den


def _sample_attn_kernel(x_ref, wqkv_ref, wqm_ref, c0_ref, c1_ref, c2_ref, cm_ref,
                        oa_ref, om_ref, kvn_ref, proj_ref, projt_ref, cols_ref, oat_ref, ms_ref):
    step = pl.program_id(0)
    caches = (c0_ref, c1_ref, c2_ref)
    n_batch, d = x_ref.shape
    qkv_w = N_GROUPS_A * 3 * GROUP_WIDTH_A

    @pl.when(step == 0)
    def _():
        xb = jnp.concatenate([x_ref[...], jnp.zeros((LANES - n_batch, d), F32)], axis=0).astype(BF16)
        for part in range(3):
            h = _dot(xb, wqkv_ref[:, part * WIDTH_A:(part + 1) * WIDTH_A].astype(BF16))
            if part == 0:
                h = h * (HEAD_DIM_A ** -0.5)
            for g in range(N_GROUPS_A):
                h_g = h[:, g * GROUP_WIDTH_A:(g + 1) * GROUP_WIDTH_A]
                dst = (3 * g + part) * GROUP_WIDTH_A
                proj_ref[:, dst:dst + GROUP_WIDTH_A] = h_g
                if part > 0:
                    kvn_ref[g, :, (part - 1) * GROUP_WIDTH_A:part * GROUP_WIDTH_A] = h_g[:n_batch]
        proj_ref[:, qkv_w:] = _dot(xb, wqm_ref[:, :WIDTH_M].astype(BF16))
        for t in range(qkv_w // LANES):
            projt_ref[t * LANES:(t + 1) * LANES, :] = proj_ref[:, t * LANES:(t + 1) * LANES].T.astype(BF16)
        oat_ref[...] = jnp.zeros(oat_ref.shape, F32)

    def one_row(j, carry):
        b = step * SAMPLE_BATCH_BLOCK + j
        pick_row = (lax.broadcasted_iota(jnp.int32, (LANES, LANES), 0) == b).astype(BF16)
        cols_ref[...] = _dot(projt_ref[...], pick_row)
        per_group = [_window_cache_attention(caches[g], j, cols_ref, g * 3 * GROUP_WIDTH_A, dil)
                     for g, (_, dil) in enumerate(DILATED_GROUPS)]
        this_lane = lax.broadcasted_iota(jnp.int32, (1, LANES), 1) == b
        for h in range(HEADS_PER_GROUP_A):
            outs = [per_group[g][h][0] for g in range(N_GROUPS_A)]
            lses = [per_group[g][h][1] for g in range(N_GROUPS_A)]
            m3 = jnp.maximum(jnp.maximum(lses[0], lses[1]), lses[2])
            es = [jnp.exp(l - m3) for l in lses]
            oa = (es[0] * outs[0] + es[1] * outs[1] + es[2] * outs[2]) / (es[0] + es[1] + es[2])
            rows = slice(h * HEAD_DIM_A, (h + 1) * HEAD_DIM_A)
            oat_ref[rows, :] = jnp.where(this_lane, oa, oat_ref[rows, :])

        qm = _to_heads(proj_ref[pl.ds(b, 1), qkv_w:], N_HEADS_M, HEAD_DIM_M)
        om_ref[pl.ds(b, 1), :] = _from_heads(_memory_cache_attention(cm_ref, j, qm, ms_ref))
        return carry

    lax.fori_loop(0, SAMPLE_BATCH_BLOCK, one_row, 0)

    @pl.when(step == pl.num_programs(0) - 1)
    def _():
        oa_ref[...] = oat_ref[...].T[:n_batch, :]


def _sample_attn(xs, w_in, caches, cache_mem):
    n_batch, d = xs.shape
    assert n_batch <= LANES
    bb = SAMPLE_BATCH_BLOCK
    qkv_w = N_GROUPS_A * 3 * GROUP_WIDTH_A
    qm_col = 3 * WIDTH_A + 2 * WIDTH_B
    assert qm_col % WIDTH_B == 0 and WIDTH_M <= WIDTH_B
    w_specs = [pl.BlockSpec((d, qkv_w), lambda i: (0, 0), pipeline_mode=pl.Buffered(1)),
               pl.BlockSpec((d, WIDTH_B), lambda i: (0, qm_col // WIDTH_B), pipeline_mode=pl.Buffered(1))]
    cache_specs = [pl.BlockSpec((bb,) + c.shape[1:], lambda i: (i, 0, 0, 0, 0)) for c in caches]
    whole = lambda shape: pl.BlockSpec(shape, lambda i: (0,) * len(shape))
    return pl.pallas_call(
        _sample_attn_kernel,
        grid=(n_batch // bb,),
        in_specs=[whole(xs.shape)] + w_specs + cache_specs
        + [pl.BlockSpec((bb,) + cache_mem.shape[1:], lambda i: (i, 0, 0, 0, 0))],
        out_specs=[whole((n_batch, GROUP_WIDTH_A)), whole((n_batch, WIDTH_M)),
                   whole((N_GROUPS_A, n_batch, 2 * GROUP_WIDTH_A))],
        out_shape=[jax.ShapeDtypeStruct((n_batch, GROUP_WIDTH_A), F32),
                   jax.ShapeDtypeStruct((n_batch, WIDTH_M), F32),
                   jax.ShapeDtypeStruct((N_GROUPS_A, n_batch, 2 * GROUP_WIDTH_A), F32)],
        scratch_shapes=[pltpu.VMEM((LANES, qkv_w + WIDTH_M), F32),
                        pltpu.VMEM((qkv_w, LANES), BF16),
                        pltpu.VMEM((qkv_w, LANES), F32),
                        pltpu.VMEM((GROUP_WIDTH_A, LANES), F32),
                        pltpu.VMEM((cache_mem.shape[1], N_HEADS_M, HEAD_DIM_M), F32)],
        compiler_params=_params("arbitrary"),
        name="sample_attn",
    )(xs, w_in, w_in, *caches, cache_mem)


def _mix_kernel(*refs, sample, alpha):
    if sample:
        (x_ref, oa_ref, om_ref, w_ref, bgate_ref, lnvg_ref, lnvb_ref, ws_ref, bs_ref,
         wba_ref, wbb_ref, wbm_ref, wout_ref, ln1g_ref, ln1b_ref, x1_ref, vrows_ref) = refs
    else:
        (x_ref, oa_ref, mk_ref, mv_ref, w_ref, bgate_ref, lnvg_ref, lnvb_ref, ws_ref, bs_ref,
         wba_ref, wbb_ref, wbm_ref, wout_ref, ln1g_ref, ln1b_ref, x1_ref) = refs
    d = x_ref.shape[1]
    rows = x_ref.shape[0]
    x = x_ref[...]
    xb = x.astype(BF16)
    col_u, col_v, col_qm, col_gate = 0, WIDTH_B, 2 * WIDTH_B, 2 * WIDTH_B + WIDTH_M

    h_all = _dot(xb, w_ref[...])

    def gate(k):
        z = h_all[:, col_gate + k * d:col_gate + (k + 1) * d]
        return jax.nn.sigmoid(z + bgate_ref[k:k + 1, :])

    mixed = gate(0) * _dot(oa_ref[...].astype(BF16), wba_ref[...])

    u = jax.nn.gelu(h_all[:, col_u:col_u + WIDTH_B])
    v = _layer_norm(jax.nn.gelu(h_all[:, col_v:col_v + WIDTH_B]), lnvg_ref[...], lnvb_ref[...])
    if sample:
        vrows_ref[...] = v
        spatial = v.astype(BF16).astype(F32) * ws_ref[...].astype(BF16).astype(F32) + bs_ref[...]
    else:
        r_i = lax.broadcasted_iota(jnp.int32, (N_GROUPS_B * CHUNK, CHUNK), 0)
        c_i = lax.broadcasted_iota(jnp.int32, (N_GROUPS_B * CHUNK, CHUNK), 1)
        w_s = jnp.where((r_i % CHUNK) >= c_i, ws_ref[...], 0.0).astype(BF16)
        lane = lax.broadcasted_iota(jnp.int32, (CHUNK, WIDTH_B), 1)
        in_group = [(lane >= gb * GROUP_DIM_B) & (lane < (gb + 1) * GROUP_DIM_B) for gb in range(N_GROUPS_B)]
        vb = v.astype(BF16)
        parts = []
        for ch in range(rows // CHUNK):
            allg = _dot(w_s, vb[ch * CHUNK:(ch + 1) * CHUNK, :])
            sp = bs_ref[...]
            for gb in range(N_GROUPS_B):
                sp = sp + jnp.where(in_group[gb], allg[gb * CHUNK:(gb + 1) * CHUNK, :], 0.0)
            parts.append(sp)
        spatial = jnp.concatenate(parts, axis=0)
    o_b = u * spatial
    mixed = mixed + gate(1) * _dot(o_b.astype(BF16), wbb_ref[...])

    if sample:
        o_m = om_ref[...]
    else:
        qm = h_all[:, col_qm:col_qm + WIDTH_M].astype(BF16)
        heads = []
        for h in range(N_HEADS_M):
            hs = slice(h * HEAD_DIM_M, (h + 1) * HEAD_DIM_M)
            s = _dot_nt(qm[:, hs], mk_ref[:, hs]) * (HEAD_DIM_M ** -0.5)
            p = jnp.exp(s - jnp.max(s, axis=-1, keepdims=True))
            heads.append(_dot(p.astype(BF16), mv_ref[:, hs]) / jnp.sum(p, axis=-1, keepdims=True))
        o_m = jnp.concatenate(heads, axis=-1)
    mixed = mixed + gate(2) * _dot(o_m.astype(BF16), wbm_ref[...])

    x1_ref[...] = _layer_norm(alpha * x + _dot(mixed.astype(BF16), wout_ref[...]), ln1g_ref[...], ln1b_ref[...])


def _mix(x, o_a, mem_or_om, w_rest, b_gate, ln_v_g, ln_v_b, w_s, b_s, w_ba, w_bb, w_bm, w_out, ln1_g, ln1_b,
         *, sample, alpha):
    s, d = x.shape
    rows = s if sample else MIX_ROWS
    tile = lambda width: pl.BlockSpec((rows, width), lambda i: (i, 0))
    consts = (w_rest, b_gate, ln_v_g, ln_v_b, w_s, b_s, w_ba, w_bb, w_bm, w_out, ln1_g, ln1_b)
    const_specs = [_resident(c.shape) for c in consts]
    if sample:
        inputs = (x, o_a, mem_or_om)
        in_specs = [tile(d), tile(GROUP_WIDTH_A), tile(WIDTH_M)]
        out_specs = [tile(d), tile(WIDTH_B)]
        out_shape = [jax.ShapeDtypeStruct((s, d), F32), jax.ShapeDtypeStruct((s, WIDTH_B), F32)]
    else:
        mk, mv = mem_or_om
        inputs = (x, o_a, mk, mv)
        in_specs = [tile(d), tile(GROUP_WIDTH_A), _resident(mk.shape), _resident(mv.shape)]
        out_specs = tile(d)
        out_shape = jax.ShapeDtypeStruct((s, d), F32)
    return pl.pallas_call(
        functools.partial(_mix_kernel, sample=sample, alpha=alpha),
        grid=(s // rows,),
        in_specs=in_specs + const_specs,
        out_specs=out_specs,
        out_shape=out_shape,
        compiler_params=_params("arbitrary"),
        name="mix_sample" if sample else "mix_prompt",
    )(*inputs, *consts)


def _ffn_chunks(d_ff):
    bounds = list(range(0, d_ff, FFN_COLS)) + [d_ff]
    return list(zip(bounds[:-1], bounds[1:]))


def _ffn_kernel(*refs, sample, alpha):
    if sample:
        (x1_ref, st0_ref, st1_ref, wup_ref, cw_ref, cb_ref, wdown_ref, g_ref, b_ref, y_ref, a_ref) = refs
    else:
        (x1_ref, wup_ref, cw_ref, cb_ref, wdown_ref, g_ref, b_ref, y_ref, a_ref, abuf_ref) = refs
    rows = x1_ref.shape[0]
    d_ff = wdown_ref.shape[0]
    if not sample:
        @pl.when(pl.program_id(0) == 0)
        def _():
            abuf_ref[0:SUBLANES, :] = jnp.zeros((SUBLANES, d_ff), F32)

    sub = rows if sample else FFN_SUB_ROWS
    for r0 in range(0, rows, sub):
        x1 = x1_ref[r0:r0 + sub, :]
        xb = x1.astype(BF16)
        acc = jnp.zeros(x1.shape, F32)
        for lo, hi in _ffn_chunks(d_ff):
            cs = slice(lo, hi)
            a = _dot(xb, wup_ref[:, cs])
            val = _dot(xb, wup_ref[:, d_ff + lo:d_ff + hi])
            if sample:
                a_ref[:, cs] = a
                a_m2, a_m1 = st0_ref[:, cs], st1_ref[:, cs]
            else:
                first = SUBLANES + r0
                abuf_ref[first:first + sub, cs] = a
                a_m1 = abuf_ref[first - 1:first - 1 + sub, cs]
                a_m2 = abuf_ref[first - 2:first - 2 + sub, cs]
            conv = cb_ref[:, cs] + cw_ref[0:1, cs] * a_m2 + cw_ref[1:2, cs] * a_m1 + cw_ref[2:3, cs] * a
            h = jax.nn.gelu(conv) * val
            acc = acc + _dot(h.astype(BF16), wdown_ref[cs, :])
        y_ref[r0:r0 + sub, :] = _layer_norm(alpha * x1 + acc, g_ref[...], b_ref[...])
    if not sample:
        tail = abuf_ref[rows:rows + SUBLANES, :]
        abuf_ref[0:SUBLANES, :] = tail
        a_ref[...] = tail


def _ffn(x1, state, w_up, conv_w, conv_b, w_down, ln2_g, ln2_b, *, sample, alpha):
    s, d = x1.shape
    d_ff = w_down.shape[0]
    rows = s if sample else FFN_ROWS
    tile = lambda width: pl.BlockSpec((rows, width), lambda i: (i, 0))
    consts = (w_up, conv_w, conv_b, w_down, ln2_g, ln2_b)
    const_specs = [_resident(c.shape) for c in consts]
    if sample:
        inputs = (x1, state[:, 0], state[:, 1])
        in_specs = [tile(d), tile(d_ff), tile(d_ff)]
        out_specs = [tile(d), tile(d_ff)]
        out_shape = [jax.ShapeDtypeStruct((s, d), F32), jax.ShapeDtypeStruct((s, d_ff), F32)]
        scratch = []
    else:
        inputs = (x1,)
        in_specs = [tile(d)]
        out_specs = [tile(d), pl.BlockSpec((SUBLANES, d_ff), lambda i: (0, 0))]
        out_shape = [jax.ShapeDtypeStruct((s, d), F32), jax.ShapeDtypeStruct((SUBLANES, d_ff), F32)]
        scratch = [pltpu.VMEM((rows + SUBLANES, d_ff), F32)]
    return pl.pallas_call(
        functools.partial(_ffn_kernel, sample=sample, alpha=alpha),
        grid=(s // rows,),
        in_specs=in_specs + const_specs,
        out_specs=out_specs,
        out_shape=out_shape,
        scratch_shapes=scratch,
        compiler_params=_params("arbitrary"),
        name="ffn_sample" if sample else "ffn_prompt",
    )(*inputs, *consts)


def kernel(x_prompt, x_sample, mem_prompt, cache_win128_kv, cache_win512_kv, cache_win2048_kv, cache_mem_kv, state_ffn_conv, w_in, b_gate, ln_v_g, ln_v_b, w_spatial, b_spatial, w_mem_kv, w_branch_a, w_branch_b, w_branch_m, w_out, ln1_g, ln1_b, w_up, conv_w, conv_b, w_down, ln2_g, ln2_b):
    depth = w_in.shape[0]
    batch, seq, d_model = x_prompt.shape
    dec_batch, dec_seq, _ = x_sample.shape
    assert batch == 1 and dec_seq == 1 and seq % ATTN_BLOCK == 0
    alpha = (2.0 * depth) ** 0.25
    win_caches = (cache_win128_kv, cache_win512_kv, cache_win2048_kv)
    for cache, (win, _) in zip(win_caches, DILATED_GROUPS):
        assert cache.shape[2] == win and PAST_LEN >= win

    yp = x_prompt.reshape(seq, d_model)
    ys = x_sample.reshape(dec_batch, d_model)
    win_p = [[] for _ in range(N_GROUPS_A)]
    win_s = [[] for _ in range(N_GROUPS_A)]
    mem_p, conv_p, gmlp_s, conv_s = [], [], [], []
    for l in range(depth):
        w_l = w_in[l]
        w_rest = w_l[:, 3 * WIDTH_A:].astype(BF16)
        row2 = lambda t: t.reshape(1, -1)
        per_lane = lambda t: jnp.repeat(t, GROUP_DIM_B, axis=-1)
        consts_tail = (w_branch_a[l].astype(BF16), w_branch_b[l].astype(BF16), w_branch_m[l].astype(BF16),
                       w_out[l].astype(BF16), row2(ln1_g[l]), row2(ln1_b[l]))
        ffn_consts = (w_up[l].astype(BF16), conv_w[l], row2(conv_b[l]), w_down[l].astype(BF16),
                      row2(ln2_g[l]), row2(ln2_b[l]))

        mkv = _mem_kv_proj(mem_prompt[0], w_mem_kv[l].astype(BF16))
        mem_p.append(mkv.reshape(1, -1, 2, N_HEADS_M, HEAD_DIM_M))
        qkv, *kv_tails = _qkv_proj(yp, w_l)
        for g, (win, _) in enumerate(DILATED_GROUPS):
            tail = kv_tails[g].reshape(2, HEADS_PER_GROUP_A, HEAD_DIM_A, win)
            win_p[g].append(jnp.transpose(tail, (3, 0, 1, 2))[None])
        o_a = _dilated_attn(qkv)
        x1 = _mix(yp, o_a, (mkv[:, :WIDTH_M].astype(BF16), mkv[:, WIDTH_M:].astype(BF16)), w_rest, b_gate[l],
                  row2(ln_v_g[l]), row2(ln_v_b[l]), w_spatial[l].reshape(N_GROUPS_B * CHUNK, CHUNK),
                  per_lane(b_spatial[l].T), *consts_tail, sample=False, alpha=alpha)
        yp_next, a_tail = _ffn(x1, None, *ffn_consts, sample=False, alpha=alpha)
        conv_p.append(a_tail[SUBLANES - (CONV_W - 1):][None])

        caches = [jnp.transpose(c[l], (0, 2, 3, 4, 1)) for c in win_caches]
        cache_mem = cache_mem_kv[l]
        oa_s, om_s, kv_new = _sample_attn(ys, w_l, caches, cache_mem)
        for g in range(N_GROUPS_A):
            win_s[g].append(kv_new[g].reshape(dec_batch, 1, 2, HEADS_PER_GROUP_A, HEAD_DIM_A))
        x1_s, v_rows = _mix(ys, oa_s, om_s, w_rest, b_gate[l], row2(ln_v_g[l]), row2(ln_v_b[l]),
                            row2(per_lane(w_spatial[l][:, 0, 0])), row2(per_lane(b_spatial[l][:, 0])),
                            *consts_tail, sample=True, alpha=alpha)
        gmlp_s.append(v_rows.reshape(dec_batch, 1, WIDTH_B))
        ys_next, a_s = _ffn(x1_s, state_ffn_conv[l], *ffn_consts, sample=True, alpha=alpha)
        conv_s.append(jnp.stack([state_ffn_conv[l][:, 1], a_s], axis=1))
        yp, ys = yp_next, ys_next

    return (yp.reshape(batch, seq, d_model), ys.reshape(dec_batch, dec_seq, d_model),
            jnp.stack(win_p[0]), jnp.stack(win_p[1]), jnp.stack(win_p[2]),
            jnp.stack(mem_p), jnp.stack(conv_p),
            jnp.stack(win_s[0]), jnp.stack(win_s[1]), jnp.stack(win_s[2]),
            jnp.stack(gmlp_s), jnp.stack(conv_s))
```

```python
import functools

import jax
import jax.numpy as jnp
from jax import lax
from jax.experimental import pallas as pl
from jax.experimental.pallas import tpu as pltpu

BF16 = jnp.bfloat16
F32 = jnp.float32

HEAD_DIM_A = 64
HEADS_PER_GROUP_A = 4
DILATED_GROUPS = ((128, 1), (512, 4), (2048, 16))
N_GROUPS_A = len(DILATED_GROUPS)
GROUP_WIDTH_A = HEADS_PER_GROUP_A * HEAD_DIM_A
WIDTH_A = N_GROUPS_A * GROUP_WIDTH_A
NK = 128
CHUNK = 128
N_GROUPS_B = 4
WIDTH_B = 768
GROUP_DIM_B = WIDTH_B // N_GROUPS_B
N_HEADS_M = 4
HEAD_DIM_M = 128
WIDTH_M = N_HEADS_M * HEAD_DIM_M
N_BRANCH = 3
CONV_W = 3
LN_EPS = 1e-5
NEG = -1e30
PAST_LEN = 16384

LANES = 128
SUBLANES = 8
VMEM_LIMIT_BYTES = 56 * 1024 * 1024

ATTN_BLOCK = 2048
DEINTERLEAVE_STRIDE = 4
MIX_ROWS = 512
FFN_ROWS = 512
FFN_SUB_ROWS = 512
FFN_COLS = 2816
SAMPLE_BATCH_BLOCK = 2
SAMPLE_LANE_CHUNK = 512
SAMPLE_MEM_CHUNK = 32

_NT = (((1,), (1,)), ((), ()))


def _dot(a, b):
    return jnp.dot(a, b, preferred_element_type=F32)


def _dot_nt(a, b):
    return lax.dot_general(a, b, _NT, preferred_element_type=F32)


def _layer_norm(x, g, b):
    mu = jnp.mean(x, axis=-1, keepdims=True)
    var = jnp.mean(jnp.square(x - mu), axis=-1, keepdims=True)
    return (x - mu) * lax.rsqrt(var + LN_EPS) * g + b


def _params(*semantics):
    return pltpu.CompilerParams(dimension_semantics=semantics, vmem_limit_bytes=VMEM_LIMIT_BYTES)


def _resident(shape):
    zeros = (0,) * len(shape)
    return pl.BlockSpec(shape, lambda *_: zeros, pipeline_mode=pl.Buffered(1))


def _mem_kv_kernel(mem_ref, w_ref, o_ref):
    o_ref[...] = _dot(mem_ref[...].astype(BF16), w_ref[...])


def _mem_kv_proj(mem, w):
    n, d = mem.shape
    return pl.pallas_call(
        _mem_kv_kernel,
        out_shape=jax.ShapeDtypeStruct((n, w.shape[1]), F32),
        name="mem_kv_proj",
    )(mem, w)


def _qkv_proj_kernel(x_ref, wq_ref, wk_ref, wv_ref, o_ref, kvt0_ref, kvt1_ref, kvt2_ref, stage_ref, tmp_ref):
    xb = x_ref[...].astype(BF16)
    g_id = pl.program_id(1)
    last_block = pl.program_id(0) == pl.num_programs(0) - 1
    for g, (win, dil) in enumerate(DILATED_GROUPS):
        kvt_ref = (kvt0_ref, kvt1_ref, kvt2_ref)[g]

        @pl.when(g_id == g)
        def _(win=win, dil=dil, kvt_ref=kvt_ref):
            for c, w_ref in enumerate((wq_ref, wk_ref, wv_ref)):
                w = w_ref[...]
                if c == 0:
                    w = w * (HEAD_DIM_A ** -0.5)
                res = _dot(xb, w.astype(BF16))
                if dil == 1:
                    o_ref[:, c * 256:(c + 1) * 256] = res.astype(BF16)
                    continue
                stage_ref[0] = res[:, :LANES]
                stage_ref[1] = res[:, LANES:]
                inner = DEINTERLEAVE_STRIDE if dil > DEINTERLEAVE_STRIDE else dil
                outer = dil // inner
                for s in range(2):
                    col = c * 256 + s * LANES
                    for blk in range(ATTN_BLOCK // win):
                        base = blk * win
                        src_ref = stage_ref
                        if outer > 1:
                            part = win // inner
                            for lo in range(inner):
                                tmp_ref[s, base + lo * part:base + (lo + 1) * part, :] = (
                                    stage_ref[s, pl.ds(base + lo, part, stride=inner), :])
                            src_ref = tmp_ref
                        for lo in range(inner):
                            for hi in range(outer):
                                first = base + lo * (win // inner) + hi if outer > 1 else base + lo
                                rows = src_ref[s, pl.ds(first, NK, stride=outer if outer > 1 else inner), :]
                                dst = base + (hi * inner + lo) * NK
                                o_ref[dst:dst + NK, col:col + LANES] = rows.astype(BF16)

            @pl.when(last_block)
            def _():
                tail = xb[ATTN_BLOCK - win:, :]
                kvt_ref[0:GROUP_WIDTH_A, :] = _dot(tail, wk_ref[...].astype(BF16)).T
                kvt_ref[GROUP_WIDTH_A:2 * GROUP_WIDTH_A, :] = _dot(tail, wv_ref[...].astype(BF16)).T


def _qkv_proj(x, w_in):
    s, d = x.shape
    n_blk = s // ATTN_BLOCK
    col_block = lambda part: pl.BlockSpec((d, GROUP_WIDTH_A), lambda n, g: (0, part * N_GROUPS_A + g))
    kvt_shapes = [(2 * GROUP_WIDTH_A, win) for win, _ in DILATED_GROUPS]
    return pl.pallas_call(
        _qkv_proj_kernel,
        grid=(n_blk, N_GROUPS_A),
        in_specs=[pl.BlockSpec((ATTN_BLOCK, d), lambda n, g: (n, 0)), col_block(0), col_block(1), col_block(2)],
        out_specs=[pl.BlockSpec((None, ATTN_BLOCK, 3 * GROUP_WIDTH_A), lambda n, g: (g, n, 0))]
        + [pl.BlockSpec(shape, lambda n, g: (0, 0)) for shape in kvt_shapes],
        out_shape=[jax.ShapeDtypeStruct((N_GROUPS_A, s, 3 * GROUP_WIDTH_A), BF16)]
        + [jax.ShapeDtypeStruct(shape, F32) for shape in kvt_shapes],
        scratch_shapes=[pltpu.VMEM((GROUP_WIDTH_A // LANES, ATTN_BLOCK, LANES), F32),
                        pltpu.VMEM((GROUP_WIDTH_A // LANES, ATTN_BLOCK, LANES), F32)],
        compiler_params=_params("arbitrary", "arbitrary"),
        name="qkv_proj",
    )(x, w_in, w_in, w_in)


def _attn_units(units, prev_shifts):
    first_head = lax.broadcasted_iota(jnp.int32, (NK, LANES), 1) < HEAD_DIM_A
    row = lax.broadcasted_iota(jnp.int32, (2 * NK, NK), 0) % NK
    col = lax.broadcasted_iota(jnp.int32, (2 * NK, NK), 1)
    cur_ok = col <= row
    prev_ok = {}
    scores = []
    for (q, k_p, k_c, _, _), shift in zip(units, prev_shifts):
        zero = jnp.zeros_like(q)
        q_st = jnp.concatenate([jnp.where(first_head, q, zero), jnp.where(first_head, zero, q)], axis=0)
        if id(shift) not in prev_ok:
            prev_ok[id(shift)] = col >= row + shift
        scores.append((jnp.where(cur_ok, _dot_nt(q_st, k_c), NEG),
                       jnp.where(prev_ok[id(shift)], _dot_nt(q_st, k_p), NEG)))
    maxes = [jnp.max(jnp.maximum(s_c, s_p), axis=-1, keepdims=True) for s_c, s_p in scores]
    probs = [(jnp.exp(s_c - m), jnp.exp(s_p - m)) for (s_c, s_p), m in zip(scores, maxes)]
    dens = [jnp.sum(p_c + p_p, axis=-1, keepdims=True) for p_c, p_p in probs]
    results = []
    for (_, _, _, v_p, v_c), (p_c, p_p), m, den in zip(units, probs, maxes, dens):
        o_st = _dot(p_c.astype(BF16), v_c) + _dot(p_p.astype(BF16), v_p)
        pick = lambda t: jnp.where(first_head, jnp.broadcast_to(t[:NK], (NK, LANES)), jnp.broadcast_to(t[NK:], (NK, LANES)))
        den_sel = pick(den)
        results.append((pick(o_st) / den_sel, pick(m) + jnp.log(den_sel)))
    return results


def _dilated_attn_kernel(qkv0_ref, qkv1_ref, qkv2_ref, o_ref, kv0_ref, kv1_ref, kv2_ref, og_ref, lg_ref):
    n = pl.program_id(0)
    qkv_refs = (qkv0_ref, qkv1_ref, qkv2_ref)
    kv_refs = (kv0_ref, kv1_ref, kv2_ref)
    n_sub = ATTN_BLOCK // NK
    n_hp = GROUP_WIDTH_A // LANES

    for (win, _), qkv_ref, kv_ref in zip(DILATED_GROUPS, qkv_refs, kv_refs):
        @pl.when(n == 0)
        def _(kv_ref=kv_ref, win=win):
            kv_ref[0:win, :] = jnp.zeros((win, 2 * GROUP_WIDTH_A), BF16)

        kv_ref[win:win + ATTN_BLOCK, :] = qkv_ref[:, GROUP_WIDTH_A:]

    def sub_block(t, carry):
        off = pl.multiple_of(t * NK, NK)
        units, shifts, dsts = [], [], []
        for g, (win, dil) in enumerate(DILATED_GROUPS):
            win_blk = t // dil
            res = t - win_blk * dil
            has_prev = (n * (ATTN_BLOCK // win) + win_blk) > 0
            nat = win_blk * win + res
            shift = jnp.where(has_prev, 0, 2 * NK)
            for hp in range(n_hp):
                ks = slice(hp * LANES, (hp + 1) * LANES)
                vs = slice(GROUP_WIDTH_A + hp * LANES, GROUP_WIDTH_A + (hp + 1) * LANES)
                units.append((qkv_refs[g][pl.ds(off, NK), ks],
                              kv_refs[g][pl.ds(off, NK), ks], kv_refs[g][pl.ds(off + win, NK), ks],
                              kv_refs[g][pl.ds(off, NK), vs], kv_refs[g][pl.ds(off + win, NK), vs]))
                shifts.append(shift)
                dsts.append((g, hp, pl.ds(off, NK) if dil == 1 else pl.ds(nat, NK, stride=dil)))
        for (g, hp, dst), (out, lse) in zip(dsts, _attn_units(units, shifts)):
            og_ref[g, hp, dst, :] = out
            lg_ref[g, hp, dst, :] = lse
        return carry

    lax.fori_loop(0, n_sub, sub_block, 0)
    for (win, _), kv_ref in zip(DILATED_GROUPS, kv_refs):
        kv_ref[0:win, :] = kv_ref[ATTN_BLOCK:ATTN_BLOCK + win, :]

    def merge(t, carry):
        off = pl.multiple_of(t * NK, NK)
        for hp in range(GROUP_WIDTH_A // LANES):
            lses = [lg_ref[g, hp, pl.ds(off, NK), :] for g in range(N_GROUPS_A)]
            m = jnp.maximum(jnp.maximum(lses[0], lses[1]), lses[2])
            es = [jnp.exp(l - m) for l in lses]
            num = sum(es[g] * og_ref[g, hp, pl.ds(off, NK), :] for g in range(N_GROUPS_A))
            o_ref[pl.ds(off, NK), hp * LANES:(hp + 1) * LANES] = (num / (es[0] + es[1] + es[2])).astype(BF16)
        return carry

    lax.fori_loop(0, n_sub, merge, 0)


def _dilated_attn(qkv):
    _, s, _ = qkv.shape
    n_blk = s // ATTN_BLOCK
    n_hp = GROUP_WIDTH_A // LANES
    in_specs = [pl.BlockSpec((None, ATTN_BLOCK, 3 * GROUP_WIDTH_A), functools.partial(lambda n, g: (g, n, 0), g=g))
                for g in range(N_GROUPS_A)]
    return pl.pallas_call(
        _dilated_attn_kernel,
        grid=(n_blk,),
        in_specs=in_specs,
        out_specs=pl.BlockSpec((ATTN_BLOCK, GROUP_WIDTH_A), lambda n: (n, 0)),
        out_shape=jax.ShapeDtypeStruct((s, GROUP_WIDTH_A), BF16),
        scratch_shapes=[pltpu.VMEM((win + ATTN_BLOCK, 2 * GROUP_WIDTH_A), BF16) for win, _ in DILATED_GROUPS]
        + [pltpu.VMEM((N_GROUPS_A, n_hp, ATTN_BLOCK, LANES), F32),
           pltpu.VMEM((N_GROUPS_A, n_hp, ATTN_BLOCK, LANES), F32)],
        compiler_params=_params("arbitrary"),
        name="dilated_attn",
    )(qkv, qkv, qkv)


def _to_heads(row, n_heads, head_dim):
    return jnp.concatenate([row[:, h * head_dim:(h + 1) * head_dim] for h in range(n_heads)], axis=0)


def _from_heads(t):
    return jnp.concatenate([t[h:h + 1] for h in range(t.shape[0])], axis=1)


def _window_cache_attention(c_ref, j, cols_ref, base, dil):
    win = c_ref.shape[-1]
    lc = min(win, SAMPLE_LANE_CHUNK)
    n_chunks = win // lc
    lane = lax.broadcasted_iota(jnp.int32, (1, win), 1)
    wanted = (lane % dil) == 0
    out = []
    for h in range(HEADS_PER_GROUP_A):
        lo = base + h * HEAD_DIM_A
        q, k_new, v_new = (cols_ref[lo + c * GROUP_WIDTH_A:lo + c * GROUP_WIDTH_A + HEAD_DIM_A, :] for c in range(3))
        q_wide = jnp.concatenate([q] * (lc // LANES), axis=1)
        parts = [jnp.sum(c_ref[j, 0, h, :, i * lc:(i + 1) * lc] * q_wide, axis=0, keepdims=True)
                 for i in range(n_chunks)]
        s = jnp.where(wanted, jnp.concatenate(parts, axis=1), NEG)
        s_n = jnp.sum(k_new[:, 0:1] * q[:, 0:1], axis=0, keepdims=True)
        m = jnp.maximum(jnp.max(s, axis=-1, keepdims=True), s_n)
        p = jnp.exp(s - m)
        p_n = jnp.exp(s_n - m)
        den = jnp.sum(p, axis=-1, keepdims=True) + p_n
        acc = p_n * v_new[:, 0:1]
        for i in range(n_chunks):
            acc = acc + jnp.sum(c_ref[j, 1, h, :, i * lc:(i + 1) * lc] * p[:, i * lc:(i + 1) * lc],
                                axis=-1, keepdims=True)
        out.append((acc / den, m + jnp.log(den)))
    return out


def _memory_cache_attention(cm_ref, j, q, s_ref):
    n_heads, head_dim = q.shape
    chunk = SAMPLE_MEM_CHUNK
    n_chunks = cm_ref.shape[1] // chunk
    m = jnp.full((n_heads, head_dim), NEG, F32)
    for c in range(n_chunks):
        pos = slice(c * chunk, (c + 1) * chunk)
        s = jnp.sum(cm_ref[j, pos, 0] * q[None], axis=-1, keepdims=True) * (head_dim ** -0.5)
        s = jnp.broadcast_to(s, (chunk, n_heads, head_dim))
        s_ref[pos] = s
        m = jnp.maximum(m, jnp.max(s, axis=0))
    den = jnp.zeros((n_heads, head_dim), F32)
    acc = jnp.zeros((n_heads, head_dim), F32)
    for c in range(n_chunks):
        pos = slice(c * chunk, (c + 1) * chunk)
        p = jnp.exp(s_ref[pos] - m[None])
        den = den + jnp.sum(p, axis=0)
        acc = acc + jnp.sum(p * cm_ref[j, pos, 1], axis=0)
    return acc / den


def _sample_attn_kernel(x_ref, wqkv_ref, wqm_ref, c0_ref, c1_ref, c2_ref, cm_ref,
                        oa_ref, om_ref, kvn_ref, proj_ref, projt_ref, cols_ref, oat_ref, ms_ref):
    step = pl.program_id(0)
    caches = (c0_ref, c1_ref, c2_ref)
    n_batch, d = x_ref.shape
    qkv_w = N_GROUPS_A * 3 * GROUP_WIDTH_A

    @pl.when(step == 0)
    def _():
        xb = jnp.concatenate([x_ref[...], jnp.zeros((LANES - n_batch, d), F32)], axis=0).astype(BF16)
        for part in range(3):
            h = _dot(xb, wqkv_ref[:, part * WIDTH_A:(part + 1) * WIDTH_A].astype(BF16))
            if part == 0:
                h = h * (HEAD_DIM_A ** -0.5)
            for g in range(N_GROUPS_A):
                h_g = h[:, g * GROUP_WIDTH_A:(g + 1) * GROUP_WIDTH_A]
                dst = (3 * g + part) * GROUP_WIDTH_A
                proj_ref[:, dst:dst + GROUP_WIDTH_A] = h_g
                if part > 0:
                    kvn_ref[g, :, (part - 1) * GROUP_WIDTH_A:part * GROUP_WIDTH_A] = h_g[:n_batch]
        proj_ref[:, qkv_w:] = _dot(xb, wqm_ref[:, :WIDTH_M].astype(BF16))
        for t in range(qkv_w // LANES):
            projt_ref[t * LANES:(t + 1) * LANES, :] = proj_ref[:, t * LANES:(t + 1) * LANES].T.astype(BF16)
        oat_ref[...] = jnp.zeros(oat_ref.shape, F32)

    def one_row(j, carry):
        b = step * SAMPLE_BATCH_BLOCK + j
        pick_row = (lax.broadcasted_iota(jnp.int32, (LANES, LANES), 0) == b).astype(BF16)
        cols_ref[...] = _dot(projt_ref[...], pick_row)
        per_group = [_window_cache_attention(caches[g], j, cols_ref, g * 3 * GROUP_WIDTH_A, dil)
                     for g, (_, dil) in enumerate(DILATED_GROUPS)]
        this_lane = lax.broadcasted_iota(jnp.int32, (1, LANES), 1) == b
        for h in range(HEADS_PER_GROUP_A):
            outs = [per_group[g][h][0] for g in range(N_GROUPS_A)]
            lses = [per_group[g][h][1] for g in range(N_GROUPS_A)]
            m3 = jnp.maximum(jnp.maximum(lses[0], lses[1]), lses[2])
            es = [jnp.exp(l - m3) for l in lses]
            oa = (es[0] * outs[0] + es[1] * outs[1] + es[2] * outs[2]) / (es[0] + es[1] + es[2])
            rows = slice(h * HEAD_DIM_A, (h + 1) * HEAD_DIM_A)
            oat_ref[rows, :] = jnp.where(this_lane, oa, oat_ref[rows, :])

        qm = _to_heads(proj_ref[pl.ds(b, 1), qkv_w:], N_HEADS_M, HEAD_DIM_M)
        om_ref[pl.ds(b, 1), :] = _from_heads(_memory_cache_attention(cm_ref, j, qm, ms_ref))
        return carry

    lax.fori_loop(0, SAMPLE_BATCH_BLOCK, one_row, 0)

    @pl.when(step == pl.num_programs(0) - 1)
    def _():
        oa_ref[...] = oat_ref[...].T[:n_batch, :]


def _sample_attn(xs, w_in, caches, cache_mem):
    n_batch, d = xs.shape
    assert n_batch <= LANES
    bb = SAMPLE_BATCH_BLOCK
    qkv_w = N_GROUPS_A * 3 * GROUP_WIDTH_A
    qm_col = 3 * WIDTH_A + 2 * WIDTH_B
    assert qm_col % WIDTH_B == 0 and WIDTH_M <= WIDTH_B
    w_specs = [pl.BlockSpec((d, qkv_w), lambda i: (0, 0), pipeline_mode=pl.Buffered(1)),
               pl.BlockSpec((d, WIDTH_B), lambda i: (0, qm_col // WIDTH_B), pipeline_mode=pl.Buffered(1))]
    cache_specs = [pl.BlockSpec((bb,) + c.shape[1:], lambda i: (i, 0, 0, 0, 0)) for c in caches]
    whole = lambda shape: pl.BlockSpec(shape, lambda i: (0,) * len(shape))
    return pl.pallas_call(
        _sample_attn_kernel,
        grid=(n_batch // bb,),
        in_specs=[whole(xs.shape)] + w_specs + cache_specs
        + [pl.BlockSpec((bb,) + cache_mem.shape[1:], lambda i: (i, 0, 0, 0, 0))],
        out_specs=[whole((n_batch, GROUP_WIDTH_A)), whole((n_batch, WIDTH_M)),
                   whole((N_GROUPS_A, n_batch, 2 * GROUP_WIDTH_A))],
        out_shape=[jax.ShapeDtypeStruct((n_batch, GROUP_WIDTH_A), F32),
                   jax.ShapeDtypeStruct((n_batch, WIDTH_M), F32),
                   jax.ShapeDtypeStruct((N_GROUPS_A, n_batch, 2 * GROUP_WIDTH_A), F32)],
        scratch_shapes=[pltpu.VMEM((LANES, qkv_w + WIDTH_M), F32),
                        pltpu.VMEM((qkv_w, LANES), BF16),
                        pltpu.VMEM((qkv_w, LANES), F32),
                        pltpu.VMEM((GROUP_WIDTH_A, LANES), F32),
                        pltpu.VMEM((cache_mem.shape[1], N_HEADS_M, HEAD_DIM_M), F32)],
        compiler_params=_params("arbitrary"),
        name="sample_attn",
    )(xs, w_in, w_in, *caches, cache_mem)


def _mix_kernel(x_ref, oa_ref, mk_ref, mv_ref, xs_ref, oas_ref, oms_ref, w_ref, bgate_ref, lnvg_ref, lnvb_ref,
                ws_ref, bs_ref, wss_ref, bss_ref, wba_ref, wbb_ref, wbm_ref, wout_ref, ln1g_ref, ln1b_ref,
                x1_ref, x1s_ref, vrows_ref, *, alpha):
    shared = (w_ref, bgate_ref, lnvg_ref, lnvb_ref)
    tail = (wba_ref, wbb_ref, wbm_ref, wout_ref, ln1g_ref, ln1b_ref)
    is_sample_step = pl.program_id(0) == pl.num_programs(0) - 1

    @pl.when(jnp.logical_not(is_sample_step))
    def _():
        _mix_body(x_ref, oa_ref, (mk_ref, mv_ref), *shared, ws_ref, bs_ref, *tail, x1_ref, None,
                  sample=False, alpha=alpha)

    @pl.when(is_sample_step)
    def _():
        _mix_body(xs_ref, oas_ref, oms_ref, *shared, wss_ref, bss_ref, *tail, x1s_ref, vrows_ref,
                  sample=True, alpha=alpha)


def _mix_body(x_ref, oa_ref, mem_or_om, w_ref, bgate_ref, lnvg_ref, lnvb_ref, ws_ref, bs_ref,
              wba_ref, wbb_ref, wbm_ref, wout_ref, ln1g_ref, ln1b_ref, x1_ref, vrows_ref, *, sample, alpha):
    if sample:
        om_ref = mem_or_om
    else:
        mk_ref, mv_ref = mem_or_om
    d = x_ref.shape[1]
    rows = x_ref.shape[0]
    x = x_ref[...]
    xb = x.astype(BF16)
    col_u, col_v, col_qm, col_gate = 0, WIDTH_B, 2 * WIDTH_B, 2 * WIDTH_B + WIDTH_M

    h_all = _dot(xb, w_ref[...])

    def gate(k):
        z = h_all[:, col_gate + k * d:col_gate + (k + 1) * d]
        return jax.nn.sigmoid(z + bgate_ref[k:k + 1, :])

    mixed = gate(0) * _dot(oa_ref[...].astype(BF16), wba_ref[...])

    u = jax.nn.gelu(h_all[:, col_u:col_u + WIDTH_B])
    v = _layer_norm(jax.nn.gelu(h_all[:, col_v:col_v + WIDTH_B]), lnvg_ref[...], lnvb_ref[...])
    if sample:
        vrows_ref[...] = v
        spatial = v.astype(BF16).astype(F32) * ws_ref[...].astype(BF16).astype(F32) + bs_ref[...]
    else:
        r_i = lax.broadcasted_iota(jnp.int32, (N_GROUPS_B * CHUNK, CHUNK), 0)
        c_i = lax.broadcasted_iota(jnp.int32, (N_GROUPS_B * CHUNK, CHUNK), 1)
        w_s = jnp.where((r_i % CHUNK) >= c_i, ws_ref[...], 0.0).astype(BF16)
        lane = lax.broadcasted_iota(jnp.int32, (CHUNK, WIDTH_B), 1)
        in_group = [(lane >= gb * GROUP_DIM_B) & (lane < (gb + 1) * GROUP_DIM_B) for gb in range(N_GROUPS_B)]
        vb = v.astype(BF16)
        parts = []
        for ch in range(rows // CHUNK):
            allg = _dot(w_s, vb[ch * CHUNK:(ch + 1) * CHUNK, :])
            sp = bs_ref[...]
            for gb in range(N_GROUPS_B):
                sp = sp + jnp.where(in_group[gb], allg[gb * CHUNK:(gb + 1) * CHUNK, :], 0.0)
            parts.append(sp)
        spatial = jnp.concatenate(parts, axis=0)
    o_b = u * spatial
    mixed = mixed + gate(1) * _dot(o_b.astype(BF16), wbb_ref[...])

    if sample:
        o_m = om_ref[...]
    else:
        qm = h_all[:, col_qm:col_qm + WIDTH_M].astype(BF16)
        heads = []
        for h in range(N_HEADS_M):
            hs = slice(h * HEAD_DIM_M, (h + 1) * HEAD_DIM_M)
            s = _dot_nt(qm[:, hs], mk_ref[:, hs]) * (HEAD_DIM_M ** -0.5)
            p = jnp.exp(s - jnp.max(s, axis=-1, keepdims=True))
            heads.append(_dot(p.astype(BF16), mv_ref[:, hs]) / jnp.sum(p, axis=-1, keepdims=True))
        o_m = jnp.concatenate(heads, axis=-1)
    mixed = mixed + gate(2) * _dot(o_m.astype(BF16), wbm_ref[...])

    x1_ref[...] = _layer_norm(alpha * x + _dot(mixed.astype(BF16), wout_ref[...]), ln1g_ref[...], ln1b_ref[...])


def _mix(x, o_a, mk, mv, xs, oa_s, om_s, w_rest, b_gate, ln_v_g, ln_v_b, w_s, b_s, w_s_row, b_s_row,
         w_ba, w_bb, w_bm, w_out, ln1_g, ln1_b, *, alpha):
    s, d = x.shape
    n_tiles = s // MIX_ROWS
    tile = lambda width: pl.BlockSpec((MIX_ROWS, width), lambda i: (jnp.minimum(i, n_tiles - 1), 0))
    whole = lambda shape: pl.BlockSpec(shape, lambda i: (0,) * len(shape))
    operands = (x, o_a, mk, mv, xs, oa_s, om_s, w_rest, b_gate, ln_v_g, ln_v_b, w_s, b_s, w_s_row, b_s_row,
                w_ba, w_bb, w_bm, w_out, ln1_g, ln1_b)
    n_s = xs.shape[0]
    return pl.pallas_call(
        functools.partial(_mix_kernel, alpha=alpha),
        grid=(n_tiles + 1,),
        in_specs=[tile(d), tile(GROUP_WIDTH_A)] + [_resident(t.shape) for t in operands[2:]],
        out_specs=[tile(d), whole((n_s, d)), whole((n_s, WIDTH_B))],
        out_shape=[jax.ShapeDtypeStruct((s, d), F32), jax.ShapeDtypeStruct((n_s, d), F32),
                   jax.ShapeDtypeStruct((n_s, WIDTH_B), F32)],
        compiler_params=_params("arbitrary"),
        name="mix",
    )(*operands)


def _ffn_chunks(d_ff):
    bounds = list(range(0, d_ff, FFN_COLS)) + [d_ff]
    return list(zip(bounds[:-1], bounds[1:]))


def _ffn_kernel(x1_ref, x1s_ref, st0_ref, st1_ref, wup_ref, cw_ref, cb_ref, wdown_ref, g_ref, b_ref,
                y_ref, a_ref, ys_ref, as_ref, abuf_ref, *, alpha):
    consts = (wup_ref, cw_ref, cb_ref, wdown_ref, g_ref, b_ref)
    is_sample_step = pl.program_id(0) == pl.num_programs(0) - 1

    @pl.when(jnp.logical_not(is_sample_step))
    def _():
        _ffn_body(x1_ref, None, None, *consts, y_ref, a_ref, abuf_ref, sample=False, alpha=alpha)

    @pl.when(is_sample_step)
    def _():
        _ffn_body(x1s_ref, st0_ref, st1_ref, *consts, ys_ref, as_ref, None, sample=True, alpha=alpha)


def _ffn_body(x1_ref, st0_ref, st1_ref, wup_ref, cw_ref, cb_ref, wdown_ref, g_ref, b_ref, y_ref, a_ref, abuf_ref,
              *, sample, alpha):
    rows = x1_ref.shape[0]
    d_ff = wdown_ref.shape[0]
    if not sample:
        @pl.when(pl.program_id(0) == 0)
        def _():
            abuf_ref[0:SUBLANES, :] = jnp.zeros((SUBLANES, d_ff), F32)

    sub = rows if sample else FFN_SUB_ROWS
    for r0 in range(0, rows, sub):
        x1 = x1_ref[r0:r0 + sub, :]
        xb = x1.astype(BF16)
        acc = jnp.zeros(x1.shape, F32)
        for lo, hi in _ffn_chunks(d_ff):
            cs = slice(lo, hi)
            a = _dot(xb, wup_ref[:, cs])
            val = _dot(xb, wup_ref[:, d_ff + lo:d_ff + hi])
            if sample:
                a_ref[:, cs] = a
                a_m2, a_m1 = st0_ref[:, cs], st1_ref[:, cs]
            else:
                first = SUBLANES + r0
                abuf_ref[first:first + sub, cs] = a
                a_m1 = abuf_ref[first - 1:first - 1 + sub, cs]
                a_m2 = abuf_ref[first - 2:first - 2 + sub, cs]
            conv = cb_ref[:, cs] + cw_ref[0:1, cs] * a_m2 + cw_ref[1:2, cs] * a_m1 + cw_ref[2:3, cs] * a
            h = jax.nn.gelu(conv) * val
            acc = acc + _dot(h.astype(BF16), wdown_ref[cs, :])
        y_ref[r0:r0 + sub, :] = _layer_norm(alpha * x1 + acc, g_ref[...], b_ref[...])
    if not sample:
        tail = abuf_ref[rows:rows + SUBLANES, :]
        abuf_ref[0:SUBLANES, :] = tail
        a_ref[...] = tail


def _ffn(x1, x1_s, state, w_up, conv_w, conv_b, w_down, ln2_g, ln2_b, *, alpha):
    s, d = x1.shape
    n_s = x1_s.shape[0]
    d_ff = w_down.shape[0]
    n_tiles = s // FFN_ROWS
    tile = lambda width: pl.BlockSpec((FFN_ROWS, width), lambda i: (jnp.minimum(i, n_tiles - 1), 0))
    whole = lambda shape: pl.BlockSpec(shape, lambda i: (0,) * len(shape))
    operands = (x1, x1_s, state[:, 0], state[:, 1], w_up, conv_w, conv_b, w_down, ln2_g, ln2_b)
    return pl.pallas_call(
        functools.partial(_ffn_kernel, alpha=alpha),
        grid=(n_tiles + 1,),
        in_specs=[tile(d)] + [_resident(t.shape) for t in operands[1:]],
        out_specs=[tile(d), whole((SUBLANES, d_ff)), whole((n_s, d)), whole((n_s, d_ff))],
        out_shape=[jax.ShapeDtypeStruct((s, d), F32), jax.ShapeDtypeStruct((SUBLANES, d_ff), F32),
                   jax.ShapeDtypeStruct((n_s, d), F32), jax.ShapeDtypeStruct((n_s, d_ff), F32)],
        scratch_shapes=[pltpu.VMEM((FFN_ROWS + SUBLANES, d_ff), F32)],
        compiler_params=_params("arbitrary"),
        name="ffn",
    )(*operands)


def kernel(x_prompt, x_sample, mem_prompt, cache_win128_kv, cache_win512_kv, cache_win2048_kv, cache_mem_kv, state_ffn_conv, w_in, b_gate, ln_v_g, ln_v_b, w_spatial, b_spatial, w_mem_kv, w_branch_a, w_branch_b, w_branch_m, w_out, ln1_g, ln1_b, w_up, conv_w, conv_b, w_down, ln2_g, ln2_b):
    depth = w_in.shape[0]
    batch, seq, d_model = x_prompt.shape
    dec_batch, dec_seq, _ = x_sample.shape
    assert batch == 1 and dec_seq == 1 and seq % ATTN_BLOCK == 0
    alpha = (2.0 * depth) ** 0.25
    win_caches = (cache_win128_kv, cache_win512_kv, cache_win2048_kv)
    for cache, (win, _) in zip(win_caches, DILATED_GROUPS):
        assert cache.shape[2] == win and PAST_LEN >= win

    yp = x_prompt.reshape(seq, d_model)
    ys = x_sample.reshape(dec_batch, d_model)
    win_p = [[] for _ in range(N_GROUPS_A)]
    win_s = [[] for _ in range(N_GROUPS_A)]
    mem_p, conv_p, gmlp_s, conv_s = [], [], [], []
    for l in range(depth):
        w_l = w_in[l]
        w_rest = w_l[:, 3 * WIDTH_A:].astype(BF16)
        row2 = lambda t: t.reshape(1, -1)
        per_lane = lambda t: jnp.repeat(t, GROUP_DIM_B, axis=-1)
        consts_tail = (w_branch_a[l].astype(BF16), w_branch_b[l].astype(BF16), w_branch_m[l].astype(BF16),
                       w_out[l].astype(BF16), row2(ln1_g[l]), row2(ln1_b[l]))
        ffn_consts = (w_up[l].astype(BF16), conv_w[l], row2(conv_b[l]), w_down[l].astype(BF16),
                      row2(ln2_g[l]), row2(ln2_b[l]))

        mkv = _mem_kv_proj(mem_prompt[0], w_mem_kv[l].astype(BF16))
        mem_p.append(mkv.reshape(1, -1, 2, N_HEADS_M, HEAD_DIM_M))
        qkv, *kv_tails = _qkv_proj(yp, w_l)
        for g, (win, _) in enumerate(DILATED_GROUPS):
            tail = kv_tails[g].reshape(2, HEADS_PER_GROUP_A, HEAD_DIM_A, win)
            win_p[g].append(jnp.transpose(tail, (3, 0, 1, 2))[None])
        o_a = _dilated_attn(qkv)
        caches = [jnp.transpose(c[l], (0, 2, 3, 4, 1)) for c in win_caches]
        oa_s, om_s, kv_new = _sample_attn(ys, w_l, caches, cache_mem_kv[l])
        for g in range(N_GROUPS_A):
            win_s[g].append(kv_new[g].reshape(dec_batch, 1, 2, HEADS_PER_GROUP_A, HEAD_DIM_A))

        x1, x1_s, v_rows = _mix(yp, o_a, mkv[:, :WIDTH_M].astype(BF16), mkv[:, WIDTH_M:].astype(BF16), ys, oa_s, om_s,
                                w_rest, b_gate[l], row2(ln_v_g[l]), row2(ln_v_b[l]),
                                w_spatial[l].reshape(N_GROUPS_B * CHUNK, CHUNK), per_lane(b_spatial[l].T),
                                row2(per_lane(w_spatial[l][:, 0, 0])), row2(per_lane(b_spatial[l][:, 0])),
                                *consts_tail, alpha=alpha)
        gmlp_s.append(v_rows.reshape(dec_batch, 1, WIDTH_B))
        yp_next, a_tail, ys_next, a_s = _ffn(x1, x1_s, state_ffn_conv[l], *ffn_consts, alpha=alpha)
        conv_p.append(a_tail[SUBLANES - (CONV_W - 1):][None])
        conv_s.append(jnp.stack([state_ffn_conv[l][:, 1], a_s], axis=1))
        yp, ys = yp_next, ys_next

    return (yp.reshape(batch, seq, d_model), ys.reshape(dec_batch, dec_seq, d_model),
            jnp.stack(win_p[0]), jnp.stack(win_p[1]), jnp.stack(win_p[2]),
            jnp.stack(mem_p), jnp.stack(conv_p),
            jnp.stack(win_s[0]), jnp.stack(win_s[1]), jnp.stack(win_s[2]),
            jnp.stack(gmlp_s), jnp.stack(conv_s))
```

```python
import functools

import jax
import jax.numpy as jnp
from jax import lax
from jax.experimental import pallas as pl
from jax.experimental.pallas import tpu as pltpu

BF16 = jnp.bfloat16
F32 = jnp.float32

HEAD_DIM_A = 64
HEADS_PER_GROUP_A = 4
DILATED_GROUPS = ((128, 1), (512, 4), (2048, 16))
N_GROUPS_A = len(DILATED_GROUPS)
GROUP_WIDTH_A = HEADS_PER_GROUP_A * HEAD_DIM_A
WIDTH_A = N_GROUPS_A * GROUP_WIDTH_A
NK = 128
CHUNK = 128
N_GROUPS_B = 4
WIDTH_B = 768
GROUP_DIM_B = WIDTH_B // N_GROUPS_B
N_HEADS_M = 4
HEAD_DIM_M = 128
WIDTH_M = N_HEADS_M * HEAD_DIM_M
N_BRANCH = 3
CONV_W = 3
LN_EPS = 1e-5
NEG = -1e30
PAST_LEN = 16384

LANES = 128
SUBLANES = 8
VMEM_LIMIT_BYTES = 58 * 1024 * 1024

ATTN_BLOCK = 2048
DEINTERLEAVE_STRIDE = 4
MIX_ROWS = 512
FFN_ROWS = 512
FFN_SUB_ROWS = 512
FFN_COLS = 2816
SAMPLE_BATCH_BLOCK = 2
SAMPLE_LANE_CHUNK = 512
SAMPLE_MEM_CHUNK = 32

_NT = (((1,), (1,)), ((), ()))


def _dot(a, b):
    return jnp.dot(a, b, preferred_element_type=F32)


def _dot_nt(a, b):
    return lax.dot_general(a, b, _NT, preferred_element_type=F32)


def _layer_norm(x, g, b):
    mu = jnp.mean(x, axis=-1, keepdims=True)
    var = jnp.mean(jnp.square(x - mu), axis=-1, keepdims=True)
    return (x - mu) * lax.rsqrt(var + LN_EPS) * g + b


def _params(*semantics):
    return pltpu.CompilerParams(dimension_semantics=semantics, vmem_limit_bytes=VMEM_LIMIT_BYTES)


def _resident(shape):
    zeros = (0,) * len(shape)
    return pl.BlockSpec(shape, lambda *_: zeros, pipeline_mode=pl.Buffered(1))


def _mem_kv_kernel(mem_ref, w_ref, o_ref):
    o_ref[...] = _dot(mem_ref[...].astype(BF16), w_ref[...])


def _mem_kv_proj(mem, w):
    n, d = mem.shape
    return pl.pallas_call(
        _mem_kv_kernel,
        out_shape=jax.ShapeDtypeStruct((n, w.shape[1]), F32),
        name="mem_kv_proj",
    )(mem, w)


def _qkv_proj_kernel(xt_ref, xbot_ref, wq_ref, wk_ref, wv_ref, o_ref, kvt0_ref, kvt1_ref, kvt2_ref,
                     xb_ref, stage_ref, tmp_ref):
    g_id = pl.program_id(1)

    @pl.when(g_id == 0)
    def _():
        half = ATTN_BLOCK // 2
        xb_ref[0:half, :] = xt_ref[...].astype(BF16)
        xb_ref[half:, :] = xbot_ref[...].astype(BF16)

    xb = xb_ref[...]
    last_block = pl.program_id(0) == pl.num_programs(0) - 1
    for g, (win, dil) in enumerate(DILATED_GROUPS):
        kvt_ref = (kvt0_ref, kvt1_ref, kvt2_ref)[g]

        @pl.when(g_id == g)
        def _(g=g, win=win, dil=dil, kvt_ref=kvt_ref):
            w_cols = lambda c: (wq_ref, wk_ref, wv_ref)[c][...]
            for c in range(3):
                w = w_cols(c)
                if c == 0:
                    w = w * (HEAD_DIM_A ** -0.5)
                res = _dot(xb, w.astype(BF16))
                if dil == 1:
                    o_ref[:, c * 256:(c + 1) * 256] = res.astype(BF16)
                    continue
                stage_ref[0] = res[:, :LANES]
                stage_ref[1] = res[:, LANES:]
                inner = DEINTERLEAVE_STRIDE if dil > DEINTERLEAVE_STRIDE else dil
                outer = dil // inner
                for s in range(2):
                    col = c * 256 + s * LANES
                    for blk in range(ATTN_BLOCK // win):
                        base = blk * win
                        src_ref = stage_ref
                        if outer > 1:
                            part = win // inner
                            for lo in range(inner):
                                tmp_ref[s, base + lo * part:base + (lo + 1) * part, :] = (
                                    stage_ref[s, pl.ds(base + lo, part, stride=inner), :])
                            src_ref = tmp_ref
                        for lo in range(inner):
                            for hi in range(outer):
                                first = base + lo * (win // inner) + hi if outer > 1 else base + lo
                                rows = src_ref[s, pl.ds(first, NK, stride=outer if outer > 1 else inner), :]
                                dst = base + (hi * inner + lo) * NK
                                o_ref[dst:dst + NK, col:col + LANES] = rows.astype(BF16)

            @pl.when(last_block)
            def _():
                tail = xb[ATTN_BLOCK - win:, :]
                kvt_ref[0:GROUP_WIDTH_A, :] = _dot(tail, w_cols(1).astype(BF16)).T
                kvt_ref[GROUP_WIDTH_A:2 * GROUP_WIDTH_A, :] = _dot(tail, w_cols(2).astype(BF16)).T


def _qkv_proj(x, w_in):
    s, d = x.shape
    n_blk = s // ATTN_BLOCK
    col_block = lambda part: pl.BlockSpec((d, GROUP_WIDTH_A), lambda n, g: (0, part * N_GROUPS_A + g))
    half = ATTN_BLOCK // 2
    x_half = lambda which: pl.BlockSpec(
        (half, d), lambda n, g: (2 * jnp.minimum(n + jnp.minimum(jnp.maximum(g - which, 0), 1), n_blk - 1) + which, 0))
    kvt_shapes = [(2 * GROUP_WIDTH_A, win) for win, _ in DILATED_GROUPS]
    return pl.pallas_call(
        _qkv_proj_kernel,
        grid=(n_blk, N_GROUPS_A),
        in_specs=[x_half(0), x_half(1), col_block(0), col_block(1), col_block(2)],
        out_specs=[pl.BlockSpec((None, ATTN_BLOCK, 3 * GROUP_WIDTH_A), lambda n, g: (g, n, 0))]
        + [pl.BlockSpec(shape, lambda n, g: (0, 0)) for shape in kvt_shapes],
        out_shape=[jax.ShapeDtypeStruct((N_GROUPS_A, s, 3 * GROUP_WIDTH_A), BF16)]
        + [jax.ShapeDtypeStruct(shape, F32) for shape in kvt_shapes],
        scratch_shapes=[pltpu.VMEM((ATTN_BLOCK, d), BF16),
                        pltpu.VMEM((GROUP_WIDTH_A // LANES, ATTN_BLOCK, LANES), F32),
                        pltpu.VMEM((GROUP_WIDTH_A // LANES, ATTN_BLOCK, LANES), F32)],
        compiler_params=_params("arbitrary", "arbitrary"),
        name="qkv_proj",
    )(x, x, w_in, w_in, w_in)


def _attn_units(units, prev_shifts):
    first_head = lax.broadcasted_iota(jnp.int32, (NK, LANES), 1) < HEAD_DIM_A
    row = lax.broadcasted_iota(jnp.int32, (2 * NK, NK), 0) % NK
    col = lax.broadcasted_iota(jnp.int32, (2 * NK, NK), 1)
    cur_ok = col <= row
    prev_ok = {}
    scores = []
    for (q, k_p, k_c, _, _), shift in zip(units, prev_shifts):
        zero = jnp.zeros_like(q)
        q_st = jnp.concatenate([jnp.where(first_head, q, zero), jnp.where(first_head, zero, q)], axis=0)
        if id(shift) not in prev_ok:
            prev_ok[id(shift)] = col >= row + shift
        scores.append((jnp.where(cur_ok, _dot_nt(q_st, k_c), NEG),
                       jnp.where(prev_ok[id(shift)], _dot_nt(q_st, k_p), NEG)))
    maxes = [jnp.max(jnp.maximum(s_c, s_p), axis=-1, keepdims=True) for s_c, s_p in scores]
    probs = [(jnp.exp(s_c - m), jnp.exp(s_p - m)) for (s_c, s_p), m in zip(scores, maxes)]
    dens = [jnp.sum(p_c + p_p, axis=-1, keepdims=True) for p_c, p_p in probs]
    results = []
    for (_, _, _, v_p, v_c), (p_c, p_p), m, den in zip(units, probs, maxes, dens):
        o_st = _dot(p_c.astype(BF16), v_c) + _dot(p_p.astype(BF16), v_p)
        pick = lambda t: jnp.where(first_head, jnp.broadcast_to(t[:NK], (NK, LANES)), jnp.broadcast_to(t[NK:], (NK, LANES)))
        den_sel = pick(den)
        results.append((pick(o_st) / den_sel, pick(m) + jnp.log(den_sel)))
    return results


def _dilated_attn_kernel(qkv0_ref, qkv1_ref, qkv2_ref, o_ref, kv0_ref, kv1_ref, kv2_ref, og_ref, lg_ref):
    n = pl.program_id(0)
    qkv_refs = (qkv0_ref, qkv1_ref, qkv2_ref)
    kv_refs = (kv0_ref, kv1_ref, kv2_ref)
    n_sub = ATTN_BLOCK // NK
    n_hp = GROUP_WIDTH_A // LANES

    for (win, _), qkv_ref, kv_ref in zip(DILATED_GROUPS, qkv_refs, kv_refs):
        @pl.when(n == 0)
        def _(kv_ref=kv_ref, win=win):
            kv_ref[0:win, :] = jnp.zeros((win, 2 * GROUP_WIDTH_A), BF16)

        kv_ref[win:win + ATTN_BLOCK, :] = qkv_ref[:, GROUP_WIDTH_A:]

    def sub_block(t, carry):
        off = pl.multiple_of(t * NK, NK)
        units, shifts, dsts = [], [], []
        for g, (win, dil) in enumerate(DILATED_GROUPS):
            win_blk = t // dil
            res = t - win_blk * dil
            has_prev = (n * (ATTN_BLOCK // win) + win_blk) > 0
            nat = win_blk * win + res
            shift = jnp.where(has_prev, 0, 2 * NK)
            for hp in range(n_hp):
                ks = slice(hp * LANES, (hp + 1) * LANES)
                vs = slice(GROUP_WIDTH_A + hp * LANES, GROUP_WIDTH_A + (hp + 1) * LANES)
                units.append((qkv_refs[g][pl.ds(off, NK), ks],
                              kv_refs[g][pl.ds(off, NK), ks], kv_refs[g][pl.ds(off + win, NK), ks],
                              kv_refs[g][pl.ds(off, NK), vs], kv_refs[g][pl.ds(off + win, NK), vs]))
                shifts.append(shift)
                dsts.append((g, hp, pl.ds(off, NK) if dil == 1 else pl.ds(nat, NK, stride=dil)))
        for (g, hp, dst), (out, lse) in zip(dsts, _attn_units(units, shifts)):
            og_ref[g, hp, dst, :] = out
            lg_ref[g, hp, dst, :] = lse
        return carry

    lax.fori_loop(0, n_sub, sub_block, 0)
    for (win, _), kv_ref in zip(DILATED_GROUPS, kv_refs):
        kv_ref[0:win, :] = kv_ref[ATTN_BLOCK:ATTN_BLOCK + win, :]

    def merge(t, carry):
        off = pl.multiple_of(t * NK, NK)
        for hp in range(GROUP_WIDTH_A // LANES):
            lses = [lg_ref[g, hp, pl.ds(off, NK), :] for g in range(N_GROUPS_A)]
            m = jnp.maximum(jnp.maximum(lses[0], lses[1]), lses[2])
            es = [jnp.exp(l - m) for l in lses]
            num = sum(es[g] * og_ref[g, hp, pl.ds(off, NK), :] for g in range(N_GROUPS_A))
            o_ref[pl.ds(off, NK), hp * LANES:(hp + 1) * LANES] = (num / (es[0] + es[1] + es[2])).astype(BF16)
        return carry

    lax.fori_loop(0, n_sub, merge, 0)


def _dilated_attn(qkv):
    _, s, _ = qkv.shape
    n_blk = s // ATTN_BLOCK
    n_hp = GROUP_WIDTH_A // LANES
    in_specs = [pl.BlockSpec((None, ATTN_BLOCK, 3 * GROUP_WIDTH_A), functools.partial(lambda n, g: (g, n, 0), g=g))
                for g in range(N_GROUPS_A)]
    return pl.pallas_call(
        _dilated_attn_kernel,
        grid=(n_blk,),
        in_specs=in_specs,
        out_specs=pl.BlockSpec((ATTN_BLOCK, GROUP_WIDTH_A), lambda n: (n, 0)),
        out_shape=jax.ShapeDtypeStruct((s, GROUP_WIDTH_A), BF16),
        scratch_shapes=[pltpu.VMEM((win + ATTN_BLOCK, 2 * GROUP_WIDTH_A), BF16) for win, _ in DILATED_GROUPS]
        + [pltpu.VMEM((N_GROUPS_A, n_hp, ATTN_BLOCK, LANES), F32),
           pltpu.VMEM((N_GROUPS_A, n_hp, ATTN_BLOCK, LANES), F32)],
        compiler_params=_params("arbitrary"),
        name="dilated_attn",
    )(qkv, qkv, qkv)


def _to_heads(row, n_heads, head_dim):
    return jnp.concatenate([row[:, h * head_dim:(h + 1) * head_dim] for h in range(n_heads)], axis=0)


def _from_heads(t):
    return jnp.concatenate([t[h:h + 1] for h in range(t.shape[0])], axis=1)


def _window_cache_attention(c_ref, j, cols_ref, base, dil):
    win = c_ref.shape[-1]
    lc = min(win, SAMPLE_LANE_CHUNK)
    n_chunks = win // lc
    lane = lax.broadcasted_iota(jnp.int32, (1, win), 1)
    wanted = (lane % dil) == 0
    out = []
    for h in range(HEADS_PER_GROUP_A):
        lo = base + h * HEAD_DIM_A
        q, k_new, v_new = (cols_ref[lo + c * GROUP_WIDTH_A:lo + c * GROUP_WIDTH_A + HEAD_DIM_A, :] for c in range(3))
        q_wide = jnp.concatenate([q] * (lc // LANES), axis=1)
        parts = [jnp.sum(c_ref[j, 0, h, :, i * lc:(i + 1) * lc] * q_wide, axis=0, keepdims=True)
                 for i in range(n_chunks)]
        s = jnp.where(wanted, jnp.concatenate(parts, axis=1), NEG)
        s_n = jnp.sum(k_new[:, 0:1] * q[:, 0:1], axis=0, keepdims=True)
        m = jnp.maximum(jnp.max(s, axis=-1, keepdims=True), s_n)
        p = jnp.exp(s - m)
        p_n = jnp.exp(s_n - m)
        den = jnp.sum(p, axis=-1, keepdims=True) + p_n
        acc = p_n * v_new[:, 0:1]
        for i in range(n_chunks):
            acc = acc + jnp.sum(c_ref[j, 1, h, :, i * lc:(i + 1) * lc] * p[:, i * lc:(i + 1) * lc],
                                axis=-1, keepdims=True)
        out.append((acc / den, m + jnp.log(den)))
    return out


def _memory_cache_attention(cm_ref, j, q, s_ref):
    n_heads, head_dim = q.shape
    chunk = SAMPLE_MEM_CHUNK
    n_chunks = cm_ref.shape[1] // chunk
    m = jnp.full((n_heads, head_dim), NEG, F32)
    for c in range(n_chunks):
        pos = slice(c * chunk, (c + 1) * chunk)
        s = jnp.sum(cm_ref[j, pos, 0] * q[None], axis=-1, keepdims=True) * (head_dim ** -0.5)
        s = jnp.broadcast_to(s, (chunk, n_heads, head_dim))
        s_ref[pos] = s
        m = jnp.maximum(m, jnp.max(s, axis=0))
    den = jnp.zeros((n_heads, head_dim), F32)
    acc = jnp.zeros((n_heads, head_dim), F32)
    for c in range(n_chunks):
        pos = slice(c * chunk, (c + 1) * chunk)
        p = jnp.exp(s_ref[pos] - m[None])
        den = den + jnp.sum(p, axis=0)
        acc = acc + jnp.sum(p * cm_ref[j, pos, 1], axis=0)
    return acc / den


def _sample_attn_kernel(x_ref, wqkv_ref, wqm_ref, c0_ref, c1_ref, c2_ref, cm_ref,
                        oa_ref, om_ref, kvn_ref, proj_ref, projt_ref, cols_ref, oat_ref, ms_ref):
    step = pl.program_id(0)
    caches = (c0_ref, c1_ref, c2_ref)
    n_batch, d = x_ref.shape
    qkv_w = N_GROUPS_A * 3 * GROUP_WIDTH_A

    @pl.when(step == 0)
    def _():
        xb = jnp.concatenate([x_ref[...], jnp.zeros((LANES - n_batch, d), F32)], axis=0).astype(BF16)
        for part in range(3):
            h = _dot(xb, wqkv_ref[:, part * WIDTH_A:(part + 1) * WIDTH_A].astype(BF16))
            if part == 0:
                h = h * (HEAD_DIM_A ** -0.5)
            for g in range(N_GROUPS_A):
                h_g = h[:, g * GROUP_WIDTH_A:(g + 1) * GROUP_WIDTH_A]
                dst = (3 * g + part) * GROUP_WIDTH_A
                proj_ref[:, dst:dst + GROUP_WIDTH_A] = h_g
                if part > 0:
                    kvn_ref[g, :, (part - 1) * GROUP_WIDTH_A:part * GROUP_WIDTH_A] = h_g[:n_batch]
        proj_ref[:, qkv_w:] = _dot(xb, wqm_ref[:, :WIDTH_M].astype(BF16))
        for t in range(qkv_w // LANES):
            projt_ref[t * LANES:(t + 1) * LANES, :] = proj_ref[:, t * LANES:(t + 1) * LANES].T.astype(BF16)
        oat_ref[...] = jnp.zeros(oat_ref.shape, F32)

    def one_row(j, carry):
        b = step * SAMPLE_BATCH_BLOCK + j
        pick_row = (lax.broadcasted_iota(jnp.int32, (LANES, LANES), 0) == b).astype(BF16)
        cols_ref[...] = _dot(projt_ref[...], pick_row)
        per_group = [_window_cache_attention(caches[g], j, cols_ref, g * 3 * GROUP_WIDTH_A, dil)
                     for g, (_, dil) in enumerate(DILATED_GROUPS)]
        this_lane = lax.broadcasted_iota(jnp.int32, (1, LANES), 1) == b
        for h in range(HEADS_PER_GROUP_A):
            outs = [per_group[g][h][0] for g in range(N_GROUPS_A)]
            lses = [per_group[g][h][1] for g in range(N_GROUPS_A)]
            m3 = jnp.maximum(jnp.maximum(lses[0], lses[1]), lses[2])
            es = [jnp.exp(l - m3) for l in lses]
            oa = (es[0] * outs[0] + es[1] * outs[1] + es[2] * outs[2]) / (es[0] + es[1] + es[2])
            rows = slice(h * HEAD_DIM_A, (h + 1) * HEAD_DIM_A)
            oat_ref[rows, :] = jnp.where(this_lane, oa, oat_ref[rows, :])

        qm = _to_heads(proj_ref[pl.ds(b, 1), qkv_w:], N_HEADS_M, HEAD_DIM_M)
        om_ref[pl.ds(b, 1), :] = _from_heads(_memory_cache_attention(cm_ref, j, qm, ms_ref))
        return carry

    lax.fori_loop(0, SAMPLE_BATCH_BLOCK, one_row, 0)

    @pl.when(step == pl.num_programs(0) - 1)
    def _():
        oa_ref[...] = oat_ref[...].T[:n_batch, :]


def _sample_attn(xs, w_in, caches, cache_mem):
    n_batch, d = xs.shape
    assert n_batch <= LANES
    bb = SAMPLE_BATCH_BLOCK
    qkv_w = N_GROUPS_A * 3 * GROUP_WIDTH_A
    qm_col = 3 * WIDTH_A + 2 * WIDTH_B
    assert qm_col % WIDTH_B == 0 and WIDTH_M <= WIDTH_B
    w_specs = [pl.BlockSpec((d, qkv_w), lambda i: (0, 0), pipeline_mode=pl.Buffered(1)),
               pl.BlockSpec((d, WIDTH_B), lambda i: (0, qm_col // WIDTH_B), pipeline_mode=pl.Buffered(1))]
    cache_specs = [pl.BlockSpec((bb,) + c.shape[1:], lambda i: (i, 0, 0, 0, 0)) for c in caches]
    whole = lambda shape: pl.BlockSpec(shape, lambda i: (0,) * len(shape))
    return pl.pallas_call(
        _sample_attn_kernel,
        grid=(n_batch // bb,),
        in_specs=[whole(xs.shape)] + w_specs + cache_specs
        + [pl.BlockSpec((bb,) + cache_mem.shape[1:], lambda i: (i, 0, 0, 0, 0))],
        out_specs=[whole((n_batch, GROUP_WIDTH_A)), whole((n_batch, WIDTH_M)),
                   whole((N_GROUPS_A, n_batch, 2 * GROUP_WIDTH_A))],
        out_shape=[jax.ShapeDtypeStruct((n_batch, GROUP_WIDTH_A), F32),
                   jax.ShapeDtypeStruct((n_batch, WIDTH_M), F32),
                   jax.ShapeDtypeStruct((N_GROUPS_A, n_batch, 2 * GROUP_WIDTH_A), F32)],
        scratch_shapes=[pltpu.VMEM((LANES, qkv_w + WIDTH_M), F32),
                        pltpu.VMEM((qkv_w, LANES), BF16),
                        pltpu.VMEM((qkv_w, LANES), F32),
                        pltpu.VMEM((GROUP_WIDTH_A, LANES), F32),
                        pltpu.VMEM((cache_mem.shape[1], N_HEADS_M, HEAD_DIM_M), F32)],
        compiler_params=_params("arbitrary"),
        name="sample_attn",
    )(xs, w_in, w_in, *caches, cache_mem)


def _mix_kernel(x_ref, oa_ref, mk_ref, mv_ref, xs_ref, oas_ref, oms_ref, w_ref, bgate_ref, lnvg_ref, lnvb_ref,
                ws_ref, bs_ref, wss_ref, bss_ref, wba_ref, wbb_ref, wbm_ref, wout_ref, ln1g_ref, ln1b_ref,
                x1_ref, x1s_ref, vrows_ref, *, alpha):
    shared = (w_ref, bgate_ref, lnvg_ref, lnvb_ref)
    tail = (wba_ref, wbb_ref, wbm_ref, wout_ref, ln1g_ref, ln1b_ref)
    is_sample_step = pl.program_id(0) == pl.num_programs(0) - 1

    @pl.when(jnp.logical_not(is_sample_step))
    def _():
        _mix_body(x_ref, oa_ref, (mk_ref, mv_ref), *shared, ws_ref, bs_ref, *tail, x1_ref, None,
                  sample=False, alpha=alpha)

    @pl.when(is_sample_step)
    def _():
        _mix_body(xs_ref, oas_ref, oms_ref, *shared, wss_ref, bss_ref, *tail, x1s_ref, vrows_ref,
                  sample=True, alpha=alpha)


def _mix_body(x_ref, oa_ref, mem_or_om, w_ref, bgate_ref, lnvg_ref, lnvb_ref, ws_ref, bs_ref,
              wba_ref, wbb_ref, wbm_ref, wout_ref, ln1g_ref, ln1b_ref, x1_ref, vrows_ref, *, sample, alpha):
    if sample:
        om_ref = mem_or_om
    else:
        mk_ref, mv_ref = mem_or_om
    d = x_ref.shape[1]
    rows = x_ref.shape[0]
    x = x_ref[...]
    xb = x.astype(BF16)
    col_u, col_v, col_qm, col_gate = 0, WIDTH_B, 2 * WIDTH_B, 2 * WIDTH_B + WIDTH_M

    h_all = _dot(xb, w_ref[...])

    def gate(k):
        z = h_all[:, col_gate + k * d:col_gate + (k + 1) * d]
        return jax.nn.sigmoid(z + bgate_ref[k:k + 1, :])

    mixed = gate(0) * _dot(oa_ref[...].astype(BF16), wba_ref[...])

    u = jax.nn.gelu(h_all[:, col_u:col_u + WIDTH_B])
    v = _layer_norm(jax.nn.gelu(h_all[:, col_v:col_v + WIDTH_B]), lnvg_ref[...], lnvb_ref[...])
    if sample:
        vrows_ref[...] = v
        spatial = v.astype(BF16).astype(F32) * ws_ref[...].astype(BF16).astype(F32) + bs_ref[...]
    else:
        r_i = lax.broadcasted_iota(jnp.int32, (N_GROUPS_B * CHUNK, CHUNK), 0)
        c_i = lax.broadcasted_iota(jnp.int32, (N_GROUPS_B * CHUNK, CHUNK), 1)
        w_s = jnp.where((r_i % CHUNK) >= c_i, ws_ref[...], 0.0).astype(BF16)
        lane = lax.broadcasted_iota(jnp.int32, (CHUNK, WIDTH_B), 1)
        in_group = [(lane >= gb * GROUP_DIM_B) & (lane < (gb + 1) * GROUP_DIM_B) for gb in range(N_GROUPS_B)]
        vb = v.astype(BF16)
        parts = []
        for ch in range(rows // CHUNK):
            allg = _dot(w_s, vb[ch * CHUNK:(ch + 1) * CHUNK, :])
            sp = bs_ref[...]
            for gb in range(N_GROUPS_B):
                sp = sp + jnp.where(in_group[gb], allg[gb * CHUNK:(gb + 1) * CHUNK, :], 0.0)
            parts.append(sp)
        spatial = jnp.concatenate(parts, axis=0)
    o_b = u * spatial
    mixed = mixed + gate(1) * _dot(o_b.astype(BF16), wbb_ref[...])

    if sample:
        o_m = om_ref[...]
    else:
        qm = h_all[:, col_qm:col_qm + WIDTH_M].astype(BF16)
        heads = []
        for h in range(N_HEADS_M):
            hs = slice(h * HEAD_DIM_M, (h + 1) * HEAD_DIM_M)
            s = _dot_nt(qm[:, hs], mk_ref[:, hs]) * (HEAD_DIM_M ** -0.5)
            p = jnp.exp(s - jnp.max(s, axis=-1, keepdims=True))
            heads.append(_dot(p.astype(BF16), mv_ref[:, hs]) / jnp.sum(p, axis=-1, keepdims=True))
        o_m = jnp.concatenate(heads, axis=-1)
    mixed = mixed + gate(2) * _dot(o_m.astype(BF16), wbm_ref[...])

    x1_ref[...] = _layer_norm(alpha * x + _dot(mixed.astype(BF16), wout_ref[...]), ln1g_ref[...], ln1b_ref[...])


def _mix(x, o_a, mk, mv, xs, oa_s, om_s, w_rest, b_gate, ln_v_g, ln_v_b, w_s, b_s, w_s_row, b_s_row,
         w_ba, w_bb, w_bm, w_out, ln1_g, ln1_b, *, alpha):
    s, d = x.shape
    n_tiles = s // MIX_ROWS
    tile = lambda width: pl.BlockSpec((MIX_ROWS, width), lambda i: (jnp.minimum(i, n_tiles - 1), 0))
    whole = lambda shape: pl.BlockSpec(shape, lambda i: (0,) * len(shape))
    operands = (x, o_a, mk, mv, xs, oa_s, om_s, w_rest, b_gate, ln_v_g, ln_v_b, w_s, b_s, w_s_row, b_s_row,
                w_ba, w_bb, w_bm, w_out, ln1_g, ln1_b)
    n_s = xs.shape[0]
    return pl.pallas_call(
        functools.partial(_mix_kernel, alpha=alpha),
        grid=(n_tiles + 1,),
        in_specs=[tile(d), tile(GROUP_WIDTH_A)] + [_resident(t.shape) for t in operands[2:]],
        out_specs=[tile(d), whole((n_s, d)), whole((n_s, WIDTH_B))],
        out_shape=[jax.ShapeDtypeStruct((s, d), F32), jax.ShapeDtypeStruct((n_s, d), F32),
                   jax.ShapeDtypeStruct((n_s, WIDTH_B), F32)],
        compiler_params=_params("arbitrary"),
        name="mix",
    )(*operands)


def _ffn_chunks(d_ff):
    bounds = list(range(0, d_ff, FFN_COLS)) + [d_ff]
    return list(zip(bounds[:-1], bounds[1:]))


def _ffn_kernel(x1_ref, x1s_ref, st0_ref, st1_ref, wup_ref, cw_ref, cb_ref, wdown_ref, g_ref, b_ref,
                y_ref, a_ref, ys_ref, as_ref, abuf_ref, *, alpha):
    consts = (wup_ref, cw_ref, cb_ref, wdown_ref, g_ref, b_ref)
    is_sample_step = pl.program_id(0) == pl.num_programs(0) - 1

    @pl.when(jnp.logical_not(is_sample_step))
    def _():
        _ffn_body(x1_ref, None, None, *consts, y_ref, a_ref, abuf_ref, sample=False, alpha=alpha)

    @pl.when(is_sample_step)
    def _():
        _ffn_body(x1s_ref, st0_ref, st1_ref, *consts, ys_ref, as_ref, None, sample=True, alpha=alpha)


def _ffn_body(x1_ref, st0_ref, st1_ref, wup_ref, cw_ref, cb_ref, wdown_ref, g_ref, b_ref, y_ref, a_ref, abuf_ref,
              *, sample, alpha):
    rows = x1_ref.shape[0]
    d_ff = wdown_ref.shape[0]
    if not sample:
        @pl.when(pl.program_id(0) == 0)
        def _():
            abuf_ref[0:SUBLANES, :] = jnp.zeros((SUBLANES, d_ff), F32)

    sub = rows if sample else FFN_SUB_ROWS
    for r0 in range(0, rows, sub):
        x1 = x1_ref[r0:r0 + sub, :]
        xb = x1.astype(BF16)
        acc = jnp.zeros(x1.shape, F32)
        for lo, hi in _ffn_chunks(d_ff):
            cs = slice(lo, hi)
            a = _dot(xb, wup_ref[:, cs])
            val = _dot(xb, wup_ref[:, d_ff + lo:d_ff + hi])
            if sample:
                a_ref[:, cs] = a
                a_m2, a_m1 = st0_ref[:, cs], st1_ref[:, cs]
            else:
                first = SUBLANES + r0
                abuf_ref[first:first + sub, cs] = a
                a_m1 = abuf_ref[first - 1:first - 1 + sub, cs]
                a_m2 = abuf_ref[first - 2:first - 2 + sub, cs]
            conv = cb_ref[:, cs] + cw_ref[0:1, cs] * a_m2 + cw_ref[1:2, cs] * a_m1 + cw_ref[2:3, cs] * a
            h = jax.nn.gelu(conv) * val
            acc = acc + _dot(h.astype(BF16), wdown_ref[cs, :])
        y_ref[r0:r0 + sub, :] = _layer_norm(alpha * x1 + acc, g_ref[...], b_ref[...])
    if not sample:
        tail = abuf_ref[rows:rows + SUBLANES, :]
        abuf_ref[0:SUBLANES, :] = tail
        a_ref[...] = tail


def _ffn(x1, x1_s, state, w_up, conv_w, conv_b, w_down, ln2_g, ln2_b, *, alpha):
    s, d = x1.shape
    n_s = x1_s.shape[0]
    d_ff = w_down.shape[0]
    n_tiles = s // FFN_ROWS
    tile = lambda width: pl.BlockSpec((FFN_ROWS, width), lambda i: (jnp.minimum(i, n_tiles - 1), 0))
    whole = lambda shape: pl.BlockSpec(shape, lambda i: (0,) * len(shape))
    operands = (x1, x1_s, state[:, 0], state[:, 1], w_up, conv_w, conv_b, w_down, ln2_g, ln2_b)
    return pl.pallas_call(
        functools.partial(_ffn_kernel, alpha=alpha),
        grid=(n_tiles + 1,),
        in_specs=[tile(d)] + [_resident(t.shape) for t in operands[1:]],
        out_specs=[tile(d), whole((SUBLANES, d_ff)), whole((n_s, d)), whole((n_s, d_ff))],
        out_shape=[jax.ShapeDtypeStruct((s, d), F32), jax.ShapeDtypeStruct((SUBLANES, d_ff), F32),
                   jax.ShapeDtypeStruct((n_s, d), F32), jax.ShapeDtypeStruct((n_s, d_ff), F32)],
        scratch_shapes=[pltpu.VMEM((FFN_ROWS + SUBLANES, d_ff), F32)],
        compiler_params=_params("arbitrary"),
        name="ffn",
    )(*operands)


def kernel(x_prompt, x_sample, mem_prompt, cache_win128_kv, cache_win512_kv, cache_win2048_kv, cache_mem_kv, state_ffn_conv, w_in, b_gate, ln_v_g, ln_v_b, w_spatial, b_spatial, w_mem_kv, w_branch_a, w_branch_b, w_branch_m, w_out, ln1_g, ln1_b, w_up, conv_w, conv_b, w_down, ln2_g, ln2_b):
    depth = w_in.shape[0]
    batch, seq, d_model = x_prompt.shape
    dec_batch, dec_seq, _ = x_sample.shape
    assert batch == 1 and dec_seq == 1 and seq % ATTN_BLOCK == 0
    alpha = (2.0 * depth) ** 0.25
    win_caches = (cache_win128_kv, cache_win512_kv, cache_win2048_kv)
    for cache, (win, _) in zip(win_caches, DILATED_GROUPS):
        assert cache.shape[2] == win and PAST_LEN >= win

    yp = x_prompt.reshape(seq, d_model)
    ys = x_sample.reshape(dec_batch, d_model)
    win_p = [[] for _ in range(N_GROUPS_A)]
    win_s = [[] for _ in range(N_GROUPS_A)]
    mem_p, conv_p, gmlp_s, conv_s = [], [], [], []
    for l in range(depth):
        w_l = w_in[l]
        w_rest = w_l[:, 3 * WIDTH_A:].astype(BF16)
        row2 = lambda t: t.reshape(1, -1)
        per_lane = lambda t: jnp.repeat(t, GROUP_DIM_B, axis=-1)
        consts_tail = (w_branch_a[l].astype(BF16), w_branch_b[l].astype(BF16), w_branch_m[l].astype(BF16),
                       w_out[l].astype(BF16), row2(ln1_g[l]), row2(ln1_b[l]))
        ffn_consts = (w_up[l].astype(BF16), conv_w[l], row2(conv_b[l]), w_down[l].astype(BF16),
                      row2(ln2_g[l]), row2(ln2_b[l]))

        mkv = _mem_kv_proj(mem_prompt[0], w_mem_kv[l].astype(BF16))
        mem_p.append(mkv.reshape(1, -1, 2, N_HEADS_M, HEAD_DIM_M))
        qkv, *kv_tails = _qkv_proj(yp, w_l)
        for g, (win, _) in enumerate(DILATED_GROUPS):
            tail = kv_tails[g].reshape(2, HEADS_PER_GROUP_A, HEAD_DIM_A, win)
            win_p[g].append(jnp.transpose(tail, (3, 0, 1, 2))[None])
        o_a = _dilated_attn(qkv)
        caches = [jnp.transpose(c[l], (0, 2, 3, 4, 1)) for c in win_caches]
        oa_s, om_s, kv_new = _sample_attn(ys, w_l, caches, cache_mem_kv[l])
        for g in range(N_GROUPS_A):
            win_s[g].append(kv_new[g].reshape(dec_batch, 1, 2, HEADS_PER_GROUP_A, HEAD_DIM_A))

        x1, x1_s, v_rows = _mix(yp, o_a, mkv[:, :WIDTH_M].astype(BF16), mkv[:, WIDTH_M:].astype(BF16), ys, oa_s, om_s,
                                w_rest, b_gate[l], row2(ln_v_g[l]), row2(ln_v_b[l]),
                                w_spatial[l].reshape(N_GROUPS_B * CHUNK, CHUNK), per_lane(b_spatial[l].T),
                                row2(per_lane(w_spatial[l][:, 0, 0])), row2(per_lane(b_spatial[l][:, 0])),
                                *consts_tail, alpha=alpha)
        gmlp_s.append(v_rows.reshape(dec_batch, 1, WIDTH_B))
        yp_next, a_tail, ys_next, a_s = _ffn(x1, x1_s, state_ffn_conv[l], *ffn_consts, alpha=alpha)
        conv_p.append(a_tail[SUBLANES - (CONV_W - 1):][None])
        conv_s.append(jnp.stack([state_ffn_conv[l][:, 1], a_s], axis=1))
        yp, ys = yp_next, ys_next

    return (yp.reshape(batch, seq, d_model), ys.reshape(dec_batch, dec_seq, d_model),
            jnp.stack(win_p[0]), jnp.stack(win_p[1]), jnp.stack(win_p[2]),
            jnp.stack(mem_p), jnp.stack(conv_p),
            jnp.stack(win_s[0]), jnp.stack(win_s[1]), jnp.stack(win_s[2]),
            jnp.stack(gmlp_s), jnp.stack(conv_s))
```

```python
import functools

import jax
import jax.numpy as jnp
from jax import lax
from jax.experimental import pallas as pl
from jax.experimental.pallas import tpu as pltpu

BF16 = jnp.bfloat16
F32 = jnp.float32

HEAD_DIM_A = 64
HEADS_PER_GROUP_A = 4
DILATED_GROUPS = ((128, 1), (512, 4), (2048, 16))
N_GROUPS_A = len(DILATED_GROUPS)
GROUP_WIDTH_A = HEADS_PER_GROUP_A * HEAD_DIM_A
WIDTH_A = N_GROUPS_A * GROUP_WIDTH_A
NK = 128
CHUNK = 128
N_GROUPS_B = 4
WIDTH_B = 768
GROUP_DIM_B = WIDTH_B // N_GROUPS_B
N_HEADS_M = 4
HEAD_DIM_M = 128
WIDTH_M = N_HEADS_M * HEAD_DIM_M
N_BRANCH = 3
CONV_W = 3
LN_EPS = 1e-5
NEG = -1e30
PAST_LEN = 16384

LANES = 128
SUBLANES = 8
VMEM_LIMIT_BYTES = 58 * 1024 * 1024

ATTN_BLOCK = 2048
DEINTERLEAVE_STRIDE = 4
MIX_ROWS = 512
FFN_ROWS = 512
FFN_SUB_ROWS = 512
FFN_COLS = 2816
SAMPLE_BATCH_BLOCK = 2
SAMPLE_LANE_CHUNK = 512
SAMPLE_MEM_CHUNK = 32

_NT = (((1,), (1,)), ((), ()))


def _dot(a, b):
    return jnp.dot(a, b, preferred_element_type=F32)


def _dot_nt(a, b):
    return lax.dot_general(a, b, _NT, preferred_element_type=F32)


def _layer_norm(x, g, b):
    mu = jnp.mean(x, axis=-1, keepdims=True)
    var = jnp.mean(jnp.square(x - mu), axis=-1, keepdims=True)
    return (x - mu) * lax.rsqrt(var + LN_EPS) * g + b


def _params(*semantics):
    return pltpu.CompilerParams(dimension_semantics=semantics, vmem_limit_bytes=VMEM_LIMIT_BYTES)


def _resident(shape):
    zeros = (0,) * len(shape)
    return pl.BlockSpec(shape, lambda *_: zeros, pipeline_mode=pl.Buffered(1))


def _mem_kv_kernel(mem_ref, w_ref, o_ref):
    o_ref[...] = _dot(mem_ref[...].astype(BF16), w_ref[...])


def _mem_kv_proj(mem, w):
    n, d = mem.shape
    return pl.pallas_call(
        _mem_kv_kernel,
        out_shape=jax.ShapeDtypeStruct((n, w.shape[1]), F32),
        name="mem_kv_proj",
    )(mem, w)


def _qkv_proj_kernel(xt_ref, xbot_ref, w_ref, o_ref, kvt0_ref, kvt1_ref, kvt2_ref, xb_ref, stage_ref, tmp_ref):
    g_id = pl.program_id(1)

    @pl.when(g_id == 0)
    def _():
        half = ATTN_BLOCK // 2
        xb_ref[0:half, :] = xt_ref[...].astype(BF16)
        xb_ref[half:, :] = xbot_ref[...].astype(BF16)

    xb = xb_ref[...]
    last_block = pl.program_id(0) == pl.num_programs(0) - 1
    for g, (win, dil) in enumerate(DILATED_GROUPS):
        kvt_ref = (kvt0_ref, kvt1_ref, kvt2_ref)[g]

        @pl.when(g_id == g)
        def _(g=g, win=win, dil=dil, kvt_ref=kvt_ref):
            w_cols = lambda c: w_ref[:, c * WIDTH_A + g * GROUP_WIDTH_A:c * WIDTH_A + (g + 1) * GROUP_WIDTH_A]
            for c in range(3):
                w = w_cols(c)
                if c == 0:
                    w = w * jnp.asarray(HEAD_DIM_A ** -0.5, BF16)
                res = _dot(xb, w)
                if dil == 1:
                    o_ref[:, c * 256:(c + 1) * 256] = res.astype(BF16)
                    continue
                stage_ref[0] = res[:, :LANES]
                stage_ref[1] = res[:, LANES:]
                inner = DEINTERLEAVE_STRIDE if dil > DEINTERLEAVE_STRIDE else dil
                outer = dil // inner
                for s in range(2):
                    col = c * 256 + s * LANES
                    for blk in range(ATTN_BLOCK // win):
                        base = blk * win
                        src_ref = stage_ref
                        if outer > 1:
                            part = win // inner
                            for lo in range(inner):
                                tmp_ref[s, base + lo * part:base + (lo + 1) * part, :] = (
                                    stage_ref[s, pl.ds(base + lo, part, stride=inner), :])
                            src_ref = tmp_ref
                        for lo in range(inner):
                            for hi in range(outer):
                                first = base + lo * (win // inner) + hi if outer > 1 else base + lo
                                rows = src_ref[s, pl.ds(first, NK, stride=outer if outer > 1 else inner), :]
                                dst = base + (hi * inner + lo) * NK
                                o_ref[dst:dst + NK, col:col + LANES] = rows.astype(BF16)

            @pl.when(last_block)
            def _():
                tail = xb[ATTN_BLOCK - win:, :]
                kvt_ref[0:GROUP_WIDTH_A, :] = _dot(tail, w_cols(1)).T
                kvt_ref[GROUP_WIDTH_A:2 * GROUP_WIDTH_A, :] = _dot(tail, w_cols(2)).T


def _qkv_proj(x, w_qkv):
    s, d = x.shape
    n_blk = s // ATTN_BLOCK
    half = ATTN_BLOCK // 2
    x_half = lambda which: pl.BlockSpec(
        (half, d), lambda n, g: (2 * jnp.minimum(n + jnp.minimum(jnp.maximum(g - which, 0), 1), n_blk - 1) + which, 0))
    kvt_shapes = [(2 * GROUP_WIDTH_A, win) for win, _ in DILATED_GROUPS]
    return pl.pallas_call(
        _qkv_proj_kernel,
        grid=(n_blk, N_GROUPS_A),
        in_specs=[x_half(0), x_half(1), pl.BlockSpec(w_qkv.shape, lambda n, g: (0, 0), pipeline_mode=pl.Buffered(1))],
        out_specs=[pl.BlockSpec((None, ATTN_BLOCK, 3 * GROUP_WIDTH_A), lambda n, g: (g, n, 0))]
        + [pl.BlockSpec(shape, lambda n, g: (0, 0)) for shape in kvt_shapes],
        out_shape=[jax.ShapeDtypeStruct((N_GROUPS_A, s, 3 * GROUP_WIDTH_A), BF16)]
        + [jax.ShapeDtypeStruct(shape, F32) for shape in kvt_shapes],
        scratch_shapes=[pltpu.VMEM((ATTN_BLOCK, d), BF16),
                        pltpu.VMEM((GROUP_WIDTH_A // LANES, ATTN_BLOCK, LANES), F32),
                        pltpu.VMEM((GROUP_WIDTH_A // LANES, ATTN_BLOCK, LANES), F32)],
        compiler_params=_params("arbitrary", "arbitrary"),
        name="qkv_proj",
    )(x, x, w_qkv)


def _attn_units(units, prev_shifts):
    first_head = lax.broadcasted_iota(jnp.int32, (NK, LANES), 1) < HEAD_DIM_A
    row = lax.broadcasted_iota(jnp.int32, (2 * NK, NK), 0) % NK
    col = lax.broadcasted_iota(jnp.int32, (2 * NK, NK), 1)
    cur_ok = col <= row
    prev_ok = {}
    scores = []
    for (q, k_p, k_c, _, _), shift in zip(units, prev_shifts):
        zero = jnp.zeros_like(q)
        q_st = jnp.concatenate([jnp.where(first_head, q, zero), jnp.where(first_head, zero, q)], axis=0)
        if id(shift) not in prev_ok:
            prev_ok[id(shift)] = col >= row + shift
        scores.append((jnp.where(cur_ok, _dot_nt(q_st, k_c), NEG),
                       jnp.where(prev_ok[id(shift)], _dot_nt(q_st, k_p), NEG)))
    maxes = [jnp.max(jnp.maximum(s_c, s_p), axis=-1, keepdims=True) for s_c, s_p in scores]
    probs = [(jnp.exp(s_c - m), jnp.exp(s_p - m)) for (s_c, s_p), m in zip(scores, maxes)]
    dens = [jnp.sum(p_c + p_p, axis=-1, keepdims=True) for p_c, p_p in probs]
    results = []
    for (_, _, _, v_p, v_c), (p_c, p_p), m, den in zip(units, probs, maxes, dens):
        o_st = _dot(p_c.astype(BF16), v_c) + _dot(p_p.astype(BF16), v_p)
        pick = lambda t: jnp.where(first_head, jnp.broadcast_to(t[:NK], (NK, LANES)), jnp.broadcast_to(t[NK:], (NK, LANES)))
        den_sel = pick(den)
        results.append((pick(o_st) / den_sel, pick(m) + jnp.log(den_sel)))
    return results


def _dilated_attn_kernel(qkv0_ref, qkv1_ref, qkv2_ref, o_ref, kv0_ref, kv1_ref, kv2_ref, og_ref, lg_ref):
    n = pl.program_id(0)
    qkv_refs = (qkv0_ref, qkv1_ref, qkv2_ref)
    kv_refs = (kv0_ref, kv1_ref, kv2_ref)
    n_sub = ATTN_BLOCK // NK
    n_hp = GROUP_WIDTH_A // LANES

    for (win, _), qkv_ref, kv_ref in zip(DILATED_GROUPS, qkv_refs, kv_refs):
        @pl.when(n == 0)
        def _(kv_ref=kv_ref, win=win):
            kv_ref[0:win, :] = jnp.zeros((win, 2 * GROUP_WIDTH_A), BF16)

        kv_ref[win:win + ATTN_BLOCK, :] = qkv_ref[:, GROUP_WIDTH_A:]

    def sub_block(t, carry):
        off = pl.multiple_of(t * NK, NK)
        units, shifts, dsts = [], [], []
        for g, (win, dil) in enumerate(DILATED_GROUPS):
            win_blk = t // dil
            res = t - win_blk * dil
            has_prev = (n * (ATTN_BLOCK // win) + win_blk) > 0
            nat = win_blk * win + res
            shift = jnp.where(has_prev, 0, 2 * NK)
            for hp in range(n_hp):
                ks = slice(hp * LANES, (hp + 1) * LANES)
                vs = slice(GROUP_WIDTH_A + hp * LANES, GROUP_WIDTH_A + (hp + 1) * LANES)
                units.append((qkv_refs[g][pl.ds(off, NK), ks],
                              kv_refs[g][pl.ds(off, NK), ks], kv_refs[g][pl.ds(off + win, NK), ks],
                              kv_refs[g][pl.ds(off, NK), vs], kv_refs[g][pl.ds(off + win, NK), vs]))
                shifts.append(shift)
                dsts.append((g, hp, pl.ds(off, NK) if dil == 1 else pl.ds(nat, NK, stride=dil)))
        for (g, hp, dst), (out, lse) in zip(dsts, _attn_units(units, shifts)):
            og_ref[g, hp, dst, :] = out
            lg_ref[g, hp, dst, :] = lse
        return carry

    lax.fori_loop(0, n_sub, sub_block, 0)
    for (win, _), kv_ref in zip(DILATED_GROUPS, kv_refs):
        kv_ref[0:win, :] = kv_ref[ATTN_BLOCK:ATTN_BLOCK + win, :]

    def merge(t, carry):
        off = pl.multiple_of(t * NK, NK)
        for hp in range(GROUP_WIDTH_A // LANES):
            lses = [lg_ref[g, hp, pl.ds(off, NK), :] for g in range(N_GROUPS_A)]
            m = jnp.maximum(jnp.maximum(lses[0], lses[1]), lses[2])
            es = [jnp.exp(l - m) for l in lses]
            num = sum(es[g] * og_ref[g, hp, pl.ds(off, NK), :] for g in range(N_GROUPS_A))
            o_ref[pl.ds(off, NK), hp * LANES:(hp + 1) * LANES] = (num / (es[0] + es[1] + es[2])).astype(BF16)
        return carry

    lax.fori_loop(0, n_sub, merge, 0)


def _dilated_attn(qkv):
    _, s, _ = qkv.shape
    n_blk = s // ATTN_BLOCK
    n_hp = GROUP_WIDTH_A // LANES
    in_specs = [pl.BlockSpec((None, ATTN_BLOCK, 3 * GROUP_WIDTH_A), functools.partial(lambda n, g: (g, n, 0), g=g))
                for g in range(N_GROUPS_A)]
    return pl.pallas_call(
        _dilated_attn_kernel,
        grid=(n_blk,),
        in_specs=in_specs,
        out_specs=pl.BlockSpec((ATTN_BLOCK, GROUP_WIDTH_A), lambda n: (n, 0)),
        out_shape=jax.ShapeDtypeStruct((s, GROUP_WIDTH_A), BF16),
        scratch_shapes=[pltpu.VMEM((win + ATTN_BLOCK, 2 * GROUP_WIDTH_A), BF16) for win, _ in DILATED_GROUPS]
        + [pltpu.VMEM((N_GROUPS_A, n_hp, ATTN_BLOCK, LANES), F32),
           pltpu.VMEM((N_GROUPS_A, n_hp, ATTN_BLOCK, LANES), F32)],
        compiler_params=_params("arbitrary"),
        name="dilated_attn",
    )(qkv, qkv, qkv)


def _to_heads(row, n_heads, head_dim):
    return jnp.concatenate([row[:, h * head_dim:(h + 1) * head_dim] for h in range(n_heads)], axis=0)


def _from_heads(t):
    return jnp.concatenate([t[h:h + 1] for h in range(t.shape[0])], axis=1)


def _window_cache_attention(c_ref, j, cols_ref, base, dil):
    win = c_ref.shape[-1]
    lc = min(win, SAMPLE_LANE_CHUNK)
    n_chunks = win // lc
    lane = lax.broadcasted_iota(jnp.int32, (1, win), 1)
    wanted = (lane % dil) == 0
    out = []
    for h in range(HEADS_PER_GROUP_A):
        lo = base + h * HEAD_DIM_A
        q, k_new, v_new = (cols_ref[lo + c * GROUP_WIDTH_A:lo + c * GROUP_WIDTH_A + HEAD_DIM_A, :] for c in range(3))
        q_wide = jnp.concatenate([q] * (lc // LANES), axis=1)
        parts = [jnp.sum(c_ref[j, 0, h, :, i * lc:(i + 1) * lc] * q_wide, axis=0, keepdims=True)
                 for i in range(n_chunks)]
        s = jnp.where(wanted, jnp.concatenate(parts, axis=1), NEG)
        s_n = jnp.sum(k_new[:, 0:1] * q[:, 0:1], axis=0, keepdims=True)
        m = jnp.maximum(jnp.max(s, axis=-1, keepdims=True), s_n)
        p = jnp.exp(s - m)
        p_n = jnp.exp(s_n - m)
        den = jnp.sum(p, axis=-1, keepdims=True) + p_n
        acc = p_n * v_new[:, 0:1]
        for i in range(n_chunks):
            acc = acc + jnp.sum(c_ref[j, 1, h, :, i * lc:(i + 1) * lc] * p[:, i * lc:(i + 1) * lc],
                                axis=-1, keepdims=True)
        out.append((acc / den, m + jnp.log(den)))
    return out


def _memory_cache_attention(cm_ref, j, q, s_ref):
    n_heads, head_dim = q.shape
    chunk = SAMPLE_MEM_CHUNK
    n_chunks = cm_ref.shape[1] // chunk
    m = jnp.full((n_heads, head_dim), NEG, F32)
    for c in range(n_chunks):
        pos = slice(c * chunk, (c + 1) * chunk)
        s = jnp.sum(cm_ref[j, pos, 0] * q[None], axis=-1, keepdims=True) * (head_dim ** -0.5)
        s = jnp.broadcast_to(s, (chunk, n_heads, head_dim))
        s_ref[pos] = s
        m = jnp.maximum(m, jnp.max(s, axis=0))
    den = jnp.zeros((n_heads, head_dim), F32)
    acc = jnp.zeros((n_heads, head_dim), F32)
    for c in range(n_chunks):
        pos = slice(c * chunk, (c + 1) * chunk)
        p = jnp.exp(s_ref[pos] - m[None])
        den = den + jnp.sum(p, axis=0)
        acc = acc + jnp.sum(p * cm_ref[j, pos, 1], axis=0)
    return acc / den


def _sample_attn_kernel(x_ref, wqkv_ref, wqm_ref, c0_ref, c1_ref, c2_ref, cm_ref,
                        oa_ref, om_ref, kvn_ref, proj_ref, projt_ref, cols_ref, oat_ref, ms_ref):
    step = pl.program_id(0)
    caches = (c0_ref, c1_ref, c2_ref)
    n_batch, d = x_ref.shape
    qkv_w = N_GROUPS_A * 3 * GROUP_WIDTH_A

    @pl.when(step == 0)
    def _():
        xb = jnp.concatenate([x_ref[...], jnp.zeros((LANES - n_batch, d), F32)], axis=0).astype(BF16)
        for part in range(3):
            h = _dot(xb, wqkv_ref[:, part * WIDTH_A:(part + 1) * WIDTH_A])
            if part == 0:
                h = h * (HEAD_DIM_A ** -0.5)
            for g in range(N_GROUPS_A):
                h_g = h[:, g * GROUP_WIDTH_A:(g + 1) * GROUP_WIDTH_A]
                dst = (3 * g + part) * GROUP_WIDTH_A
                proj_ref[:, dst:dst + GROUP_WIDTH_A] = h_g
                if part > 0:
                    kvn_ref[g, :, (part - 1) * GROUP_WIDTH_A:part * GROUP_WIDTH_A] = h_g[:n_batch]
        proj_ref[:, qkv_w:] = _dot(xb, wqm_ref[...])
        for t in range(qkv_w // LANES):
            projt_ref[t * LANES:(t + 1) * LANES, :] = proj_ref[:, t * LANES:(t + 1) * LANES].T.astype(BF16)
        oat_ref[...] = jnp.zeros(oat_ref.shape, F32)

    def one_row(j, carry):
        b = step * SAMPLE_BATCH_BLOCK + j
        pick_row = (lax.broadcasted_iota(jnp.int32, (LANES, LANES), 0) == b).astype(BF16)
        cols_ref[...] = _dot(projt_ref[...], pick_row)
        per_group = [_window_cache_attention(caches[g], j, cols_ref, g * 3 * GROUP_WIDTH_A, dil)
                     for g, (_, dil) in enumerate(DILATED_GROUPS)]
        this_lane = lax.broadcasted_iota(jnp.int32, (1, LANES), 1) == b
        for h in range(HEADS_PER_GROUP_A):
            outs = [per_group[g][h][0] for g in range(N_GROUPS_A)]
            lses = [per_group[g][h][1] for g in range(N_GROUPS_A)]
            m3 = jnp.maximum(jnp.maximum(lses[0], lses[1]), lses[2])
            es = [jnp.exp(l - m3) for l in lses]
            oa = (es[0] * outs[0] + es[1] * outs[1] + es[2] * outs[2]) / (es[0] + es[1] + es[2])
            rows = slice(h * HEAD_DIM_A, (h + 1) * HEAD_DIM_A)
            oat_ref[rows, :] = jnp.where(this_lane, oa, oat_ref[rows, :])

        qm = _to_heads(proj_ref[pl.ds(b, 1), qkv_w:], N_HEADS_M, HEAD_DIM_M)
        om_ref[pl.ds(b, 1), :] = _from_heads(_memory_cache_attention(cm_ref, j, qm, ms_ref))
        return carry

    lax.fori_loop(0, SAMPLE_BATCH_BLOCK, one_row, 0)

    @pl.when(step == pl.num_programs(0) - 1)
    def _():
        oa_ref[...] = oat_ref[...].T[:n_batch, :]


def _sample_attn(xs, w_qkv, w_rest, caches, cache_mem):
    n_batch, d = xs.shape
    assert n_batch <= LANES
    bb = SAMPLE_BATCH_BLOCK
    qkv_w = N_GROUPS_A * 3 * GROUP_WIDTH_A
    qm_col = 2 * WIDTH_B
    assert qm_col % WIDTH_M == 0
    w_specs = [pl.BlockSpec((d, qkv_w), lambda i: (0, 0), pipeline_mode=pl.Buffered(1)),
               pl.BlockSpec((d, WIDTH_M), lambda i: (0, qm_col // WIDTH_M), pipeline_mode=pl.Buffered(1))]
    cache_specs = [pl.BlockSpec((bb,) + c.shape[1:], lambda i: (i, 0, 0, 0, 0)) for c in caches]
    whole = lambda shape: pl.BlockSpec(shape, lambda i: (0,) * len(shape))
    return pl.pallas_call(
        _sample_attn_kernel,
        grid=(n_batch // bb,),
        in_specs=[whole(xs.shape)] + w_specs + cache_specs
        + [pl.BlockSpec((bb,) + cache_mem.shape[1:], lambda i: (i, 0, 0, 0, 0))],
        out_specs=[whole((n_batch, GROUP_WIDTH_A)), whole((n_batch, WIDTH_M)),
                   whole((N_GROUPS_A, n_batch, 2 * GROUP_WIDTH_A))],
        out_shape=[jax.ShapeDtypeStruct((n_batch, GROUP_WIDTH_A), F32),
                   jax.ShapeDtypeStruct((n_batch, WIDTH_M), F32),
                   jax.ShapeDtypeStruct((N_GROUPS_A, n_batch, 2 * GROUP_WIDTH_A), F32)],
        scratch_shapes=[pltpu.VMEM((LANES, qkv_w + WIDTH_M), F32),
                        pltpu.VMEM((qkv_w, LANES), BF16),
                        pltpu.VMEM((qkv_w, LANES), F32),
                        pltpu.VMEM((GROUP_WIDTH_A, LANES), F32),
                        pltpu.VMEM((cache_mem.shape[1], N_HEADS_M, HEAD_DIM_M), F32)],
        compiler_params=_params("arbitrary"),
        name="sample_attn",
    )(xs, w_qkv, w_rest, *caches, cache_mem)


def _mix_kernel(x_ref, oa_ref, mk_ref, mv_ref, xs_ref, oas_ref, oms_ref, w_ref, bgate_ref, lnvg_ref, lnvb_ref,
                ws_ref, bs_ref, wss_ref, bss_ref, wba_ref, wbb_ref, wbm_ref, wout_ref, ln1g_ref, ln1b_ref,
                x1_ref, x1s_ref, vrows_ref, *, alpha):
    shared = (w_ref, bgate_ref, lnvg_ref, lnvb_ref)
    tail = (wba_ref, wbb_ref, wbm_ref, wout_ref, ln1g_ref, ln1b_ref)
    is_sample_step = pl.program_id(0) == pl.num_programs(0) - 1

    @pl.when(jnp.logical_not(is_sample_step))
    def _():
        _mix_body(x_ref, oa_ref, (mk_ref, mv_ref), *shared, ws_ref, bs_ref, *tail, x1_ref, None,
                  sample=False, alpha=alpha)

    @pl.when(is_sample_step)
    def _():
        _mix_body(xs_ref, oas_ref, oms_ref, *shared, wss_ref, bss_ref, *tail, x1s_ref, vrows_ref,
                  sample=True, alpha=alpha)


def _mix_body(x_ref, oa_ref, mem_or_om, w_ref, bgate_ref, lnvg_ref, lnvb_ref, ws_ref, bs_ref,
              wba_ref, wbb_ref, wbm_ref, wout_ref, ln1g_ref, ln1b_ref, x1_ref, vrows_ref, *, sample, alpha):
    if sample:
        om_ref = mem_or_om
    else:
        mk_ref, mv_ref = mem_or_om
    d = x_ref.shape[1]
    rows = x_ref.shape[0]
    x = x_ref[...]
    xb = x.astype(BF16)
    col_u, col_v, col_qm, col_gate = 0, WIDTH_B, 2 * WIDTH_B, 2 * WIDTH_B + WIDTH_M

    h_all = _dot(xb, w_ref[...])

    def gate(k):
        z = h_all[:, col_gate + k * d:col_gate + (k + 1) * d]
        return jax.nn.sigmoid(z + bgate_ref[k:k + 1, :])

    mixed = gate(0) * _dot(oa_ref[...].astype(BF16), wba_ref[...])

    u = jax.nn.gelu(h_all[:, col_u:col_u + WIDTH_B])
    v = _layer_norm(jax.nn.gelu(h_all[:, col_v:col_v + WIDTH_B]), lnvg_ref[...], lnvb_ref[...])
    if sample:
        vrows_ref[...] = v
        spatial = v.astype(BF16).astype(F32) * ws_ref[...].astype(BF16).astype(F32) + bs_ref[...]
    else:
        r_i = lax.broadcasted_iota(jnp.int32, (N_GROUPS_B * CHUNK, CHUNK), 0)
        c_i = lax.broadcasted_iota(jnp.int32, (N_GROUPS_B * CHUNK, CHUNK), 1)
        w_s = jnp.where((r_i % CHUNK) >= c_i, ws_ref[...], 0.0).astype(BF16)
        lane = lax.broadcasted_iota(jnp.int32, (CHUNK, WIDTH_B), 1)
        in_group = [(lane >= gb * GROUP_DIM_B) & (lane < (gb + 1) * GROUP_DIM_B) for gb in range(N_GROUPS_B)]
        vb = v.astype(BF16)
        parts = []
        for ch in range(rows // CHUNK):
            allg = _dot(w_s, vb[ch * CHUNK:(ch + 1) * CHUNK, :])
            sp = bs_ref[...]
            for gb in range(N_GROUPS_B):
                sp = sp + jnp.where(in_group[gb], allg[gb * CHUNK:(gb + 1) * CHUNK, :], 0.0)
            parts.append(sp)
        spatial = jnp.concatenate(parts, axis=0)
    o_b = u * spatial
    mixed = mixed + gate(1) * _dot(o_b.astype(BF16), wbb_ref[...])

    if sample:
        o_m = om_ref[...]
    else:
        qm = h_all[:, col_qm:col_qm + WIDTH_M].astype(BF16)
        heads = []
        for h in range(N_HEADS_M):
            hs = slice(h * HEAD_DIM_M, (h + 1) * HEAD_DIM_M)
            s = _dot_nt(qm[:, hs], mk_ref[:, hs]) * (HEAD_DIM_M ** -0.5)
            p = jnp.exp(s - jnp.max(s, axis=-1, keepdims=True))
            heads.append(_dot(p.astype(BF16), mv_ref[:, hs]) / jnp.sum(p, axis=-1, keepdims=True))
        o_m = jnp.concatenate(heads, axis=-1)
    mixed = mixed + gate(2) * _dot(o_m.astype(BF16), wbm_ref[...])

    x1_ref[...] = _layer_norm(alpha * x + _dot(mixed.astype(BF16), wout_ref[...]), ln1g_ref[...], ln1b_ref[...])


def _mix(x, o_a, mk, mv, xs, oa_s, om_s, w_rest, b_gate, ln_v_g, ln_v_b, w_s, b_s, w_s_row, b_s_row,
         w_ba, w_bb, w_bm, w_out, ln1_g, ln1_b, *, alpha):
    s, d = x.shape
    n_tiles = s // MIX_ROWS
    tile = lambda width: pl.BlockSpec((MIX_ROWS, width), lambda i: (jnp.minimum(i, n_tiles - 1), 0))
    whole = lambda shape: pl.BlockSpec(shape, lambda i: (0,) * len(shape))
    operands = (x, o_a, mk, mv, xs, oa_s, om_s, w_rest, b_gate, ln_v_g, ln_v_b, w_s, b_s, w_s_row, b_s_row,
                w_ba, w_bb, w_bm, w_out, ln1_g, ln1_b)
    n_s = xs.shape[0]
    return pl.pallas_call(
        functools.partial(_mix_kernel, alpha=alpha),
        grid=(n_tiles + 1,),
        in_specs=[tile(d), tile(GROUP_WIDTH_A)] + [_resident(t.shape) for t in operands[2:]],
        out_specs=[tile(d), whole((n_s, d)), whole((n_s, WIDTH_B))],
        out_shape=[jax.ShapeDtypeStruct((s, d), F32), jax.ShapeDtypeStruct((n_s, d), F32),
                   jax.ShapeDtypeStruct((n_s, WIDTH_B), F32)],
        compiler_params=_params("arbitrary"),
        name="mix",
    )(*operands)


def _ffn_chunks(d_ff):
    bounds = list(range(0, d_ff, FFN_COLS)) + [d_ff]
    return list(zip(bounds[:-1], bounds[1:]))


def _ffn_kernel(x1_ref, x1s_ref, st0_ref, st1_ref, wup_ref, cw_ref, cb_ref, wdown_ref, g_ref, b_ref,
                y_ref, a_ref, ys_ref, as_ref, abuf_ref, *, alpha):
    consts = (wup_ref, cw_ref, cb_ref, wdown_ref, g_ref, b_ref)
    is_sample_step = pl.program_id(0) == pl.num_programs(0) - 1

    @pl.when(jnp.logical_not(is_sample_step))
    def _():
        _ffn_body(x1_ref, None, None, *consts, y_ref, a_ref, abuf_ref, sample=False, alpha=alpha)

    @pl.when(is_sample_step)
    def _():
        _ffn_body(x1s_ref, st0_ref, st1_ref, *consts, ys_ref, as_ref, None, sample=True, alpha=alpha)


def _ffn_body(x1_ref, st0_ref, st1_ref, wup_ref, cw_ref, cb_ref, wdown_ref, g_ref, b_ref, y_ref, a_ref, abuf_ref,
              *, sample, alpha):
    rows = x1_ref.shape[0]
    d_ff = wdown_ref.shape[0]
    if not sample:
        @pl.when(pl.program_id(0) == 0)
        def _():
            abuf_ref[0:SUBLANES, :] = jnp.zeros((SUBLANES, d_ff), F32)

    sub = rows if sample else FFN_SUB_ROWS
    for r0 in range(0, rows, sub):
        x1 = x1_ref[r0:r0 + sub, :]
        xb = x1.astype(BF16)
        acc = jnp.zeros(x1.shape, F32)
        for lo, hi in _ffn_chunks(d_ff):
            cs = slice(lo, hi)
            a = _dot(xb, wup_ref[:, cs])
            val = _dot(xb, wup_ref[:, d_ff + lo:d_ff + hi])
            if sample:
                a_ref[:, cs] = a
                a_m2, a_m1 = st0_ref[:, cs], st1_ref[:, cs]
            else:
                first = SUBLANES + r0
                abuf_ref[first:first + sub, cs] = a
                a_m1 = abuf_ref[first - 1:first - 1 + sub, cs]
                a_m2 = abuf_ref[first - 2:first - 2 + sub, cs]
            conv = cb_ref[:, cs] + cw_ref[0:1, cs] * a_m2 + cw_ref[1:2, cs] * a_m1 + cw_ref[2:3, cs] * a
            h = jax.nn.gelu(conv) * val
            acc = acc + _dot(h.astype(BF16), wdown_ref[cs, :])
        y_ref[r0:r0 + sub, :] = _layer_norm(alpha * x1 + acc, g_ref[...], b_ref[...])
    if not sample:
        tail = abuf_ref[rows:rows + SUBLANES, :]
        abuf_ref[0:SUBLANES, :] = tail
        a_ref[...] = tail


def _ffn(x1, x1_s, state, w_up, conv_w, conv_b, w_down, ln2_g, ln2_b, *, alpha):
    s, d = x1.shape
    n_s = x1_s.shape[0]
    d_ff = w_down.shape[0]
    n_tiles = s // FFN_ROWS
    tile = lambda width: pl.BlockSpec((FFN_ROWS, width), lambda i: (jnp.minimum(i, n_tiles - 1), 0))
    whole = lambda shape: pl.BlockSpec(shape, lambda i: (0,) * len(shape))
    operands = (x1, x1_s, state[:, 0], state[:, 1], w_up, conv_w, conv_b, w_down, ln2_g, ln2_b)
    return pl.pallas_call(
        functools.partial(_ffn_kernel, alpha=alpha),
        grid=(n_tiles + 1,),
        in_specs=[tile(d)] + [_resident(t.shape) for t in operands[1:]],
        out_specs=[tile(d), whole((SUBLANES, d_ff)), whole((n_s, d)), whole((n_s, d_ff))],
        out_shape=[jax.ShapeDtypeStruct((s, d), F32), jax.ShapeDtypeStruct((SUBLANES, d_ff), F32),
                   jax.ShapeDtypeStruct((n_s, d), F32), jax.ShapeDtypeStruct((n_s, d_ff), F32)],
        scratch_shapes=[pltpu.VMEM((FFN_ROWS + SUBLANES, d_ff), F32)],
        compiler_params=_params("arbitrary"),
        name="ffn",
    )(*operands)


def kernel(x_prompt, x_sample, mem_prompt, cache_win128_kv, cache_win512_kv, cache_win2048_kv, cache_mem_kv, state_ffn_conv, w_in, b_gate, ln_v_g, ln_v_b, w_spatial, b_spatial, w_mem_kv, w_branch_a, w_branch_b, w_branch_m, w_out, ln1_g, ln1_b, w_up, conv_w, conv_b, w_down, ln2_g, ln2_b):
    depth = w_in.shape[0]
    batch, seq, d_model = x_prompt.shape
    dec_batch, dec_seq, _ = x_sample.shape
    assert batch == 1 and dec_seq == 1 and seq % ATTN_BLOCK == 0
    alpha = (2.0 * depth) ** 0.25
    win_caches = (cache_win128_kv, cache_win512_kv, cache_win2048_kv)
    for cache, (win, _) in zip(win_caches, DILATED_GROUPS):
        assert cache.shape[2] == win and PAST_LEN >= win

    yp = x_prompt.reshape(seq, d_model)
    ys = x_sample.reshape(dec_batch, d_model)
    win_p = [[] for _ in range(N_GROUPS_A)]
    win_s = [[] for _ in range(N_GROUPS_A)]
    mem_p, conv_p, gmlp_s, conv_s = [], [], [], []
    for l in range(depth):
        w_qkv = w_in[l][:, :3 * WIDTH_A].astype(BF16)
        w_rest = w_in[l][:, 3 * WIDTH_A:].astype(BF16)
        row2 = lambda t: t.reshape(1, -1)
        per_lane = lambda t: jnp.repeat(t, GROUP_DIM_B, axis=-1)
        consts_tail = (w_branch_a[l].astype(BF16), w_branch_b[l].astype(BF16), w_branch_m[l].astype(BF16),
                       w_out[l].astype(BF16), row2(ln1_g[l]), row2(ln1_b[l]))
        ffn_consts = (w_up[l].astype(BF16), conv_w[l], row2(conv_b[l]), w_down[l].astype(BF16),
                      row2(ln2_g[l]), row2(ln2_b[l]))

        mkv = _mem_kv_proj(mem_prompt[0], w_mem_kv[l].astype(BF16))
        mem_p.append(mkv.reshape(1, -1, 2, N_HEADS_M, HEAD_DIM_M))
        qkv, *kv_tails = _qkv_proj(yp, w_qkv)
        for g, (win, _) in enumerate(DILATED_GROUPS):
            tail = kv_tails[g].reshape(2, HEADS_PER_GROUP_A, HEAD_DIM_A, win)
            win_p[g].append(jnp.transpose(tail, (3, 0, 1, 2))[None])
        o_a = _dilated_attn(qkv)
        caches = [jnp.transpose(c[l], (0, 2, 3, 4, 1)) for c in win_caches]
        oa_s, om_s, kv_new = _sample_attn(ys, w_qkv, w_rest, caches, cache_mem_kv[l])
        for g in range(N_GROUPS_A):
            win_s[g].append(kv_new[g].reshape(dec_batch, 1, 2, HEADS_PER_GROUP_A, HEAD_DIM_A))

        x1, x1_s, v_rows = _mix(yp, o_a, mkv[:, :WIDTH_M].astype(BF16), mkv[:, WIDTH_M:].astype(BF16), ys, oa_s, om_s,
                                w_rest, b_gate[l], row2(ln_v_g[l]), row2(ln_v_b[l]),
                                w_spatial[l].reshape(N_GROUPS_B * CHUNK, CHUNK), per_lane(b_spatial[l].T),
                                row2(per_lane(w_spatial[l][:, 0, 0])), row2(per_lane(b_spatial[l][:, 0])),
                                *consts_tail, alpha=alpha)
        gmlp_s.append(v_rows.reshape(dec_batch, 1, WIDTH_B))
        yp_next, a_tail, ys_next, a_s = _ffn(x1, x1_s, state_ffn_conv[l], *ffn_consts, alpha=alpha)
        conv_p.append(a_tail[SUBLANES - (CONV_W - 1):][None])
        conv_s.append(jnp.stack([state_ffn_conv[l][:, 1], a_s], axis=1))
        yp, ys = yp_next, ys_next

    return (yp.reshape(batch, seq, d_model), ys.reshape(dec_batch, dec_seq, d_model),
            jnp.stack(win_p[0]), jnp.stack(win_p[1]), jnp.stack(win_p[2]),
            jnp.stack(mem_p), jnp.stack(conv_p),
            jnp.stack(win_s[0]), jnp.stack(win_s[1]), jnp.stack(win_s[2]),
            jnp.stack(gmlp_s), jnp.stack(conv_s))
```

```python
import functools

import jax
import jax.numpy as jnp
from jax import lax
from jax.experimental import pallas as pl
from jax.experimental.pallas import tpu as pltpu

BF16 = jnp.bfloat16
F32 = jnp.float32

HEAD_DIM_A = 64
HEADS_PER_GROUP_A = 4
DILATED_GROUPS = ((128, 1), (512, 4), (2048, 16))
N_GROUPS_A = len(DILATED_GROUPS)
GROUP_WIDTH_A = HEADS_PER_GROUP_A * HEAD_DIM_A
WIDTH_A = N_GROUPS_A * GROUP_WIDTH_A
NK = 128
CHUNK = 128
N_GROUPS_B = 4
WIDTH_B = 768
GROUP_DIM_B = WIDTH_B // N_GROUPS_B
N_HEADS_M = 4
HEAD_DIM_M = 128
WIDTH_M = N_HEADS_M * HEAD_DIM_M
N_BRANCH = 3
CONV_W = 3
LN_EPS = 1e-5
NEG = -1e30
PAST_LEN = 16384

LANES = 128
SUBLANES = 8
VMEM_LIMIT_BYTES = 58 * 1024 * 1024

ATTN_BLOCK = 2048
DEINTERLEAVE_STRIDE = 4
MIX_ROWS = 512
FFN_ROWS = 512
FFN_SUB_ROWS = 512
FFN_COLS = 2816
SAMPLE_BATCH_BLOCK = 2
SAMPLE_LANE_CHUNK = 512
SAMPLE_MEM_CHUNK = 32

_NT = (((1,), (1,)), ((), ()))


def _dot(a, b):
    return jnp.dot(a, b, preferred_element_type=F32)


def _dot_nt(a, b):
    return lax.dot_general(a, b, _NT, preferred_element_type=F32)


def _layer_norm(x, g, b):
    mu = jnp.mean(x, axis=-1, keepdims=True)
    var = jnp.mean(jnp.square(x - mu), axis=-1, keepdims=True)
    return (x - mu) * lax.rsqrt(var + LN_EPS) * g + b


def _params(*semantics):
    return pltpu.CompilerParams(dimension_semantics=semantics, vmem_limit_bytes=VMEM_LIMIT_BYTES)


def _resident(shape):
    zeros = (0,) * len(shape)
    return pl.BlockSpec(shape, lambda *_: zeros, pipeline_mode=pl.Buffered(1))


def _mem_kv_kernel(mem_ref, w_ref, o_ref):
    o_ref[...] = _dot(mem_ref[...].astype(BF16), w_ref[...])


def _mem_kv_proj(mem, w):
    n, d = mem.shape
    return pl.pallas_call(
        _mem_kv_kernel,
        out_shape=jax.ShapeDtypeStruct((n, w.shape[1]), F32),
        name="mem_kv_proj",
    )(mem, w)


def _qkv_proj_kernel(xt_ref, xbot_ref, w_ref, o_ref, kvt0_ref, kvt1_ref, kvt2_ref, xb_ref, stage_ref, tmp_ref):
    g_id = pl.program_id(1)

    @pl.when(g_id == 0)
    def _():
        half = ATTN_BLOCK // 2
        xb_ref[0:half, :] = xt_ref[...].astype(BF16)
        xb_ref[half:, :] = xbot_ref[...].astype(BF16)

    xb = xb_ref[...]
    last_block = pl.program_id(0) == pl.num_programs(0) - 1
    for g, (win, dil) in enumerate(DILATED_GROUPS):
        kvt_ref = (kvt0_ref, kvt1_ref, kvt2_ref)[g]

        @pl.when(g_id == g)
        def _(g=g, win=win, dil=dil, kvt_ref=kvt_ref):
            w_cols = lambda c: w_ref[:, c * WIDTH_A + g * GROUP_WIDTH_A:c * WIDTH_A + (g + 1) * GROUP_WIDTH_A]
            for c in range(3):
                w = w_cols(c)
                if c == 0:
                    w = w * jnp.asarray(HEAD_DIM_A ** -0.5, BF16)
                res = _dot(xb, w)
                if dil == 1:
                    o_ref[:, c * 256:(c + 1) * 256] = res.astype(BF16)
                    continue
                stage_ref[0] = res[:, :LANES]
                stage_ref[1] = res[:, LANES:]
                inner = DEINTERLEAVE_STRIDE if dil > DEINTERLEAVE_STRIDE else dil
                outer = dil // inner
                for s in range(2):
                    col = c * 256 + s * LANES
                    for blk in range(ATTN_BLOCK // win):
                        base = blk * win
                        src_ref = stage_ref
                        if outer > 1:
                            part = win // inner
                            for lo in range(inner):
                                tmp_ref[s, base + lo * part:base + (lo + 1) * part, :] = (
                                    stage_ref[s, pl.ds(base + lo, part, stride=inner), :])
                            src_ref = tmp_ref
                        for lo in range(inner):
                            for hi in range(outer):
                                first = base + lo * (win // inner) + hi if outer > 1 else base + lo
                                rows = src_ref[s, pl.ds(first, NK, stride=outer if outer > 1 else inner), :]
                                dst = base + (hi * inner + lo) * NK
                                o_ref[dst:dst + NK, col:col + LANES] = rows.astype(BF16)

            @pl.when(last_block)
            def _():
                tail = xb[ATTN_BLOCK - win:, :]
                kvt_ref[0:GROUP_WIDTH_A, :] = _dot(tail, w_cols(1)).T
                kvt_ref[GROUP_WIDTH_A:2 * GROUP_WIDTH_A, :] = _dot(tail, w_cols(2)).T


def _qkv_proj(x, w_all):
    s, d = x.shape
    n_blk = s // ATTN_BLOCK
    half = ATTN_BLOCK // 2
    x_half = lambda which: pl.BlockSpec(
        (half, d), lambda n, g: (2 * jnp.minimum(n + jnp.minimum(jnp.maximum(g - which, 0), 1), n_blk - 1) + which, 0))
    kvt_shapes = [(2 * GROUP_WIDTH_A, win) for win, _ in DILATED_GROUPS]
    return pl.pallas_call(
        _qkv_proj_kernel,
        grid=(n_blk, N_GROUPS_A),
        in_specs=[x_half(0), x_half(1),
                  pl.BlockSpec((d, 3 * WIDTH_A), lambda n, g: (0, 0), pipeline_mode=pl.Buffered(1))],
        out_specs=[pl.BlockSpec((None, ATTN_BLOCK, 3 * GROUP_WIDTH_A), lambda n, g: (g, n, 0))]
        + [pl.BlockSpec(shape, lambda n, g: (0, 0)) for shape in kvt_shapes],
        out_shape=[jax.ShapeDtypeStruct((N_GROUPS_A, s, 3 * GROUP_WIDTH_A), BF16)]
        + [jax.ShapeDtypeStruct(shape, F32) for shape in kvt_shapes],
        scratch_shapes=[pltpu.VMEM((ATTN_BLOCK, d), BF16),
                        pltpu.VMEM((GROUP_WIDTH_A // LANES, ATTN_BLOCK, LANES), F32),
                        pltpu.VMEM((GROUP_WIDTH_A // LANES, ATTN_BLOCK, LANES), F32)],
        compiler_params=_params("arbitrary", "arbitrary"),
        name="qkv_proj",
    )(x, x, w_all)


def _attn_units(units, prev_shifts):
    first_head = lax.broadcasted_iota(jnp.int32, (NK, LANES), 1) < HEAD_DIM_A
    row = lax.broadcasted_iota(jnp.int32, (2 * NK, NK), 0) % NK
    col = lax.broadcasted_iota(jnp.int32, (2 * NK, NK), 1)
    cur_ok = col <= row
    prev_ok = {}
    scores = []
    for (q, k_p, k_c, _, _), shift in zip(units, prev_shifts):
        zero = jnp.zeros_like(q)
        q_st = jnp.concatenate([jnp.where(first_head, q, zero), jnp.where(first_head, zero, q)], axis=0)
        if id(shift) not in prev_ok:
            prev_ok[id(shift)] = col >= row + shift
        scores.append((jnp.where(cur_ok, _dot_nt(q_st, k_c), NEG),
                       jnp.where(prev_ok[id(shift)], _dot_nt(q_st, k_p), NEG)))
    maxes = [jnp.max(jnp.maximum(s_c, s_p), axis=-1, keepdims=True) for s_c, s_p in scores]
    probs = [(jnp.exp(s_c - m), jnp.exp(s_p - m)) for (s_c, s_p), m in zip(scores, maxes)]
    dens = [jnp.sum(p_c + p_p, axis=-1, keepdims=True) for p_c, p_p in probs]
    results = []
    for (_, _, _, v_p, v_c), (p_c, p_p), m, den in zip(units, probs, maxes, dens):
        o_st = _dot(p_c.astype(BF16), v_c) + _dot(p_p.astype(BF16), v_p)
        pick = lambda t: jnp.where(first_head, jnp.broadcast_to(t[:NK], (NK, LANES)), jnp.broadcast_to(t[NK:], (NK, LANES)))
        den_sel = pick(den)
        results.append((pick(o_st) / den_sel, pick(m) + jnp.log(den_sel)))
    return results


def _dilated_attn_kernel(qkv0_ref, qkv1_ref, qkv2_ref, o_ref, kv0_ref, kv1_ref, kv2_ref, og_ref, lg_ref):
    n = pl.program_id(0)
    qkv_refs = (qkv0_ref, qkv1_ref, qkv2_ref)
    kv_refs = (kv0_ref, kv1_ref, kv2_ref)
    n_sub = ATTN_BLOCK // NK
    n_hp = GROUP_WIDTH_A // LANES

    for (win, _), qkv_ref, kv_ref in zip(DILATED_GROUPS, qkv_refs, kv_refs):
        @pl.when(n == 0)
        def _(kv_ref=kv_ref, win=win):
            kv_ref[0:win, :] = jnp.zeros((win, 2 * GROUP_WIDTH_A), BF16)

        kv_ref[win:win + ATTN_BLOCK, :] = qkv_ref[:, GROUP_WIDTH_A:]

    def sub_block(t, carry):
        off = pl.multiple_of(t * NK, NK)
        units, shifts, dsts = [], [], []
        for g, (win, dil) in enumerate(DILATED_GROUPS):
            win_blk = t // dil
            res = t - win_blk * dil
            has_prev = (n * (ATTN_BLOCK // win) + win_blk) > 0
            nat = win_blk * win + res
            shift = jnp.where(has_prev, 0, 2 * NK)
            for hp in range(n_hp):
                ks = slice(hp * LANES, (hp + 1) * LANES)
                vs = slice(GROUP_WIDTH_A + hp * LANES, GROUP_WIDTH_A + (hp + 1) * LANES)
                units.append((qkv_refs[g][pl.ds(off, NK), ks],
                              kv_refs[g][pl.ds(off, NK), ks], kv_refs[g][pl.ds(off + win, NK), ks],
                              kv_refs[g][pl.ds(off, NK), vs], kv_refs[g][pl.ds(off + win, NK), vs]))
                shifts.append(shift)
                dsts.append((g, hp, pl.ds(off, NK) if dil == 1 else pl.ds(nat, NK, stride=dil)))
        for (g, hp, dst), (out, lse) in zip(dsts, _attn_units(units, shifts)):
            og_ref[g, hp, dst, :] = out
            lg_ref[g, hp, dst, :] = lse
        return carry

    lax.fori_loop(0, n_sub, sub_block, 0)
    for (win, _), kv_ref in zip(DILATED_GROUPS, kv_refs):
        kv_ref[0:win, :] = kv_ref[ATTN_BLOCK:ATTN_BLOCK + win, :]

    def merge(t, carry):
        off = pl.multiple_of(t * NK, NK)
        for hp in range(GROUP_WIDTH_A // LANES):
            lses = [lg_ref[g, hp, pl.ds(off, NK), :] for g in range(N_GROUPS_A)]
            m = jnp.maximum(jnp.maximum(lses[0], lses[1]), lses[2])
            es = [jnp.exp(l - m) for l in lses]
            num = sum(es[g] * og_ref[g, hp, pl.ds(off, NK), :] for g in range(N_GROUPS_A))
            o_ref[pl.ds(off, NK), hp * LANES:(hp + 1) * LANES] = (num / (es[0] + es[1] + es[2])).astype(BF16)
        return carry

    lax.fori_loop(0, n_sub, merge, 0)


def _dilated_attn(qkv):
    _, s, _ = qkv.shape
    n_blk = s // ATTN_BLOCK
    n_hp = GROUP_WIDTH_A // LANES
    in_specs = [pl.BlockSpec((None, ATTN_BLOCK, 3 * GROUP_WIDTH_A), functools.partial(lambda n, g: (g, n, 0), g=g))
                for g in range(N_GROUPS_A)]
    return pl.pallas_call(
        _dilated_attn_kernel,
        grid=(n_blk,),
        in_specs=in_specs,
        out_specs=pl.BlockSpec((ATTN_BLOCK, GROUP_WIDTH_A), lambda n: (n, 0)),
        out_shape=jax.ShapeDtypeStruct((s, GROUP_WIDTH_A), BF16),
        scratch_shapes=[pltpu.VMEM((win + ATTN_BLOCK, 2 * GROUP_WIDTH_A), BF16) for win, _ in DILATED_GROUPS]
        + [pltpu.VMEM((N_GROUPS_A, n_hp, ATTN_BLOCK, LANES), F32),
           pltpu.VMEM((N_GROUPS_A, n_hp, ATTN_BLOCK, LANES), F32)],
        compiler_params=_params("arbitrary"),
        name="dilated_attn",
    )(qkv, qkv, qkv)


def _to_heads(row, n_heads, head_dim):
    return jnp.concatenate([row[:, h * head_dim:(h + 1) * head_dim] for h in range(n_heads)], axis=0)


def _from_heads(t):
    return jnp.concatenate([t[h:h + 1] for h in range(t.shape[0])], axis=1)


def _window_cache_attention(c_ref, j, cols_ref, base, dil):
    win = c_ref.shape[-1]
    lc = min(win, SAMPLE_LANE_CHUNK)
    n_chunks = win // lc
    lane = lax.broadcasted_iota(jnp.int32, (1, win), 1)
    wanted = (lane % dil) == 0
    out = []
    for h in range(HEADS_PER_GROUP_A):
        lo = base + h * HEAD_DIM_A
        q, k_new, v_new = (cols_ref[lo + c * GROUP_WIDTH_A:lo + c * GROUP_WIDTH_A + HEAD_DIM_A, :] for c in range(3))
        q_wide = jnp.concatenate([q] * (lc // LANES), axis=1)
        parts = [jnp.sum(c_ref[j, 0, h, :, i * lc:(i + 1) * lc] * q_wide, axis=0, keepdims=True)
                 for i in range(n_chunks)]
        s = jnp.where(wanted, jnp.concatenate(parts, axis=1), NEG)
        s_n = jnp.sum(k_new[:, 0:1] * q[:, 0:1], axis=0, keepdims=True)
        m = jnp.maximum(jnp.max(s, axis=-1, keepdims=True), s_n)
        p = jnp.exp(s - m)
        p_n = jnp.exp(s_n - m)
        den = jnp.sum(p, axis=-1, keepdims=True) + p_n
        acc = p_n * v_new[:, 0:1]
        for i in range(n_chunks):
            acc = acc + jnp.sum(c_ref[j, 1, h, :, i * lc:(i + 1) * lc] * p[:, i * lc:(i + 1) * lc],
                                axis=-1, keepdims=True)
        out.append((acc / den, m + jnp.log(den)))
    return out


def _memory_cache_attention(cm_ref, j, q, s_ref):
    n_heads, head_dim = q.shape
    chunk = SAMPLE_MEM_CHUNK
    n_chunks = cm_ref.shape[1] // chunk
    m = jnp.full((n_heads, head_dim), NEG, F32)
    for c in range(n_chunks):
        pos = slice(c * chunk, (c + 1) * chunk)
        s = jnp.sum(cm_ref[j, pos, 0] * q[None], axis=-1, keepdims=True) * (head_dim ** -0.5)
        s = jnp.broadcast_to(s, (chunk, n_heads, head_dim))
        s_ref[pos] = s
        m = jnp.maximum(m, jnp.max(s, axis=0))
    den = jnp.zeros((n_heads, head_dim), F32)
    acc = jnp.zeros((n_heads, head_dim), F32)
    for c in range(n_chunks):
        pos = slice(c * chunk, (c + 1) * chunk)
        p = jnp.exp(s_ref[pos] - m[None])
        den = den + jnp.sum(p, axis=0)
        acc = acc + jnp.sum(p * cm_ref[j, pos, 1], axis=0)
    return acc / den


def _sample_attn_kernel(x_ref, wqkv_ref, wqm_ref, c0_ref, c1_ref, c2_ref, cm_ref,
                        oa_ref, om_ref, kvn_ref, proj_ref, projt_ref, cols_ref, oat_ref, ms_ref):
    step = pl.program_id(0)
    caches = (c0_ref, c1_ref, c2_ref)
    n_batch, d = x_ref.shape
    qkv_w = N_GROUPS_A * 3 * GROUP_WIDTH_A

    @pl.when(step == 0)
    def _():
        xb = jnp.concatenate([x_ref[...], jnp.zeros((LANES - n_batch, d), F32)], axis=0).astype(BF16)
        for part in range(3):
            h = _dot(xb, wqkv_ref[:, part * WIDTH_A:(part + 1) * WIDTH_A])
            if part == 0:
                h = h * (HEAD_DIM_A ** -0.5)
            for g in range(N_GROUPS_A):
                h_g = h[:, g * GROUP_WIDTH_A:(g + 1) * GROUP_WIDTH_A]
                dst = (3 * g + part) * GROUP_WIDTH_A
                proj_ref[:, dst:dst + GROUP_WIDTH_A] = h_g
                if part > 0:
                    kvn_ref[g, :, (part - 1) * GROUP_WIDTH_A:part * GROUP_WIDTH_A] = h_g[:n_batch]
        proj_ref[:, qkv_w:] = _dot(xb, wqm_ref[:, :WIDTH_M])
        for t in range(qkv_w // LANES):
            projt_ref[t * LANES:(t + 1) * LANES, :] = proj_ref[:, t * LANES:(t + 1) * LANES].T.astype(BF16)
        oat_ref[...] = jnp.zeros(oat_ref.shape, F32)

    def one_row(j, carry):
        b = step * SAMPLE_BATCH_BLOCK + j
        pick_row = (lax.broadcasted_iota(jnp.int32, (LANES, LANES), 0) == b).astype(BF16)
        cols_ref[...] = _dot(projt_ref[...], pick_row)
        per_group = [_window_cache_attention(caches[g], j, cols_ref, g * 3 * GROUP_WIDTH_A, dil)
                     for g, (_, dil) in enumerate(DILATED_GROUPS)]
        this_lane = lax.broadcasted_iota(jnp.int32, (1, LANES), 1) == b
        for h in range(HEADS_PER_GROUP_A):
            outs = [per_group[g][h][0] for g in range(N_GROUPS_A)]
            lses = [per_group[g][h][1] for g in range(N_GROUPS_A)]
            m3 = jnp.maximum(jnp.maximum(lses[0], lses[1]), lses[2])
            es = [jnp.exp(l - m3) for l in lses]
            oa = (es[0] * outs[0] + es[1] * outs[1] + es[2] * outs[2]) / (es[0] + es[1] + es[2])
            rows = slice(h * HEAD_DIM_A, (h + 1) * HEAD_DIM_A)
            oat_ref[rows, :] = jnp.where(this_lane, oa, oat_ref[rows, :])

        qm = _to_heads(proj_ref[pl.ds(b, 1), qkv_w:], N_HEADS_M, HEAD_DIM_M)
        om_ref[pl.ds(b, 1), :] = _from_heads(_memory_cache_attention(cm_ref, j, qm, ms_ref))
        return carry

    lax.fori_loop(0, SAMPLE_BATCH_BLOCK, one_row, 0)

    @pl.when(step == pl.num_programs(0) - 1)
    def _():
        oa_ref[...] = oat_ref[...].T[:n_batch, :]


def _sample_attn(xs, w_all, caches, cache_mem):
    n_batch, d = xs.shape
    assert n_batch <= LANES
    bb = SAMPLE_BATCH_BLOCK
    qkv_w = N_GROUPS_A * 3 * GROUP_WIDTH_A
    qm_col = 3 * WIDTH_A + 2 * WIDTH_B
    assert qm_col % WIDTH_B == 0 and WIDTH_M <= WIDTH_B
    w_specs = [pl.BlockSpec((d, qkv_w), lambda i: (0, 0), pipeline_mode=pl.Buffered(1)),
               pl.BlockSpec((d, WIDTH_B), lambda i: (0, qm_col // WIDTH_B), pipeline_mode=pl.Buffered(1))]
    cache_specs = [pl.BlockSpec((bb,) + c.shape[1:], lambda i: (i, 0, 0, 0, 0)) for c in caches]
    whole = lambda shape: pl.BlockSpec(shape, lambda i: (0,) * len(shape))
    return pl.pallas_call(
        _sample_attn_kernel,
        grid=(n_batch // bb,),
        in_specs=[whole(xs.shape)] + w_specs + cache_specs
        + [pl.BlockSpec((bb,) + cache_mem.shape[1:], lambda i: (i, 0, 0, 0, 0))],
        out_specs=[whole((n_batch, GROUP_WIDTH_A)), whole((n_batch, WIDTH_M)),
                   whole((N_GROUPS_A, n_batch, 2 * GROUP_WIDTH_A))],
        out_shape=[jax.ShapeDtypeStruct((n_batch, GROUP_WIDTH_A), F32),
                   jax.ShapeDtypeStruct((n_batch, WIDTH_M), F32),
                   jax.ShapeDtypeStruct((N_GROUPS_A, n_batch, 2 * GROUP_WIDTH_A), F32)],
        scratch_shapes=[pltpu.VMEM((LANES, qkv_w + WIDTH_M), F32),
                        pltpu.VMEM((qkv_w, LANES), BF16),
                        pltpu.VMEM((qkv_w, LANES), F32),
                        pltpu.VMEM((GROUP_WIDTH_A, LANES), F32),
                        pltpu.VMEM((cache_mem.shape[1], N_HEADS_M, HEAD_DIM_M), F32)],
        compiler_params=_params("arbitrary"),
        name="sample_attn",
    )(xs, w_all, w_all, *caches, cache_mem)


def _mix_kernel(x_ref, oa_ref, mk_ref, mv_ref, xs_ref, oas_ref, oms_ref, w_ref, bgate_ref, lnvg_ref, lnvb_ref,
                ws_ref, bs_ref, wss_ref, bss_ref, wba_ref, wbb_ref, wbm_ref, wout_ref, ln1g_ref, ln1b_ref,
                x1_ref, x1s_ref, vrows_ref, *, alpha):
    shared = (w_ref, bgate_ref, lnvg_ref, lnvb_ref)
    tail = (wba_ref, wbb_ref, wbm_ref, wout_ref, ln1g_ref, ln1b_ref)
    is_sample_step = pl.program_id(0) == pl.num_programs(0) - 1

    @pl.when(jnp.logical_not(is_sample_step))
    def _():
        _mix_body(x_ref, oa_ref, (mk_ref, mv_ref), *shared, ws_ref, bs_ref, *tail, x1_ref, None,
                  sample=False, alpha=alpha)

    @pl.when(is_sample_step)
    def _():
        _mix_body(xs_ref, oas_ref, oms_ref, *shared, wss_ref, bss_ref, *tail, x1s_ref, vrows_ref,
                  sample=True, alpha=alpha)


def _mix_body(x_ref, oa_ref, mem_or_om, w_ref, bgate_ref, lnvg_ref, lnvb_ref, ws_ref, bs_ref,
              wba_ref, wbb_ref, wbm_ref, wout_ref, ln1g_ref, ln1b_ref, x1_ref, vrows_ref, *, sample, alpha):
    if sample:
        om_ref = mem_or_om
    else:
        mk_ref, mv_ref = mem_or_om
    d = x_ref.shape[1]
    rows = x_ref.shape[0]
    x = x_ref[...]
    xb = x.astype(BF16)
    col_u, col_v, col_qm, col_gate = 0, WIDTH_B, 2 * WIDTH_B, 2 * WIDTH_B + WIDTH_M

    h_all = _dot(xb, w_ref[:, 3 * WIDTH_A:])

    def gate(k):
        z = h_all[:, col_gate + k * d:col_gate + (k + 1) * d]
        return jax.nn.sigmoid(z + bgate_ref[k:k + 1, :])

    mixed = gate(0) * _dot(oa_ref[...].astype(BF16), wba_ref[...])

    u = jax.nn.gelu(h_all[:, col_u:col_u + WIDTH_B])
    v = _layer_norm(jax.nn.gelu(h_all[:, col_v:col_v + WIDTH_B]), lnvg_ref[...], lnvb_ref[...])
    if sample:
        vrows_ref[...] = v
        spatial = v.astype(BF16).astype(F32) * ws_ref[...].astype(BF16).astype(F32) + bs_ref[...]
    else:
        r_i = lax.broadcasted_iota(jnp.int32, (N_GROUPS_B * CHUNK, CHUNK), 0)
        c_i = lax.broadcasted_iota(jnp.int32, (N_GROUPS_B * CHUNK, CHUNK), 1)
        w_s = jnp.where((r_i % CHUNK) >= c_i, ws_ref[...], 0.0).astype(BF16)
        lane = lax.broadcasted_iota(jnp.int32, (CHUNK, WIDTH_B), 1)
        in_group = [(lane >= gb * GROUP_DIM_B) & (lane < (gb + 1) * GROUP_DIM_B) for gb in range(N_GROUPS_B)]
        vb = v.astype(BF16)
        parts = []
        for ch in range(rows // CHUNK):
            allg = _dot(w_s, vb[ch * CHUNK:(ch + 1) * CHUNK, :])
            sp = bs_ref[...]
            for gb in range(N_GROUPS_B):
                sp = sp + jnp.where(in_group[gb], allg[gb * CHUNK:(gb + 1) * CHUNK, :], 0.0)
            parts.append(sp)
        spatial = jnp.concatenate(parts, axis=0)
    o_b = u * spatial
    mixed = mixed + gate(1) * _dot(o_b.astype(BF16), wbb_ref[...])

    if sample:
        o_m = om_ref[...]
    else:
        qm = h_all[:, col_qm:col_qm + WIDTH_M].astype(BF16)
        heads = []
        for h in range(N_HEADS_M):
            hs = slice(h * HEAD_DIM_M, (h + 1) * HEAD_DIM_M)
            s = _dot_nt(qm[:, hs], mk_ref[:, hs]) * (HEAD_DIM_M ** -0.5)
            p = jnp.exp(s - jnp.max(s, axis=-1, keepdims=True))
            heads.append(_dot(p.astype(BF16), mv_ref[:, hs]) / jnp.sum(p, axis=-1, keepdims=True))
        o_m = jnp.concatenate(heads, axis=-1)
    mixed = mixed + gate(2) * _dot(o_m.astype(BF16), wbm_ref[...])

    x1_ref[...] = _layer_norm(alpha * x + _dot(mixed.astype(BF16), wout_ref[...]), ln1g_ref[...], ln1b_ref[...])


def _mix(x, o_a, mk, mv, xs, oa_s, om_s, w_all, b_gate, ln_v_g, ln_v_b, w_s, b_s, w_s_row, b_s_row,
         w_ba, w_bb, w_bm, w_out, ln1_g, ln1_b, *, alpha):
    s, d = x.shape
    n_tiles = s // MIX_ROWS
    tile = lambda width: pl.BlockSpec((MIX_ROWS, width), lambda i: (jnp.minimum(i, n_tiles - 1), 0))
    whole = lambda shape: pl.BlockSpec(shape, lambda i: (0,) * len(shape))
    operands = (x, o_a, mk, mv, xs, oa_s, om_s, w_all, b_gate, ln_v_g, ln_v_b, w_s, b_s, w_s_row, b_s_row,
                w_ba, w_bb, w_bm, w_out, ln1_g, ln1_b)
    n_s = xs.shape[0]
    return pl.pallas_call(
        functools.partial(_mix_kernel, alpha=alpha),
        grid=(n_tiles + 1,),
        in_specs=[tile(d), tile(GROUP_WIDTH_A)] + [_resident(t.shape) for t in operands[2:]],
        out_specs=[tile(d), whole((n_s, d)), whole((n_s, WIDTH_B))],
        out_shape=[jax.ShapeDtypeStruct((s, d), F32), jax.ShapeDtypeStruct((n_s, d), F32),
                   jax.ShapeDtypeStruct((n_s, WIDTH_B), F32)],
        compiler_params=_params("arbitrary"),
        name="mix",
    )(*operands)


def _ffn_chunks(d_ff):
    bounds = list(range(0, d_ff, FFN_COLS)) + [d_ff]
    return list(zip(bounds[:-1], bounds[1:]))


def _ffn_kernel(x1_ref, x1s_ref, st0_ref, st1_ref, wup_ref, cw_ref, cb_ref, wdown_ref, g_ref, b_ref,
                y_ref, a_ref, ys_ref, as_ref, abuf_ref, *, alpha):
    consts = (wup_ref, cw_ref, cb_ref, wdown_ref, g_ref, b_ref)
    is_sample_step = pl.program_id(0) == pl.num_programs(0) - 1

    @pl.when(jnp.logical_not(is_sample_step))
    def _():
        _ffn_body(x1_ref, None, None, *consts, y_ref, a_ref, abuf_ref, sample=False, alpha=alpha)

    @pl.when(is_sample_step)
    def _():
        _ffn_body(x1s_ref, st0_ref, st1_ref, *consts, ys_ref, as_ref, None, sample=True, alpha=alpha)


def _ffn_body(x1_ref, st0_ref, st1_ref, wup_ref, cw_ref, cb_ref, wdown_ref, g_ref, b_ref, y_ref, a_ref, abuf_ref,
              *, sample, alpha):
    rows = x1_ref.shape[0]
    d_ff = wdown_ref.shape[0]
    if not sample:
        @pl.when(pl.program_id(0) == 0)
        def _():
            abuf_ref[0:SUBLANES, :] = jnp.zeros((SUBLANES, d_ff), F32)

    sub = rows if sample else FFN_SUB_ROWS
    for r0 in range(0, rows, sub):
        x1 = x1_ref[r0:r0 + sub, :]
        xb = x1.astype(BF16)
        acc = jnp.zeros(x1.shape, F32)
        for lo, hi in _ffn_chunks(d_ff):
            cs = slice(lo, hi)
            a = _dot(xb, wup_ref[:, cs])
            val = _dot(xb, wup_ref[:, d_ff + lo:d_ff + hi])
            if sample:
                a_ref[:, cs] = a
                a_m2, a_m1 = st0_ref[:, cs], st1_ref[:, cs]
            else:
                first = SUBLANES + r0
                abuf_ref[first:first + sub, cs] = a
                a_m1 = abuf_ref[first - 1:first - 1 + sub, cs]
                a_m2 = abuf_ref[first - 2:first - 2 + sub, cs]
            conv = cb_ref[:, cs] + cw_ref[0:1, cs] * a_m2 + cw_ref[1:2, cs] * a_m1 + cw_ref[2:3, cs] * a
            h = jax.nn.gelu(conv) * val
            acc = acc + _dot(h.astype(BF16), wdown_ref[cs, :])
        y_ref[r0:r0 + sub, :] = _layer_norm(alpha * x1 + acc, g_ref[...], b_ref[...])
    if not sample:
        tail = abuf_ref[rows:rows + SUBLANES, :]
        abuf_ref[0:SUBLANES, :] = tail
        a_ref[...] = tail


def _ffn(x1, x1_s, state, w_up, conv_w, conv_b, w_down, ln2_g, ln2_b, *, alpha):
    s, d = x1.shape
    n_s = x1_s.shape[0]
    d_ff = w_down.shape[0]
    n_tiles = s // FFN_ROWS
    tile = lambda width: pl.BlockSpec((FFN_ROWS, width), lambda i: (jnp.minimum(i, n_tiles - 1), 0))
    whole = lambda shape: pl.BlockSpec(shape, lambda i: (0,) * len(shape))
    operands = (x1, x1_s, state[:, 0], state[:, 1], w_up, conv_w, conv_b, w_down, ln2_g, ln2_b)
    return pl.pallas_call(
        functools.partial(_ffn_kernel, alpha=alpha),
        grid=(n_tiles + 1,),
        in_specs=[tile(d)] + [_resident(t.shape) for t in operands[1:]],
        out_specs=[tile(d), whole((SUBLANES, d_ff)), whole((n_s, d)), whole((n_s, d_ff))],
        out_shape=[jax.ShapeDtypeStruct((s, d), F32), jax.ShapeDtypeStruct((SUBLANES, d_ff), F32),
                   jax.ShapeDtypeStruct((n_s, d), F32), jax.ShapeDtypeStruct((n_s, d_ff), F32)],
        scratch_shapes=[pltpu.VMEM((FFN_ROWS + SUBLANES, d_ff), F32)],
        compiler_params=_params("arbitrary"),
        name="ffn",
    )(*operands)


def kernel(x_prompt, x_sample, mem_prompt, cache_win128_kv, cache_win512_kv, cache_win2048_kv, cache_mem_kv, state_ffn_conv, w_in, b_gate, ln_v_g, ln_v_b, w_spatial, b_spatial, w_mem_kv, w_branch_a, w_branch_b, w_branch_m, w_out, ln1_g, ln1_b, w_up, conv_w, conv_b, w_down, ln2_g, ln2_b):
    depth = w_in.shape[0]
    batch, seq, d_model = x_prompt.shape
    dec_batch, dec_seq, _ = x_sample.shape
    assert batch == 1 and dec_seq == 1 and seq % ATTN_BLOCK == 0
    alpha = (2.0 * depth) ** 0.25
    win_caches = (cache_win128_kv, cache_win512_kv, cache_win2048_kv)
    for cache, (win, _) in zip(win_caches, DILATED_GROUPS):
        assert cache.shape[2] == win and PAST_LEN >= win

    yp = x_prompt.reshape(seq, d_model)
    ys = x_sample.reshape(dec_batch, d_model)
    win_p = [[] for _ in range(N_GROUPS_A)]
    win_s = [[] for _ in range(N_GROUPS_A)]
    mem_p, conv_p, gmlp_s, conv_s = [], [], [], []
    for l in range(depth):
        w_all = w_in[l].astype(BF16)
        row2 = lambda t: t.reshape(1, -1)
        per_lane = lambda t: jnp.repeat(t, GROUP_DIM_B, axis=-1)
        consts_tail = (w_branch_a[l].astype(BF16), w_branch_b[l].astype(BF16), w_branch_m[l].astype(BF16),
                       w_out[l].astype(BF16), row2(ln1_g[l]), row2(ln1_b[l]))
        ffn_consts = (w_up[l].astype(BF16), conv_w[l], row2(conv_b[l]), w_down[l].astype(BF16),
                      row2(ln2_g[l]), row2(ln2_b[l]))

        mkv = _mem_kv_proj(mem_prompt[0], w_mem_kv[l].astype(BF16))
        mem_p.append(mkv.reshape(1, -1, 2, N_HEADS_M, HEAD_DIM_M))
        qkv, *kv_tails = _qkv_proj(yp, w_all)
        for g, (win, _) in enumerate(DILATED_GROUPS):
            tail = kv_tails[g].reshape(2, HEADS_PER_GROUP_A, HEAD_DIM_A, win)
            win_p[g].append(jnp.transpose(tail, (3, 0, 1, 2))[None])
        o_a = _dilated_attn(qkv)
        caches = [jnp.transpose(c[l], (0, 2, 3, 4, 1)) for c in win_caches]
        oa_s, om_s, kv_new = _sample_attn(ys, w_all, caches, cache_mem_kv[l])
        for g in range(N_GROUPS_A):
            win_s[g].append(kv_new[g].reshape(dec_batch, 1, 2, HEADS_PER_GROUP_A, HEAD_DIM_A))

        x1, x1_s, v_rows = _mix(yp, o_a, mkv[:, :WIDTH_M].astype(BF16), mkv[:, WIDTH_M:].astype(BF16), ys, oa_s, om_s,
                                w_all, b_gate[l], row2(ln_v_g[l]), row2(ln_v_b[l]),
                                w_spatial[l].reshape(N_GROUPS_B * CHUNK, CHUNK), per_lane(b_spatial[l].T),
                                row2(per_lane(w_spatial[l][:, 0, 0])), row2(per_lane(b_spatial[l][:, 0])),
                                *consts_tail, alpha=alpha)
        gmlp_s.append(v_rows.reshape(dec_batch, 1, WIDTH_B))
        yp_next, a_tail, ys_next, a_s = _ffn(x1, x1_s, state_ffn_conv[l], *ffn_consts, alpha=alpha)
        conv_p.append(a_tail[SUBLANES - (CONV_W - 1):][None])
        conv_s.append(jnp.stack([state_ffn_conv[l][:, 1], a_s], axis=1))
        yp, ys = yp_next, ys_next

    return (yp.reshape(batch, seq, d_model), ys.reshape(dec_batch, dec_seq, d_model),
            jnp.stack(win_p[0]), jnp.stack(win_p[1]), jnp.stack(win_p[2]),
            jnp.stack(mem_p), jnp.stack(conv_p),
            jnp.stack(win_s[0]), jnp.stack(win_s[1]), jnp.stack(win_s[2]),
            jnp.stack(gmlp_s), jnp.stack(conv_s))
```

```python
import functools

import jax
import jax.numpy as jnp
from jax import lax
from jax.experimental import pallas as pl
from jax.experimental.pallas import tpu as pltpu

BF16 = jnp.bfloat16
F32 = jnp.float32

HEAD_DIM_A = 64
HEADS_PER_GROUP_A = 4
DILATED_GROUPS = ((128, 1), (512, 4), (2048, 16))
N_GROUPS_A = len(DILATED_GROUPS)
GROUP_WIDTH_A = HEADS_PER_GROUP_A * HEAD_DIM_A
WIDTH_A = N_GROUPS_A * GROUP_WIDTH_A
NK = 128
CHUNK = 128
N_GROUPS_B = 4
WIDTH_B = 768
GROUP_DIM_B = WIDTH_B // N_GROUPS_B
N_HEADS_M = 4
HEAD_DIM_M = 128
WIDTH_M = N_HEADS_M * HEAD_DIM_M
N_BRANCH = 3
CONV_W = 3
LN_EPS = 1e-5
NEG = -1e30
PAST_LEN = 16384

LANES = 128
SUBLANES = 8
VMEM_LIMIT_BYTES = 58 * 1024 * 1024

ATTN_BLOCK = 2048
DEINTERLEAVE_STRIDE = 4
MIX_ROWS = 512
FFN_ROWS = 512
SAMPLE_BATCH_BLOCK = 2
SAMPLE_LANE_CHUNK = 512
SAMPLE_MEM_CHUNK = 32

_NT = (((1,), (1,)), ((), ()))


def _dot(a, b):
    return jnp.dot(a, b, preferred_element_type=F32)


def _dot_nt(a, b):
    return lax.dot_general(a, b, _NT, preferred_element_type=F32)


def _layer_norm(x, g, b):
    mu = jnp.mean(x, axis=-1, keepdims=True)
    var = jnp.mean(jnp.square(x - mu), axis=-1, keepdims=True)
    return (x - mu) * lax.rsqrt(var + LN_EPS) * g + b


def _params(*semantics):
    return pltpu.CompilerParams(dimension_semantics=semantics, vmem_limit_bytes=VMEM_LIMIT_BYTES)


def _resident(shape):
    zeros = (0,) * len(shape)
    return pl.BlockSpec(shape, lambda *_: zeros, pipeline_mode=pl.Buffered(1))


def _mem_kv_kernel(mem_ref, w_ref, o_ref):
    o_ref[...] = _dot(mem_ref[...].astype(BF16), w_ref[...])


def _mem_kv_proj(mem, w):
    n, d = mem.shape
    return pl.pallas_call(
        _mem_kv_kernel,
        out_shape=jax.ShapeDtypeStruct((n, w.shape[1]), F32),
        name="mem_kv_proj",
    )(mem, w)


def _qkv_proj_kernel(xt_ref, xbot_ref, w_ref, o_ref, kvt0_ref, kvt1_ref, kvt2_ref, xb_ref, stage_ref, tmp_ref):
    g_id = pl.program_id(1)

    @pl.when(g_id == 0)
    def _():
        half = ATTN_BLOCK // 2
        xb_ref[0:half, :] = xt_ref[...].astype(BF16)
        xb_ref[half:, :] = xbot_ref[...].astype(BF16)

    xb = xb_ref[...]
    last_block = pl.program_id(0) == pl.num_programs(0) - 1
    for g, (win, dil) in enumerate(DILATED_GROUPS):
        kvt_ref = (kvt0_ref, kvt1_ref, kvt2_ref)[g]

        @pl.when(g_id == g)
        def _(g=g, win=win, dil=dil, kvt_ref=kvt_ref):
            w_cols = lambda c: w_ref[:, c * WIDTH_A + g * GROUP_WIDTH_A:c * WIDTH_A + (g + 1) * GROUP_WIDTH_A]
            for c in range(3):
                w = w_cols(c)
                if c == 0:
                    w = w * jnp.asarray(HEAD_DIM_A ** -0.5, BF16)
                res = _dot(xb, w)
                if dil == 1:
                    o_ref[:, c * 256:(c + 1) * 256] = res.astype(BF16)
                    continue
                stage_ref[0] = res[:, :LANES]
                stage_ref[1] = res[:, LANES:]
                inner = DEINTERLEAVE_STRIDE if dil > DEINTERLEAVE_STRIDE else dil
                outer = dil // inner
                for s in range(2):
                    col = c * 256 + s * LANES
                    for blk in range(ATTN_BLOCK // win):
                        base = blk * win
                        src_ref = stage_ref
                        if outer > 1:
                            part = win // inner
                            for lo in range(inner):
                                tmp_ref[s, base + lo * part:base + (lo + 1) * part, :] = (
                                    stage_ref[s, pl.ds(base + lo, part, stride=inner), :])
                            src_ref = tmp_ref
                        for lo in range(inner):
                            for hi in range(outer):
                                first = base + lo * (win // inner) + hi if outer > 1 else base + lo
                                rows = src_ref[s, pl.ds(first, NK, stride=outer if outer > 1 else inner), :]
                                dst = base + (hi * inner + lo) * NK
                                o_ref[dst:dst + NK, col:col + LANES] = rows.astype(BF16)

            @pl.when(last_block)
            def _():
                tail = xb[ATTN_BLOCK - win:, :]
                kvt_ref[0:GROUP_WIDTH_A, :] = _dot(tail, w_cols(1)).T
                kvt_ref[GROUP_WIDTH_A:2 * GROUP_WIDTH_A, :] = _dot(tail, w_cols(2)).T


def _qkv_proj(x, w_all):
    s, d = x.shape
    n_blk = s // ATTN_BLOCK
    half = ATTN_BLOCK // 2
    x_half = lambda which: pl.BlockSpec(
        (half, d), lambda n, g: (2 * jnp.minimum(n + jnp.minimum(jnp.maximum(g - which, 0), 1), n_blk - 1) + which, 0))
    kvt_shapes = [(2 * GROUP_WIDTH_A, win) for win, _ in DILATED_GROUPS]
    return pl.pallas_call(
        _qkv_proj_kernel,
        grid=(n_blk, N_GROUPS_A),
        in_specs=[x_half(0), x_half(1),
                  pl.BlockSpec((d, 3 * WIDTH_A), lambda n, g: (0, 0), pipeline_mode=pl.Buffered(1))],
        out_specs=[pl.BlockSpec((None, ATTN_BLOCK, 3 * GROUP_WIDTH_A), lambda n, g: (g, n, 0))]
        + [pl.BlockSpec(shape, lambda n, g: (0, 0)) for shape in kvt_shapes],
        out_shape=[jax.ShapeDtypeStruct((N_GROUPS_A, s, 3 * GROUP_WIDTH_A), BF16)]
        + [jax.ShapeDtypeStruct(shape, F32) for shape in kvt_shapes],
        scratch_shapes=[pltpu.VMEM((ATTN_BLOCK, d), BF16),
                        pltpu.VMEM((GROUP_WIDTH_A // LANES, ATTN_BLOCK, LANES), F32),
                        pltpu.VMEM((GROUP_WIDTH_A // LANES, ATTN_BLOCK, LANES), F32)],
        compiler_params=_params("arbitrary", "arbitrary"),
        name="qkv_proj",
    )(x, x, w_all)


def _attn_units(units, prev_shifts):
    first_head = lax.broadcasted_iota(jnp.int32, (NK, LANES), 1) < HEAD_DIM_A
    row = lax.broadcasted_iota(jnp.int32, (2 * NK, NK), 0) % NK
    col = lax.broadcasted_iota(jnp.int32, (2 * NK, NK), 1)
    cur_ok = col <= row
    prev_ok = {}
    scores = []
    for (q, k_p, k_c, _, _), shift in zip(units, prev_shifts):
        zero = jnp.zeros_like(q)
        q_st = jnp.concatenate([jnp.where(first_head, q, zero), jnp.where(first_head, zero, q)], axis=0)
        if id(shift) not in prev_ok:
            prev_ok[id(shift)] = col >= row + shift
        scores.append((jnp.where(cur_ok, _dot_nt(q_st, k_c), NEG),
                       jnp.where(prev_ok[id(shift)], _dot_nt(q_st, k_p), NEG)))
    maxes = [jnp.max(jnp.maximum(s_c, s_p), axis=-1, keepdims=True) for s_c, s_p in scores]
    probs = [(jnp.exp(s_c - m), jnp.exp(s_p - m)) for (s_c, s_p), m in zip(scores, maxes)]
    dens = [jnp.sum(p_c + p_p, axis=-1, keepdims=True) for p_c, p_p in probs]
    results = []
    for (_, _, _, v_p, v_c), (p_c, p_p), m, den in zip(units, probs, maxes, dens):
        o_st = _dot(p_c.astype(BF16), v_c) + _dot(p_p.astype(BF16), v_p)
        pick = lambda t: jnp.where(first_head, jnp.broadcast_to(t[:NK], (NK, LANES)), jnp.broadcast_to(t[NK:], (NK, LANES)))
        den_sel = pick(den)
        results.append((pick(o_st) / den_sel, pick(m) + jnp.log(den_sel)))
    return results


def _dilated_attn_kernel(qkv0_ref, qkv1_ref, qkv2_ref, o_ref, kv0_ref, kv1_ref, kv2_ref, og_ref, lg_ref):
    n = pl.program_id(0)
    qkv_refs = (qkv0_ref, qkv1_ref, qkv2_ref)
    kv_refs = (kv0_ref, kv1_ref, kv2_ref)
    n_sub = ATTN_BLOCK // NK
    n_hp = GROUP_WIDTH_A // LANES

    @pl.when(n == 0)
    def _():
        for kv_ref in kv_refs:
            kv_ref[...] = jnp.zeros(kv_ref.shape, BF16)

    def sub_block(t, carry, prev_in_block):
        off = pl.multiple_of(t * NK, NK)
        units, shifts, dsts = [], [], []
        for g, (win, dil) in enumerate(DILATED_GROUPS):
            win_blk = t // dil
            res = t - win_blk * dil
            has_prev = (n * (ATTN_BLOCK // win) + win_blk) > 0
            nat = win_blk * win + res
            shift = jnp.where(has_prev, 0, 2 * NK)
            for hp in range(n_hp):
                qs = slice(hp * LANES, (hp + 1) * LANES)
                ks = slice(GROUP_WIDTH_A + hp * LANES, GROUP_WIDTH_A + (hp + 1) * LANES)
                vs = slice(2 * GROUP_WIDTH_A + hp * LANES, 2 * GROUP_WIDTH_A + (hp + 1) * LANES)
                cur = pl.ds(off, NK)
                if prev_in_block[g]:
                    prev = pl.ds(off - win, NK)
                    k_prev, v_prev = qkv_refs[g][prev, ks], qkv_refs[g][prev, vs]
                else:
                    k_prev = kv_refs[g][cur, qs]
                    v_prev = kv_refs[g][cur, GROUP_WIDTH_A + hp * LANES:GROUP_WIDTH_A + (hp + 1) * LANES]
                units.append((qkv_refs[g][cur, qs], k_prev, qkv_refs[g][cur, ks], v_prev, qkv_refs[g][cur, vs]))
                shifts.append(shift)
                dsts.append((g, hp, pl.ds(off, NK) if dil == 1 else pl.ds(nat, NK, stride=dil)))
        for (g, hp, dst), (out, lse) in zip(dsts, _attn_units(units, shifts)):
            og_ref[g, hp, dst, :] = out
            lg_ref[g, hp, dst, :] = lse
        return carry

    bounds = sorted({0, n_sub} | {min(win // NK, n_sub) for win, _ in DILATED_GROUPS})
    for lo, hi in zip(bounds[:-1], bounds[1:]):
        in_block = tuple(lo * NK >= win for win, _ in DILATED_GROUPS)
        lax.fori_loop(lo, hi, functools.partial(sub_block, prev_in_block=in_block), 0)
    for (win, _), qkv_ref, kv_ref in zip(DILATED_GROUPS, qkv_refs, kv_refs):
        kv_ref[...] = qkv_ref[ATTN_BLOCK - win:, GROUP_WIDTH_A:]

    def merge(t, carry):
        off = pl.multiple_of(t * NK, NK)
        for hp in range(GROUP_WIDTH_A // LANES):
            lses = [lg_ref[g, hp, pl.ds(off, NK), :] for g in range(N_GROUPS_A)]
            m = jnp.maximum(jnp.maximum(lses[0], lses[1]), lses[2])
            es = [jnp.exp(l - m) for l in lses]
            num = sum(es[g] * og_ref[g, hp, pl.ds(off, NK), :] for g in range(N_GROUPS_A))
            o_ref[pl.ds(off, NK), hp * LANES:(hp + 1) * LANES] = (num / (es[0] + es[1] + es[2])).astype(BF16)
        return carry

    lax.fori_loop(0, n_sub, merge, 0)


def _dilated_attn(qkv):
    _, s, _ = qkv.shape
    n_blk = s // ATTN_BLOCK
    n_hp = GROUP_WIDTH_A // LANES
    in_specs = [pl.BlockSpec((None, ATTN_BLOCK, 3 * GROUP_WIDTH_A), functools.partial(lambda n, g: (g, n, 0), g=g))
                for g in range(N_GROUPS_A)]
    return pl.pallas_call(
        _dilated_attn_kernel,
        grid=(n_blk,),
        in_specs=in_specs,
        out_specs=pl.BlockSpec((ATTN_BLOCK, GROUP_WIDTH_A), lambda n: (n, 0)),
        out_shape=jax.ShapeDtypeStruct((s, GROUP_WIDTH_A), BF16),
        scratch_shapes=[pltpu.VMEM((win, 2 * GROUP_WIDTH_A), BF16) for win, _ in DILATED_GROUPS]
        + [pltpu.VMEM((N_GROUPS_A, n_hp, ATTN_BLOCK, LANES), F32),
           pltpu.VMEM((N_GROUPS_A, n_hp, ATTN_BLOCK, LANES), F32)],
        compiler_params=_params("arbitrary"),
        name="dilated_attn",
    )(qkv, qkv, qkv)


def _to_heads(row, n_heads, head_dim):
    return jnp.concatenate([row[:, h * head_dim:(h + 1) * head_dim] for h in range(n_heads)], axis=0)


def _from_heads(t):
    return jnp.concatenate([t[h:h + 1] for h in range(t.shape[0])], axis=1)


def _window_cache_attention(c_ref, j, cols_ref, base, dil):
    win = c_ref.shape[-1]
    lc = min(win, SAMPLE_LANE_CHUNK)
    n_chunks = win // lc
    lane = lax.broadcasted_iota(jnp.int32, (1, win), 1)
    wanted = (lane % dil) == 0
    out = []
    for h in range(HEADS_PER_GROUP_A):
        lo = base + h * HEAD_DIM_A
        q, k_new, v_new = (cols_ref[lo + c * GROUP_WIDTH_A:lo + c * GROUP_WIDTH_A + HEAD_DIM_A, :] for c in range(3))
        q_wide = jnp.concatenate([q] * (lc // LANES), axis=1)
        parts = [jnp.sum(c_ref[j, 0, h, :, i * lc:(i + 1) * lc] * q_wide, axis=0, keepdims=True)
                 for i in range(n_chunks)]
        s = jnp.where(wanted, jnp.concatenate(parts, axis=1), NEG)
        s_n = jnp.sum(k_new[:, 0:1] * q[:, 0:1], axis=0, keepdims=True)
        m = jnp.maximum(jnp.max(s, axis=-1, keepdims=True), s_n)
        p = jnp.exp(s - m)
        p_n = jnp.exp(s_n - m)
        den = jnp.sum(p, axis=-1, keepdims=True) + p_n
        acc = p_n * v_new[:, 0:1]
        for i in range(n_chunks):
            acc = acc + jnp.sum(c_ref[j, 1, h, :, i * lc:(i + 1) * lc] * p[:, i * lc:(i + 1) * lc],
                                axis=-1, keepdims=True)
        out.append((acc / den, m + jnp.log(den)))
    return out


def _memory_cache_attention(cm_ref, j, q, s_ref):
    n_heads, head_dim = q.shape
    chunk = SAMPLE_MEM_CHUNK
    n_chunks = cm_ref.shape[1] // chunk
    m = jnp.full((n_heads, head_dim), NEG, F32)
    for c in range(n_chunks):
        pos = slice(c * chunk, (c + 1) * chunk)
        s = jnp.sum(cm_ref[j, pos, 0] * q[None], axis=-1, keepdims=True) * (head_dim ** -0.5)
        s = jnp.broadcast_to(s, (chunk, n_heads, head_dim))
        s_ref[pos] = s
        m = jnp.maximum(m, jnp.max(s, axis=0))
    den = jnp.zeros((n_heads, head_dim), F32)
    acc = jnp.zeros((n_heads, head_dim), F32)
    for c in range(n_chunks):
        pos = slice(c * chunk, (c + 1) * chunk)
        p = jnp.exp(s_ref[pos] - m[None])
        den = den + jnp.sum(p, axis=0)
        acc = acc + jnp.sum(p * cm_ref[j, pos, 1], axis=0)
    return acc / den


def _sample_attn_kernel(x_ref, wqkv_ref, wqm_ref, c0_ref, c1_ref, c2_ref, cm_ref,
                        oa_ref, om_ref, kvn_ref, proj_ref, projt_ref, cols_ref, oat_ref, ms_ref):
    step = pl.program_id(0)
    caches = (c0_ref, c1_ref, c2_ref)
    n_batch, d = x_ref.shape
    qkv_w = N_GROUPS_A * 3 * GROUP_WIDTH_A

    @pl.when(step == 0)
    def _():
        xb = jnp.concatenate([x_ref[...], jnp.zeros((LANES - n_batch, d), F32)], axis=0).astype(BF16)
        for part in range(3):
            h = _dot(xb, wqkv_ref[:, part * WIDTH_A:(part + 1) * WIDTH_A])
            if part == 0:
                h = h * (HEAD_DIM_A ** -0.5)
            for g in range(N_GROUPS_A):
                h_g = h[:, g * GROUP_WIDTH_A:(g + 1) * GROUP_WIDTH_A]
                dst = (3 * g + part) * GROUP_WIDTH_A
                proj_ref[:, dst:dst + GROUP_WIDTH_A] = h_g
                if part > 0:
                    kvn_ref[g, :, (part - 1) * GROUP_WIDTH_A:part * GROUP_WIDTH_A] = h_g[:n_batch]
        proj_ref[:, qkv_w:] = _dot(xb, wqm_ref[:, :WIDTH_M])
        for t in range(qkv_w // LANES):
            projt_ref[t * LANES:(t + 1) * LANES, :] = proj_ref[:, t * LANES:(t + 1) * LANES].T.astype(BF16)
        oat_ref[...] = jnp.zeros(oat_ref.shape, F32)

    def one_row(j, carry):
        b = step * SAMPLE_BATCH_BLOCK + j
        pick_row = (lax.broadcasted_iota(jnp.int32, (LANES, LANES), 0) == b).astype(BF16)
        cols_ref[...] = _dot(projt_ref[...], pick_row)
        per_group = [_window_cache_attention(caches[g], j, cols_ref, g * 3 * GROUP_WIDTH_A, dil)
                     for g, (_, dil) in enumerate(DILATED_GROUPS)]
        this_lane = lax.broadcasted_iota(jnp.int32, (1, LANES), 1) == b
        for h in range(HEADS_PER_GROUP_A):
            outs = [per_group[g][h][0] for g in range(N_GROUPS_A)]
            lses = [per_group[g][h][1] for g in range(N_GROUPS_A)]
            m3 = jnp.maximum(jnp.maximum(lses[0], lses[1]), lses[2])
            es = [jnp.exp(l - m3) for l in lses]
            oa = (es[0] * outs[0] + es[1] * outs[1] + es[2] * outs[2]) / (es[0] + es[1] + es[2])
            rows = slice(h * HEAD_DIM_A, (h + 1) * HEAD_DIM_A)
            oat_ref[rows, :] = jnp.where(this_lane, oa, oat_ref[rows, :])

        qm = _to_heads(proj_ref[pl.ds(b, 1), qkv_w:], N_HEADS_M, HEAD_DIM_M)
        om_ref[pl.ds(b, 1), :] = _from_heads(_memory_cache_attention(cm_ref, j, qm, ms_ref))
        return carry

    lax.fori_loop(0, SAMPLE_BATCH_BLOCK, one_row, 0)

    @pl.when(step == pl.num_programs(0) - 1)
    def _():
        oa_ref[...] = oat_ref[...].T[:n_batch, :]


def _sample_attn(xs, w_all, caches, cache_mem):
    n_batch, d = xs.shape
    assert n_batch <= LANES
    bb = SAMPLE_BATCH_BLOCK
    qkv_w = N_GROUPS_A * 3 * GROUP_WIDTH_A
    qm_col = 3 * WIDTH_A + 2 * WIDTH_B
    assert qm_col % WIDTH_B == 0 and WIDTH_M <= WIDTH_B
    w_specs = [pl.BlockSpec((d, qkv_w), lambda i: (0, 0), pipeline_mode=pl.Buffered(1)),
               pl.BlockSpec((d, WIDTH_B), lambda i: (0, qm_col // WIDTH_B), pipeline_mode=pl.Buffered(1))]
    cache_specs = [pl.BlockSpec((bb,) + c.shape[1:], lambda i: (i, 0, 0, 0, 0)) for c in caches]
    whole = lambda shape: pl.BlockSpec(shape, lambda i: (0,) * len(shape))
    return pl.pallas_call(
        _sample_attn_kernel,
        grid=(n_batch // bb,),
        in_specs=[whole(xs.shape)] + w_specs + cache_specs
        + [pl.BlockSpec((bb,) + cache_mem.shape[1:], lambda i: (i, 0, 0, 0, 0))],
        out_specs=[whole((n_batch, GROUP_WIDTH_A)), whole((n_batch, WIDTH_M)),
                   whole((N_GROUPS_A, n_batch, 2 * GROUP_WIDTH_A))],
        out_shape=[jax.ShapeDtypeStruct((n_batch, GROUP_WIDTH_A), F32),
                   jax.ShapeDtypeStruct((n_batch, WIDTH_M), F32),
                   jax.ShapeDtypeStruct((N_GROUPS_A, n_batch, 2 * GROUP_WIDTH_A), F32)],
        scratch_shapes=[pltpu.VMEM((LANES, qkv_w + WIDTH_M), F32),
                        pltpu.VMEM((qkv_w, LANES), BF16),
                        pltpu.VMEM((qkv_w, LANES), F32),
                        pltpu.VMEM((GROUP_WIDTH_A, LANES), F32),
                        pltpu.VMEM((cache_mem.shape[1], N_HEADS_M, HEAD_DIM_M), F32)],
        compiler_params=_params("arbitrary"),
        name="sample_attn",
    )(xs, w_all, w_all, *caches, cache_mem)


def _mix_kernel(x_ref, oa_ref, mk_ref, mv_ref, xs_ref, oas_ref, oms_ref, w_ref, bgate_ref, lnvg_ref, lnvb_ref,
                ws_ref, bs_ref, wss_ref, bss_ref, wba_ref, wbb_ref, wbm_ref, wout_ref, ln1g_ref, ln1b_ref,
                x1_ref, x1s_ref, vrows_ref, *, alpha):
    shared = (w_ref, bgate_ref, lnvg_ref, lnvb_ref)
    tail = (wba_ref, wbb_ref, wbm_ref, wout_ref, ln1g_ref, ln1b_ref)
    is_sample_step = pl.program_id(0) == pl.num_programs(0) - 1

    @pl.when(jnp.logical_not(is_sample_step))
    def _():
        _mix_body(x_ref, oa_ref, (mk_ref, mv_ref), *shared, ws_ref, bs_ref, *tail, x1_ref, None,
                  sample=False, alpha=alpha)

    @pl.when(is_sample_step)
    def _():
        _mix_body(xs_ref, oas_ref, oms_ref, *shared, wss_ref, bss_ref, *tail, x1s_ref, vrows_ref,
                  sample=True, alpha=alpha)


def _mix_body(x_ref, oa_ref, mem_or_om, w_ref, bgate_ref, lnvg_ref, lnvb_ref, ws_ref, bs_ref,
              wba_ref, wbb_ref, wbm_ref, wout_ref, ln1g_ref, ln1b_ref, x1_ref, vrows_ref, *, sample, alpha):
    if sample:
        om_ref = mem_or_om
    else:
        mk_ref, mv_ref = mem_or_om
    d = x_ref.shape[1]
    rows = x_ref.shape[0]
    x = x_ref[...]
    xb = x.astype(BF16)
    col_u, col_v, col_qm, col_gate = 0, WIDTH_B, 2 * WIDTH_B, 2 * WIDTH_B + WIDTH_M

    h_all = _dot(xb, w_ref[:, 3 * WIDTH_A:])

    def gate(k):
        z = h_all[:, col_gate + k * d:col_gate + (k + 1) * d]
        return jax.nn.sigmoid(z + bgate_ref[k:k + 1, :])

    mixed = gate(0) * _dot(oa_ref[...].astype(BF16), wba_ref[...])

    u = jax.nn.gelu(h_all[:, col_u:col_u + WIDTH_B])
    v = _layer_norm(jax.nn.gelu(h_all[:, col_v:col_v + WIDTH_B]), lnvg_ref[...], lnvb_ref[...])
    if sample:
        vrows_ref[...] = v
        spatial = v.astype(BF16).astype(F32) * ws_ref[...].astype(BF16).astype(F32) + bs_ref[...]
    else:
        r_i = lax.broadcasted_iota(jnp.int32, (N_GROUPS_B * CHUNK, CHUNK), 0)
        c_i = lax.broadcasted_iota(jnp.int32, (N_GROUPS_B * CHUNK, CHUNK), 1)
        w_s = jnp.where((r_i % CHUNK) >= c_i, ws_ref[...], 0.0).astype(BF16)
        lane = lax.broadcasted_iota(jnp.int32, (CHUNK, WIDTH_B), 1)
        in_group = [(lane >= gb * GROUP_DIM_B) & (lane < (gb + 1) * GROUP_DIM_B) for gb in range(N_GROUPS_B)]
        vb = v.astype(BF16)
        parts = []
        for ch in range(rows // CHUNK):
            allg = _dot(w_s, vb[ch * CHUNK:(ch + 1) * CHUNK, :])
            sp = bs_ref[...]
            for gb in range(N_GROUPS_B):
                sp = sp + jnp.where(in_group[gb], allg[gb * CHUNK:(gb + 1) * CHUNK, :], 0.0)
            parts.append(sp)
        spatial = jnp.concatenate(parts, axis=0)
    o_b = u * spatial
    mixed = mixed + gate(1) * _dot(o_b.astype(BF16), wbb_ref[...])

    if sample:
        o_m = om_ref[...]
    else:
        qm = h_all[:, col_qm:col_qm + WIDTH_M].astype(BF16)
        heads = []
        for h in range(N_HEADS_M):
            hs = slice(h * HEAD_DIM_M, (h + 1) * HEAD_DIM_M)
            s = _dot_nt(qm[:, hs], mk_ref[:, hs]) * (HEAD_DIM_M ** -0.5)
            p = jnp.exp(s - jnp.max(s, axis=-1, keepdims=True))
            heads.append(_dot(p.astype(BF16), mv_ref[:, hs]) / jnp.sum(p, axis=-1, keepdims=True))
        o_m = jnp.concatenate(heads, axis=-1)
    mixed = mixed + gate(2) * _dot(o_m.astype(BF16), wbm_ref[...])

    x1_ref[...] = _layer_norm(alpha * x + _dot(mixed.astype(BF16), wout_ref[...]), ln1g_ref[...], ln1b_ref[...])


def _mix(x, o_a, mk, mv, xs, oa_s, om_s, w_all, b_gate, ln_v_g, ln_v_b, w_s, b_s, w_s_row, b_s_row,
         w_ba, w_bb, w_bm, w_out, ln1_g, ln1_b, *, alpha):
    s, d = x.shape
    n_tiles = s // MIX_ROWS
    tile = lambda width: pl.BlockSpec((MIX_ROWS, width), lambda i: (jnp.minimum(i, n_tiles - 1), 0))
    whole = lambda shape: pl.BlockSpec(shape, lambda i: (0,) * len(shape))
    operands = (x, o_a, mk, mv, xs, oa_s, om_s, w_all, b_gate, ln_v_g, ln_v_b, w_s, b_s, w_s_row, b_s_row,
                w_ba, w_bb, w_bm, w_out, ln1_g, ln1_b)
    n_s = xs.shape[0]
    return pl.pallas_call(
        functools.partial(_mix_kernel, alpha=alpha),
        grid=(n_tiles + 1,),
        in_specs=[tile(d), tile(GROUP_WIDTH_A)] + [_resident(t.shape) for t in operands[2:]],
        out_specs=[tile(d), whole((n_s, d)), whole((n_s, WIDTH_B))],
        out_shape=[jax.ShapeDtypeStruct((s, d), F32), jax.ShapeDtypeStruct((n_s, d), F32),
                   jax.ShapeDtypeStruct((n_s, WIDTH_B), F32)],
        compiler_params=_params("arbitrary"),
        name="mix",
    )(*operands)


def _ffn_kernel(x1_ref, x1s_ref, st0_ref, st1_ref, wup_ref, cw_ref, cb_ref, wdown_ref, g_ref, b_ref,
                y_ref, a_ref, ys_ref, as_ref, abuf_ref, *, alpha):
    consts = (wup_ref, cw_ref, cb_ref, wdown_ref, g_ref, b_ref)
    is_sample_step = pl.program_id(0) == pl.num_programs(0) - 1

    @pl.when(jnp.logical_not(is_sample_step))
    def _():
        _ffn_body(x1_ref, None, None, *consts, y_ref, a_ref, abuf_ref, sample=False, alpha=alpha)

    @pl.when(is_sample_step)
    def _():
        _ffn_body(x1s_ref, st0_ref, st1_ref, *consts, ys_ref, as_ref, None, sample=True, alpha=alpha)


def _ffn_body(x1_ref, st0_ref, st1_ref, wup_ref, cw_ref, cb_ref, wdown_ref, g_ref, b_ref, y_ref, a_ref, abuf_ref,
              *, sample, alpha):
    rows = x1_ref.shape[0]
    d_ff = wdown_ref.shape[0]
    if not sample:
        @pl.when(pl.program_id(0) == 0)
        def _():
            abuf_ref[0:SUBLANES, :] = jnp.zeros((SUBLANES, d_ff), F32)

    x1 = x1_ref[...]
    xb = x1.astype(BF16)
    a = _dot(xb, wup_ref[:, :d_ff])
    val = _dot(xb, wup_ref[:, d_ff:])
    if sample:
        a_ref[...] = a
        a_m2, a_m1 = st0_ref[...], st1_ref[...]
    else:
        abuf_ref[SUBLANES:SUBLANES + rows, :] = a
        a_m1 = abuf_ref[SUBLANES - 1:SUBLANES - 1 + rows, :]
        a_m2 = abuf_ref[SUBLANES - 2:SUBLANES - 2 + rows, :]
    conv = cb_ref[...] + cw_ref[0:1, :] * a_m2 + cw_ref[1:2, :] * a_m1 + cw_ref[2:3, :] * a
    h = jax.nn.gelu(conv) * val
    y_ref[...] = _layer_norm(alpha * x1 + _dot(h.astype(BF16), wdown_ref[...]), g_ref[...], b_ref[...])
    if not sample:
        tail = abuf_ref[rows:rows + SUBLANES, :]
        abuf_ref[0:SUBLANES, :] = tail
        a_ref[...] = tail


def _ffn(x1, x1_s, state, w_up, conv_w, conv_b, w_down, ln2_g, ln2_b, *, alpha):
    s, d = x1.shape
    n_s = x1_s.shape[0]
    d_ff = w_down.shape[0]
    n_tiles = s // FFN_ROWS
    tile = lambda width: pl.BlockSpec((FFN_ROWS, width), lambda i: (jnp.minimum(i, n_tiles - 1), 0))
    whole = lambda shape: pl.BlockSpec(shape, lambda i: (0,) * len(shape))
    operands = (x1, x1_s, state[:, 0], state[:, 1], w_up, conv_w, conv_b, w_down, ln2_g, ln2_b)
    return pl.pallas_call(
        functools.partial(_ffn_kernel, alpha=alpha),
        grid=(n_tiles + 1,),
        in_specs=[tile(d)] + [_resident(t.shape) for t in operands[1:]],
        out_specs=[tile(d), whole((SUBLANES, d_ff)), whole((n_s, d)), whole((n_s, d_ff))],
        out_shape=[jax.ShapeDtypeStruct((s, d), F32), jax.ShapeDtypeStruct((SUBLANES, d_ff), F32),
                   jax.ShapeDtypeStruct((n_s, d), F32), jax.ShapeDtypeStruct((n_s, d_ff), F32)],
        scratch_shapes=[pltpu.VMEM((FFN_ROWS + SUBLANES, d_ff), F32)],
        compiler_params=_params("arbitrary"),
        name="ffn",
    )(*operands)


def kernel(x_prompt, x_sample, mem_prompt, cache_win128_kv, cache_win512_kv, cache_win2048_kv, cache_mem_kv, state_ffn_conv, w_in, b_gate, ln_v_g, ln_v_b, w_spatial, b_spatial, w_mem_kv, w_branch_a, w_branch_b, w_branch_m, w_out, ln1_g, ln1_b, w_up, conv_w, conv_b, w_down, ln2_g, ln2_b):
    depth = w_in.shape[0]
    batch, seq, d_model = x_prompt.shape
    dec_batch, dec_seq, _ = x_sample.shape
    assert batch == 1 and dec_seq == 1 and seq % ATTN_BLOCK == 0
    alpha = (2.0 * depth) ** 0.25
    win_caches = (cache_win128_kv, cache_win512_kv, cache_win2048_kv)
    for cache, (win, _) in zip(win_caches, DILATED_GROUPS):
        assert cache.shape[2] == win and PAST_LEN >= win

    yp = x_prompt.reshape(seq, d_model)
    ys = x_sample.reshape(dec_batch, d_model)
    win_p = [[] for _ in range(N_GROUPS_A)]
    win_s = [[] for _ in range(N_GROUPS_A)]
    mem_p, conv_p, gmlp_s, conv_s = [], [], [], []
    for l in range(depth):
        w_all = w_in[l].astype(BF16)
        row2 = lambda t: t.reshape(1, -1)
        per_lane = lambda t: jnp.repeat(t, GROUP_DIM_B, axis=-1)
        consts_tail = (w_branch_a[l].astype(BF16), w_branch_b[l].astype(BF16), w_branch_m[l].astype(BF16),
                       w_out[l].astype(BF16), row2(ln1_g[l]), row2(ln1_b[l]))
        ffn_consts = (w_up[l].astype(BF16), conv_w[l], row2(conv_b[l]), w_down[l].astype(BF16),
                      row2(ln2_g[l]), row2(ln2_b[l]))

        mkv = _mem_kv_proj(mem_prompt[0], w_mem_kv[l].astype(BF16))
        mem_p.append(mkv.reshape(1, -1, 2, N_HEADS_M, HEAD_DIM_M))
        qkv, *kv_tails = _qkv_proj(yp, w_all)
        for g, (win, _) in enumerate(DILATED_GROUPS):
            tail = kv_tails[g].reshape(2, HEADS_PER_GROUP_A, HEAD_DIM_A, win)
            win_p[g].append(jnp.transpose(tail, (3, 0, 1, 2))[None])
        o_a = _dilated_attn(qkv)
        caches = [jnp.transpose(c[l], (0, 2, 3, 4, 1)) for c in win_caches]
        oa_s, om_s, kv_new = _sample_attn(ys, w_all, caches, cache_mem_kv[l])
        for g in range(N_GROUPS_A):
            win_s[g].append(kv_new[g].reshape(dec_batch, 1, 2, HEADS_PER_GROUP_A, HEAD_DIM_A))

        x1, x1_s, v_rows = _mix(yp, o_a, mkv[:, :WIDTH_M].astype(BF16), mkv[:, WIDTH_M:].astype(BF16), ys, oa_s, om_s,
                                w_all, b_gate[l], row2(ln_v_g[l]), row2(ln_v_b[l]),
                                w_spatial[l].reshape(N_GROUPS_B * CHUNK, CHUNK), per_lane(b_spatial[l].T),
                                row2(per_lane(w_spatial[l][:, 0, 0])), row2(per_lane(b_spatial[l][:, 0])),
                                *consts_tail, alpha=alpha)
        gmlp_s.append(v_rows.reshape(dec_batch, 1, WIDTH_B))
        yp_next, a_tail, ys_next, a_s = _ffn(x1, x1_s, state_ffn_conv[l], *ffn_consts, alpha=alpha)
        conv_p.append(a_tail[SUBLANES - (CONV_W - 1):][None])
        conv_s.append(jnp.stack([state_ffn_conv[l][:, 1], a_s], axis=1))
        yp, ys = yp_next, ys_next

    return (yp.reshape(batch, seq, d_model), ys.reshape(dec_batch, dec_seq, d_model),
            jnp.stack(win_p[0]), jnp.stack(win_p[1]), jnp.stack(win_p[2]),
            jnp.stack(mem_p), jnp.stack(conv_p),
            jnp.stack(win_s[0]), jnp.stack(win_s[1]), jnp.stack(win_s[2]),
            jnp.stack(gmlp_s), jnp.stack(conv_s))
```

```python
import functools

import jax
import jax.numpy as jnp
from jax import lax
from jax.experimental import pallas as pl
from jax.experimental.pallas import tpu as pltpu

BF16 = jnp.bfloat16
F32 = jnp.float32

HEAD_DIM_A = 64
HEADS_PER_GROUP_A = 4
DILATED_GROUPS = ((128, 1), (512, 4), (2048, 16))
N_GROUPS_A = len(DILATED_GROUPS)
GROUP_WIDTH_A = HEADS_PER_GROUP_A * HEAD_DIM_A
WIDTH_A = N_GROUPS_A * GROUP_WIDTH_A
NK = 128
CHUNK = 128
N_GROUPS_B = 4
WIDTH_B = 768
GROUP_DIM_B = WIDTH_B // N_GROUPS_B
N_HEADS_M = 4
HEAD_DIM_M = 128
WIDTH_M = N_HEADS_M * HEAD_DIM_M
N_BRANCH = 3
CONV_W = 3
LN_EPS = 1e-5
NEG = -1e30
PAST_LEN = 16384

LANES = 128
SUBLANES = 8
VMEM_LIMIT_BYTES = 58 * 1024 * 1024

ATTN_BLOCK = 2048
DEINTERLEAVE_STRIDE = 4
MIX_ROWS = 512
FFN_ROWS = 512
SAMPLE_BATCH_BLOCK = 2
SAMPLE_LANE_CHUNK = 512
SAMPLE_MEM_CHUNK = 32

_NT = (((1,), (1,)), ((), ()))


def _dot(a, b):
    return jnp.dot(a, b, preferred_element_type=F32)


def _dot_nt(a, b):
    return lax.dot_general(a, b, _NT, preferred_element_type=F32)


def _layer_norm(x, g, b):
    mu = jnp.mean(x, axis=-1, keepdims=True)
    var = jnp.mean(jnp.square(x - mu), axis=-1, keepdims=True)
    return (x - mu) * lax.rsqrt(var + LN_EPS) * g + b


def _params(*semantics):
    return pltpu.CompilerParams(dimension_semantics=semantics, vmem_limit_bytes=VMEM_LIMIT_BYTES)


def _resident(shape):
    zeros = (0,) * len(shape)
    return pl.BlockSpec(shape, lambda *_: zeros, pipeline_mode=pl.Buffered(1))


def _mem_kv_kernel(mem_ref, w_ref, o_ref):
    o_ref[...] = _dot(mem_ref[...].astype(BF16), w_ref[...])


def _mem_kv_proj(mem, w):
    n, d = mem.shape
    return pl.pallas_call(
        _mem_kv_kernel,
        out_shape=jax.ShapeDtypeStruct((n, w.shape[1]), F32),
        name="mem_kv_proj",
    )(mem, w)


def _qkv_proj_kernel(xt_ref, xbot_ref, w_ref, o_ref, kvt0_ref, kvt1_ref, kvt2_ref, xb_ref, stage_ref, tmp_ref):
    g_id = pl.program_id(1)

    @pl.when(g_id == 0)
    def _():
        half = ATTN_BLOCK // 2
        xb_ref[0:half, :] = xt_ref[...].astype(BF16)
        xb_ref[half:, :] = xbot_ref[...].astype(BF16)

    xb = xb_ref[...]
    last_block = pl.program_id(0) == pl.num_programs(0) - 1
    for g, (win, dil) in enumerate(DILATED_GROUPS):
        kvt_ref = (kvt0_ref, kvt1_ref, kvt2_ref)[g]

        @pl.when(g_id == g)
        def _(g=g, win=win, dil=dil, kvt_ref=kvt_ref):
            w_cols = lambda c: w_ref[:, c * WIDTH_A + g * GROUP_WIDTH_A:c * WIDTH_A + (g + 1) * GROUP_WIDTH_A]
            for c in range(3):
                w = w_cols(c)
                if c == 0:
                    w = w * jnp.asarray(HEAD_DIM_A ** -0.5, BF16)
                res = _dot(xb, w)
                if dil == 1:
                    o_ref[:, c * GROUP_WIDTH_A:(c + 1) * GROUP_WIDTH_A] = res.astype(BF16)
                    continue
                n_slabs = GROUP_WIDTH_A // LANES
                for s in range(n_slabs):
                    stage_ref[s] = res[:, s * LANES:(s + 1) * LANES]
                inner = DEINTERLEAVE_STRIDE if dil > DEINTERLEAVE_STRIDE else dil
                outer = dil // inner
                for s in range(n_slabs):
                    col = c * GROUP_WIDTH_A + s * LANES
                    for blk in range(ATTN_BLOCK // win):
                        base = blk * win
                        src_ref = stage_ref
                        if outer > 1:
                            part = win // inner
                            for lo in range(inner):
                                tmp_ref[s, base + lo * part:base + (lo + 1) * part, :] = (
                                    stage_ref[s, pl.ds(base + lo, part, stride=inner), :])
                            src_ref = tmp_ref
                        for lo in range(inner):
                            for hi in range(outer):
                                first = base + lo * (win // inner) + hi if outer > 1 else base + lo
                                rows = src_ref[s, pl.ds(first, NK, stride=outer if outer > 1 else inner), :]
                                dst = base + (hi * inner + lo) * NK
                                o_ref[dst:dst + NK, col:col + LANES] = rows.astype(BF16)

            @pl.when(last_block)
            def _():
                tail = xb[ATTN_BLOCK - win:, :]
                kvt_ref[0:GROUP_WIDTH_A, :] = _dot(tail, w_cols(1)).T
                kvt_ref[GROUP_WIDTH_A:2 * GROUP_WIDTH_A, :] = _dot(tail, w_cols(2)).T


def _qkv_proj(x, w_all):
    s, d = x.shape
    n_blk = s // ATTN_BLOCK
    half = ATTN_BLOCK // 2
    x_half = lambda which: pl.BlockSpec(
        (half, d), lambda n, g: (2 * jnp.minimum(n + jnp.minimum(jnp.maximum(g - which, 0), 1), n_blk - 1) + which, 0))
    kvt_shapes = [(2 * GROUP_WIDTH_A, win) for win, _ in DILATED_GROUPS]
    return pl.pallas_call(
        _qkv_proj_kernel,
        grid=(n_blk, N_GROUPS_A),
        in_specs=[x_half(0), x_half(1),
                  pl.BlockSpec((d, 3 * WIDTH_A), lambda n, g: (0, 0), pipeline_mode=pl.Buffered(1))],
        out_specs=[pl.BlockSpec((None, ATTN_BLOCK, 3 * GROUP_WIDTH_A), lambda n, g: (g, n, 0))]
        + [pl.BlockSpec(shape, lambda n, g: (0, 0)) for shape in kvt_shapes],
        out_shape=[jax.ShapeDtypeStruct((N_GROUPS_A, s, 3 * GROUP_WIDTH_A), BF16)]
        + [jax.ShapeDtypeStruct(shape, F32) for shape in kvt_shapes],
        scratch_shapes=[pltpu.VMEM((ATTN_BLOCK, d), BF16),
                        pltpu.VMEM((GROUP_WIDTH_A // LANES, ATTN_BLOCK, LANES), F32),
                        pltpu.VMEM((GROUP_WIDTH_A // LANES, ATTN_BLOCK, LANES), F32)],
        compiler_params=_params("arbitrary", "arbitrary"),
        name="qkv_proj",
    )(x, x, w_all)


def _attn_units(units, prev_shifts):
    first_head = lax.broadcasted_iota(jnp.int32, (NK, LANES), 1) < HEAD_DIM_A
    row = lax.broadcasted_iota(jnp.int32, (2 * NK, NK), 0) % NK
    col = lax.broadcasted_iota(jnp.int32, (2 * NK, NK), 1)
    cur_ok = col <= row
    prev_ok = {}
    scores = []
    for (q, k_p, k_c, _, _), shift in zip(units, prev_shifts):
        zero = jnp.zeros_like(q)
        q_st = jnp.concatenate([jnp.where(first_head, q, zero), jnp.where(first_head, zero, q)], axis=0)
        if id(shift) not in prev_ok:
            prev_ok[id(shift)] = col >= row + shift
        scores.append((jnp.where(cur_ok, _dot_nt(q_st, k_c), NEG),
                       jnp.where(prev_ok[id(shift)], _dot_nt(q_st, k_p), NEG)))
    maxes = [jnp.max(jnp.maximum(s_c, s_p), axis=-1, keepdims=True) for s_c, s_p in scores]
    probs = [(jnp.exp(s_c - m), jnp.exp(s_p - m)) for (s_c, s_p), m in zip(scores, maxes)]
    dens = [jnp.sum(p_c + p_p, axis=-1, keepdims=True) for p_c, p_p in probs]
    results = []
    for (_, _, _, v_p, v_c), (p_c, p_p), m, den in zip(units, probs, maxes, dens):
        o_st = _dot(p_c.astype(BF16), v_c) + _dot(p_p.astype(BF16), v_p)
        pick = lambda t: jnp.where(first_head, jnp.broadcast_to(t[:NK], (NK, LANES)), jnp.broadcast_to(t[NK:], (NK, LANES)))
        den_sel = pick(den)
        results.append((pick(o_st) / den_sel, pick(m) + jnp.log(den_sel)))
    return results


def _dilated_attn_kernel(qkv0_ref, qkv1_ref, qkv2_ref, o_ref, kv0_ref, kv1_ref, kv2_ref, og_ref, lg_ref):
    n = pl.program_id(0)
    qkv_refs = (qkv0_ref, qkv1_ref, qkv2_ref)
    kv_refs = (kv0_ref, kv1_ref, kv2_ref)
    n_sub = ATTN_BLOCK // NK
    n_hp = GROUP_WIDTH_A // LANES

    @pl.when(n == 0)
    def _():
        for kv_ref in kv_refs:
            kv_ref[...] = jnp.zeros(kv_ref.shape, BF16)

    def sub_block(t, carry, prev_in_block):
        off = pl.multiple_of(t * NK, NK)
        units, shifts, dsts = [], [], []
        for g, (win, dil) in enumerate(DILATED_GROUPS):
            win_blk = t // dil
            res = t - win_blk * dil
            has_prev = (n * (ATTN_BLOCK // win) + win_blk) > 0
            nat = win_blk * win + res
            shift = jnp.where(has_prev, 0, 2 * NK)
            for hp in range(n_hp):
                qs = slice(hp * LANES, (hp + 1) * LANES)
                ks = slice(GROUP_WIDTH_A + hp * LANES, GROUP_WIDTH_A + (hp + 1) * LANES)
                vs = slice(2 * GROUP_WIDTH_A + hp * LANES, 2 * GROUP_WIDTH_A + (hp + 1) * LANES)
                cur = pl.ds(off, NK)
                if prev_in_block[g]:
                    prev = pl.ds(off - win, NK)
                    k_prev, v_prev = qkv_refs[g][prev, ks], qkv_refs[g][prev, vs]
                else:
                    k_prev = kv_refs[g][cur, qs]
                    v_prev = kv_refs[g][cur, GROUP_WIDTH_A + hp * LANES:GROUP_WIDTH_A + (hp + 1) * LANES]
                units.append((qkv_refs[g][cur, qs], k_prev, qkv_refs[g][cur, ks], v_prev, qkv_refs[g][cur, vs]))
                shifts.append(shift)
                dsts.append((g, hp, pl.ds(off, NK) if dil == 1 else pl.ds(nat, NK, stride=dil)))
        for (g, hp, dst), (out, lse) in zip(dsts, _attn_units(units, shifts)):
            og_ref[g, hp, dst, :] = out
            lg_ref[g, hp, dst, :] = lse
        return carry

    bounds = sorted({0, n_sub} | {min(win // NK, n_sub) for win, _ in DILATED_GROUPS})
    for lo, hi in zip(bounds[:-1], bounds[1:]):
        in_block = tuple(lo * NK >= win for win, _ in DILATED_GROUPS)
        lax.fori_loop(lo, hi, functools.partial(sub_block, prev_in_block=in_block), 0)
    for (win, _), qkv_ref, kv_ref in zip(DILATED_GROUPS, qkv_refs, kv_refs):
        kv_ref[...] = qkv_ref[ATTN_BLOCK - win:, GROUP_WIDTH_A:]

    def merge(t, carry):
        off = pl.multiple_of(t * NK, NK)
        for hp in range(GROUP_WIDTH_A // LANES):
            lses = [lg_ref[g, hp, pl.ds(off, NK), :] for g in range(N_GROUPS_A)]
            m = jnp.maximum(jnp.maximum(lses[0], lses[1]), lses[2])
            es = [jnp.exp(l - m) for l in lses]
            num = sum(es[g] * og_ref[g, hp, pl.ds(off, NK), :] for g in range(N_GROUPS_A))
            o_ref[pl.ds(off, NK), hp * LANES:(hp + 1) * LANES] = (num / (es[0] + es[1] + es[2])).astype(BF16)
        return carry

    lax.fori_loop(0, n_sub, merge, 0)


def _dilated_attn(qkv):
    _, s, _ = qkv.shape
    n_blk = s // ATTN_BLOCK
    n_hp = GROUP_WIDTH_A // LANES
    in_specs = [pl.BlockSpec((None, ATTN_BLOCK, 3 * GROUP_WIDTH_A), functools.partial(lambda n, g: (g, n, 0), g=g))
                for g in range(N_GROUPS_A)]
    return pl.pallas_call(
        _dilated_attn_kernel,
        grid=(n_blk,),
        in_specs=in_specs,
        out_specs=pl.BlockSpec((ATTN_BLOCK, GROUP_WIDTH_A), lambda n: (n, 0)),
        out_shape=jax.ShapeDtypeStruct((s, GROUP_WIDTH_A), BF16),
        scratch_shapes=[pltpu.VMEM((win, 2 * GROUP_WIDTH_A), BF16) for win, _ in DILATED_GROUPS]
        + [pltpu.VMEM((N_GROUPS_A, n_hp, ATTN_BLOCK, LANES), F32),
           pltpu.VMEM((N_GROUPS_A, n_hp, ATTN_BLOCK, LANES), F32)],
        compiler_params=_params("arbitrary"),
        name="dilated_attn",
    )(qkv, qkv, qkv)


def _to_heads(row, n_heads, head_dim):
    return jnp.concatenate([row[:, h * head_dim:(h + 1) * head_dim] for h in range(n_heads)], axis=0)


def _from_heads(t):
    return jnp.concatenate([t[h:h + 1] for h in range(t.shape[0])], axis=1)


def _window_cache_attention(c_ref, j, cols_ref, base, dil):
    win = c_ref.shape[-1]
    lc = min(win, SAMPLE_LANE_CHUNK)
    n_chunks = win // lc
    lane = lax.broadcasted_iota(jnp.int32, (1, win), 1)
    wanted = (lane % dil) == 0
    out = []
    for h in range(HEADS_PER_GROUP_A):
        lo = base + h * HEAD_DIM_A
        q, k_new, v_new = (cols_ref[lo + c * GROUP_WIDTH_A:lo + c * GROUP_WIDTH_A + HEAD_DIM_A, :] for c in range(3))
        q_wide = jnp.concatenate([q] * (lc // LANES), axis=1)
        parts = [jnp.sum(c_ref[j, 0, h, :, i * lc:(i + 1) * lc] * q_wide, axis=0, keepdims=True)
                 for i in range(n_chunks)]
        s = jnp.where(wanted, jnp.concatenate(parts, axis=1), NEG)
        s_n = jnp.sum(k_new[:, 0:1] * q[:, 0:1], axis=0, keepdims=True)
        m = jnp.maximum(jnp.max(s, axis=-1, keepdims=True), s_n)
        p = jnp.exp(s - m)
        p_n = jnp.exp(s_n - m)
        den = jnp.sum(p, axis=-1, keepdims=True) + p_n
        acc = p_n * v_new[:, 0:1]
        for i in range(n_chunks):
            acc = acc + jnp.sum(c_ref[j, 1, h, :, i * lc:(i + 1) * lc] * p[:, i * lc:(i + 1) * lc],
                                axis=-1, keepdims=True)
        out.append((acc / den, m + jnp.log(den)))
    return out


def _memory_cache_attention(cm_ref, j, q, s_ref):
    n_heads, head_dim = q.shape
    chunk = SAMPLE_MEM_CHUNK
    n_chunks = cm_ref.shape[1] // chunk
    m = jnp.full((n_heads, head_dim), NEG, F32)
    for c in range(n_chunks):
        pos = slice(c * chunk, (c + 1) * chunk)
        s = jnp.sum(cm_ref[j, pos, 0] * q[None], axis=-1, keepdims=True) * (head_dim ** -0.5)
        s = jnp.broadcast_to(s, (chunk, n_heads, head_dim))
        s_ref[pos] = s
        m = jnp.maximum(m, jnp.max(s, axis=0))
    den = jnp.zeros((n_heads, head_dim), F32)
    acc = jnp.zeros((n_heads, head_dim), F32)
    for c in range(n_chunks):
        pos = slice(c * chunk, (c + 1) * chunk)
        p = jnp.exp(s_ref[pos] - m[None])
        den = den + jnp.sum(p, axis=0)
        acc = acc + jnp.sum(p * cm_ref[j, pos, 1], axis=0)
    return acc / den


def _sample_attn_kernel(x_ref, wqkv_ref, wqm_ref, c0_ref, c1_ref, c2_ref, cm_ref,
                        oa_ref, om_ref, kvn_ref, proj_ref, projt_ref, cols_ref, oat_ref, ms_ref):
    step = pl.program_id(0)
    caches = (c0_ref, c1_ref, c2_ref)
    n_batch, d = x_ref.shape
    qkv_w = N_GROUPS_A * 3 * GROUP_WIDTH_A

    @pl.when(step == 0)
    def _():
        xb = jnp.concatenate([x_ref[...], jnp.zeros((LANES - n_batch, d), F32)], axis=0).astype(BF16)
        for part in range(3):
            h = _dot(xb, wqkv_ref[:, part * WIDTH_A:(part + 1) * WIDTH_A])
            if part == 0:
                h = h * (HEAD_DIM_A ** -0.5)
            for g in range(N_GROUPS_A):
                h_g = h[:, g * GROUP_WIDTH_A:(g + 1) * GROUP_WIDTH_A]
                dst = (3 * g + part) * GROUP_WIDTH_A
                proj_ref[:, dst:dst + GROUP_WIDTH_A] = h_g
                if part > 0:
                    kvn_ref[g, :, (part - 1) * GROUP_WIDTH_A:part * GROUP_WIDTH_A] = h_g[:n_batch]
        proj_ref[:, qkv_w:] = _dot(xb, wqm_ref[:, :WIDTH_M])
        for t in range(qkv_w // LANES):
            projt_ref[t * LANES:(t + 1) * LANES, :] = proj_ref[:, t * LANES:(t + 1) * LANES].T.astype(BF16)
        oat_ref[...] = jnp.zeros(oat_ref.shape, F32)

    def one_row(j, carry):
        b = step * SAMPLE_BATCH_BLOCK + j
        pick_row = (lax.broadcasted_iota(jnp.int32, (LANES, LANES), 0) == b).astype(BF16)
        cols_ref[...] = _dot(projt_ref[...], pick_row)
        per_group = [_window_cache_attention(caches[g], j, cols_ref, g * 3 * GROUP_WIDTH_A, dil)
                     for g, (_, dil) in enumerate(DILATED_GROUPS)]
        this_lane = lax.broadcasted_iota(jnp.int32, (1, LANES), 1) == b
        for h in range(HEADS_PER_GROUP_A):
            outs = [per_group[g][h][0] for g in range(N_GROUPS_A)]
            lses = [per_group[g][h][1] for g in range(N_GROUPS_A)]
            m3 = jnp.maximum(jnp.maximum(lses[0], lses[1]), lses[2])
            es = [jnp.exp(l - m3) for l in lses]
            oa = (es[0] * outs[0] + es[1] * outs[1] + es[2] * outs[2]) / (es[0] + es[1] + es[2])
            rows = slice(h * HEAD_DIM_A, (h + 1) * HEAD_DIM_A)
            oat_ref[rows, :] = jnp.where(this_lane, oa, oat_ref[rows, :])

        qm = _to_heads(proj_ref[pl.ds(b, 1), qkv_w:], N_HEADS_M, HEAD_DIM_M)
        om_ref[pl.ds(b, 1), :] = _from_heads(_memory_cache_attention(cm_ref, j, qm, ms_ref))
        return carry

    lax.fori_loop(0, SAMPLE_BATCH_BLOCK, one_row, 0)

    @pl.when(step == pl.num_programs(0) - 1)
    def _():
        oa_ref[...] = oat_ref[...].T[:n_batch, :]


def _sample_attn(xs, w_all, caches, cache_mem):
    n_batch, d = xs.shape
    assert n_batch <= LANES
    bb = SAMPLE_BATCH_BLOCK
    qkv_w = N_GROUPS_A * 3 * GROUP_WIDTH_A
    qm_col = 3 * WIDTH_A + 2 * WIDTH_B
    assert qm_col % WIDTH_B == 0 and WIDTH_M <= WIDTH_B
    w_specs = [pl.BlockSpec((d, qkv_w), lambda i: (0, 0), pipeline_mode=pl.Buffered(1)),
               pl.BlockSpec((d, WIDTH_B), lambda i: (0, qm_col // WIDTH_B), pipeline_mode=pl.Buffered(1))]
    cache_specs = [pl.BlockSpec((bb,) + c.shape[1:], lambda i: (i, 0, 0, 0, 0)) for c in caches]
    whole = lambda shape: pl.BlockSpec(shape, lambda i: (0,) * len(shape))
    return pl.pallas_call(
        _sample_attn_kernel,
        grid=(n_batch // bb,),
        in_specs=[whole(xs.shape)] + w_specs + cache_specs
        + [pl.BlockSpec((bb,) + cache_mem.shape[1:], lambda i: (i, 0, 0, 0, 0))],
        out_specs=[whole((n_batch, GROUP_WIDTH_A)), whole((n_batch, WIDTH_M)),
                   whole((N_GROUPS_A, n_batch, 2 * GROUP_WIDTH_A))],
        out_shape=[jax.ShapeDtypeStruct((n_batch, GROUP_WIDTH_A), F32),
                   jax.ShapeDtypeStruct((n_batch, WIDTH_M), F32),
                   jax.ShapeDtypeStruct((N_GROUPS_A, n_batch, 2 * GROUP_WIDTH_A), F32)],
        scratch_shapes=[pltpu.VMEM((LANES, qkv_w + WIDTH_M), F32),
                        pltpu.VMEM((qkv_w, LANES), BF16),
                        pltpu.VMEM((qkv_w, LANES), F32),
                        pltpu.VMEM((GROUP_WIDTH_A, LANES), F32),
                        pltpu.VMEM((cache_mem.shape[1], N_HEADS_M, HEAD_DIM_M), F32)],
        compiler_params=_params("arbitrary"),
        name="sample_attn",
    )(xs, w_all, w_all, *caches, cache_mem)


def _mix_kernel(x_ref, oa_ref, mk_ref, mv_ref, xs_ref, oas_ref, oms_ref, w_ref, bgate_ref, lnvg_ref, lnvb_ref,
                ws_ref, bs_ref, wss_ref, bss_ref, wba_ref, wbb_ref, wbm_ref, wout_ref, ln1g_ref, ln1b_ref,
                x1_ref, x1s_ref, vrows_ref, *, alpha):
    shared = (w_ref, bgate_ref, lnvg_ref, lnvb_ref)
    tail = (wba_ref, wbb_ref, wbm_ref, wout_ref, ln1g_ref, ln1b_ref)
    is_sample_step = pl.program_id(0) == pl.num_programs(0) - 1

    @pl.when(jnp.logical_not(is_sample_step))
    def _():
        _mix_body(x_ref, oa_ref, (mk_ref, mv_ref), *shared, ws_ref, bs_ref, *tail, x1_ref, None,
                  sample=False, alpha=alpha)

    @pl.when(is_sample_step)
    def _():
        _mix_body(xs_ref, oas_ref, oms_ref, *shared, wss_ref, bss_ref, *tail, x1s_ref, vrows_ref,
                  sample=True, alpha=alpha)


def _mix_body(x_ref, oa_ref, mem_or_om, w_ref, bgate_ref, lnvg_ref, lnvb_ref, ws_ref, bs_ref,
              wba_ref, wbb_ref, wbm_ref, wout_ref, ln1g_ref, ln1b_ref, x1_ref, vrows_ref, *, sample, alpha):
    if sample:
        om_ref = mem_or_om
    else:
        mk_ref, mv_ref = mem_or_om
    d = x_ref.shape[1]
    rows = x_ref.shape[0]
    x = x_ref[...]
    xb = x.astype(BF16)
    col_u, col_v, col_qm, col_gate = 0, WIDTH_B, 2 * WIDTH_B, 2 * WIDTH_B + WIDTH_M

    h_all = _dot(xb, w_ref[:, 3 * WIDTH_A:])

    def gate(k):
        z = h_all[:, col_gate + k * d:col_gate + (k + 1) * d]
        return jax.nn.sigmoid(z + bgate_ref[k:k + 1, :])

    mixed = gate(0) * _dot(oa_ref[...].astype(BF16), wba_ref[...])

    u = jax.nn.gelu(h_all[:, col_u:col_u + WIDTH_B])
    v = _layer_norm(jax.nn.gelu(h_all[:, col_v:col_v + WIDTH_B]), lnvg_ref[...], lnvb_ref[...])
    if sample:
        vrows_ref[...] = v
        spatial = v.astype(BF16).astype(F32) * ws_ref[...].astype(BF16).astype(F32) + bs_ref[...]
    else:
        r_i = lax.broadcasted_iota(jnp.int32, (N_GROUPS_B * CHUNK, CHUNK), 0)
        c_i = lax.broadcasted_iota(jnp.int32, (N_GROUPS_B * CHUNK, CHUNK), 1)
        w_s = jnp.where((r_i % CHUNK) >= c_i, ws_ref[...], 0.0).astype(BF16)
        lane = lax.broadcasted_iota(jnp.int32, (CHUNK, WIDTH_B), 1)
        in_group = [(lane >= gb * GROUP_DIM_B) & (lane < (gb + 1) * GROUP_DIM_B) for gb in range(N_GROUPS_B)]
        vb = v.astype(BF16)
        parts = []
        for ch in range(rows // CHUNK):
            allg = _dot(w_s, vb[ch * CHUNK:(ch + 1) * CHUNK, :])
            sp = bs_ref[...]
            for gb in range(N_GROUPS_B):
                sp = sp + jnp.where(in_group[gb], allg[gb * CHUNK:(gb + 1) * CHUNK, :], 0.0)
            parts.append(sp)
        spatial = jnp.concatenate(parts, axis=0)
    o_b = u * spatial
    mixed = mixed + gate(1) * _dot(o_b.astype(BF16), wbb_ref[...])

    if sample:
        o_m = om_ref[...]
    else:
        qm = h_all[:, col_qm:col_qm + WIDTH_M].astype(BF16)
        heads = []
        for h in range(N_HEADS_M):
            hs = slice(h * HEAD_DIM_M, (h + 1) * HEAD_DIM_M)
            s = _dot_nt(qm[:, hs], mk_ref[:, hs]) * (HEAD_DIM_M ** -0.5)
            p = jnp.exp(s - jnp.max(s, axis=-1, keepdims=True))
            heads.append(_dot(p.astype(BF16), mv_ref[:, hs]) / jnp.sum(p, axis=-1, keepdims=True))
        o_m = jnp.concatenate(heads, axis=-1)
    mixed = mixed + gate(2) * _dot(o_m.astype(BF16), wbm_ref[...])

    x1_ref[...] = _layer_norm(alpha * x + _dot(mixed.astype(BF16), wout_ref[...]), ln1g_ref[...], ln1b_ref[...])


def _mix(x, o_a, mk, mv, xs, oa_s, om_s, w_all, b_gate, ln_v_g, ln_v_b, w_s, b_s, w_s_row, b_s_row,
         w_ba, w_bb, w_bm, w_out, ln1_g, ln1_b, *, alpha):
    s, d = x.shape
    n_tiles = s // MIX_ROWS
    tile = lambda width: pl.BlockSpec((MIX_ROWS, width), lambda i: (jnp.minimum(i, n_tiles - 1), 0))
    whole = lambda shape: pl.BlockSpec(shape, lambda i: (0,) * len(shape))
    operands = (x, o_a, mk, mv, xs, oa_s, om_s, w_all, b_gate, ln_v_g, ln_v_b, w_s, b_s, w_s_row, b_s_row,
                w_ba, w_bb, w_bm, w_out, ln1_g, ln1_b)
    n_s = xs.shape[0]
    return pl.pallas_call(
        functools.partial(_mix_kernel, alpha=alpha),
        grid=(n_tiles + 1,),
        in_specs=[tile(d), tile(GROUP_WIDTH_A)] + [_resident(t.shape) for t in operands[2:]],
        out_specs=[tile(d), whole((n_s, d)), whole((n_s, WIDTH_B))],
        out_shape=[jax.ShapeDtypeStruct((s, d), F32), jax.ShapeDtypeStruct((n_s, d), F32),
                   jax.ShapeDtypeStruct((n_s, WIDTH_B), F32)],
        compiler_params=_params("arbitrary"),
        name="mix",
    )(*operands)


def _ffn_kernel(x1_ref, x1s_ref, st0_ref, st1_ref, wup_ref, cw_ref, cb_ref, wdown_ref, g_ref, b_ref,
                y_ref, a_ref, ys_ref, as_ref, abuf_ref, *, alpha):
    consts = (wup_ref, cw_ref, cb_ref, wdown_ref, g_ref, b_ref)
    is_sample_step = pl.program_id(0) == pl.num_programs(0) - 1

    @pl.when(jnp.logical_not(is_sample_step))
    def _():
        _ffn_body(x1_ref, None, None, *consts, y_ref, a_ref, abuf_ref, sample=False, alpha=alpha)

    @pl.when(is_sample_step)
    def _():
        _ffn_body(x1s_ref, st0_ref, st1_ref, *consts, ys_ref, as_ref, None, sample=True, alpha=alpha)


def _ffn_body(x1_ref, st0_ref, st1_ref, wup_ref, cw_ref, cb_ref, wdown_ref, g_ref, b_ref, y_ref, a_ref, abuf_ref,
              *, sample, alpha):
    rows = x1_ref.shape[0]
    d_ff = wdown_ref.shape[0]
    if not sample:
        @pl.when(pl.program_id(0) == 0)
        def _():
            abuf_ref[0:SUBLANES, :] = jnp.zeros((SUBLANES, d_ff), F32)

    x1 = x1_ref[...]
    xb = x1.astype(BF16)
    a = _dot(xb, wup_ref[:, :d_ff])
    val = _dot(xb, wup_ref[:, d_ff:])
    if sample:
        a_ref[...] = a
        a_m2, a_m1 = st0_ref[...], st1_ref[...]
    else:
        abuf_ref[SUBLANES:SUBLANES + rows, :] = a
        a_m1 = abuf_ref[SUBLANES - 1:SUBLANES - 1 + rows, :]
        a_m2 = abuf_ref[SUBLANES - 2:SUBLANES - 2 + rows, :]
    conv = cb_ref[...] + cw_ref[0:1, :] * a_m2 + cw_ref[1:2, :] * a_m1 + cw_ref[2:3, :] * a
    h = jax.nn.gelu(conv) * val
    y_ref[...] = _layer_norm(alpha * x1 + _dot(h.astype(BF16), wdown_ref[...]), g_ref[...], b_ref[...])
    if not sample:
        tail = abuf_ref[rows:rows + SUBLANES, :]
        abuf_ref[0:SUBLANES, :] = tail
        a_ref[...] = tail


def _ffn(x1, x1_s, state, w_up, conv_w, conv_b, w_down, ln2_g, ln2_b, *, alpha):
    s, d = x1.shape
    n_s = x1_s.shape[0]
    d_ff = w_down.shape[0]
    n_tiles = s // FFN_ROWS
    tile = lambda width: pl.BlockSpec((FFN_ROWS, width), lambda i: (jnp.minimum(i, n_tiles - 1), 0))
    whole = lambda shape: pl.BlockSpec(shape, lambda i: (0,) * len(shape))
    operands = (x1, x1_s, state[:, 0], state[:, 1], w_up, conv_w, conv_b, w_down, ln2_g, ln2_b)
    return pl.pallas_call(
        functools.partial(_ffn_kernel, alpha=alpha),
        grid=(n_tiles + 1,),
        in_specs=[tile(d)] + [_resident(t.shape) for t in operands[1:]],
        out_specs=[tile(d), whole((SUBLANES, d_ff)), whole((n_s, d)), whole((n_s, d_ff))],
        out_shape=[jax.ShapeDtypeStruct((s, d), F32), jax.ShapeDtypeStruct((SUBLANES, d_ff), F32),
                   jax.ShapeDtypeStruct((n_s, d), F32), jax.ShapeDtypeStruct((n_s, d_ff), F32)],
        scratch_shapes=[pltpu.VMEM((FFN_ROWS + SUBLANES, d_ff), F32)],
        compiler_params=_params("arbitrary"),
        name="ffn",
    )(*operands)


def kernel(x_prompt, x_sample, mem_prompt, cache_win128_kv, cache_win512_kv, cache_win2048_kv, cache_mem_kv, state_ffn_conv, w_in, b_gate, ln_v_g, ln_v_b, w_spatial, b_spatial, w_mem_kv, w_branch_a, w_branch_b, w_branch_m, w_out, ln1_g, ln1_b, w_up, conv_w, conv_b, w_down, ln2_g, ln2_b):
    depth = w_in.shape[0]
    batch, seq, d_model = x_prompt.shape
    dec_batch, dec_seq, _ = x_sample.shape
    assert batch == 1 and dec_seq == 1 and seq % ATTN_BLOCK == 0
    alpha = (2.0 * depth) ** 0.25
    win_caches = (cache_win128_kv, cache_win512_kv, cache_win2048_kv)
    for cache, (win, _) in zip(win_caches, DILATED_GROUPS):
        assert cache.shape[2] == win and PAST_LEN >= win

    yp = x_prompt.reshape(seq, d_model)
    ys = x_sample.reshape(dec_batch, d_model)
    win_p = [[] for _ in range(N_GROUPS_A)]
    win_s = [[] for _ in range(N_GROUPS_A)]
    mem_p, conv_p, gmlp_s, conv_s = [], [], [], []
    for l in range(depth):
        w_all = w_in[l].astype(BF16)
        row2 = lambda t: t.reshape(1, -1)
        per_lane = lambda t: jnp.repeat(t, GROUP_DIM_B, axis=-1)
        consts_tail = (w_branch_a[l].astype(BF16), w_branch_b[l].astype(BF16), w_branch_m[l].astype(BF16),
                       w_out[l].astype(BF16), row2(ln1_g[l]), row2(ln1_b[l]))
        ffn_consts = (w_up[l].astype(BF16), conv_w[l], row2(conv_b[l]), w_down[l].astype(BF16),
                      row2(ln2_g[l]), row2(ln2_b[l]))

        mkv = _mem_kv_proj(mem_prompt[0], w_mem_kv[l].astype(BF16))
        mem_p.append(mkv.reshape(1, -1, 2, N_HEADS_M, HEAD_DIM_M))
        qkv, *kv_tails = _qkv_proj(yp, w_all)
        for g, (win, _) in enumerate(DILATED_GROUPS):
            tail = kv_tails[g].reshape(2, HEADS_PER_GROUP_A, HEAD_DIM_A, win)
            win_p[g].append(jnp.transpose(tail, (3, 0, 1, 2))[None])
        o_a = _dilated_attn(qkv)
        caches = [jnp.transpose(c[l], (0, 2, 3, 4, 1)) for c in win_caches]
        oa_s, om_s, kv_new = _sample_attn(ys, w_all, caches, cache_mem_kv[l])
        for g in range(N_GROUPS_A):
            win_s[g].append(kv_new[g].reshape(dec_batch, 1, 2, HEADS_PER_GROUP_A, HEAD_DIM_A))

        x1, x1_s, v_rows = _mix(yp, o_a, mkv[:, :WIDTH_M].astype(BF16), mkv[:, WIDTH_M:].astype(BF16), ys, oa_s, om_s,
                                w_all, b_gate[l], row2(ln_v_g[l]), row2(ln_v_b[l]),
                                w_spatial[l].reshape(N_GROUPS_B * CHUNK, CHUNK), per_lane(b_spatial[l].T),
                                row2(per_lane(w_spatial[l][:, 0, 0])), row2(per_lane(b_spatial[l][:, 0])),
                                *consts_tail, alpha=alpha)
        gmlp_s.append(v_rows.reshape(dec_batch, 1, WIDTH_B))
        yp_next, a_tail, ys_next, a_s = _ffn(x1, x1_s, state_ffn_conv[l], *ffn_consts, alpha=alpha)
        conv_p.append(a_tail[SUBLANES - (CONV_W - 1):][None])
        conv_s.append(jnp.stack([state_ffn_conv[l][:, 1], a_s], axis=1))
        yp, ys = yp_next, ys_next

    return (yp.reshape(batch, seq, d_model), ys.reshape(dec_batch, dec_seq, d_model),
            jnp.stack(win_p[0]), jnp.stack(win_p[1]), jnp.stack(win_p[2]),
            jnp.stack(mem_p), jnp.stack(conv_p),
            jnp.stack(win_s[0]), jnp.stack(win_s[1]), jnp.stack(win_s[2]),
            jnp.stack(gmlp_s), jnp.stack(conv_s))
```

```python
import functools

import jax
import jax.numpy as jnp
from jax import lax
from jax.experimental import pallas as pl
from jax.experimental.pallas import tpu as pltpu

BF16 = jnp.bfloat16
F32 = jnp.float32

HEAD_DIM_A = 64
HEADS_PER_GROUP_A = 4
DILATED_GROUPS = ((128, 1), (512, 4), (2048, 16))
N_GROUPS_A = len(DILATED_GROUPS)
GROUP_WIDTH_A = HEADS_PER_GROUP_A * HEAD_DIM_A
WIDTH_A = N_GROUPS_A * GROUP_WIDTH_A
NK = 128
CHUNK = 128
N_GROUPS_B = 4
WIDTH_B = 768
GROUP_DIM_B = WIDTH_B // N_GROUPS_B
N_HEADS_M = 4
HEAD_DIM_M = 128
WIDTH_M = N_HEADS_M * HEAD_DIM_M
N_BRANCH = 3
CONV_W = 3
LN_EPS = 1e-5
NEG = -1e30
PAST_LEN = 16384

LANES = 128
SUBLANES = 8
VMEM_LIMIT_BYTES = 58 * 1024 * 1024

ATTN_BLOCK = 2048
DEINTERLEAVE_STRIDE = 4
MIX_ROWS = 512
FFN_ROWS = 512
SAMPLE_BATCH_BLOCK = 2
SAMPLE_LANE_CHUNK = 512
SAMPLE_MEM_CHUNK = 32

_NT = (((1,), (1,)), ((), ()))


def _dot(a, b):
    return jnp.dot(a, b, preferred_element_type=F32)


def _dot_nt(a, b):
    return lax.dot_general(a, b, _NT, preferred_element_type=F32)


def _layer_norm(x, g, b):
    mu = jnp.mean(x, axis=-1, keepdims=True)
    var = jnp.mean(jnp.square(x - mu), axis=-1, keepdims=True)
    return (x - mu) * lax.rsqrt(var + LN_EPS) * g + b


def _params(*semantics):
    return pltpu.CompilerParams(dimension_semantics=semantics, vmem_limit_bytes=VMEM_LIMIT_BYTES)


def _resident(shape):
    zeros = (0,) * len(shape)
    return pl.BlockSpec(shape, lambda *_: zeros, pipeline_mode=pl.Buffered(1))


def _mem_kv_kernel(mem_ref, w_ref, o_ref):
    o_ref[...] = _dot(mem_ref[...].astype(BF16), w_ref[...])


def _mem_kv_proj(mem, w):
    n, d = mem.shape
    return pl.pallas_call(
        _mem_kv_kernel,
        out_shape=jax.ShapeDtypeStruct((n, w.shape[1]), F32),
        name="mem_kv_proj",
    )(mem, w)


def _qkv_proj_kernel(xt_ref, xbot_ref, w_ref, o_ref, kvt0_ref, kvt1_ref, kvt2_ref, xb_ref, stage_ref, tmp_ref):
    g_id = pl.program_id(1)

    @pl.when(g_id == 0)
    def _():
        half = ATTN_BLOCK // 2
        xb_ref[0:half, :] = xt_ref[...].astype(BF16)
        xb_ref[half:, :] = xbot_ref[...].astype(BF16)

    xb = xb_ref[...]
    last_block = pl.program_id(0) == pl.num_programs(0) - 1
    for g, (win, dil) in enumerate(DILATED_GROUPS):
        kvt_ref = (kvt0_ref, kvt1_ref, kvt2_ref)[g]

        @pl.when(g_id == g)
        def _(g=g, win=win, dil=dil, kvt_ref=kvt_ref):
            w_cols = lambda c: w_ref[:, c * WIDTH_A + g * GROUP_WIDTH_A:c * WIDTH_A + (g + 1) * GROUP_WIDTH_A]
            for c in range(3):
                w = w_cols(c)
                if c == 0:
                    w = w * jnp.asarray(HEAD_DIM_A ** -0.5, BF16)
                res = _dot(xb, w)
                if dil == 1:
                    o_ref[:, c * GROUP_WIDTH_A:(c + 1) * GROUP_WIDTH_A] = res.astype(BF16)
                    continue
                n_slabs = GROUP_WIDTH_A // LANES
                for s in range(n_slabs):
                    stage_ref[s] = res[:, s * LANES:(s + 1) * LANES]
                inner = DEINTERLEAVE_STRIDE if dil > DEINTERLEAVE_STRIDE else dil
                outer = dil // inner
                for s in range(n_slabs):
                    col = c * GROUP_WIDTH_A + s * LANES
                    for blk in range(ATTN_BLOCK // win):
                        base = blk * win
                        src_ref = stage_ref
                        if outer > 1:
                            part = win // inner
                            for lo in range(inner):
                                tmp_ref[s, base + lo * part:base + (lo + 1) * part, :] = (
                                    stage_ref[s, pl.ds(base + lo, part, stride=inner), :])
                            src_ref = tmp_ref
                        for lo in range(inner):
                            for hi in range(outer):
                                first = base + lo * (win // inner) + hi if outer > 1 else base + lo
                                rows = src_ref[s, pl.ds(first, NK, stride=outer if outer > 1 else inner), :]
                                dst = base + (hi * inner + lo) * NK
                                o_ref[dst:dst + NK, col:col + LANES] = rows.astype(BF16)

            @pl.when(last_block)
            def _():
                tail = xb[ATTN_BLOCK - win:, :]
                kvt_ref[0:GROUP_WIDTH_A, :] = _dot(tail, w_cols(1)).T
                kvt_ref[GROUP_WIDTH_A:2 * GROUP_WIDTH_A, :] = _dot(tail, w_cols(2)).T


def _qkv_proj(x, w_all):
    s, d = x.shape
    n_blk = s // ATTN_BLOCK
    half = ATTN_BLOCK // 2
    x_half = lambda which: pl.BlockSpec(
        (half, d), lambda n, g: (2 * jnp.minimum(n + jnp.minimum(jnp.maximum(g - which, 0), 1), n_blk - 1) + which, 0))
    kvt_shapes = [(2 * GROUP_WIDTH_A, win) for win, _ in DILATED_GROUPS]
    return pl.pallas_call(
        _qkv_proj_kernel,
        grid=(n_blk, N_GROUPS_A),
        in_specs=[x_half(0), x_half(1),
                  pl.BlockSpec((d, 3 * WIDTH_A), lambda n, g: (0, 0), pipeline_mode=pl.Buffered(1))],
        out_specs=[pl.BlockSpec((None, ATTN_BLOCK, 3 * GROUP_WIDTH_A), lambda n, g: (g, n, 0))]
        + [pl.BlockSpec(shape, lambda n, g: (0, 0)) for shape in kvt_shapes],
        out_shape=[jax.ShapeDtypeStruct((N_GROUPS_A, s, 3 * GROUP_WIDTH_A), BF16)]
        + [jax.ShapeDtypeStruct(shape, F32) for shape in kvt_shapes],
        scratch_shapes=[pltpu.VMEM((ATTN_BLOCK, d), BF16),
                        pltpu.VMEM((GROUP_WIDTH_A // LANES, ATTN_BLOCK, LANES), F32),
                        pltpu.VMEM((GROUP_WIDTH_A // LANES, ATTN_BLOCK, LANES), F32)],
        compiler_params=_params("arbitrary", "arbitrary"),
        name="qkv_proj",
    )(x, x, w_all)


def _attn_units(units, prev_shifts):
    first_head = lax.broadcasted_iota(jnp.int32, (NK, LANES), 1) < HEAD_DIM_A
    row = lax.broadcasted_iota(jnp.int32, (2 * NK, NK), 0) % NK
    col = lax.broadcasted_iota(jnp.int32, (2 * NK, NK), 1)
    cur_ok = col <= row
    prev_ok = {}
    scores = []
    for (q, k_p, k_c, _, _), shift in zip(units, prev_shifts):
        zero = jnp.zeros_like(q)
        q_st = jnp.concatenate([jnp.where(first_head, q, zero), jnp.where(first_head, zero, q)], axis=0)
        if id(shift) not in prev_ok:
            prev_ok[id(shift)] = col >= row + shift
        scores.append((jnp.where(cur_ok, _dot_nt(q_st, k_c), NEG),
                       jnp.where(prev_ok[id(shift)], _dot_nt(q_st, k_p), NEG)))
    maxes = [jnp.max(jnp.maximum(s_c, s_p), axis=-1, keepdims=True) for s_c, s_p in scores]
    probs = [(jnp.exp(s_c - m), jnp.exp(s_p - m)) for (s_c, s_p), m in zip(scores, maxes)]
    dens = [jnp.sum(p_c + p_p, axis=-1, keepdims=True) for p_c, p_p in probs]
    results = []
    for (_, _, _, v_p, v_c), (p_c, p_p), m, den in zip(units, probs, maxes, dens):
        o_st = _dot(p_c.astype(BF16), v_c) + _dot(p_p.astype(BF16), v_p)
        pick = lambda t: jnp.where(first_head, jnp.broadcast_to(t[:NK], (NK, LANES)), jnp.broadcast_to(t[NK:], (NK, LANES)))
        den_sel = pick(den)
        results.append((pick(o_st) / den_sel, pick(m) + jnp.log(den_sel)))
    return results


def _dilated_attn_kernel(qkv0_ref, qkv1_ref, qkv2_ref, o_ref, kv0_ref, kv1_ref, kv2_ref, og_ref, lg_ref):
    n = pl.program_id(0)
    qkv_refs = (qkv0_ref, qkv1_ref, qkv2_ref)
    kv_refs = (kv0_ref, kv1_ref, kv2_ref)
    n_sub = ATTN_BLOCK // NK
    n_hp = GROUP_WIDTH_A // LANES

    @pl.when(n == 0)
    def _():
        for kv_ref in kv_refs:
            kv_ref[...] = jnp.zeros(kv_ref.shape, BF16)

    def sub_block(t, carry, prev_in_block):
        off = pl.multiple_of(t * NK, NK)
        units, shifts, dsts = [], [], []
        for g, (win, dil) in enumerate(DILATED_GROUPS):
            win_blk = t // dil
            res = t - win_blk * dil
            has_prev = (n * (ATTN_BLOCK // win) + win_blk) > 0
            nat = win_blk * win + res
            shift = jnp.where(has_prev, 0, 2 * NK)
            for hp in range(n_hp):
                qs = slice(hp * LANES, (hp + 1) * LANES)
                ks = slice(GROUP_WIDTH_A + hp * LANES, GROUP_WIDTH_A + (hp + 1) * LANES)
                vs = slice(2 * GROUP_WIDTH_A + hp * LANES, 2 * GROUP_WIDTH_A + (hp + 1) * LANES)
                cur = pl.ds(off, NK)
                if prev_in_block[g]:
                    prev = pl.ds(off - win, NK)
                    k_prev, v_prev = qkv_refs[g][prev, ks], qkv_refs[g][prev, vs]
                else:
                    k_prev = kv_refs[g][cur, qs]
                    v_prev = kv_refs[g][cur, GROUP_WIDTH_A + hp * LANES:GROUP_WIDTH_A + (hp + 1) * LANES]
                units.append((qkv_refs[g][cur, qs], k_prev, qkv_refs[g][cur, ks], v_prev, qkv_refs[g][cur, vs]))
                shifts.append(shift)
                dsts.append((g, hp, pl.ds(off, NK) if dil == 1 else pl.ds(nat, NK, stride=dil)))
        for (g, hp, dst), (out, lse) in zip(dsts, _attn_units(units, shifts)):
            og_ref[g, hp, dst, :] = out
            lg_ref[g, hp, dst, :] = lse
        return carry

    bounds = sorted({0, n_sub} | {min(win // NK, n_sub) for win, _ in DILATED_GROUPS})
    for lo, hi in zip(bounds[:-1], bounds[1:]):
        in_block = tuple(lo * NK >= win for win, _ in DILATED_GROUPS)
        lax.fori_loop(lo, hi, functools.partial(sub_block, prev_in_block=in_block), 0)
    for (win, _), qkv_ref, kv_ref in zip(DILATED_GROUPS, qkv_refs, kv_refs):
        kv_ref[...] = qkv_ref[ATTN_BLOCK - win:, GROUP_WIDTH_A:]

    def merge(t, carry):
        off = pl.multiple_of(t * NK, NK)
        for hp in range(GROUP_WIDTH_A // LANES):
            lses = [lg_ref[g, hp, pl.ds(off, NK), :] for g in range(N_GROUPS_A)]
            m = jnp.maximum(jnp.maximum(lses[0], lses[1]), lses[2])
            es = [jnp.exp(l - m) for l in lses]
            num = sum(es[g] * og_ref[g, hp, pl.ds(off, NK), :] for g in range(N_GROUPS_A))
            o_ref[pl.ds(off, NK), hp * LANES:(hp + 1) * LANES] = (num / (es[0] + es[1] + es[2])).astype(BF16)
        return carry

    lax.fori_loop(0, n_sub, merge, 0)


def _dilated_attn(qkv):
    _, s, _ = qkv.shape
    n_blk = s // ATTN_BLOCK
    n_hp = GROUP_WIDTH_A // LANES
    in_specs = [pl.BlockSpec((None, ATTN_BLOCK, 3 * GROUP_WIDTH_A), functools.partial(lambda n, g: (g, n, 0), g=g))
                for g in range(N_GROUPS_A)]
    return pl.pallas_call(
        _dilated_attn_kernel,
        grid=(n_blk,),
        in_specs=in_specs,
        out_specs=pl.BlockSpec((ATTN_BLOCK, GROUP_WIDTH_A), lambda n: (n, 0)),
        out_shape=jax.ShapeDtypeStruct((s, GROUP_WIDTH_A), BF16),
        scratch_shapes=[pltpu.VMEM((win, 2 * GROUP_WIDTH_A), BF16) for win, _ in DILATED_GROUPS]
        + [pltpu.VMEM((N_GROUPS_A, n_hp, ATTN_BLOCK, LANES), F32),
           pltpu.VMEM((N_GROUPS_A, n_hp, ATTN_BLOCK, LANES), F32)],
        compiler_params=_params("arbitrary"),
        name="dilated_attn",
    )(qkv, qkv, qkv)


def _to_heads(row, n_heads, head_dim):
    return jnp.concatenate([row[:, h * head_dim:(h + 1) * head_dim] for h in range(n_heads)], axis=0)


def _from_heads(t):
    return jnp.concatenate([t[h:h + 1] for h in range(t.shape[0])], axis=1)


def _window_cache_attention(c_ref, j, cols_ref, base, dil):
    win = c_ref.shape[-1]
    lc = min(win, SAMPLE_LANE_CHUNK)
    n_chunks = win // lc
    lane = lax.broadcasted_iota(jnp.int32, (1, win), 1)
    wanted = (lane % dil) == 0
    out = []
    for h in range(HEADS_PER_GROUP_A):
        lo = base + h * HEAD_DIM_A
        q, k_new, v_new = (cols_ref[lo + c * GROUP_WIDTH_A:lo + c * GROUP_WIDTH_A + HEAD_DIM_A, :] for c in range(3))
        q_wide = jnp.concatenate([q] * (lc // LANES), axis=1)
        parts = [jnp.sum(c_ref[j, 0, h, :, i * lc:(i + 1) * lc] * q_wide, axis=0, keepdims=True)
                 for i in range(n_chunks)]
        s = jnp.where(wanted, jnp.concatenate(parts, axis=1), NEG)
        s_n = jnp.sum(k_new[:, 0:1] * q[:, 0:1], axis=0, keepdims=True)
        m = jnp.maximum(jnp.max(s, axis=-1, keepdims=True), s_n)
        p = jnp.exp(s - m)
        p_n = jnp.exp(s_n - m)
        den = jnp.sum(p, axis=-1, keepdims=True) + p_n
        acc = p_n * v_new[:, 0:1]
        for i in range(n_chunks):
            acc = acc + jnp.sum(c_ref[j, 1, h, :, i * lc:(i + 1) * lc] * p[:, i * lc:(i + 1) * lc],
                                axis=-1, keepdims=True)
        out.append((acc / den, m + jnp.log(den)))
    return out


def _memory_cache_attention(cm_ref, j, q, s_ref):
    n_heads, head_dim = q.shape
    chunk = SAMPLE_MEM_CHUNK
    n_chunks = cm_ref.shape[1] // chunk
    m = jnp.full((n_heads, head_dim), NEG, F32)
    for c in range(n_chunks):
        pos = slice(c * chunk, (c + 1) * chunk)
        s = jnp.sum(cm_ref[j, pos, 0] * q[None], axis=-1, keepdims=True) * (head_dim ** -0.5)
        s = jnp.broadcast_to(s, (chunk, n_heads, head_dim))
        s_ref[pos] = s
        m = jnp.maximum(m, jnp.max(s, axis=0))
    den = jnp.zeros((n_heads, head_dim), F32)
    acc = jnp.zeros((n_heads, head_dim), F32)
    for c in range(n_chunks):
        pos = slice(c * chunk, (c + 1) * chunk)
        p = jnp.exp(s_ref[pos] - m[None])
        den = den + jnp.sum(p, axis=0)
        acc = acc + jnp.sum(p * cm_ref[j, pos, 1], axis=0)
    return acc / den


def _sample_attn_kernel(x_ref, wqkv_ref, wqm_ref, c0_ref, c1_ref, c2_ref, cm_ref,
                        oa_ref, om_ref, kvn_ref, proj_ref, projt_ref, cols_ref, oat_ref, ms_ref):
    step = pl.program_id(0)
    caches = (c0_ref, c1_ref, c2_ref)
    n_batch, d = x_ref.shape
    qkv_w = N_GROUPS_A * 3 * GROUP_WIDTH_A

    @pl.when(step == 0)
    def _():
        xb = jnp.concatenate([x_ref[...], jnp.zeros((LANES - n_batch, d), F32)], axis=0).astype(BF16)
        for part in range(3):
            h = _dot(xb, wqkv_ref[:, part * WIDTH_A:(part + 1) * WIDTH_A])
            if part == 0:
                h = h * (HEAD_DIM_A ** -0.5)
            for g in range(N_GROUPS_A):
                h_g = h[:, g * GROUP_WIDTH_A:(g + 1) * GROUP_WIDTH_A]
                dst = (3 * g + part) * GROUP_WIDTH_A
                proj_ref[:, dst:dst + GROUP_WIDTH_A] = h_g
                if part > 0:
                    kvn_ref[g, :, (part - 1) * GROUP_WIDTH_A:part * GROUP_WIDTH_A] = h_g[:n_batch]
        proj_ref[:, qkv_w:] = _dot(xb, wqm_ref[:, :WIDTH_M])
        for t in range(qkv_w // LANES):
            projt_ref[t * LANES:(t + 1) * LANES, :] = proj_ref[:, t * LANES:(t + 1) * LANES].T.astype(BF16)
        oat_ref[...] = jnp.zeros(oat_ref.shape, F32)

    def one_row(j, carry):
        b = step * SAMPLE_BATCH_BLOCK + j
        pick_row = (lax.broadcasted_iota(jnp.int32, (LANES, LANES), 0) == b).astype(BF16)
        cols_ref[...] = _dot(projt_ref[...], pick_row)
        per_group = [_window_cache_attention(caches[g], j, cols_ref, g * 3 * GROUP_WIDTH_A, dil)
                     for g, (_, dil) in enumerate(DILATED_GROUPS)]
        this_lane = lax.broadcasted_iota(jnp.int32, (1, LANES), 1) == b
        for h in range(HEADS_PER_GROUP_A):
            outs = [per_group[g][h][0] for g in range(N_GROUPS_A)]
            lses = [per_group[g][h][1] for g in range(N_GROUPS_A)]
            m3 = jnp.maximum(jnp.maximum(lses[0], lses[1]), lses[2])
            es = [jnp.exp(l - m3) for l in lses]
            oa = (es[0] * outs[0] + es[1] * outs[1] + es[2] * outs[2]) / (es[0] + es[1] + es[2])
            rows = slice(h * HEAD_DIM_A, (h + 1) * HEAD_DIM_A)
            oat_ref[rows, :] = jnp.where(this_lane, oa, oat_ref[rows, :])

        qm = _to_heads(proj_ref[pl.ds(b, 1), qkv_w:], N_HEADS_M, HEAD_DIM_M)
        om_ref[pl.ds(b, 1), :] = _from_heads(_memory_cache_attention(cm_ref, j, qm, ms_ref))
        return carry

    lax.fori_loop(0, SAMPLE_BATCH_BLOCK, one_row, 0)

    @pl.when(step == pl.num_programs(0) - 1)
    def _():
        oa_ref[...] = oat_ref[...].T[:n_batch, :]


def _sample_attn(xs, w_all, caches, cache_mem):
    n_batch, d = xs.shape
    assert n_batch <= LANES
    bb = SAMPLE_BATCH_BLOCK
    qkv_w = N_GROUPS_A * 3 * GROUP_WIDTH_A
    qm_col = 3 * WIDTH_A + 2 * WIDTH_B
    assert qm_col % WIDTH_B == 0 and WIDTH_M <= WIDTH_B
    w_specs = [pl.BlockSpec((d, qkv_w), lambda i: (0, 0), pipeline_mode=pl.Buffered(1)),
               pl.BlockSpec((d, WIDTH_B), lambda i: (0, qm_col // WIDTH_B), pipeline_mode=pl.Buffered(1))]
    cache_specs = [pl.BlockSpec((bb,) + c.shape[1:], lambda i: (i, 0, 0, 0, 0)) for c in caches]
    whole = lambda shape: pl.BlockSpec(shape, lambda i: (0,) * len(shape))
    return pl.pallas_call(
        _sample_attn_kernel,
        grid=(n_batch // bb,),
        in_specs=[whole(xs.shape)] + w_specs + cache_specs
        + [pl.BlockSpec((bb,) + cache_mem.shape[1:], lambda i: (i, 0, 0, 0, 0))],
        out_specs=[whole((n_batch, GROUP_WIDTH_A)), whole((n_batch, WIDTH_M)),
                   whole((N_GROUPS_A, n_batch, 2 * GROUP_WIDTH_A))],
        out_shape=[jax.ShapeDtypeStruct((n_batch, GROUP_WIDTH_A), F32),
                   jax.ShapeDtypeStruct((n_batch, WIDTH_M), F32),
                   jax.ShapeDtypeStruct((N_GROUPS_A, n_batch, 2 * GROUP_WIDTH_A), F32)],
        scratch_shapes=[pltpu.VMEM((LANES, qkv_w + WIDTH_M), F32),
                        pltpu.VMEM((qkv_w, LANES), BF16),
                        pltpu.VMEM((qkv_w, LANES), F32),
                        pltpu.VMEM((GROUP_WIDTH_A, LANES), F32),
                        pltpu.VMEM((cache_mem.shape[1], N_HEADS_M, HEAD_DIM_M), F32)],
        compiler_params=_params("arbitrary"),
        name="sample_attn",
    )(xs, w_all, w_all, *caches, cache_mem)


def _mix_kernel(x_ref, oa_ref, mk_ref, mv_ref, xs_ref, oas_ref, oms_ref, w_ref, bgate_ref, lnvg_ref, lnvb_ref,
                ws_ref, bs_ref, wss_ref, bss_ref, wba_ref, wbb_ref, wbm_ref, wout_ref, ln1g_ref, ln1b_ref,
                x1_ref, x1s_ref, vrows_ref, *, alpha):
    shared = (w_ref, bgate_ref, lnvg_ref, lnvb_ref)
    tail = (wba_ref, wbb_ref, wbm_ref, wout_ref, ln1g_ref, ln1b_ref)
    is_sample_step = pl.program_id(0) == pl.num_programs(0) - 1

    @pl.when(jnp.logical_not(is_sample_step))
    def _():
        _mix_body(x_ref, oa_ref, (mk_ref, mv_ref), *shared, ws_ref, bs_ref, *tail, x1_ref, None,
                  sample=False, alpha=alpha)

    @pl.when(is_sample_step)
    def _():
        _mix_body(xs_ref, oas_ref, oms_ref, *shared, wss_ref, bss_ref, *tail, x1s_ref, vrows_ref,
                  sample=True, alpha=alpha)


def _mix_body(x_ref, oa_ref, mem_or_om, w_ref, bgate_ref, lnvg_ref, lnvb_ref, ws_ref, bs_ref,
              wba_ref, wbb_ref, wbm_ref, wout_ref, ln1g_ref, ln1b_ref, x1_ref, vrows_ref, *, sample, alpha):
    if sample:
        om_ref = mem_or_om
    else:
        mk_ref, mv_ref = mem_or_om
    d = x_ref.shape[1]
    rows = x_ref.shape[0]
    x = x_ref[...]
    xb = x.astype(BF16)
    col_u, col_v, col_qm, col_gate = 0, WIDTH_B, 2 * WIDTH_B, 2 * WIDTH_B + WIDTH_M

    h_all = _dot(xb, w_ref[:, 3 * WIDTH_A:])

    def gate(k):
        z = h_all[:, col_gate + k * d:col_gate + (k + 1) * d]
        return 0.5 * jnp.tanh(0.5 * (z + bgate_ref[k:k + 1, :])) + 0.5

    mixed = gate(0) * _dot(oa_ref[...].astype(BF16), wba_ref[...])

    u = jax.nn.gelu(h_all[:, col_u:col_u + WIDTH_B])
    v = _layer_norm(jax.nn.gelu(h_all[:, col_v:col_v + WIDTH_B]), lnvg_ref[...], lnvb_ref[...])
    if sample:
        vrows_ref[...] = v
        spatial = v.astype(BF16).astype(F32) * ws_ref[...].astype(BF16).astype(F32) + bs_ref[...]
    else:
        r_i = lax.broadcasted_iota(jnp.int32, (N_GROUPS_B * CHUNK, CHUNK), 0)
        c_i = lax.broadcasted_iota(jnp.int32, (N_GROUPS_B * CHUNK, CHUNK), 1)
        w_s = jnp.where((r_i % CHUNK) >= c_i, ws_ref[...], 0.0).astype(BF16)
        lane = lax.broadcasted_iota(jnp.int32, (CHUNK, WIDTH_B), 1)
        in_group = [(lane >= gb * GROUP_DIM_B) & (lane < (gb + 1) * GROUP_DIM_B) for gb in range(N_GROUPS_B)]
        vb = v.astype(BF16)
        parts = []
        for ch in range(rows // CHUNK):
            allg = _dot(w_s, vb[ch * CHUNK:(ch + 1) * CHUNK, :])
            sp = bs_ref[...]
            for gb in range(N_GROUPS_B):
                sp = sp + jnp.where(in_group[gb], allg[gb * CHUNK:(gb + 1) * CHUNK, :], 0.0)
            parts.append(sp)
        spatial = jnp.concatenate(parts, axis=0)
    o_b = u * spatial
    mixed = mixed + gate(1) * _dot(o_b.astype(BF16), wbb_ref[...])

    if sample:
        o_m = om_ref[...]
    else:
        qm = h_all[:, col_qm:col_qm + WIDTH_M].astype(BF16)
        heads = []
        for h in range(N_HEADS_M):
            hs = slice(h * HEAD_DIM_M, (h + 1) * HEAD_DIM_M)
            s = _dot_nt(qm[:, hs], mk_ref[:, hs]) * (HEAD_DIM_M ** -0.5)
            p = jnp.exp(s - jnp.max(s, axis=-1, keepdims=True))
            heads.append(_dot(p.astype(BF16), mv_ref[:, hs]) / jnp.sum(p, axis=-1, keepdims=True))
        o_m = jnp.concatenate(heads, axis=-1)
    mixed = mixed + gate(2) * _dot(o_m.astype(BF16), wbm_ref[...])

    x1_ref[...] = _layer_norm(alpha * x + _dot(mixed.astype(BF16), wout_ref[...]), ln1g_ref[...], ln1b_ref[...])


def _mix(x, o_a, mk, mv, xs, oa_s, om_s, w_all, b_gate, ln_v_g, ln_v_b, w_s, b_s, w_s_row, b_s_row,
         w_ba, w_bb, w_bm, w_out, ln1_g, ln1_b, *, alpha):
    s, d = x.shape
    n_tiles = s // MIX_ROWS
    tile = lambda width: pl.BlockSpec((MIX_ROWS, width), lambda i: (jnp.minimum(i, n_tiles - 1), 0))
    whole = lambda shape: pl.BlockSpec(shape, lambda i: (0,) * len(shape))
    operands = (x, o_a, mk, mv, xs, oa_s, om_s, w_all, b_gate, ln_v_g, ln_v_b, w_s, b_s, w_s_row, b_s_row,
                w_ba, w_bb, w_bm, w_out, ln1_g, ln1_b)
    n_s = xs.shape[0]
    return pl.pallas_call(
        functools.partial(_mix_kernel, alpha=alpha),
        grid=(n_tiles + 1,),
        in_specs=[tile(d), tile(GROUP_WIDTH_A)] + [_resident(t.shape) for t in operands[2:]],
        out_specs=[tile(d), whole((n_s, d)), whole((n_s, WIDTH_B))],
        out_shape=[jax.ShapeDtypeStruct((s, d), F32), jax.ShapeDtypeStruct((n_s, d), F32),
                   jax.ShapeDtypeStruct((n_s, WIDTH_B), F32)],
        compiler_params=_params("arbitrary"),
        name="mix",
    )(*operands)


def _ffn_kernel(x1_ref, x1s_ref, st0_ref, st1_ref, wup_ref, cw_ref, cb_ref, wdown_ref, g_ref, b_ref,
                y_ref, a_ref, ys_ref, as_ref, abuf_ref, *, alpha):
    consts = (wup_ref, cw_ref, cb_ref, wdown_ref, g_ref, b_ref)
    is_sample_step = pl.program_id(0) == pl.num_programs(0) - 1

    @pl.when(jnp.logical_not(is_sample_step))
    def _():
        _ffn_body(x1_ref, None, None, *consts, y_ref, a_ref, abuf_ref, sample=False, alpha=alpha)

    @pl.when(is_sample_step)
    def _():
        _ffn_body(x1s_ref, st0_ref, st1_ref, *consts, ys_ref, as_ref, None, sample=True, alpha=alpha)


def _ffn_body(x1_ref, st0_ref, st1_ref, wup_ref, cw_ref, cb_ref, wdown_ref, g_ref, b_ref, y_ref, a_ref, abuf_ref,
              *, sample, alpha):
    rows = x1_ref.shape[0]
    d_ff = wdown_ref.shape[0]
    if not sample:
        @pl.when(pl.program_id(0) == 0)
        def _():
            abuf_ref[0:SUBLANES, :] = jnp.zeros((SUBLANES, d_ff), F32)

    x1 = x1_ref[...]
    xb = x1.astype(BF16)
    a = _dot(xb, wup_ref[:, :d_ff])
    val = _dot(xb, wup_ref[:, d_ff:])
    if sample:
        a_ref[...] = a
        a_m2, a_m1 = st0_ref[...], st1_ref[...]
    else:
        abuf_ref[SUBLANES:SUBLANES + rows, :] = a
        a_m1 = abuf_ref[SUBLANES - 1:SUBLANES - 1 + rows, :]
        a_m2 = abuf_ref[SUBLANES - 2:SUBLANES - 2 + rows, :]
    conv = cb_ref[...] + cw_ref[0:1, :] * a_m2 + cw_ref[1:2, :] * a_m1 + cw_ref[2:3, :] * a
    h = jax.nn.gelu(conv) * val
    y_ref[...] = _layer_norm(alpha * x1 + _dot(h.astype(BF16), wdown_ref[...]), g_ref[...], b_ref[...])
    if not sample:
        tail = abuf_ref[rows:rows + SUBLANES, :]
        abuf_ref[0:SUBLANES, :] = tail
        a_ref[...] = tail


def _ffn(x1, x1_s, state, w_up, conv_w, conv_b, w_down, ln2_g, ln2_b, *, alpha):
    s, d = x1.shape
    n_s = x1_s.shape[0]
    d_ff = w_down.shape[0]
    n_tiles = s // FFN_ROWS
    tile = lambda width: pl.BlockSpec((FFN_ROWS, width), lambda i: (jnp.minimum(i, n_tiles - 1), 0))
    whole = lambda shape: pl.BlockSpec(shape, lambda i: (0,) * len(shape))
    operands = (x1, x1_s, state[:, 0], state[:, 1], w_up, conv_w, conv_b, w_down, ln2_g, ln2_b)
    return pl.pallas_call(
        functools.partial(_ffn_kernel, alpha=alpha),
        grid=(n_tiles + 1,),
        in_specs=[tile(d)] + [_resident(t.shape) for t in operands[1:]],
        out_specs=[tile(d), whole((SUBLANES, d_ff)), whole((n_s, d)), whole((n_s, d_ff))],
        out_shape=[jax.ShapeDtypeStruct((s, d), F32), jax.ShapeDtypeStruct((SUBLANES, d_ff), F32),
                   jax.ShapeDtypeStruct((n_s, d), F32), jax.ShapeDtypeStruct((n_s, d_ff), F32)],
        scratch_shapes=[pltpu.VMEM((FFN_ROWS + SUBLANES, d_ff), F32)],
        compiler_params=_params("arbitrary"),
        name="ffn",
    )(*operands)


def kernel(x_prompt, x_sample, mem_prompt, cache_win128_kv, cache_win512_kv, cache_win2048_kv, cache_mem_kv, state_ffn_conv, w_in, b_gate, ln_v_g, ln_v_b, w_spatial, b_spatial, w_mem_kv, w_branch_a, w_branch_b, w_branch_m, w_out, ln1_g, ln1_b, w_up, conv_w, conv_b, w_down, ln2_g, ln2_b):
    depth = w_in.shape[0]
    batch, seq, d_model = x_prompt.shape
    dec_batch, dec_seq, _ = x_sample.shape
    assert batch == 1 and dec_seq == 1 and seq % ATTN_BLOCK == 0
    alpha = (2.0 * depth) ** 0.25
    win_caches = (cache_win128_kv, cache_win512_kv, cache_win2048_kv)
    for cache, (win, _) in zip(win_caches, DILATED_GROUPS):
        assert cache.shape[2] == win and PAST_LEN >= win

    yp = x_prompt.reshape(seq, d_model)
    ys = x_sample.reshape(dec_batch, d_model)
    win_p = [[] for _ in range(N_GROUPS_A)]
    win_s = [[] for _ in range(N_GROUPS_A)]
    mem_p, conv_p, gmlp_s, conv_s = [], [], [], []
    for l in range(depth):
        w_all = w_in[l].astype(BF16)
        row2 = lambda t: t.reshape(1, -1)
        per_lane = lambda t: jnp.repeat(t, GROUP_DIM_B, axis=-1)
        consts_tail = (w_branch_a[l].astype(BF16), w_branch_b[l].astype(BF16), w_branch_m[l].astype(BF16),
                       w_out[l].astype(BF16), row2(ln1_g[l]), row2(ln1_b[l]))
        ffn_consts = (w_up[l].astype(BF16), conv_w[l], row2(conv_b[l]), w_down[l].astype(BF16),
                      row2(ln2_g[l]), row2(ln2_b[l]))

        mkv = _mem_kv_proj(mem_prompt[0], w_mem_kv[l].astype(BF16))
        mem_p.append(mkv.reshape(1, -1, 2, N_HEADS_M, HEAD_DIM_M))
        qkv, *kv_tails = _qkv_proj(yp, w_all)
        for g, (win, _) in enumerate(DILATED_GROUPS):
            tail = kv_tails[g].reshape(2, HEADS_PER_GROUP_A, HEAD_DIM_A, win)
            win_p[g].append(jnp.transpose(tail, (3, 0, 1, 2))[None])
        o_a = _dilated_attn(qkv)
        caches = [jnp.transpose(c[l], (0, 2, 3, 4, 1)) for c in win_caches]
        oa_s, om_s, kv_new = _sample_attn(ys, w_all, caches, cache_mem_kv[l])
        for g in range(N_GROUPS_A):
            win_s[g].append(kv_new[g].reshape(dec_batch, 1, 2, HEADS_PER_GROUP_A, HEAD_DIM_A))

        x1, x1_s, v_rows = _mix(yp, o_a, mkv[:, :WIDTH_M].astype(BF16), mkv[:, WIDTH_M:].astype(BF16), ys, oa_s, om_s,
                                w_all, b_gate[l], row2(ln_v_g[l]), row2(ln_v_b[l]),
                                w_spatial[l].reshape(N_GROUPS_B * CHUNK, CHUNK), per_lane(b_spatial[l].T),
                                row2(per_lane(w_spatial[l][:, 0, 0])), row2(per_lane(b_spatial[l][:, 0])),
                                *consts_tail, alpha=alpha)
        gmlp_s.append(v_rows.reshape(dec_batch, 1, WIDTH_B))
        yp_next, a_tail, ys_next, a_s = _ffn(x1, x1_s, state_ffn_conv[l], *ffn_consts, alpha=alpha)
        conv_p.append(a_tail[SUBLANES - (CONV_W - 1):][None])
        conv_s.append(jnp.stack([state_ffn_conv[l][:, 1], a_s], axis=1))
        yp, ys = yp_next, ys_next

    return (yp.reshape(batch, seq, d_model), ys.reshape(dec_batch, dec_seq, d_model),
            jnp.stack(win_p[0]), jnp.stack(win_p[1]), jnp.stack(win_p[2]),
            jnp.stack(mem_p), jnp.stack(conv_p),
            jnp.stack(win_s[0]), jnp.stack(win_s[1]), jnp.stack(win_s[2]),
            jnp.stack(gmlp_s), jnp.stack(conv_s))
```

```python
import functools

import jax
import jax.numpy as jnp
from jax import lax
from jax.experimental import pallas as pl
from jax.experimental.pallas import tpu as pltpu

BF16 = jnp.bfloat16
F32 = jnp.float32

HEAD_DIM_A = 64
HEADS_PER_GROUP_A = 4
DILATED_GROUPS = ((128, 1), (512, 4), (2048, 16))
N_GROUPS_A = len(DILATED_GROUPS)
GROUP_WIDTH_A = HEADS_PER_GROUP_A * HEAD_DIM_A
WIDTH_A = N_GROUPS_A * GROUP_WIDTH_A
NK = 128
CHUNK = 128
N_GROUPS_B = 4
WIDTH_B = 768
GROUP_DIM_B = WIDTH_B // N_GROUPS_B
N_HEADS_M = 4
HEAD_DIM_M = 128
WIDTH_M = N_HEADS_M * HEAD_DIM_M
N_BRANCH = 3
CONV_W = 3
LN_EPS = 1e-5
NEG = -1e30
PAST_LEN = 16384

LANES = 128
SUBLANES = 8
VMEM_LIMIT_BYTES = 58 * 1024 * 1024

ATTN_BLOCK = 2048
DEINTERLEAVE_STRIDE = 4
MIX_ROWS = 512
FFN_ROWS = 512
NORM_ROW_PARTS = 4
SAMPLE_BATCH_BLOCK = 2
SAMPLE_LANE_CHUNK = 512
SAMPLE_MEM_CHUNK = 32

_NT = (((1,), (1,)), ((), ()))


def _dot(a, b):
    return jnp.dot(a, b, preferred_element_type=F32)


def _dot_nt(a, b):
    return lax.dot_general(a, b, _NT, preferred_element_type=F32)


def _layer_norm(x, g, b):
    mu = jnp.mean(x, axis=-1, keepdims=True)
    var = jnp.mean(jnp.square(x - mu), axis=-1, keepdims=True)
    return (x - mu) * lax.rsqrt(var + LN_EPS) * g + b


def _project_add_norm(lhs, w_ref, residual, g_ref, b_ref, o_ref):
    rows = lhs.shape[0]
    n_parts = NORM_ROW_PARTS if rows % (NORM_ROW_PARTS * 2 * SUBLANES) == 0 else 1
    part = rows // n_parts
    for r0 in range(0, rows, part):
        rs = slice(r0, r0 + part)
        o_ref[rs, :] = _layer_norm(residual[rs] + _dot(lhs[rs], w_ref[...]), g_ref[...], b_ref[...])


def _params(*semantics):
    return pltpu.CompilerParams(dimension_semantics=semantics, vmem_limit_bytes=VMEM_LIMIT_BYTES)


def _resident(shape):
    zeros = (0,) * len(shape)
    return pl.BlockSpec(shape, lambda *_: zeros, pipeline_mode=pl.Buffered(1))


def _mem_kv_kernel(mem_ref, w_ref, o_ref):
    o_ref[...] = _dot(mem_ref[...].astype(BF16), w_ref[...])


def _mem_kv_proj(mem, w):
    n, d = mem.shape
    return pl.pallas_call(
        _mem_kv_kernel,
        out_shape=jax.ShapeDtypeStruct((n, w.shape[1]), F32),
        name="mem_kv_proj",
    )(mem, w)


def _qkv_proj_kernel(xt_ref, xbot_ref, w_ref, o_ref, kvt0_ref, kvt1_ref, kvt2_ref, xb_ref, stage_ref, tmp_ref):
    g_id = pl.program_id(1)

    @pl.when(g_id == 0)
    def _():
        half = ATTN_BLOCK // 2
        xb_ref[0:half, :] = xt_ref[...].astype(BF16)
        xb_ref[half:, :] = xbot_ref[...].astype(BF16)

    xb = xb_ref[...]
    last_block = pl.program_id(0) == pl.num_programs(0) - 1
    for g, (win, dil) in enumerate(DILATED_GROUPS):
        kvt_ref = (kvt0_ref, kvt1_ref, kvt2_ref)[g]

        @pl.when(g_id == g)
        def _(g=g, win=win, dil=dil, kvt_ref=kvt_ref):
            w_cols = lambda c: w_ref[:, c * WIDTH_A + g * GROUP_WIDTH_A:c * WIDTH_A + (g + 1) * GROUP_WIDTH_A]
            for c in range(3):
                w = w_cols(c)
                if c == 0:
                    w = w * jnp.asarray(HEAD_DIM_A ** -0.5, BF16)
                res = _dot(xb, w)
                if dil == 1:
                    o_ref[:, c * GROUP_WIDTH_A:(c + 1) * GROUP_WIDTH_A] = res.astype(BF16)
                    continue
                n_slabs = GROUP_WIDTH_A // LANES
                for s in range(n_slabs):
                    stage_ref[s] = res[:, s * LANES:(s + 1) * LANES]
                inner = DEINTERLEAVE_STRIDE if dil > DEINTERLEAVE_STRIDE else dil
                outer = dil // inner
                for s in range(n_slabs):
                    col = c * GROUP_WIDTH_A + s * LANES
                    for blk in range(ATTN_BLOCK // win):
                        base = blk * win
                        src_ref = stage_ref
                        if outer > 1:
                            part = win // inner
                            for lo in range(inner):
                                tmp_ref[s, base + lo * part:base + (lo + 1) * part, :] = (
                                    stage_ref[s, pl.ds(base + lo, part, stride=inner), :])
                            src_ref = tmp_ref
                        for lo in range(inner):
                            for hi in range(outer):
                                first = base + lo * (win // inner) + hi if outer > 1 else base + lo
                                rows = src_ref[s, pl.ds(first, NK, stride=outer if outer > 1 else inner), :]
                                dst = base + (hi * inner + lo) * NK
                                o_ref[dst:dst + NK, col:col + LANES] = rows.astype(BF16)

            @pl.when(last_block)
            def _():
                tail = xb[ATTN_BLOCK - win:, :]
                kvt_ref[0:GROUP_WIDTH_A, :] = _dot(tail, w_cols(1)).T
                kvt_ref[GROUP_WIDTH_A:2 * GROUP_WIDTH_A, :] = _dot(tail, w_cols(2)).T


def _qkv_proj(x, w_all):
    s, d = x.shape
    n_blk = s // ATTN_BLOCK
    half = ATTN_BLOCK // 2
    x_half = lambda which: pl.BlockSpec(
        (half, d), lambda n, g: (2 * jnp.minimum(n + jnp.minimum(jnp.maximum(g - which, 0), 1), n_blk - 1) + which, 0))
    kvt_shapes = [(2 * GROUP_WIDTH_A, win) for win, _ in DILATED_GROUPS]
    return pl.pallas_call(
        _qkv_proj_kernel,
        grid=(n_blk, N_GROUPS_A),
        in_specs=[x_half(0), x_half(1),
                  pl.BlockSpec((d, 3 * WIDTH_A), lambda n, g: (0, 0), pipeline_mode=pl.Buffered(1))],
        out_specs=[pl.BlockSpec((None, ATTN_BLOCK, 3 * GROUP_WIDTH_A), lambda n, g: (g, n, 0))]
        + [pl.BlockSpec(shape, lambda n, g: (0, 0)) for shape in kvt_shapes],
        out_shape=[jax.ShapeDtypeStruct((N_GROUPS_A, s, 3 * GROUP_WIDTH_A), BF16)]
        + [jax.ShapeDtypeStruct(shape, F32) for shape in kvt_shapes],
        scratch_shapes=[pltpu.VMEM((ATTN_BLOCK, d), BF16),
                        pltpu.VMEM((GROUP_WIDTH_A // LANES, ATTN_BLOCK, LANES), F32),
                        pltpu.VMEM((GROUP_WIDTH_A // LANES, ATTN_BLOCK, LANES), F32)],
        compiler_params=_params("arbitrary", "arbitrary"),
        name="qkv_proj",
    )(x, x, w_all)


def _attn_units(units, prev_shifts):
    first_head = lax.broadcasted_iota(jnp.int32, (NK, LANES), 1) < HEAD_DIM_A
    row = lax.broadcasted_iota(jnp.int32, (2 * NK, NK), 0) % NK
    col = lax.broadcasted_iota(jnp.int32, (2 * NK, NK), 1)
    cur_ok = col <= row
    prev_ok = {}
    scores = []
    for (q, k_p, k_c, _, _), shift in zip(units, prev_shifts):
        zero = jnp.zeros_like(q)
        q_st = jnp.concatenate([jnp.where(first_head, q, zero), jnp.where(first_head, zero, q)], axis=0)
        if id(shift) not in prev_ok:
            prev_ok[id(shift)] = col >= row + shift
        scores.append((jnp.where(cur_ok, _dot_nt(q_st, k_c), NEG),
                       jnp.where(prev_ok[id(shift)], _dot_nt(q_st, k_p), NEG)))
    maxes = [jnp.max(jnp.maximum(s_c, s_p), axis=-1, keepdims=True) for s_c, s_p in scores]
    probs = [(jnp.exp(s_c - m), jnp.exp(s_p - m)) for (s_c, s_p), m in zip(scores, maxes)]
    dens = [jnp.sum(p_c + p_p, axis=-1, keepdims=True) for p_c, p_p in probs]
    results = []
    for (_, _, _, v_p, v_c), (p_c, p_p), m, den in zip(units, probs, maxes, dens):
        o_st = _dot(p_c.astype(BF16), v_c) + _dot(p_p.astype(BF16), v_p)
        pick = lambda t: jnp.where(first_head, jnp.broadcast_to(t[:NK], (NK, LANES)), jnp.broadcast_to(t[NK:], (NK, LANES)))
        den_sel = pick(den)
        results.append((pick(o_st) / den_sel, pick(m) + jnp.log(den_sel)))
    return results


def _dilated_attn_kernel(qkv0_ref, qkv1_ref, qkv2_ref, o_ref, kv0_ref, kv1_ref, kv2_ref, og_ref, lg_ref):
    n = pl.program_id(0)
    qkv_refs = (qkv0_ref, qkv1_ref, qkv2_ref)
    kv_refs = (kv0_ref, kv1_ref, kv2_ref)
    n_sub = ATTN_BLOCK // NK
    n_hp = GROUP_WIDTH_A // LANES

    @pl.when(n == 0)
    def _():
        for kv_ref in kv_refs:
            kv_ref[...] = jnp.zeros(kv_ref.shape, BF16)

    def sub_block(t, carry, prev_in_block):
        off = pl.multiple_of(t * NK, NK)
        units, shifts, dsts = [], [], []
        for g, (win, dil) in enumerate(DILATED_GROUPS):
            win_blk = t // dil
            res = t - win_blk * dil
            has_prev = (n * (ATTN_BLOCK // win) + win_blk) > 0
            nat = win_blk * win + res
            shift = jnp.where(has_prev, 0, 2 * NK)
            for hp in range(n_hp):
                qs = slice(hp * LANES, (hp + 1) * LANES)
                ks = slice(GROUP_WIDTH_A + hp * LANES, GROUP_WIDTH_A + (hp + 1) * LANES)
                vs = slice(2 * GROUP_WIDTH_A + hp * LANES, 2 * GROUP_WIDTH_A + (hp + 1) * LANES)
                cur = pl.ds(off, NK)
                if prev_in_block[g]:
                    prev = pl.ds(off - win, NK)
                    k_prev, v_prev = qkv_refs[g][prev, ks], qkv_refs[g][prev, vs]
                else:
                    k_prev = kv_refs[g][cur, qs]
                    v_prev = kv_refs[g][cur, GROUP_WIDTH_A + hp * LANES:GROUP_WIDTH_A + (hp + 1) * LANES]
                units.append((qkv_refs[g][cur, qs], k_prev, qkv_refs[g][cur, ks], v_prev, qkv_refs[g][cur, vs]))
                shifts.append(shift)
                dsts.append((g, hp, pl.ds(off, NK) if dil == 1 else pl.ds(nat, NK, stride=dil)))
        for (g, hp, dst), (out, lse) in zip(dsts, _attn_units(units, shifts)):
            og_ref[g, hp, dst, :] = out
            lg_ref[g, hp, dst, :] = lse
        return carry

    bounds = sorted({0, n_sub} | {min(win // NK, n_sub) for win, _ in DILATED_GROUPS})
    for lo, hi in zip(bounds[:-1], bounds[1:]):
        in_block = tuple(lo * NK >= win for win, _ in DILATED_GROUPS)
        lax.fori_loop(lo, hi, functools.partial(sub_block, prev_in_block=in_block), 0)
    for (win, _), qkv_ref, kv_ref in zip(DILATED_GROUPS, qkv_refs, kv_refs):
        kv_ref[...] = qkv_ref[ATTN_BLOCK - win:, GROUP_WIDTH_A:]

    def merge(t, carry):
        off = pl.multiple_of(t * NK, NK)
        for hp in range(GROUP_WIDTH_A // LANES):
            lses = [lg_ref[g, hp, pl.ds(off, NK), :] for g in range(N_GROUPS_A)]
            m = jnp.maximum(jnp.maximum(lses[0], lses[1]), lses[2])
            es = [jnp.exp(l - m) for l in lses]
            num = sum(es[g] * og_ref[g, hp, pl.ds(off, NK), :] for g in range(N_GROUPS_A))
            o_ref[pl.ds(off, NK), hp * LANES:(hp + 1) * LANES] = (num / (es[0] + es[1] + es[2])).astype(BF16)
        return carry

    lax.fori_loop(0, n_sub, merge, 0)


def _dilated_attn(qkv):
    _, s, _ = qkv.shape
    n_blk = s // ATTN_BLOCK
    n_hp = GROUP_WIDTH_A // LANES
    in_specs = [pl.BlockSpec((None, ATTN_BLOCK, 3 * GROUP_WIDTH_A), functools.partial(lambda n, g: (g, n, 0), g=g))
                for g in range(N_GROUPS_A)]
    return pl.pallas_call(
        _dilated_attn_kernel,
        grid=(n_blk,),
        in_specs=in_specs,
        out_specs=pl.BlockSpec((ATTN_BLOCK, GROUP_WIDTH_A), lambda n: (n, 0)),
        out_shape=jax.ShapeDtypeStruct((s, GROUP_WIDTH_A), BF16),
        scratch_shapes=[pltpu.VMEM((win, 2 * GROUP_WIDTH_A), BF16) for win, _ in DILATED_GROUPS]
        + [pltpu.VMEM((N_GROUPS_A, n_hp, ATTN_BLOCK, LANES), F32),
           pltpu.VMEM((N_GROUPS_A, n_hp, ATTN_BLOCK, LANES), F32)],
        compiler_params=_params("arbitrary"),
        name="dilated_attn",
    )(qkv, qkv, qkv)


def _to_heads(row, n_heads, head_dim):
    return jnp.concatenate([row[:, h * head_dim:(h + 1) * head_dim] for h in range(n_heads)], axis=0)


def _from_heads(t):
    return jnp.concatenate([t[h:h + 1] for h in range(t.shape[0])], axis=1)


def _window_cache_attention(c_ref, j, cols_ref, base, dil):
    win = c_ref.shape[-1]
    lc = min(win, SAMPLE_LANE_CHUNK)
    n_chunks = win // lc
    lane = lax.broadcasted_iota(jnp.int32, (1, win), 1)
    wanted = (lane % dil) == 0
    out = []
    for h in range(HEADS_PER_GROUP_A):
        lo = base + h * HEAD_DIM_A
        q, k_new, v_new = (cols_ref[lo + c * GROUP_WIDTH_A:lo + c * GROUP_WIDTH_A + HEAD_DIM_A, :] for c in range(3))
        q_wide = jnp.concatenate([q] * (lc // LANES), axis=1)
        parts = [jnp.sum(c_ref[j, 0, h, :, i * lc:(i + 1) * lc] * q_wide, axis=0, keepdims=True)
                 for i in range(n_chunks)]
        s = jnp.where(wanted, jnp.concatenate(parts, axis=1), NEG)
        s_n = jnp.sum(k_new[:, 0:1] * q[:, 0:1], axis=0, keepdims=True)
        m = jnp.maximum(jnp.max(s, axis=-1, keepdims=True), s_n)
        p = jnp.exp(s - m)
        p_n = jnp.exp(s_n - m)
        den = jnp.sum(p, axis=-1, keepdims=True) + p_n
        acc = p_n * v_new[:, 0:1]
        for i in range(n_chunks):
            acc = acc + jnp.sum(c_ref[j, 1, h, :, i * lc:(i + 1) * lc] * p[:, i * lc:(i + 1) * lc],
                                axis=-1, keepdims=True)
        out.append((acc / den, m + jnp.log(den)))
    return out


def _memory_cache_attention(cm_ref, j, q, s_ref):
    n_heads, head_dim = q.shape
    chunk = SAMPLE_MEM_CHUNK
    n_chunks = cm_ref.shape[1] // chunk
    m = jnp.full((n_heads, head_dim), NEG, F32)
    for c in range(n_chunks):
        pos = slice(c * chunk, (c + 1) * chunk)
        s = jnp.sum(cm_ref[j, pos, 0] * q[None], axis=-1, keepdims=True) * (head_dim ** -0.5)
        s = jnp.broadcast_to(s, (chunk, n_heads, head_dim))
        s_ref[pos] = s
        m = jnp.maximum(m, jnp.max(s, axis=0))
    den = jnp.zeros((n_heads, head_dim), F32)
    acc = jnp.zeros((n_heads, head_dim), F32)
    for c in range(n_chunks):
        pos = slice(c * chunk, (c + 1) * chunk)
        p = jnp.exp(s_ref[pos] - m[None])
        den = den + jnp.sum(p, axis=0)
        acc = acc + jnp.sum(p * cm_ref[j, pos, 1], axis=0)
    return acc / den


def _sample_attn_kernel(x_ref, wqkv_ref, wqm_ref, c0_ref, c1_ref, c2_ref, cm_ref,
                        oa_ref, om_ref, kvn_ref, proj_ref, projt_ref, cols_ref, oat_ref, ms_ref):
    step = pl.program_id(0)
    caches = (c0_ref, c1_ref, c2_ref)
    n_batch, d = x_ref.shape
    qkv_w = N_GROUPS_A * 3 * GROUP_WIDTH_A

    @pl.when(step == 0)
    def _():
        xb = jnp.concatenate([x_ref[...], jnp.zeros((LANES - n_batch, d), F32)], axis=0).astype(BF16)
        for part in range(3):
            h = _dot(xb, wqkv_ref[:, part * WIDTH_A:(part + 1) * WIDTH_A])
            if part == 0:
                h = h * (HEAD_DIM_A ** -0.5)
            for g in range(N_GROUPS_A):
                h_g = h[:, g * GROUP_WIDTH_A:(g + 1) * GROUP_WIDTH_A]
                dst = (3 * g + part) * GROUP_WIDTH_A
                proj_ref[:, dst:dst + GROUP_WIDTH_A] = h_g
                if part > 0:
                    kvn_ref[g, :, (part - 1) * GROUP_WIDTH_A:part * GROUP_WIDTH_A] = h_g[:n_batch]
        proj_ref[:, qkv_w:] = _dot(xb, wqm_ref[:, :WIDTH_M])
        for t in range(qkv_w // LANES):
            projt_ref[t * LANES:(t + 1) * LANES, :] = proj_ref[:, t * LANES:(t + 1) * LANES].T.astype(BF16)
        oat_ref[...] = jnp.zeros(oat_ref.shape, F32)

    def one_row(j, carry):
        b = step * SAMPLE_BATCH_BLOCK + j
        pick_row = (lax.broadcasted_iota(jnp.int32, (LANES, LANES), 0) == b).astype(BF16)
        cols_ref[...] = _dot(projt_ref[...], pick_row)
        per_group = [_window_cache_attention(caches[g], j, cols_ref, g * 3 * GROUP_WIDTH_A, dil)
                     for g, (_, dil) in enumerate(DILATED_GROUPS)]
        this_lane = lax.broadcasted_iota(jnp.int32, (1, LANES), 1) == b
        for h in range(HEADS_PER_GROUP_A):
            outs = [per_group[g][h][0] for g in range(N_GROUPS_A)]
            lses = [per_group[g][h][1] for g in range(N_GROUPS_A)]
            m3 = jnp.maximum(jnp.maximum(lses[0], lses[1]), lses[2])
            es = [jnp.exp(l - m3) for l in lses]
            oa = (es[0] * outs[0] + es[1] * outs[1] + es[2] * outs[2]) / (es[0] + es[1] + es[2])
            rows = slice(h * HEAD_DIM_A, (h + 1) * HEAD_DIM_A)
            oat_ref[rows, :] = jnp.where(this_lane, oa, oat_ref[rows, :])

        qm = _to_heads(proj_ref[pl.ds(b, 1), qkv_w:], N_HEADS_M, HEAD_DIM_M)
        om_ref[pl.ds(b, 1), :] = _from_heads(_memory_cache_attention(cm_ref, j, qm, ms_ref))
        return carry

    lax.fori_loop(0, SAMPLE_BATCH_BLOCK, one_row, 0)

    @pl.when(step == pl.num_programs(0) - 1)
    def _():
        oa_ref[...] = oat_ref[...].T[:n_batch, :]


def _sample_attn(xs, w_all, caches, cache_mem):
    n_batch, d = xs.shape
    assert n_batch <= LANES
    bb = SAMPLE_BATCH_BLOCK
    qkv_w = N_GROUPS_A * 3 * GROUP_WIDTH_A
    qm_col = 3 * WIDTH_A + 2 * WIDTH_B
    assert qm_col % WIDTH_B == 0 and WIDTH_M <= WIDTH_B
    w_specs = [pl.BlockSpec((d, qkv_w), lambda i: (0, 0), pipeline_mode=pl.Buffered(1)),
               pl.BlockSpec((d, WIDTH_B), lambda i: (0, qm_col // WIDTH_B), pipeline_mode=pl.Buffered(1))]
    cache_specs = [pl.BlockSpec((bb,) + c.shape[1:], lambda i: (i, 0, 0, 0, 0)) for c in caches]
    whole = lambda shape: pl.BlockSpec(shape, lambda i: (0,) * len(shape))
    return pl.pallas_call(
        _sample_attn_kernel,
        grid=(n_batch // bb,),
        in_specs=[whole(xs.shape)] + w_specs + cache_specs
        + [pl.BlockSpec((bb,) + cache_mem.shape[1:], lambda i: (i, 0, 0, 0, 0))],
        out_specs=[whole((n_batch, GROUP_WIDTH_A)), whole((n_batch, WIDTH_M)),
                   whole((N_GROUPS_A, n_batch, 2 * GROUP_WIDTH_A))],
        out_shape=[jax.ShapeDtypeStruct((n_batch, GROUP_WIDTH_A), F32),
                   jax.ShapeDtypeStruct((n_batch, WIDTH_M), F32),
                   jax.ShapeDtypeStruct((N_GROUPS_A, n_batch, 2 * GROUP_WIDTH_A), F32)],
        scratch_shapes=[pltpu.VMEM((LANES, qkv_w + WIDTH_M), F32),
                        pltpu.VMEM((qkv_w, LANES), BF16),
                        pltpu.VMEM((qkv_w, LANES), F32),
                        pltpu.VMEM((GROUP_WIDTH_A, LANES), F32),
                        pltpu.VMEM((cache_mem.shape[1], N_HEADS_M, HEAD_DIM_M), F32)],
        compiler_params=_params("arbitrary"),
        name="sample_attn",
    )(xs, w_all, w_all, *caches, cache_mem)


def _mix_kernel(x_ref, oa_ref, mk_ref, mv_ref, xs_ref, oas_ref, oms_ref, w_ref, bgate_ref, lnvg_ref, lnvb_ref,
                ws_ref, bs_ref, wss_ref, bss_ref, wba_ref, wbb_ref, wbm_ref, wout_ref, ln1g_ref, ln1b_ref,
                x1_ref, x1s_ref, vrows_ref, *, alpha):
    shared = (w_ref, bgate_ref, lnvg_ref, lnvb_ref)
    tail = (wba_ref, wbb_ref, wbm_ref, wout_ref, ln1g_ref, ln1b_ref)
    is_sample_step = pl.program_id(0) == pl.num_programs(0) - 1

    @pl.when(jnp.logical_not(is_sample_step))
    def _():
        _mix_body(x_ref, oa_ref, (mk_ref, mv_ref), *shared, ws_ref, bs_ref, *tail, x1_ref, None,
                  sample=False, alpha=alpha)

    @pl.when(is_sample_step)
    def _():
        _mix_body(xs_ref, oas_ref, oms_ref, *shared, wss_ref, bss_ref, *tail, x1s_ref, vrows_ref,
                  sample=True, alpha=alpha)


def _mix_body(x_ref, oa_ref, mem_or_om, w_ref, bgate_ref, lnvg_ref, lnvb_ref, ws_ref, bs_ref,
              wba_ref, wbb_ref, wbm_ref, wout_ref, ln1g_ref, ln1b_ref, x1_ref, vrows_ref, *, sample, alpha):
    if sample:
        om_ref = mem_or_om
    else:
        mk_ref, mv_ref = mem_or_om
    d = x_ref.shape[1]
    rows = x_ref.shape[0]
    x = x_ref[...]
    xb = x.astype(BF16)
    col_u, col_v, col_qm, col_gate = 0, WIDTH_B, 2 * WIDTH_B, 2 * WIDTH_B + WIDTH_M

    h_all = _dot(xb, w_ref[:, 3 * WIDTH_A:])

    def gate(k):
        z = h_all[:, col_gate + k * d:col_gate + (k + 1) * d]
        return 0.5 * jnp.tanh(0.5 * (z + bgate_ref[k:k + 1, :])) + 0.5

    mixed = gate(0) * _dot(oa_ref[...].astype(BF16), wba_ref[...])

    u = jax.nn.gelu(h_all[:, col_u:col_u + WIDTH_B])
    v = _layer_norm(jax.nn.gelu(h_all[:, col_v:col_v + WIDTH_B]), lnvg_ref[...], lnvb_ref[...])
    if sample:
        vrows_ref[...] = v
        spatial = v.astype(BF16).astype(F32) * ws_ref[...].astype(BF16).astype(F32) + bs_ref[...]
    else:
        r_i = lax.broadcasted_iota(jnp.int32, (N_GROUPS_B * CHUNK, CHUNK), 0)
        c_i = lax.broadcasted_iota(jnp.int32, (N_GROUPS_B * CHUNK, CHUNK), 1)
        w_s = jnp.where((r_i % CHUNK) >= c_i, ws_ref[...], 0.0).astype(BF16)
        lane = lax.broadcasted_iota(jnp.int32, (CHUNK, WIDTH_B), 1)
        in_group = [(lane >= gb * GROUP_DIM_B) & (lane < (gb + 1) * GROUP_DIM_B) for gb in range(N_GROUPS_B)]
        vb = v.astype(BF16)
        parts = []
        for ch in range(rows // CHUNK):
            allg = _dot(w_s, vb[ch * CHUNK:(ch + 1) * CHUNK, :])
            sp = bs_ref[...]
            for gb in range(N_GROUPS_B):
                sp = sp + jnp.where(in_group[gb], allg[gb * CHUNK:(gb + 1) * CHUNK, :], 0.0)
            parts.append(sp)
        spatial = jnp.concatenate(parts, axis=0)
    o_b = u * spatial
    mixed = mixed + gate(1) * _dot(o_b.astype(BF16), wbb_ref[...])

    if sample:
        o_m = om_ref[...]
    else:
        qm = h_all[:, col_qm:col_qm + WIDTH_M].astype(BF16)
        head_cols = [slice(h * HEAD_DIM_M, (h + 1) * HEAD_DIM_M) for h in range(N_HEADS_M)]
        scores = [_dot_nt(qm[:, hs], mk_ref[:, hs]) * (HEAD_DIM_M ** -0.5) for hs in head_cols]
        maxes = [jnp.max(s, axis=-1, keepdims=True) for s in scores]
        probs = [jnp.exp(s - m) for s, m in zip(scores, maxes)]
        dens = [jnp.sum(p, axis=-1, keepdims=True) for p in probs]
        o_m = jnp.concatenate([_dot(p.astype(BF16), mv_ref[:, hs]) / den
                               for p, den, hs in zip(probs, dens, head_cols)], axis=-1)
    mixed = mixed + gate(2) * _dot(o_m.astype(BF16), wbm_ref[...])

    _project_add_norm(mixed.astype(BF16), wout_ref, alpha * x, ln1g_ref, ln1b_ref, x1_ref)


def _mix(x, o_a, mk, mv, xs, oa_s, om_s, w_all, b_gate, ln_v_g, ln_v_b, w_s, b_s, w_s_row, b_s_row,
         w_ba, w_bb, w_bm, w_out, ln1_g, ln1_b, *, alpha):
    s, d = x.shape
    n_tiles = s // MIX_ROWS
    tile = lambda width: pl.BlockSpec((MIX_ROWS, width), lambda i: (jnp.minimum(i, n_tiles - 1), 0))
    whole = lambda shape: pl.BlockSpec(shape, lambda i: (0,) * len(shape))
    operands = (x, o_a, mk, mv, xs, oa_s, om_s, w_all, b_gate, ln_v_g, ln_v_b, w_s, b_s, w_s_row, b_s_row,
                w_ba, w_bb, w_bm, w_out, ln1_g, ln1_b)
    n_s = xs.shape[0]
    return pl.pallas_call(
        functools.partial(_mix_kernel, alpha=alpha),
        grid=(n_tiles + 1,),
        in_specs=[tile(d), tile(GROUP_WIDTH_A)] + [_resident(t.shape) for t in operands[2:]],
        out_specs=[tile(d), whole((n_s, d)), whole((n_s, WIDTH_B))],
        out_shape=[jax.ShapeDtypeStruct((s, d), F32), jax.ShapeDtypeStruct((n_s, d), F32),
                   jax.ShapeDtypeStruct((n_s, WIDTH_B), F32)],
        compiler_params=_params("arbitrary"),
        name="mix",
    )(*operands)


def _ffn_kernel(x1_ref, x1s_ref, st0_ref, st1_ref, wup_ref, cw_ref, cb_ref, wdown_ref, g_ref, b_ref,
                y_ref, a_ref, ys_ref, as_ref, abuf_ref, *, alpha):
    consts = (wup_ref, cw_ref, cb_ref, wdown_ref, g_ref, b_ref)
    is_sample_step = pl.program_id(0) == pl.num_programs(0) - 1

    @pl.when(jnp.logical_not(is_sample_step))
    def _():
        _ffn_body(x1_ref, None, None, *consts, y_ref, a_ref, abuf_ref, sample=False, alpha=alpha)

    @pl.when(is_sample_step)
    def _():
        _ffn_body(x1s_ref, st0_ref, st1_ref, *consts, ys_ref, as_ref, None, sample=True, alpha=alpha)


def _ffn_body(x1_ref, st0_ref, st1_ref, wup_ref, cw_ref, cb_ref, wdown_ref, g_ref, b_ref, y_ref, a_ref, abuf_ref,
              *, sample, alpha):
    rows = x1_ref.shape[0]
    d_ff = wdown_ref.shape[0]
    if not sample:
        @pl.when(pl.program_id(0) == 0)
        def _():
            abuf_ref[0:SUBLANES, :] = jnp.zeros((SUBLANES, d_ff), F32)

    x1 = x1_ref[...]
    xb = x1.astype(BF16)
    a = _dot(xb, wup_ref[:, :d_ff])
    val = _dot(xb, wup_ref[:, d_ff:])
    if sample:
        a_ref[...] = a
        a_m2, a_m1 = st0_ref[...], st1_ref[...]
    else:
        abuf_ref[SUBLANES:SUBLANES + rows, :] = a
        a_m1 = abuf_ref[SUBLANES - 1:SUBLANES - 1 + rows, :]
        a_m2 = abuf_ref[SUBLANES - 2:SUBLANES - 2 + rows, :]
    conv = cb_ref[...] + cw_ref[0:1, :] * a_m2 + cw_ref[1:2, :] * a_m1 + cw_ref[2:3, :] * a
    h = jax.nn.gelu(conv) * val
    _project_add_norm(h.astype(BF16), wdown_ref, alpha * x1, g_ref, b_ref, y_ref)
    if not sample:
        tail = abuf_ref[rows:rows + SUBLANES, :]
        abuf_ref[0:SUBLANES, :] = tail
        a_ref[...] = tail


def _ffn(x1, x1_s, state, w_up, conv_w, conv_b, w_down, ln2_g, ln2_b, *, alpha):
    s, d = x1.shape
    n_s = x1_s.shape[0]
    d_ff = w_down.shape[0]
    n_tiles = s // FFN_ROWS
    tile = lambda width: pl.BlockSpec((FFN_ROWS, width), lambda i: (jnp.minimum(i, n_tiles - 1), 0))
    whole = lambda shape: pl.BlockSpec(shape, lambda i: (0,) * len(shape))
    operands = (x1, x1_s, state[:, 0], state[:, 1], w_up, conv_w, conv_b, w_down, ln2_g, ln2_b)
    return pl.pallas_call(
        functools.partial(_ffn_kernel, alpha=alpha),
        grid=(n_tiles + 1,),
        in_specs=[tile(d)] + [_resident(t.shape) for t in operands[1:]],
        out_specs=[tile(d), whole((SUBLANES, d_ff)), whole((n_s, d)), whole((n_s, d_ff))],
        out_shape=[jax.ShapeDtypeStruct((s, d), F32), jax.ShapeDtypeStruct((SUBLANES, d_ff), F32),
                   jax.ShapeDtypeStruct((n_s, d), F32), jax.ShapeDtypeStruct((n_s, d_ff), F32)],
        scratch_shapes=[pltpu.VMEM((FFN_ROWS + SUBLANES, d_ff), F32)],
        compiler_params=_params("arbitrary"),
        name="ffn",
    )(*operands)


def kernel(x_prompt, x_sample, mem_prompt, cache_win128_kv, cache_win512_kv, cache_win2048_kv, cache_mem_kv, state_ffn_conv, w_in, b_gate, ln_v_g, ln_v_b, w_spatial, b_spatial, w_mem_kv, w_branch_a, w_branch_b, w_branch_m, w_out, ln1_g, ln1_b, w_up, conv_w, conv_b, w_down, ln2_g, ln2_b):
    depth = w_in.shape[0]
    batch, seq, d_model = x_prompt.shape
    dec_batch, dec_seq, _ = x_sample.shape
    assert batch == 1 and dec_seq == 1 and seq % ATTN_BLOCK == 0
    alpha = (2.0 * depth) ** 0.25
    win_caches = (cache_win128_kv, cache_win512_kv, cache_win2048_kv)
    for cache, (win, _) in zip(win_caches, DILATED_GROUPS):
        assert cache.shape[2] == win and PAST_LEN >= win

    yp = x_prompt.reshape(seq, d_model)
    ys = x_sample.reshape(dec_batch, d_model)
    win_p = [[] for _ in range(N_GROUPS_A)]
    win_s = [[] for _ in range(N_GROUPS_A)]
    mem_p, conv_p, gmlp_s, conv_s = [], [], [], []
    for l in range(depth):
        w_all = w_in[l].astype(BF16)
        row2 = lambda t: t.reshape(1, -1)
        per_lane = lambda t: jnp.repeat(t, GROUP_DIM_B, axis=-1)
        consts_tail = (w_branch_a[l].astype(BF16), w_branch_b[l].astype(BF16), w_branch_m[l].astype(BF16),
                       w_out[l].astype(BF16), row2(ln1_g[l]), row2(ln1_b[l]))
        ffn_consts = (w_up[l].astype(BF16), conv_w[l], row2(conv_b[l]), w_down[l].astype(BF16),
                      row2(ln2_g[l]), row2(ln2_b[l]))

        mkv = _mem_kv_proj(mem_prompt[0], w_mem_kv[l].astype(BF16))
        mem_p.append(mkv.reshape(1, -1, 2, N_HEADS_M, HEAD_DIM_M))
        qkv, *kv_tails = _qkv_proj(yp, w_all)
        for g, (win, _) in enumerate(DILATED_GROUPS):
            tail = kv_tails[g].reshape(2, HEADS_PER_GROUP_A, HEAD_DIM_A, win)
            win_p[g].append(jnp.transpose(tail, (3, 0, 1, 2))[None])
        o_a = _dilated_attn(qkv)
        caches = [jnp.transpose(c[l], (0, 2, 3, 4, 1)) for c in win_caches]
        oa_s, om_s, kv_new = _sample_attn(ys, w_all, caches, cache_mem_kv[l])
        for g in range(N_GROUPS_A):
            win_s[g].append(kv_new[g].reshape(dec_batch, 1, 2, HEADS_PER_GROUP_A, HEAD_DIM_A))

        x1, x1_s, v_rows = _mix(yp, o_a, mkv[:, :WIDTH_M].astype(BF16), mkv[:, WIDTH_M:].astype(BF16), ys, oa_s, om_s,
                                w_all, b_gate[l], row2(ln_v_g[l]), row2(ln_v_b[l]),
                                w_spatial[l].reshape(N_GROUPS_B * CHUNK, CHUNK), per_lane(b_spatial[l].T),
                                row2(per_lane(w_spatial[l][:, 0, 0])), row2(per_lane(b_spatial[l][:, 0])),
                                *consts_tail, alpha=alpha)
        gmlp_s.append(v_rows.reshape(dec_batch, 1, WIDTH_B))
        yp_next, a_tail, ys_next, a_s = _ffn(x1, x1_s, state_ffn_conv[l], *ffn_consts, alpha=alpha)
        conv_p.append(a_tail[SUBLANES - (CONV_W - 1):][None])
        conv_s.append(jnp.stack([state_ffn_conv[l][:, 1], a_s], axis=1))
        yp, ys = yp_next, ys_next

    return (yp.reshape(batch, seq, d_model), ys.reshape(dec_batch, dec_seq, d_model),
            jnp.stack(win_p[0]), jnp.stack(win_p[1]), jnp.stack(win_p[2]),
            jnp.stack(mem_p), jnp.stack(conv_p),
            jnp.stack(win_s[0]), jnp.stack(win_s[1]), jnp.stack(win_s[2]),
            jnp.stack(gmlp_s), jnp.stack(conv_s))
```

```python
import functools

import jax
import jax.numpy as jnp
from jax import lax
from jax.experimental import pallas as pl
from jax.experimental.pallas import tpu as pltpu

BF16 = jnp.bfloat16
F32 = jnp.float32

HEAD_DIM_A = 64
HEADS_PER_GROUP_A = 4
DILATED_GROUPS = ((128, 1), (512, 4), (2048, 16))
N_GROUPS_A = len(DILATED_GROUPS)
GROUP_WIDTH_A = HEADS_PER_GROUP_A * HEAD_DIM_A
WIDTH_A = N_GROUPS_A * GROUP_WIDTH_A
NK = 128
CHUNK = 128
N_GROUPS_B = 4
WIDTH_B = 768
GROUP_DIM_B = WIDTH_B // N_GROUPS_B
N_HEADS_M = 4
HEAD_DIM_M = 128
WIDTH_M = N_HEADS_M * HEAD_DIM_M
N_BRANCH = 3
CONV_W = 3
LN_EPS = 1e-5
NEG = -1e30
PAST_LEN = 16384

LANES = 128
SUBLANES = 8
VMEM_LIMIT_BYTES = 58 * 1024 * 1024

ATTN_BLOCK = 2048
QKV_ROW_PART = 512
DEINTERLEAVE_STRIDE = 4
MIX_ROWS = 512
FFN_ROWS = 512
NORM_ROW_PARTS = 4
SAMPLE_BATCH_BLOCK = 2
SAMPLE_LANE_CHUNK = 512
SAMPLE_MEM_CHUNK = 32

_NT = (((1,), (1,)), ((), ()))


def _dot(a, b):
    return jnp.dot(a, b, preferred_element_type=F32)


def _dot_nt(a, b):
    return lax.dot_general(a, b, _NT, preferred_element_type=F32)


def _layer_norm(x, g, b):
    mu = jnp.mean(x, axis=-1, keepdims=True)
    var = jnp.mean(jnp.square(x - mu), axis=-1, keepdims=True)
    return (x - mu) * lax.rsqrt(var + LN_EPS) * g + b


def _project_add_norm(lhs, w_ref, residual, g_ref, b_ref, o_ref):
    rows = lhs.shape[0]
    n_parts = NORM_ROW_PARTS if rows % (NORM_ROW_PARTS * 2 * SUBLANES) == 0 else 1
    part = rows // n_parts
    for r0 in range(0, rows, part):
        rs = slice(r0, r0 + part)
        o_ref[rs, :] = _layer_norm(residual[rs] + _dot(lhs[rs], w_ref[...]), g_ref[...], b_ref[...])


def _params(*semantics):
    return pltpu.CompilerParams(dimension_semantics=semantics, vmem_limit_bytes=VMEM_LIMIT_BYTES)


def _resident(shape):
    zeros = (0,) * len(shape)
    return pl.BlockSpec(shape, lambda *_: zeros, pipeline_mode=pl.Buffered(1))


def _mem_kv_kernel(mem_ref, w_ref, o_ref):
    o_ref[...] = _dot(mem_ref[...].astype(BF16), w_ref[...])


def _mem_kv_proj(mem, w):
    n, d = mem.shape
    return pl.pallas_call(
        _mem_kv_kernel,
        out_shape=jax.ShapeDtypeStruct((n, w.shape[1]), F32),
        name="mem_kv_proj",
    )(mem, w)


def _qkv_proj_kernel(xt_ref, xbot_ref, w_ref, o_ref, kvt0_ref, kvt1_ref, kvt2_ref, xb_ref, stage_ref, tmp_ref):
    g_id = pl.program_id(1)

    @pl.when(g_id == 0)
    def _():
        half = ATTN_BLOCK // 2
        xb_ref[0:half, :] = xt_ref[...].astype(BF16)
        xb_ref[half:, :] = xbot_ref[...].astype(BF16)

    xb = xb_ref[...]
    last_block = pl.program_id(0) == pl.num_programs(0) - 1
    for g, (win, dil) in enumerate(DILATED_GROUPS):
        kvt_ref = (kvt0_ref, kvt1_ref, kvt2_ref)[g]

        @pl.when(g_id == g)
        def _(g=g, win=win, dil=dil, kvt_ref=kvt_ref):
            w_cols = lambda c: w_ref[:, c * WIDTH_A + g * GROUP_WIDTH_A:c * WIDTH_A + (g + 1) * GROUP_WIDTH_A]
            for c in range(3):
                w = w_cols(c)
                if c == 0:
                    w = w * jnp.asarray(HEAD_DIM_A ** -0.5, BF16)
                n_slabs = GROUP_WIDTH_A // LANES
                inner = DEINTERLEAVE_STRIDE if dil > DEINTERLEAVE_STRIDE else dil
                outer = dil // inner
                part_rows = win if dil > 1 else QKV_ROW_PART
                for base in range(0, ATTN_BLOCK, part_rows):
                    res = _dot(xb[base:base + part_rows], w)
                    if dil == 1:
                        o_ref[base:base + part_rows, c * GROUP_WIDTH_A:(c + 1) * GROUP_WIDTH_A] = res.astype(BF16)
                        continue
                    for s in range(n_slabs):
                        stage_ref[s, base:base + win, :] = res[:, s * LANES:(s + 1) * LANES]
                    for s in range(n_slabs):
                        col = c * GROUP_WIDTH_A + s * LANES
                        src_ref = stage_ref
                        if outer > 1:
                            sub = win // inner
                            for lo in range(inner):
                                tmp_ref[s, base + lo * sub:base + (lo + 1) * sub, :] = (
                                    stage_ref[s, pl.ds(base + lo, sub, stride=inner), :])
                            src_ref = tmp_ref
                        for lo in range(inner):
                            for hi in range(outer):
                                first = base + lo * (win // inner) + hi if outer > 1 else base + lo
                                rows = src_ref[s, pl.ds(first, NK, stride=outer if outer > 1 else inner), :]
                                dst = base + (hi * inner + lo) * NK
                                o_ref[dst:dst + NK, col:col + LANES] = rows.astype(BF16)

            @pl.when(last_block)
            def _():
                tail = xb[ATTN_BLOCK - win:, :]
                kvt_ref[0:GROUP_WIDTH_A, :] = _dot(tail, w_cols(1)).T
                kvt_ref[GROUP_WIDTH_A:2 * GROUP_WIDTH_A, :] = _dot(tail, w_cols(2)).T


def _qkv_proj(x, w_all):
    s, d = x.shape
    n_blk = s // ATTN_BLOCK
    half = ATTN_BLOCK // 2
    x_half = lambda which: pl.BlockSpec(
        (half, d), lambda n, g: (2 * jnp.minimum(n + jnp.minimum(jnp.maximum(g - which, 0), 1), n_blk - 1) + which, 0))
    kvt_shapes = [(2 * GROUP_WIDTH_A, win) for win, _ in DILATED_GROUPS]
    return pl.pallas_call(
        _qkv_proj_kernel,
        grid=(n_blk, N_GROUPS_A),
        in_specs=[x_half(0), x_half(1),
                  pl.BlockSpec((d, 3 * WIDTH_A), lambda n, g: (0, 0), pipeline_mode=pl.Buffered(1))],
        out_specs=[pl.BlockSpec((None, ATTN_BLOCK, 3 * GROUP_WIDTH_A), lambda n, g: (g, n, 0))]
        + [pl.BlockSpec(shape, lambda n, g: (0, 0)) for shape in kvt_shapes],
        out_shape=[jax.ShapeDtypeStruct((N_GROUPS_A, s, 3 * GROUP_WIDTH_A), BF16)]
        + [jax.ShapeDtypeStruct(shape, F32) for shape in kvt_shapes],
        scratch_shapes=[pltpu.VMEM((ATTN_BLOCK, d), BF16),
                        pltpu.VMEM((GROUP_WIDTH_A // LANES, ATTN_BLOCK, LANES), F32),
                        pltpu.VMEM((GROUP_WIDTH_A // LANES, ATTN_BLOCK, LANES), F32)],
        compiler_params=_params("arbitrary", "arbitrary"),
        name="qkv_proj",
    )(x, x, w_all)


def _attn_units(units, prev_shifts):
    first_head = lax.broadcasted_iota(jnp.int32, (NK, LANES), 1) < HEAD_DIM_A
    row = lax.broadcasted_iota(jnp.int32, (2 * NK, NK), 0) % NK
    col = lax.broadcasted_iota(jnp.int32, (2 * NK, NK), 1)
    cur_ok = col <= row
    prev_ok = {}
    scores = []
    for (q, k_p, k_c, _, _), shift in zip(units, prev_shifts):
        zero = jnp.zeros_like(q)
        q_st = jnp.concatenate([jnp.where(first_head, q, zero), jnp.where(first_head, zero, q)], axis=0)
        if id(shift) not in prev_ok:
            prev_ok[id(shift)] = col >= row + shift
        scores.append((jnp.where(cur_ok, _dot_nt(q_st, k_c), NEG),
                       jnp.where(prev_ok[id(shift)], _dot_nt(q_st, k_p), NEG)))
    maxes = [jnp.max(jnp.maximum(s_c, s_p), axis=-1, keepdims=True) for s_c, s_p in scores]
    probs = [(jnp.exp(s_c - m), jnp.exp(s_p - m)) for (s_c, s_p), m in zip(scores, maxes)]
    dens = [jnp.sum(p_c + p_p, axis=-1, keepdims=True) for p_c, p_p in probs]
    results = []
    for (_, _, _, v_p, v_c), (p_c, p_p), m, den in zip(units, probs, maxes, dens):
        o_st = _dot(p_c.astype(BF16), v_c) + _dot(p_p.astype(BF16), v_p)
        pick = lambda t: jnp.where(first_head, jnp.broadcast_to(t[:NK], (NK, LANES)), jnp.broadcast_to(t[NK:], (NK, LANES)))
        den_sel = pick(den)
        results.append((pick(o_st) / den_sel, pick(m) + jnp.log(den_sel)))
    return results


def _dilated_attn_kernel(qkv0_ref, qkv1_ref, qkv2_ref, o_ref, kv0_ref, kv1_ref, kv2_ref, og_ref, lg_ref):
    n = pl.program_id(0)
    qkv_refs = (qkv0_ref, qkv1_ref, qkv2_ref)
    kv_refs = (kv0_ref, kv1_ref, kv2_ref)
    n_sub = ATTN_BLOCK // NK
    n_hp = GROUP_WIDTH_A // LANES

    @pl.when(n == 0)
    def _():
        for kv_ref in kv_refs:
            kv_ref[...] = jnp.zeros(kv_ref.shape, BF16)

    def sub_block(t, carry, prev_in_block):
        off = pl.multiple_of(t * NK, NK)
        units, shifts, dsts = [], [], []
        for g, (win, dil) in enumerate(DILATED_GROUPS):
            win_blk = t // dil
            res = t - win_blk * dil
            has_prev = (n * (ATTN_BLOCK // win) + win_blk) > 0
            nat = win_blk * win + res
            shift = jnp.where(has_prev, 0, 2 * NK)
            for hp in range(n_hp):
                qs = slice(hp * LANES, (hp + 1) * LANES)
                ks = slice(GROUP_WIDTH_A + hp * LANES, GROUP_WIDTH_A + (hp + 1) * LANES)
                vs = slice(2 * GROUP_WIDTH_A + hp * LANES, 2 * GROUP_WIDTH_A + (hp + 1) * LANES)
                cur = pl.ds(off, NK)
                if prev_in_block[g]:
                    prev = pl.ds(off - win, NK)
                    k_prev, v_prev = qkv_refs[g][prev, ks], qkv_refs[g][prev, vs]
                else:
                    k_prev = kv_refs[g][cur, qs]
                    v_prev = kv_refs[g][cur, GROUP_WIDTH_A + hp * LANES:GROUP_WIDTH_A + (hp + 1) * LANES]
                units.append((qkv_refs[g][cur, qs], k_prev, qkv_refs[g][cur, ks], v_prev, qkv_refs[g][cur, vs]))
                shifts.append(shift)
                dsts.append((g, hp, pl.ds(off, NK) if dil == 1 else pl.ds(nat, NK, stride=dil)))
        for (g, hp, dst), (out, lse) in zip(dsts, _attn_units(units, shifts)):
            og_ref[g, hp, dst, :] = out
            lg_ref[g, hp, dst, :] = lse
        return carry

    bounds = sorted({0, n_sub} | {min(win // NK, n_sub) for win, _ in DILATED_GROUPS})
    for lo, hi in zip(bounds[:-1], bounds[1:]):
        in_block = tuple(lo * NK >= win for win, _ in DILATED_GROUPS)
        lax.fori_loop(lo, hi, functools.partial(sub_block, prev_in_block=in_block), 0)
    for (win, _), qkv_ref, kv_ref in zip(DILATED_GROUPS, qkv_refs, kv_refs):
        kv_ref[...] = qkv_ref[ATTN_BLOCK - win:, GROUP_WIDTH_A:]

    def merge(t, carry):
        off = pl.multiple_of(t * NK, NK)
        for hp in range(GROUP_WIDTH_A // LANES):
            lses = [lg_ref[g, hp, pl.ds(off, NK), :] for g in range(N_GROUPS_A)]
            m = jnp.maximum(jnp.maximum(lses[0], lses[1]), lses[2])
            es = [jnp.exp(l - m) for l in lses]
            num = sum(es[g] * og_ref[g, hp, pl.ds(off, NK), :] for g in range(N_GROUPS_A))
            o_ref[pl.ds(off, NK), hp * LANES:(hp + 1) * LANES] = (num / (es[0] + es[1] + es[2])).astype(BF16)
        return carry

    lax.fori_loop(0, n_sub, merge, 0)


def _dilated_attn(qkv):
    _, s, _ = qkv.shape
    n_blk = s // ATTN_BLOCK
    n_hp = GROUP_WIDTH_A // LANES
    in_specs = [pl.BlockSpec((None, ATTN_BLOCK, 3 * GROUP_WIDTH_A), functools.partial(lambda n, g: (g, n, 0), g=g))
                for g in range(N_GROUPS_A)]
    return pl.pallas_call(
        _dilated_attn_kernel,
        grid=(n_blk,),
        in_specs=in_specs,
        out_specs=pl.BlockSpec((ATTN_BLOCK, GROUP_WIDTH_A), lambda n: (n, 0)),
        out_shape=jax.ShapeDtypeStruct((s, GROUP_WIDTH_A), BF16),
        scratch_shapes=[pltpu.VMEM((win, 2 * GROUP_WIDTH_A), BF16) for win, _ in DILATED_GROUPS]
        + [pltpu.VMEM((N_GROUPS_A, n_hp, ATTN_BLOCK, LANES), F32),
           pltpu.VMEM((N_GROUPS_A, n_hp, ATTN_BLOCK, LANES), F32)],
        compiler_params=_params("arbitrary"),
        name="dilated_attn",
    )(qkv, qkv, qkv)


def _to_heads(row, n_heads, head_dim):
    return jnp.concatenate([row[:, h * head_dim:(h + 1) * head_dim] for h in range(n_heads)], axis=0)


def _from_heads(t):
    return jnp.concatenate([t[h:h + 1] for h in range(t.shape[0])], axis=1)


def _window_cache_attention(c_ref, j, cols_ref, base, dil):
    win = c_ref.shape[-1]
    lc = min(win, SAMPLE_LANE_CHUNK)
    n_chunks = win // lc
    lane = lax.broadcasted_iota(jnp.int32, (1, win), 1)
    wanted = (lane % dil) == 0
    out = []
    for h in range(HEADS_PER_GROUP_A):
        lo = base + h * HEAD_DIM_A
        q, k_new, v_new = (cols_ref[lo + c * GROUP_WIDTH_A:lo + c * GROUP_WIDTH_A + HEAD_DIM_A, :] for c in range(3))
        q_wide = jnp.concatenate([q] * (lc // LANES), axis=1)
        parts = [jnp.sum(c_ref[j, 0, h, :, i * lc:(i + 1) * lc] * q_wide, axis=0, keepdims=True)
                 for i in range(n_chunks)]
        s = jnp.where(wanted, jnp.concatenate(parts, axis=1), NEG)
        s_n = jnp.sum(k_new[:, 0:1] * q[:, 0:1], axis=0, keepdims=True)
        m = jnp.maximum(jnp.max(s, axis=-1, keepdims=True), s_n)
        p = jnp.exp(s - m)
        p_n = jnp.exp(s_n - m)
        den = jnp.sum(p, axis=-1, keepdims=True) + p_n
        acc = p_n * v_new[:, 0:1]
        for i in range(n_chunks):
            acc = acc + jnp.sum(c_ref[j, 1, h, :, i * lc:(i + 1) * lc] * p[:, i * lc:(i + 1) * lc],
                                axis=-1, keepdims=True)
        out.append((acc / den, m + jnp.log(den)))
    return out


def _memory_cache_attention(cm_ref, j, q, s_ref):
    n_heads, head_dim = q.shape
    chunk = SAMPLE_MEM_CHUNK
    n_chunks = cm_ref.shape[1] // chunk
    m = jnp.full((n_heads, head_dim), NEG, F32)
    for c in range(n_chunks):
        pos = slice(c * chunk, (c + 1) * chunk)
        s = jnp.sum(cm_ref[j, pos, 0] * q[None], axis=-1, keepdims=True) * (head_dim ** -0.5)
        s = jnp.broadcast_to(s, (chunk, n_heads, head_dim))
        s_ref[pos] = s
        m = jnp.maximum(m, jnp.max(s, axis=0))
    den = jnp.zeros((n_heads, head_dim), F32)
    acc = jnp.zeros((n_heads, head_dim), F32)
    for c in range(n_chunks):
        pos = slice(c * chunk, (c + 1) * chunk)
        p = jnp.exp(s_ref[pos] - m[None])
        den = den + jnp.sum(p, axis=0)
        acc = acc + jnp.sum(p * cm_ref[j, pos, 1], axis=0)
    return acc / den


def _sample_attn_kernel(x_ref, wqkv_ref, wqm_ref, c0_ref, c1_ref, c2_ref, cm_ref,
                        oa_ref, om_ref, kvn_ref, proj_ref, projt_ref, cols_ref, oat_ref, ms_ref):
    step = pl.program_id(0)
    caches = (c0_ref, c1_ref, c2_ref)
    n_batch, d = x_ref.shape
    qkv_w = N_GROUPS_A * 3 * GROUP_WIDTH_A

    @pl.when(step == 0)
    def _():
        xb = jnp.concatenate([x_ref[...], jnp.zeros((LANES - n_batch, d), F32)], axis=0).astype(BF16)
        for part in range(3):
            h = _dot(xb, wqkv_ref[:, part * WIDTH_A:(part + 1) * WIDTH_A])
            if part == 0:
                h = h * (HEAD_DIM_A ** -0.5)
            for g in range(N_GROUPS_A):
                h_g = h[:, g * GROUP_WIDTH_A:(g + 1) * GROUP_WIDTH_A]
                dst = (3 * g + part) * GROUP_WIDTH_A
                proj_ref[:, dst:dst + GROUP_WIDTH_A] = h_g
                if part > 0:
                    kvn_ref[g, :, (part - 1) * GROUP_WIDTH_A:part * GROUP_WIDTH_A] = h_g[:n_batch]
        proj_ref[:, qkv_w:] = _dot(xb, wqm_ref[:, :WIDTH_M])
        for t in range(qkv_w // LANES):
            projt_ref[t * LANES:(t + 1) * LANES, :] = proj_ref[:, t * LANES:(t + 1) * LANES].T.astype(BF16)
        oat_ref[...] = jnp.zeros(oat_ref.shape, F32)

    def one_row(j, carry):
        b = step * SAMPLE_BATCH_BLOCK + j
        pick_row = (lax.broadcasted_iota(jnp.int32, (LANES, LANES), 0) == b).astype(BF16)
        cols_ref[...] = _dot(projt_ref[...], pick_row)
        per_group = [_window_cache_attention(caches[g], j, cols_ref, g * 3 * GROUP_WIDTH_A, dil)
                     for g, (_, dil) in enumerate(DILATED_GROUPS)]
        this_lane = lax.broadcasted_iota(jnp.int32, (1, LANES), 1) == b
        for h in range(HEADS_PER_GROUP_A):
            outs = [per_group[g][h][0] for g in range(N_GROUPS_A)]
            lses = [per_group[g][h][1] for g in range(N_GROUPS_A)]
            m3 = jnp.maximum(jnp.maximum(lses[0], lses[1]), lses[2])
            es = [jnp.exp(l - m3) for l in lses]
            oa = (es[0] * outs[0] + es[1] * outs[1] + es[2] * outs[2]) / (es[0] + es[1] + es[2])
            rows = slice(h * HEAD_DIM_A, (h + 1) * HEAD_DIM_A)
            oat_ref[rows, :] = jnp.where(this_lane, oa, oat_ref[rows, :])

        qm = _to_heads(proj_ref[pl.ds(b, 1), qkv_w:], N_HEADS_M, HEAD_DIM_M)
        om_ref[pl.ds(b, 1), :] = _from_heads(_memory_cache_attention(cm_ref, j, qm, ms_ref))
        return carry

    lax.fori_loop(0, SAMPLE_BATCH_BLOCK, one_row, 0)

    @pl.when(step == pl.num_programs(0) - 1)
    def _():
        oa_ref[...] = oat_ref[...].T[:n_batch, :]


def _sample_attn(xs, w_all, caches, cache_mem):
    n_batch, d = xs.shape
    assert n_batch <= LANES
    bb = SAMPLE_BATCH_BLOCK
    qkv_w = N_GROUPS_A * 3 * GROUP_WIDTH_A
    qm_col = 3 * WIDTH_A + 2 * WIDTH_B
    assert qm_col % WIDTH_B == 0 and WIDTH_M <= WIDTH_B
    w_specs = [pl.BlockSpec((d, qkv_w), lambda i: (0, 0), pipeline_mode=pl.Buffered(1)),
               pl.BlockSpec((d, WIDTH_B), lambda i: (0, qm_col // WIDTH_B), pipeline_mode=pl.Buffered(1))]
    cache_specs = [pl.BlockSpec((bb,) + c.shape[1:], lambda i: (i, 0, 0, 0, 0)) for c in caches]
    whole = lambda shape: pl.BlockSpec(shape, lambda i: (0,) * len(shape))
    return pl.pallas_call(
        _sample_attn_kernel,
        grid=(n_batch // bb,),
        in_specs=[whole(xs.shape)] + w_specs + cache_specs
        + [pl.BlockSpec((bb,) + cache_mem.shape[1:], lambda i: (i, 0, 0, 0, 0))],
        out_specs=[whole((n_batch, GROUP_WIDTH_A)), whole((n_batch, WIDTH_M)),
                   whole((N_GROUPS_A, n_batch, 2 * GROUP_WIDTH_A))],
        out_shape=[jax.ShapeDtypeStruct((n_batch, GROUP_WIDTH_A), F32),
                   jax.ShapeDtypeStruct((n_batch, WIDTH_M), F32),
                   jax.ShapeDtypeStruct((N_GROUPS_A, n_batch, 2 * GROUP_WIDTH_A), F32)],
        scratch_shapes=[pltpu.VMEM((LANES, qkv_w + WIDTH_M), F32),
                        pltpu.VMEM((qkv_w, LANES), BF16),
                        pltpu.VMEM((qkv_w, LANES), F32),
                        pltpu.VMEM((GROUP_WIDTH_A, LANES), F32),
                        pltpu.VMEM((cache_mem.shape[1], N_HEADS_M, HEAD_DIM_M), F32)],
        compiler_params=_params("arbitrary"),
        name="sample_attn",
    )(xs, w_all, w_all, *caches, cache_mem)


def _mix_kernel(x_ref, oa_ref, mk_ref, mv_ref, xs_ref, oas_ref, oms_ref, w_ref, bgate_ref, lnvg_ref, lnvb_ref,
                ws_ref, bs_ref, wss_ref, bss_ref, wba_ref, wbb_ref, wbm_ref, wout_ref, ln1g_ref, ln1b_ref,
                x1_ref, x1s_ref, vrows_ref, *, alpha):
    shared = (w_ref, bgate_ref, lnvg_ref, lnvb_ref)
    tail = (wba_ref, wbb_ref, wbm_ref, wout_ref, ln1g_ref, ln1b_ref)
    is_sample_step = pl.program_id(0) == pl.num_programs(0) - 1

    @pl.when(jnp.logical_not(is_sample_step))
    def _():
        _mix_body(x_ref, oa_ref, (mk_ref, mv_ref), *shared, ws_ref, bs_ref, *tail, x1_ref, None,
                  sample=False, alpha=alpha)

    @pl.when(is_sample_step)
    def _():
        _mix_body(xs_ref, oas_ref, oms_ref, *shared, wss_ref, bss_ref, *tail, x1s_ref, vrows_ref,
                  sample=True, alpha=alpha)


def _mix_body(x_ref, oa_ref, mem_or_om, w_ref, bgate_ref, lnvg_ref, lnvb_ref, ws_ref, bs_ref,
              wba_ref, wbb_ref, wbm_ref, wout_ref, ln1g_ref, ln1b_ref, x1_ref, vrows_ref, *, sample, alpha):
    if sample:
        om_ref = mem_or_om
    else:
        mk_ref, mv_ref = mem_or_om
    d = x_ref.shape[1]
    rows = x_ref.shape[0]
    x = x_ref[...]
    xb = x.astype(BF16)
    col_u, col_v, col_qm, col_gate = 0, WIDTH_B, 2 * WIDTH_B, 2 * WIDTH_B + WIDTH_M

    h_all = _dot(xb, w_ref[:, 3 * WIDTH_A:])

    def gate(k):
        z = h_all[:, col_gate + k * d:col_gate + (k + 1) * d]
        return 0.5 * jnp.tanh(0.5 * (z + bgate_ref[k:k + 1, :])) + 0.5

    mixed = gate(0) * _dot(oa_ref[...].astype(BF16), wba_ref[...])

    u = jax.nn.gelu(h_all[:, col_u:col_u + WIDTH_B])
    v = _layer_norm(jax.nn.gelu(h_all[:, col_v:col_v + WIDTH_B]), lnvg_ref[...], lnvb_ref[...])
    if sample:
        vrows_ref[...] = v
        spatial = v.astype(BF16).astype(F32) * ws_ref[...].astype(BF16).astype(F32) + bs_ref[...]
    else:
        r_i = lax.broadcasted_iota(jnp.int32, (N_GROUPS_B * CHUNK, CHUNK), 0)
        c_i = lax.broadcasted_iota(jnp.int32, (N_GROUPS_B * CHUNK, CHUNK), 1)
        w_s = jnp.where((r_i % CHUNK) >= c_i, ws_ref[...], 0.0).astype(BF16)
        lane = lax.broadcasted_iota(jnp.int32, (CHUNK, WIDTH_B), 1)
        in_group = [(lane >= gb * GROUP_DIM_B) & (lane < (gb + 1) * GROUP_DIM_B) for gb in range(N_GROUPS_B)]
        vb = v.astype(BF16)
        parts = []
        for ch in range(rows // CHUNK):
            allg = _dot(w_s, vb[ch * CHUNK:(ch + 1) * CHUNK, :])
            sp = bs_ref[...]
            for gb in range(N_GROUPS_B):
                sp = sp + jnp.where(in_group[gb], allg[gb * CHUNK:(gb + 1) * CHUNK, :], 0.0)
            parts.append(sp)
        spatial = jnp.concatenate(parts, axis=0)
    o_b = u * spatial
    mixed = mixed + gate(1) * _dot(o_b.astype(BF16), wbb_ref[...])

    if sample:
        o_m = om_ref[...]
    else:
        qm = h_all[:, col_qm:col_qm + WIDTH_M].astype(BF16)
        head_cols = [slice(h * HEAD_DIM_M, (h + 1) * HEAD_DIM_M) for h in range(N_HEADS_M)]
        scores = [_dot_nt(qm[:, hs], mk_ref[:, hs]) * (HEAD_DIM_M ** -0.5) for hs in head_cols]
        maxes = [jnp.max(s, axis=-1, keepdims=True) for s in scores]
        probs = [jnp.exp(s - m) for s, m in zip(scores, maxes)]
        dens = [jnp.sum(p, axis=-1, keepdims=True) for p in probs]
        o_m = jnp.concatenate([_dot(p.astype(BF16), mv_ref[:, hs]) / den
                               for p, den, hs in zip(probs, dens, head_cols)], axis=-1)
    mixed = mixed + gate(2) * _dot(o_m.astype(BF16), wbm_ref[...])

    _project_add_norm(mixed.astype(BF16), wout_ref, alpha * x, ln1g_ref, ln1b_ref, x1_ref)


def _mix(x, o_a, mk, mv, xs, oa_s, om_s, w_all, b_gate, ln_v_g, ln_v_b, w_s, b_s, w_s_row, b_s_row,
         w_ba, w_bb, w_bm, w_out, ln1_g, ln1_b, *, alpha):
    s, d = x.shape
    n_tiles = s // MIX_ROWS
    tile = lambda width: pl.BlockSpec((MIX_ROWS, width), lambda i: (jnp.minimum(i, n_tiles - 1), 0))
    whole = lambda shape: pl.BlockSpec(shape, lambda i: (0,) * len(shape))
    operands = (x, o_a, mk, mv, xs, oa_s, om_s, w_all, b_gate, ln_v_g, ln_v_b, w_s, b_s, w_s_row, b_s_row,
                w_ba, w_bb, w_bm, w_out, ln1_g, ln1_b)
    n_s = xs.shape[0]
    return pl.pallas_call(
        functools.partial(_mix_kernel, alpha=alpha),
        grid=(n_tiles + 1,),
        in_specs=[tile(d), tile(GROUP_WIDTH_A)] + [_resident(t.shape) for t in operands[2:]],
        out_specs=[tile(d), whole((n_s, d)), whole((n_s, WIDTH_B))],
        out_shape=[jax.ShapeDtypeStruct((s, d), F32), jax.ShapeDtypeStruct((n_s, d), F32),
                   jax.ShapeDtypeStruct((n_s, WIDTH_B), F32)],
        compiler_params=_params("arbitrary"),
        name="mix",
    )(*operands)


def _ffn_kernel(x1_ref, x1s_ref, st0_ref, st1_ref, wup_ref, cw_ref, cb_ref, wdown_ref, g_ref, b_ref,
                y_ref, a_ref, ys_ref, as_ref, abuf_ref, *, alpha):
    consts = (wup_ref, cw_ref, cb_ref, wdown_ref, g_ref, b_ref)
    is_sample_step = pl.program_id(0) == pl.num_programs(0) - 1

    @pl.when(jnp.logical_not(is_sample_step))
    def _():
        _ffn_body(x1_ref, None, None, *consts, y_ref, a_ref, abuf_ref, sample=False, alpha=alpha)

    @pl.when(is_sample_step)
    def _():
        _ffn_body(x1s_ref, st0_ref, st1_ref, *consts, ys_ref, as_ref, None, sample=True, alpha=alpha)


def _ffn_body(x1_ref, st0_ref, st1_ref, wup_ref, cw_ref, cb_ref, wdown_ref, g_ref, b_ref, y_ref, a_ref, abuf_ref,
              *, sample, alpha):
    rows = x1_ref.shape[0]
    d_ff = wdown_ref.shape[0]
    if not sample:
        @pl.when(pl.program_id(0) == 0)
        def _():
            abuf_ref[0:SUBLANES, :] = jnp.zeros((SUBLANES, d_ff), F32)

    x1 = x1_ref[...]
    xb = x1.astype(BF16)
    a = _dot(xb, wup_ref[:, :d_ff])
    val = _dot(xb, wup_ref[:, d_ff:])
    if sample:
        a_ref[...] = a
        a_m2, a_m1 = st0_ref[...], st1_ref[...]
    else:
        abuf_ref[SUBLANES:SUBLANES + rows, :] = a
        a_m1 = abuf_ref[SUBLANES - 1:SUBLANES - 1 + rows, :]
        a_m2 = abuf_ref[SUBLANES - 2:SUBLANES - 2 + rows, :]
    conv = cb_ref[...] + cw_ref[0:1, :] * a_m2 + cw_ref[1:2, :] * a_m1 + cw_ref[2:3, :] * a
    h = jax.nn.gelu(conv) * val
    _project_add_norm(h.astype(BF16), wdown_ref, alpha * x1, g_ref, b_ref, y_ref)
    if not sample:
        tail = abuf_ref[rows:rows + SUBLANES, :]
        abuf_ref[0:SUBLANES, :] = tail
        a_ref[...] = tail


def _ffn(x1, x1_s, state, w_up, conv_w, conv_b, w_down, ln2_g, ln2_b, *, alpha):
    s, d = x1.shape
    n_s = x1_s.shape[0]
    d_ff = w_down.shape[0]
    n_tiles = s // FFN_ROWS
    tile = lambda width: pl.BlockSpec((FFN_ROWS, width), lambda i: (jnp.minimum(i, n_tiles - 1), 0))
    whole = lambda shape: pl.BlockSpec(shape, lambda i: (0,) * len(shape))
    operands = (x1, x1_s, state[:, 0], state[:, 1], w_up, conv_w, conv_b, w_down, ln2_g, ln2_b)
    return pl.pallas_call(
        functools.partial(_ffn_kernel, alpha=alpha),
        grid=(n_tiles + 1,),
        in_specs=[tile(d)] + [_resident(t.shape) for t in operands[1:]],
        out_specs=[tile(d), whole((SUBLANES, d_ff)), whole((n_s, d)), whole((n_s, d_ff))],
        out_shape=[jax.ShapeDtypeStruct((s, d), F32), jax.ShapeDtypeStruct((SUBLANES, d_ff), F32),
                   jax.ShapeDtypeStruct((n_s, d), F32), jax.ShapeDtypeStruct((n_s, d_ff), F32)],
        scratch_shapes=[pltpu.VMEM((FFN_ROWS + SUBLANES, d_ff), F32)],
        compiler_params=_params("arbitrary"),
        name="ffn",
    )(*operands)


def kernel(x_prompt, x_sample, mem_prompt, cache_win128_kv, cache_win512_kv, cache_win2048_kv, cache_mem_kv, state_ffn_conv, w_in, b_gate, ln_v_g, ln_v_b, w_spatial, b_spatial, w_mem_kv, w_branch_a, w_branch_b, w_branch_m, w_out, ln1_g, ln1_b, w_up, conv_w, conv_b, w_down, ln2_g, ln2_b):
    depth = w_in.shape[0]
    batch, seq, d_model = x_prompt.shape
    dec_batch, dec_seq, _ = x_sample.shape
    assert batch == 1 and dec_seq == 1 and seq % ATTN_BLOCK == 0
    alpha = (2.0 * depth) ** 0.25
    win_caches = (cache_win128_kv, cache_win512_kv, cache_win2048_kv)
    for cache, (win, _) in zip(win_caches, DILATED_GROUPS):
        assert cache.shape[2] == win and PAST_LEN >= win

    yp = x_prompt.reshape(seq, d_model)
    ys = x_sample.reshape(dec_batch, d_model)
    win_p = [[] for _ in range(N_GROUPS_A)]
    win_s = [[] for _ in range(N_GROUPS_A)]
    mem_p, conv_p, gmlp_s, conv_s = [], [], [], []
    for l in range(depth):
        w_all = w_in[l].astype(BF16)
        row2 = lambda t: t.reshape(1, -1)
        per_lane = lambda t: jnp.repeat(t, GROUP_DIM_B, axis=-1)
        consts_tail = (w_branch_a[l].astype(BF16), w_branch_b[l].astype(BF16), w_branch_m[l].astype(BF16),
                       w_out[l].astype(BF16), row2(ln1_g[l]), row2(ln1_b[l]))
        ffn_consts = (w_up[l].astype(BF16), conv_w[l], row2(conv_b[l]), w_down[l].astype(BF16),
                      row2(ln2_g[l]), row2(ln2_b[l]))

        mkv = _mem_kv_proj(mem_prompt[0], w_mem_kv[l].astype(BF16))
        mem_p.append(mkv.reshape(1, -1, 2, N_HEADS_M, HEAD_DIM_M))
        qkv, *kv_tails = _qkv_proj(yp, w_all)
        for g, (win, _) in enumerate(DILATED_GROUPS):
            tail = kv_tails[g].reshape(2, HEADS_PER_GROUP_A, HEAD_DIM_A, win)
            win_p[g].append(jnp.transpose(tail, (3, 0, 1, 2))[None])
        o_a = _dilated_attn(qkv)
        caches = [jnp.transpose(c[l], (0, 2, 3, 4, 1)) for c in win_caches]
        oa_s, om_s, kv_new = _sample_attn(ys, w_all, caches, cache_mem_kv[l])
        for g in range(N_GROUPS_A):
            win_s[g].append(kv_new[g].reshape(dec_batch, 1, 2, HEADS_PER_GROUP_A, HEAD_DIM_A))

        x1, x1_s, v_rows = _mix(yp, o_a, mkv[:, :WIDTH_M].astype(BF16), mkv[:, WIDTH_M:].astype(BF16), ys, oa_s, om_s,
                                w_all, b_gate[l], row2(ln_v_g[l]), row2(ln_v_b[l]),
                                w_spatial[l].reshape(N_GROUPS_B * CHUNK, CHUNK), per_lane(b_spatial[l].T),
                                row2(per_lane(w_spatial[l][:, 0, 0])), row2(per_lane(b_spatial[l][:, 0])),
                                *consts_tail, alpha=alpha)
        gmlp_s.append(v_rows.reshape(dec_batch, 1, WIDTH_B))
        yp_next, a_tail, ys_next, a_s = _ffn(x1, x1_s, state_ffn_conv[l], *ffn_consts, alpha=alpha)
        conv_p.append(a_tail[SUBLANES - (CONV_W - 1):][None])
        conv_s.append(jnp.stack([state_ffn_conv[l][:, 1], a_s], axis=1))
        yp, ys = yp_next, ys_next

    return (yp.reshape(batch, seq, d_model), ys.reshape(dec_batch, dec_seq, d_model),
            jnp.stack(win_p[0]), jnp.stack(win_p[1]), jnp.stack(win_p[2]),
            jnp.stack(mem_p), jnp.stack(conv_p),
            jnp.stack(win_s[0]), jnp.stack(win_s[1]), jnp.stack(win_s[2]),
            jnp.stack(gmlp_s), jnp.stack(conv_s))
```

```python
import functools

import jax
import jax.numpy as jnp
from jax import lax
from jax.experimental import pallas as pl
from jax.experimental.pallas import tpu as pltpu

BF16 = jnp.bfloat16
F32 = jnp.float32

HEAD_DIM_A = 64
HEADS_PER_GROUP_A = 4
DILATED_GROUPS = ((128, 1), (512, 4), (2048, 16))
N_GROUPS_A = len(DILATED_GROUPS)
GROUP_WIDTH_A = HEADS_PER_GROUP_A * HEAD_DIM_A
WIDTH_A = N_GROUPS_A * GROUP_WIDTH_A
NK = 128
CHUNK = 128
N_GROUPS_B = 4
WIDTH_B = 768
GROUP_DIM_B = WIDTH_B // N_GROUPS_B
N_HEADS_M = 4
HEAD_DIM_M = 128
WIDTH_M = N_HEADS_M * HEAD_DIM_M
N_BRANCH = 3
CONV_W = 3
LN_EPS = 1e-5
NEG = -1e30
PAST_LEN = 16384

LANES = 128
SUBLANES = 8
VMEM_LIMIT_BYTES = 58 * 1024 * 1024

ATTN_BLOCK = 2048
QKV_ROW_PART = 512
DEINTERLEAVE_STRIDE = 4
MIX_ROWS = 512
FFN_ROWS = 512
NORM_ROW_PARTS = 4
SAMPLE_BATCH_BLOCK = 2
SAMPLE_LANE_CHUNK = 512
SAMPLE_MEM_CHUNK = 32

_NT = (((1,), (1,)), ((), ()))


def _dot(a, b):
    return jnp.dot(a, b, preferred_element_type=F32)


def _dot_nt(a, b):
    return lax.dot_general(a, b, _NT, preferred_element_type=F32)


def _layer_norm(x, g, b):
    mu = jnp.mean(x, axis=-1, keepdims=True)
    var = jnp.mean(jnp.square(x - mu), axis=-1, keepdims=True)
    return (x - mu) * lax.rsqrt(var + LN_EPS) * g + b


def _project_add_norm(lhs, w_ref, residual, g_ref, b_ref, o_ref):
    rows = lhs.shape[0]
    n_parts = NORM_ROW_PARTS if rows % (NORM_ROW_PARTS * 2 * SUBLANES) == 0 else 1
    part = rows // n_parts
    for r0 in range(0, rows, part):
        rs = slice(r0, r0 + part)
        o_ref[rs, :] = _layer_norm(residual[rs] + _dot(lhs[rs], w_ref[...]), g_ref[...], b_ref[...])


def _params(*semantics):
    return pltpu.CompilerParams(dimension_semantics=semantics, vmem_limit_bytes=VMEM_LIMIT_BYTES)


def _resident(shape):
    zeros = (0,) * len(shape)
    return pl.BlockSpec(shape, lambda *_: zeros, pipeline_mode=pl.Buffered(1))


def _mem_kv_kernel(mem_ref, w_ref, o_ref):
    o_ref[...] = _dot(mem_ref[...].astype(BF16), w_ref[...])


def _mem_kv_proj(mem, w):
    n, d = mem.shape
    return pl.pallas_call(
        _mem_kv_kernel,
        out_shape=jax.ShapeDtypeStruct((n, w.shape[1]), F32),
        name="mem_kv_proj",
    )(mem, w)


def _qkv_proj_kernel(xt_ref, xbot_ref, w_ref, o_ref, kvt0_ref, kvt1_ref, kvt2_ref, xb_ref, stage_ref, tmp_ref):
    g_id = pl.program_id(1)

    @pl.when(g_id == 0)
    def _():
        half = ATTN_BLOCK // 2
        xb_ref[0:half, :] = xt_ref[...].astype(BF16)
        xb_ref[half:, :] = xbot_ref[...].astype(BF16)

    xb = xb_ref[...]
    last_block = pl.program_id(0) == pl.num_programs(0) - 1
    for g, (win, dil) in enumerate(DILATED_GROUPS):
        kvt_ref = (kvt0_ref, kvt1_ref, kvt2_ref)[g]

        @pl.when(g_id == g)
        def _(g=g, win=win, dil=dil, kvt_ref=kvt_ref):
            w_cols = lambda c: w_ref[:, c * WIDTH_A + g * GROUP_WIDTH_A:c * WIDTH_A + (g + 1) * GROUP_WIDTH_A]
            for c in range(3):
                w = w_cols(c)
                if c == 0:
                    w = w * jnp.asarray(HEAD_DIM_A ** -0.5, BF16)
                n_slabs = GROUP_WIDTH_A // LANES
                inner = DEINTERLEAVE_STRIDE if dil > DEINTERLEAVE_STRIDE else dil
                outer = dil // inner
                part_rows = min(win, QKV_ROW_PART) if dil > 1 else QKV_ROW_PART
                for start in range(0, ATTN_BLOCK, part_rows):
                    res = _dot(xb[start:start + part_rows], w)
                    if dil == 1:
                        o_ref[start:start + part_rows, c * GROUP_WIDTH_A:(c + 1) * GROUP_WIDTH_A] = res.astype(BF16)
                        continue
                    for s in range(n_slabs):
                        stage_ref[s, start:start + part_rows, :] = res[:, s * LANES:(s + 1) * LANES]
                    if (start + part_rows) % win:
                        continue
                    base = start + part_rows - win
                    for s in range(n_slabs):
                        col = c * GROUP_WIDTH_A + s * LANES
                        src_ref = stage_ref
                        if outer > 1:
                            sub = win // inner
                            for lo in range(inner):
                                tmp_ref[s, base + lo * sub:base + (lo + 1) * sub, :] = (
                                    stage_ref[s, pl.ds(base + lo, sub, stride=inner), :])
                            src_ref = tmp_ref
                        for lo in range(inner):
                            for hi in range(outer):
                                first = base + lo * (win // inner) + hi if outer > 1 else base + lo
                                rows = src_ref[s, pl.ds(first, NK, stride=outer if outer > 1 else inner), :]
                                dst = base + (hi * inner + lo) * NK
                                o_ref[dst:dst + NK, col:col + LANES] = rows.astype(BF16)

            @pl.when(last_block)
            def _():
                tail = xb[ATTN_BLOCK - win:, :]
                kvt_ref[0:GROUP_WIDTH_A, :] = _dot(tail, w_cols(1)).T
                kvt_ref[GROUP_WIDTH_A:2 * GROUP_WIDTH_A, :] = _dot(tail, w_cols(2)).T


def _qkv_proj(x, w_all):
    s, d = x.shape
    n_blk = s // ATTN_BLOCK
    half = ATTN_BLOCK // 2
    x_half = lambda which: pl.BlockSpec(
        (half, d), lambda n, g: (2 * jnp.minimum(n + jnp.minimum(jnp.maximum(g - which, 0), 1), n_blk - 1) + which, 0))
    kvt_shapes = [(2 * GROUP_WIDTH_A, win) for win, _ in DILATED_GROUPS]
    return pl.pallas_call(
        _qkv_proj_kernel,
        grid=(n_blk, N_GROUPS_A),
        in_specs=[x_half(0), x_half(1),
                  pl.BlockSpec((d, 3 * WIDTH_A), lambda n, g: (0, 0), pipeline_mode=pl.Buffered(1))],
        out_specs=[pl.BlockSpec((None, ATTN_BLOCK, 3 * GROUP_WIDTH_A), lambda n, g: (g, n, 0))]
        + [pl.BlockSpec(shape, lambda n, g: (0, 0)) for shape in kvt_shapes],
        out_shape=[jax.ShapeDtypeStruct((N_GROUPS_A, s, 3 * GROUP_WIDTH_A), BF16)]
        + [jax.ShapeDtypeStruct(shape, F32) for shape in kvt_shapes],
        scratch_shapes=[pltpu.VMEM((ATTN_BLOCK, d), BF16),
                        pltpu.VMEM((GROUP_WIDTH_A // LANES, ATTN_BLOCK, LANES), F32),
                        pltpu.VMEM((GROUP_WIDTH_A // LANES, ATTN_BLOCK, LANES), F32)],
        compiler_params=_params("arbitrary", "arbitrary"),
        name="qkv_proj",
    )(x, x, w_all)


def _attn_units(units, prev_shifts):
    first_head = lax.broadcasted_iota(jnp.int32, (NK, LANES), 1) < HEAD_DIM_A
    row = lax.broadcasted_iota(jnp.int32, (2 * NK, NK), 0) % NK
    col = lax.broadcasted_iota(jnp.int32, (2 * NK, NK), 1)
    cur_ok = col <= row
    prev_ok = {}
    scores = []
    for (q, k_p, k_c, _, _), shift in zip(units, prev_shifts):
        zero = jnp.zeros_like(q)
        q_st = jnp.concatenate([jnp.where(first_head, q, zero), jnp.where(first_head, zero, q)], axis=0)
        if id(shift) not in prev_ok:
            prev_ok[id(shift)] = col >= row + shift
        scores.append((jnp.where(cur_ok, _dot_nt(q_st, k_c), NEG),
                       jnp.where(prev_ok[id(shift)], _dot_nt(q_st, k_p), NEG)))
    maxes = [jnp.max(jnp.maximum(s_c, s_p), axis=-1, keepdims=True) for s_c, s_p in scores]
    probs = [(jnp.exp(s_c - m), jnp.exp(s_p - m)) for (s_c, s_p), m in zip(scores, maxes)]
    dens = [jnp.sum(p_c + p_p, axis=-1, keepdims=True) for p_c, p_p in probs]
    results = []
    for (_, _, _, v_p, v_c), (p_c, p_p), m, den in zip(units, probs, maxes, dens):
        o_st = _dot(p_c.astype(BF16), v_c) + _dot(p_p.astype(BF16), v_p)
        pick = lambda t: jnp.where(first_head, jnp.broadcast_to(t[:NK], (NK, LANES)), jnp.broadcast_to(t[NK:], (NK, LANES)))
        den_sel = pick(den)
        results.append((pick(o_st) / den_sel, pick(m) + jnp.log(den_sel)))
    return results


def _dilated_attn_kernel(qkv0_ref, qkv1_ref, qkv2_ref, o_ref, kv0_ref, kv1_ref, kv2_ref, og_ref, lg_ref):
    n = pl.program_id(0)
    qkv_refs = (qkv0_ref, qkv1_ref, qkv2_ref)
    kv_refs = (kv0_ref, kv1_ref, kv2_ref)
    n_sub = ATTN_BLOCK // NK
    n_hp = GROUP_WIDTH_A // LANES

    @pl.when(n == 0)
    def _():
        for kv_ref in kv_refs:
            kv_ref[...] = jnp.zeros(kv_ref.shape, BF16)

    def sub_block(t, carry, prev_in_block):
        off = pl.multiple_of(t * NK, NK)
        units, shifts, dsts = [], [], []
        for g, (win, dil) in enumerate(DILATED_GROUPS):
            win_blk = t // dil
            res = t - win_blk * dil
            has_prev = (n * (ATTN_BLOCK // win) + win_blk) > 0
            nat = win_blk * win + res
            shift = jnp.where(has_prev, 0, 2 * NK)
            for hp in range(n_hp):
                qs = slice(hp * LANES, (hp + 1) * LANES)
                ks = slice(GROUP_WIDTH_A + hp * LANES, GROUP_WIDTH_A + (hp + 1) * LANES)
                vs = slice(2 * GROUP_WIDTH_A + hp * LANES, 2 * GROUP_WIDTH_A + (hp + 1) * LANES)
                cur = pl.ds(off, NK)
                if prev_in_block[g]:
                    prev = pl.ds(off - win, NK)
                    k_prev, v_prev = qkv_refs[g][prev, ks], qkv_refs[g][prev, vs]
                else:
                    k_prev = kv_refs[g][cur, qs]
                    v_prev = kv_refs[g][cur, GROUP_WIDTH_A + hp * LANES:GROUP_WIDTH_A + (hp + 1) * LANES]
                units.append((qkv_refs[g][cur, qs], k_prev, qkv_refs[g][cur, ks], v_prev, qkv_refs[g][cur, vs]))
                shifts.append(shift)
                dsts.append((g, hp, pl.ds(off, NK) if dil == 1 else pl.ds(nat, NK, stride=dil)))
        for (g, hp, dst), (out, lse) in zip(dsts, _attn_units(units, shifts)):
            og_ref[g, hp, dst, :] = out
            lg_ref[g, hp, dst, :] = lse
        return carry

    bounds = sorted({0, n_sub} | {min(win // NK, n_sub) for win, _ in DILATED_GROUPS})
    for lo, hi in zip(bounds[:-1], bounds[1:]):
        in_block = tuple(lo * NK >= win for win, _ in DILATED_GROUPS)
        lax.fori_loop(lo, hi, functools.partial(sub_block, prev_in_block=in_block), 0)
    for (win, _), qkv_ref, kv_ref in zip(DILATED_GROUPS, qkv_refs, kv_refs):
        kv_ref[...] = qkv_ref[ATTN_BLOCK - win:, GROUP_WIDTH_A:]

    def merge(t, carry):
        off = pl.multiple_of(t * NK, NK)
        for hp in range(GROUP_WIDTH_A // LANES):
            lses = [lg_ref[g, hp, pl.ds(off, NK), :] for g in range(N_GROUPS_A)]
            m = jnp.maximum(jnp.maximum(lses[0], lses[1]), lses[2])
            es = [jnp.exp(l - m) for l in lses]
            num = sum(es[g] * og_ref[g, hp, pl.ds(off, NK), :] for g in range(N_GROUPS_A))
            o_ref[pl.ds(off, NK), hp * LANES:(hp + 1) * LANES] = (num / (es[0] + es[1] + es[2])).astype(BF16)
        return carry

    lax.fori_loop(0, n_sub, merge, 0)


def _dilated_attn(qkv):
    _, s, _ = qkv.shape
    n_blk = s // ATTN_BLOCK
    n_hp = GROUP_WIDTH_A // LANES
    in_specs = [pl.BlockSpec((None, ATTN_BLOCK, 3 * GROUP_WIDTH_A), functools.partial(lambda n, g: (g, n, 0), g=g))
                for g in range(N_GROUPS_A)]
    return pl.pallas_call(
        _dilated_attn_kernel,
        grid=(n_blk,),
        in_specs=in_specs,
        out_specs=pl.BlockSpec((ATTN_BLOCK, GROUP_WIDTH_A), lambda n: (n, 0)),
        out_shape=jax.ShapeDtypeStruct((s, GROUP_WIDTH_A), BF16),
        scratch_shapes=[pltpu.VMEM((win, 2 * GROUP_WIDTH_A), BF16) for win, _ in DILATED_GROUPS]
        + [pltpu.VMEM((N_GROUPS_A, n_hp, ATTN_BLOCK, LANES), F32),
           pltpu.VMEM((N_GROUPS_A, n_hp, ATTN_BLOCK, LANES), F32)],
        compiler_params=_params("arbitrary"),
        name="dilated_attn",
    )(qkv, qkv, qkv)


def _to_heads(row, n_heads, head_dim):
    return jnp.concatenate([row[:, h * head_dim:(h + 1) * head_dim] for h in range(n_heads)], axis=0)


def _from_heads(t):
    return jnp.concatenate([t[h:h + 1] for h in range(t.shape[0])], axis=1)


def _window_cache_attention(c_ref, j, cols_ref, base, dil):
    win = c_ref.shape[-1]
    lc = min(win, SAMPLE_LANE_CHUNK)
    n_chunks = win // lc
    lane = lax.broadcasted_iota(jnp.int32, (1, win), 1)
    wanted = (lane % dil) == 0
    out = []
    for h in range(HEADS_PER_GROUP_A):
        lo = base + h * HEAD_DIM_A
        q, k_new, v_new = (cols_ref[lo + c * GROUP_WIDTH_A:lo + c * GROUP_WIDTH_A + HEAD_DIM_A, :] for c in range(3))
        q_wide = jnp.concatenate([q] * (lc // LANES), axis=1)
        parts = [jnp.sum(c_ref[j, 0, h, :, i * lc:(i + 1) * lc] * q_wide, axis=0, keepdims=True)
                 for i in range(n_chunks)]
        s = jnp.where(wanted, jnp.concatenate(parts, axis=1), NEG)
        s_n = jnp.sum(k_new[:, 0:1] * q[:, 0:1], axis=0, keepdims=True)
        m = jnp.maximum(jnp.max(s, axis=-1, keepdims=True), s_n)
        p = jnp.exp(s - m)
        p_n = jnp.exp(s_n - m)
        den = jnp.sum(p, axis=-1, keepdims=True) + p_n
        acc = p_n * v_new[:, 0:1]
        for i in range(n_chunks):
            acc = acc + jnp.sum(c_ref[j, 1, h, :, i * lc:(i + 1) * lc] * p[:, i * lc:(i + 1) * lc],
                                axis=-1, keepdims=True)
        out.append((acc / den, m + jnp.log(den)))
    return out


def _memory_cache_attention(cm_ref, j, q, s_ref):
    n_heads, head_dim = q.shape
    chunk = SAMPLE_MEM_CHUNK
    n_chunks = cm_ref.shape[1] // chunk
    m = jnp.full((n_heads, head_dim), NEG, F32)
    for c in range(n_chunks):
        pos = slice(c * chunk, (c + 1) * chunk)
        s = jnp.sum(cm_ref[j, pos, 0] * q[None], axis=-1, keepdims=True) * (head_dim ** -0.5)
        s = jnp.broadcast_to(s, (chunk, n_heads, head_dim))
        s_ref[pos] = s
        m = jnp.maximum(m, jnp.max(s, axis=0))
    den = jnp.zeros((n_heads, head_dim), F32)
    acc = jnp.zeros((n_heads, head_dim), F32)
    for c in range(n_chunks):
        pos = slice(c * chunk, (c + 1) * chunk)
        p = jnp.exp(s_ref[pos] - m[None])
        den = den + jnp.sum(p, axis=0)
        acc = acc + jnp.sum(p * cm_ref[j, pos, 1], axis=0)
    return acc / den


def _sample_attn_kernel(x_ref, wqkv_ref, wqm_ref, c0_ref, c1_ref, c2_ref, cm_ref,
                        oa_ref, om_ref, kvn_ref, proj_ref, projt_ref, cols_ref, oat_ref, ms_ref):
    step = pl.program_id(0)
    caches = (c0_ref, c1_ref, c2_ref)
    n_batch, d = x_ref.shape
    qkv_w = N_GROUPS_A * 3 * GROUP_WIDTH_A

    @pl.when(step == 0)
    def _():
        xb = jnp.concatenate([x_ref[...], jnp.zeros((LANES - n_batch, d), F32)], axis=0).astype(BF16)
        for part in range(3):
            h = _dot(xb, wqkv_ref[:, part * WIDTH_A:(part + 1) * WIDTH_A])
            if part == 0:
                h = h * (HEAD_DIM_A ** -0.5)
            for g in range(N_GROUPS_A):
                h_g = h[:, g * GROUP_WIDTH_A:(g + 1) * GROUP_WIDTH_A]
                dst = (3 * g + part) * GROUP_WIDTH_A
                proj_ref[:, dst:dst + GROUP_WIDTH_A] = h_g
                if part > 0:
                    kvn_ref[g, :, (part - 1) * GROUP_WIDTH_A:part * GROUP_WIDTH_A] = h_g[:n_batch]
        proj_ref[:, qkv_w:] = _dot(xb, wqm_ref[:, :WIDTH_M])
        for t in range(qkv_w // LANES):
            projt_ref[t * LANES:(t + 1) * LANES, :] = proj_ref[:, t * LANES:(t + 1) * LANES].T.astype(BF16)
        oat_ref[...] = jnp.zeros(oat_ref.shape, F32)

    def one_row(j, carry):
        b = step * SAMPLE_BATCH_BLOCK + j
        pick_row = (lax.broadcasted_iota(jnp.int32, (LANES, LANES), 0) == b).astype(BF16)
        cols_ref[...] = _dot(projt_ref[...], pick_row)
        per_group = [_window_cache_attention(caches[g], j, cols_ref, g * 3 * GROUP_WIDTH_A, dil)
                     for g, (_, dil) in enumerate(DILATED_GROUPS)]
        this_lane = lax.broadcasted_iota(jnp.int32, (1, LANES), 1) == b
        for h in range(HEADS_PER_GROUP_A):
            outs = [per_group[g][h][0] for g in range(N_GROUPS_A)]
            lses = [per_group[g][h][1] for g in range(N_GROUPS_A)]
            m3 = jnp.maximum(jnp.maximum(lses[0], lses[1]), lses[2])
            es = [jnp.exp(l - m3) for l in lses]
            oa = (es[0] * outs[0] + es[1] * outs[1] + es[2] * outs[2]) / (es[0] + es[1] + es[2])
            rows = slice(h * HEAD_DIM_A, (h + 1) * HEAD_DIM_A)
            oat_ref[rows, :] = jnp.where(this_lane, oa, oat_ref[rows, :])

        qm = _to_heads(proj_ref[pl.ds(b, 1), qkv_w:], N_HEADS_M, HEAD_DIM_M)
        om_ref[pl.ds(b, 1), :] = _from_heads(_memory_cache_attention(cm_ref, j, qm, ms_ref))
        return carry

    lax.fori_loop(0, SAMPLE_BATCH_BLOCK, one_row, 0)

    @pl.when(step == pl.num_programs(0) - 1)
    def _():
        oa_ref[...] = oat_ref[...].T[:n_batch, :]


def _sample_attn(xs, w_all, caches, cache_mem):
    n_batch, d = xs.shape
    assert n_batch <= LANES
    bb = SAMPLE_BATCH_BLOCK
    qkv_w = N_GROUPS_A * 3 * GROUP_WIDTH_A
    qm_col = 3 * WIDTH_A + 2 * WIDTH_B
    assert qm_col % WIDTH_B == 0 and WIDTH_M <= WIDTH_B
    w_specs = [pl.BlockSpec((d, qkv_w), lambda i: (0, 0), pipeline_mode=pl.Buffered(1)),
               pl.BlockSpec((d, WIDTH_B), lambda i: (0, qm_col // WIDTH_B), pipeline_mode=pl.Buffered(1))]
    cache_specs = [pl.BlockSpec((bb,) + c.shape[1:], lambda i: (i, 0, 0, 0, 0)) for c in caches]
    whole = lambda shape: pl.BlockSpec(shape, lambda i: (0,) * len(shape))
    return pl.pallas_call(
        _sample_attn_kernel,
        grid=(n_batch // bb,),
        in_specs=[whole(xs.shape)] + w_specs + cache_specs
        + [pl.BlockSpec((bb,) + cache_mem.shape[1:], lambda i: (i, 0, 0, 0, 0))],
        out_specs=[whole((n_batch, GROUP_WIDTH_A)), whole((n_batch, WIDTH_M)),
                   whole((N_GROUPS_A, n_batch, 2 * GROUP_WIDTH_A))],
        out_shape=[jax.ShapeDtypeStruct((n_batch, GROUP_WIDTH_A), F32),
                   jax.ShapeDtypeStruct((n_batch, WIDTH_M), F32),
                   jax.ShapeDtypeStruct((N_GROUPS_A, n_batch, 2 * GROUP_WIDTH_A), F32)],
        scratch_shapes=[pltpu.VMEM((LANES, qkv_w + WIDTH_M), F32),
                        pltpu.VMEM((qkv_w, LANES), BF16),
                        pltpu.VMEM((qkv_w, LANES), F32),
                        pltpu.VMEM((GROUP_WIDTH_A, LANES), F32),
                        pltpu.VMEM((cache_mem.shape[1], N_HEADS_M, HEAD_DIM_M), F32)],
        compiler_params=_params("arbitrary"),
        name="sample_attn",
    )(xs, w_all, w_all, *caches, cache_mem)


def _mix_kernel(x_ref, oa_ref, mk_ref, mv_ref, xs_ref, oas_ref, oms_ref, w_ref, bgate_ref, lnvg_ref, lnvb_ref,
                ws_ref, bs_ref, wss_ref, bss_ref, wba_ref, wbb_ref, wbm_ref, wout_ref, ln1g_ref, ln1b_ref,
                x1_ref, x1s_ref, vrows_ref, *, alpha):
    shared = (w_ref, bgate_ref, lnvg_ref, lnvb_ref)
    tail = (wba_ref, wbb_ref, wbm_ref, wout_ref, ln1g_ref, ln1b_ref)
    is_sample_step = pl.program_id(0) == pl.num_programs(0) - 1

    @pl.when(jnp.logical_not(is_sample_step))
    def _():
        _mix_body(x_ref, oa_ref, (mk_ref, mv_ref), *shared, ws_ref, bs_ref, *tail, x1_ref, None,
                  sample=False, alpha=alpha)

    @pl.when(is_sample_step)
    def _():
        _mix_body(xs_ref, oas_ref, oms_ref, *shared, wss_ref, bss_ref, *tail, x1s_ref, vrows_ref,
                  sample=True, alpha=alpha)


def _mix_body(x_ref, oa_ref, mem_or_om, w_ref, bgate_ref, lnvg_ref, lnvb_ref, ws_ref, bs_ref,
              wba_ref, wbb_ref, wbm_ref, wout_ref, ln1g_ref, ln1b_ref, x1_ref, vrows_ref, *, sample, alpha):
    if sample:
        om_ref = mem_or_om
    else:
        mk_ref, mv_ref = mem_or_om
    d = x_ref.shape[1]
    rows = x_ref.shape[0]
    x = x_ref[...]
    xb = x.astype(BF16)
    col_u, col_v, col_qm, col_gate = 0, WIDTH_B, 2 * WIDTH_B, 2 * WIDTH_B + WIDTH_M

    h_all = _dot(xb, w_ref[:, 3 * WIDTH_A:])

    def gate(k):
        z = h_all[:, col_gate + k * d:col_gate + (k + 1) * d]
        return 0.5 * jnp.tanh(0.5 * (z + bgate_ref[k:k + 1, :])) + 0.5

    mixed = gate(0) * _dot(oa_ref[...].astype(BF16), wba_ref[...])

    u = jax.nn.gelu(h_all[:, col_u:col_u + WIDTH_B])
    v = _layer_norm(jax.nn.gelu(h_all[:, col_v:col_v + WIDTH_B]), lnvg_ref[...], lnvb_ref[...])
    if sample:
        vrows_ref[...] = v
        spatial = v.astype(BF16).astype(F32) * ws_ref[...].astype(BF16).astype(F32) + bs_ref[...]
    else:
        r_i = lax.broadcasted_iota(jnp.int32, (N_GROUPS_B * CHUNK, CHUNK), 0)
        c_i = lax.broadcasted_iota(jnp.int32, (N_GROUPS_B * CHUNK, CHUNK), 1)
        w_s = jnp.where((r_i % CHUNK) >= c_i, ws_ref[...], 0.0).astype(BF16)
        lane = lax.broadcasted_iota(jnp.int32, (CHUNK, WIDTH_B), 1)
        in_group = [(lane >= gb * GROUP_DIM_B) & (lane < (gb + 1) * GROUP_DIM_B) for gb in range(N_GROUPS_B)]
        vb = v.astype(BF16)
        parts = []
        for ch in range(rows // CHUNK):
            allg = _dot(w_s, vb[ch * CHUNK:(ch + 1) * CHUNK, :])
            sp = bs_ref[...]
            for gb in range(N_GROUPS_B):
                sp = sp + jnp.where(in_group[gb], allg[gb * CHUNK:(gb + 1) * CHUNK, :], 0.0)
            parts.append(sp)
        spatial = jnp.concatenate(parts, axis=0)
    o_b = u * spatial
    mixed = mixed + gate(1) * _dot(o_b.astype(BF16), wbb_ref[...])

    if sample:
        o_m = om_ref[...]
    else:
        qm = h_all[:, col_qm:col_qm + WIDTH_M].astype(BF16)
        head_cols = [slice(h * HEAD_DIM_M, (h + 1) * HEAD_DIM_M) for h in range(N_HEADS_M)]
        scores = [_dot_nt(qm[:, hs], mk_ref[:, hs]) * (HEAD_DIM_M ** -0.5) for hs in head_cols]
        maxes = [jnp.max(s, axis=-1, keepdims=True) for s in scores]
        probs = [jnp.exp(s - m) for s, m in zip(scores, maxes)]
        dens = [jnp.sum(p, axis=-1, keepdims=True) for p in probs]
        o_m = jnp.concatenate([_dot(p.astype(BF16), mv_ref[:, hs]) / den
                               for p, den, hs in zip(probs, dens, head_cols)], axis=-1)
    mixed = mixed + gate(2) * _dot(o_m.astype(BF16), wbm_ref[...])

    _project_add_norm(mixed.astype(BF16), wout_ref, alpha * x, ln1g_ref, ln1b_ref, x1_ref)


def _mix(x, o_a, mk, mv, xs, oa_s, om_s, w_all, b_gate, ln_v_g, ln_v_b, w_s, b_s, w_s_row, b_s_row,
         w_ba, w_bb, w_bm, w_out, ln1_g, ln1_b, *, alpha):
    s, d = x.shape
    n_tiles = s // MIX_ROWS
    tile = lambda width: pl.BlockSpec((MIX_ROWS, width), lambda i: (jnp.minimum(i, n_tiles - 1), 0))
    whole = lambda shape: pl.BlockSpec(shape, lambda i: (0,) * len(shape))
    operands = (x, o_a, mk, mv, xs, oa_s, om_s, w_all, b_gate, ln_v_g, ln_v_b, w_s, b_s, w_s_row, b_s_row,
                w_ba, w_bb, w_bm, w_out, ln1_g, ln1_b)
    n_s = xs.shape[0]
    return pl.pallas_call(
        functools.partial(_mix_kernel, alpha=alpha),
        grid=(n_tiles + 1,),
        in_specs=[tile(d), tile(GROUP_WIDTH_A)] + [_resident(t.shape) for t in operands[2:]],
        out_specs=[tile(d), whole((n_s, d)), whole((n_s, WIDTH_B))],
        out_shape=[jax.ShapeDtypeStruct((s, d), F32), jax.ShapeDtypeStruct((n_s, d), F32),
                   jax.ShapeDtypeStruct((n_s, WIDTH_B), F32)],
        compiler_params=_params("arbitrary"),
        name="mix",
    )(*operands)


def _ffn_kernel(x1_ref, x1s_ref, st0_ref, st1_ref, wup_ref, cw_ref, cb_ref, wdown_ref, g_ref, b_ref,
                y_ref, a_ref, ys_ref, as_ref, abuf_ref, *, alpha):
    consts = (wup_ref, cw_ref, cb_ref, wdown_ref, g_ref, b_ref)
    is_sample_step = pl.program_id(0) == pl.num_programs(0) - 1

    @pl.when(jnp.logical_not(is_sample_step))
    def _():
        _ffn_body(x1_ref, None, None, *consts, y_ref, a_ref, abuf_ref, sample=False, alpha=alpha)

    @pl.when(is_sample_step)
    def _():
        _ffn_body(x1s_ref, st0_ref, st1_ref, *consts, ys_ref, as_ref, None, sample=True, alpha=alpha)


def _ffn_body(x1_ref, st0_ref, st1_ref, wup_ref, cw_ref, cb_ref, wdown_ref, g_ref, b_ref, y_ref, a_ref, abuf_ref,
              *, sample, alpha):
    rows = x1_ref.shape[0]
    d_ff = wdown_ref.shape[0]
    if not sample:
        @pl.when(pl.program_id(0) == 0)
        def _():
            abuf_ref[0:SUBLANES, :] = jnp.zeros((SUBLANES, d_ff), F32)

    x1 = x1_ref[...]
    xb = x1.astype(BF16)
    a = _dot(xb, wup_ref[:, :d_ff])
    val = _dot(xb, wup_ref[:, d_ff:])
    if sample:
        a_ref[...] = a
        a_m2, a_m1 = st0_ref[...], st1_ref[...]
    else:
        abuf_ref[SUBLANES:SUBLANES + rows, :] = a
        a_m1 = abuf_ref[SUBLANES - 1:SUBLANES - 1 + rows, :]
        a_m2 = abuf_ref[SUBLANES - 2:SUBLANES - 2 + rows, :]
    conv = cb_ref[...] + cw_ref[0:1, :] * a_m2 + cw_ref[1:2, :] * a_m1 + cw_ref[2:3, :] * a
    h = jax.nn.gelu(conv) * val
    _project_add_norm(h.astype(BF16), wdown_ref, alpha * x1, g_ref, b_ref, y_ref)
    if not sample:
        tail = abuf_ref[rows:rows + SUBLANES, :]
        abuf_ref[0:SUBLANES, :] = tail
        a_ref[...] = tail


def _ffn(x1, x1_s, state, w_up, conv_w, conv_b, w_down, ln2_g, ln2_b, *, alpha):
    s, d = x1.shape
    n_s = x1_s.shape[0]
    d_ff = w_down.shape[0]
    n_tiles = s // FFN_ROWS
    tile = lambda width: pl.BlockSpec((FFN_ROWS, width), lambda i: (jnp.minimum(i, n_tiles - 1), 0))
    whole = lambda shape: pl.BlockSpec(shape, lambda i: (0,) * len(shape))
    operands = (x1, x1_s, state[:, 0], state[:, 1], w_up, conv_w, conv_b, w_down, ln2_g, ln2_b)
    return pl.pallas_call(
        functools.partial(_ffn_kernel, alpha=alpha),
        grid=(n_tiles + 1,),
        in_specs=[tile(d)] + [_resident(t.shape) for t in operands[1:]],
        out_specs=[tile(d), whole((SUBLANES, d_ff)), whole((n_s, d)), whole((n_s, d_ff))],
        out_shape=[jax.ShapeDtypeStruct((s, d), F32), jax.ShapeDtypeStruct((SUBLANES, d_ff), F32),
                   jax.ShapeDtypeStruct((n_s, d), F32), jax.ShapeDtypeStruct((n_s, d_ff), F32)],
        scratch_shapes=[pltpu.VMEM((FFN_ROWS + SUBLANES, d_ff), F32)],
        compiler_params=_params("arbitrary"),
        name="ffn",
    )(*operands)


def kernel(x_prompt, x_sample, mem_prompt, cache_win128_kv, cache_win512_kv, cache_win2048_kv, cache_mem_kv, state_ffn_conv, w_in, b_gate, ln_v_g, ln_v_b, w_spatial, b_spatial, w_mem_kv, w_branch_a, w_branch_b, w_branch_m, w_out, ln1_g, ln1_b, w_up, conv_w, conv_b, w_down, ln2_g, ln2_b):
    depth = w_in.shape[0]
    batch, seq, d_model = x_prompt.shape
    dec_batch, dec_seq, _ = x_sample.shape
    assert batch == 1 and dec_seq == 1 and seq % ATTN_BLOCK == 0
    alpha = (2.0 * depth) ** 0.25
    win_caches = (cache_win128_kv, cache_win512_kv, cache_win2048_kv)
    for cache, (win, _) in zip(win_caches, DILATED_GROUPS):
        assert cache.shape[2] == win and PAST_LEN >= win

    yp = x_prompt.reshape(seq, d_model)
    ys = x_sample.reshape(dec_batch, d_model)
    win_p = [[] for _ in range(N_GROUPS_A)]
    win_s = [[] for _ in range(N_GROUPS_A)]
    mem_p, conv_p, gmlp_s, conv_s = [], [], [], []
    for l in range(depth):
        w_all = w_in[l].astype(BF16)
        row2 = lambda t: t.reshape(1, -1)
        per_lane = lambda t: jnp.repeat(t, GROUP_DIM_B, axis=-1)
        consts_tail = (w_branch_a[l].astype(BF16), w_branch_b[l].astype(BF16), w_branch_m[l].astype(BF16),
                       w_out[l].astype(BF16), row2(ln1_g[l]), row2(ln1_b[l]))
        ffn_consts = (w_up[l].astype(BF16), conv_w[l], row2(conv_b[l]), w_down[l].astype(BF16),
                      row2(ln2_g[l]), row2(ln2_b[l]))

        mkv = _mem_kv_proj(mem_prompt[0], w_mem_kv[l].astype(BF16))
        mem_p.append(mkv.reshape(1, -1, 2, N_HEADS_M, HEAD_DIM_M))
        qkv, *kv_tails = _qkv_proj(yp, w_all)
        for g, (win, _) in enumerate(DILATED_GROUPS):
            tail = kv_tails[g].reshape(2, HEADS_PER_GROUP_A, HEAD_DIM_A, win)
            win_p[g].append(jnp.transpose(tail, (3, 0, 1, 2))[None])
        o_a = _dilated_attn(qkv)
        caches = [jnp.transpose(c[l], (0, 2, 3, 4, 1)) for c in win_caches]
        oa_s, om_s, kv_new = _sample_attn(ys, w_all, caches, cache_mem_kv[l])
        for g in range(N_GROUPS_A):
            win_s[g].append(kv_new[g].reshape(dec_batch, 1, 2, HEADS_PER_GROUP_A, HEAD_DIM_A))

        x1, x1_s, v_rows = _mix(yp, o_a, mkv[:, :WIDTH_M].astype(BF16), mkv[:, WIDTH_M:].astype(BF16), ys, oa_s, om_s,
                                w_all, b_gate[l], row2(ln_v_g[l]), row2(ln_v_b[l]),
                                w_spatial[l].reshape(N_GROUPS_B * CHUNK, CHUNK), per_lane(b_spatial[l].T),
                                row2(per_lane(w_spatial[l][:, 0, 0])), row2(per_lane(b_spatial[l][:, 0])),
                                *consts_tail, alpha=alpha)
        gmlp_s.append(v_rows.reshape(dec_batch, 1, WIDTH_B))
        yp_next, a_tail, ys_next, a_s = _ffn(x1, x1_s, state_ffn_conv[l], *ffn_consts, alpha=alpha)
        conv_p.append(a_tail[SUBLANES - (CONV_W - 1):][None])
        conv_s.append(jnp.stack([state_ffn_conv[l][:, 1], a_s], axis=1))
        yp, ys = yp_next, ys_next

    return (yp.reshape(batch, seq, d_model), ys.reshape(dec_batch, dec_seq, d_model),
            jnp.stack(win_p[0]), jnp.stack(win_p[1]), jnp.stack(win_p[2]),
            jnp.stack(mem_p), jnp.stack(conv_p),
            jnp.stack(win_s[0]), jnp.stack(win_s[1]), jnp.stack(win_s[2]),
            jnp.stack(gmlp_s), jnp.stack(conv_s))
```

```python
import functools

import jax
import jax.numpy as jnp
from jax import lax
from jax.experimental import pallas as pl
from jax.experimental.pallas import tpu as pltpu

BF16 = jnp.bfloat16
F32 = jnp.float32

HEAD_DIM_A = 64
HEADS_PER_GROUP_A = 4
DILATED_GROUPS = ((128, 1), (512, 4), (2048, 16))
N_GROUPS_A = len(DILATED_GROUPS)
GROUP_WIDTH_A = HEADS_PER_GROUP_A * HEAD_DIM_A
WIDTH_A = N_GROUPS_A * GROUP_WIDTH_A
NK = 128
CHUNK = 128
N_GROUPS_B = 4
WIDTH_B = 768
GROUP_DIM_B = WIDTH_B // N_GROUPS_B
N_HEADS_M = 4
HEAD_DIM_M = 128
WIDTH_M = N_HEADS_M * HEAD_DIM_M
N_BRANCH = 3
CONV_W = 3
LN_EPS = 1e-5
NEG = -1e30
PAST_LEN = 16384

LANES = 128
SUBLANES = 8
VMEM_LIMIT_BYTES = 58 * 1024 * 1024

ATTN_BLOCK = 2048
QKV_ROW_PART = 512
DEINTERLEAVE_STRIDE = 4
MIX_ROWS = 512
FFN_ROWS = 512
NORM_ROW_PARTS = 4
SAMPLE_BATCH_BLOCK = 2
SAMPLE_LANE_CHUNK = 2048
SAMPLE_MEM_CHUNK = 32

_NT = (((1,), (1,)), ((), ()))


def _dot(a, b):
    return jnp.dot(a, b, preferred_element_type=F32)


def _dot_nt(a, b):
    return lax.dot_general(a, b, _NT, preferred_element_type=F32)


def _layer_norm(x, g, b):
    mu = jnp.mean(x, axis=-1, keepdims=True)
    var = jnp.mean(jnp.square(x - mu), axis=-1, keepdims=True)
    return (x - mu) * lax.rsqrt(var + LN_EPS) * g + b


def _project_add_norm(lhs, w_ref, residual, g_ref, b_ref, o_ref):
    rows = lhs.shape[0]
    n_parts = NORM_ROW_PARTS if rows % (NORM_ROW_PARTS * 2 * SUBLANES) == 0 else 1
    part = rows // n_parts
    for r0 in range(0, rows, part):
        rs = slice(r0, r0 + part)
        o_ref[rs, :] = _layer_norm(residual[rs] + _dot(lhs[rs], w_ref[...]), g_ref[...], b_ref[...])


def _params(*semantics):
    return pltpu.CompilerParams(dimension_semantics=semantics, vmem_limit_bytes=VMEM_LIMIT_BYTES)


def _resident(shape):
    zeros = (0,) * len(shape)
    return pl.BlockSpec(shape, lambda *_: zeros, pipeline_mode=pl.Buffered(1))


def _mem_kv_kernel(mem_ref, w_ref, o_ref):
    o_ref[...] = _dot(mem_ref[...].astype(BF16), w_ref[...])


def _mem_kv_proj(mem, w):
    n, d = mem.shape
    return pl.pallas_call(
        _mem_kv_kernel,
        out_shape=jax.ShapeDtypeStruct((n, w.shape[1]), F32),
        name="mem_kv_proj",
    )(mem, w)


def _qkv_proj_kernel(xt_ref, xbot_ref, w_ref, o_ref, kvt0_ref, kvt1_ref, kvt2_ref, xb_ref, stage_ref, tmp_ref):
    g_id = pl.program_id(1)

    @pl.when(g_id == 0)
    def _():
        half = ATTN_BLOCK // 2
        xb_ref[0:half, :] = xt_ref[...].astype(BF16)
        xb_ref[half:, :] = xbot_ref[...].astype(BF16)

    xb = xb_ref[...]
    last_block = pl.program_id(0) == pl.num_programs(0) - 1
    for g, (win, dil) in enumerate(DILATED_GROUPS):
        kvt_ref = (kvt0_ref, kvt1_ref, kvt2_ref)[g]

        @pl.when(g_id == g)
        def _(g=g, win=win, dil=dil, kvt_ref=kvt_ref):
            w_cols = lambda c: w_ref[:, c * WIDTH_A + g * GROUP_WIDTH_A:c * WIDTH_A + (g + 1) * GROUP_WIDTH_A]
            for c in range(3):
                w = w_cols(c)
                if c == 0:
                    w = w * jnp.asarray(HEAD_DIM_A ** -0.5, BF16)
                n_slabs = GROUP_WIDTH_A // LANES
                inner = DEINTERLEAVE_STRIDE if dil > DEINTERLEAVE_STRIDE else dil
                outer = dil // inner
                part_rows = win if dil > 1 else QKV_ROW_PART
                for base in range(0, ATTN_BLOCK, part_rows):
                    res = _dot(xb[base:base + part_rows], w)
                    if dil == 1:
                        o_ref[base:base + part_rows, c * GROUP_WIDTH_A:(c + 1) * GROUP_WIDTH_A] = res.astype(BF16)
                        continue
                    for s in range(n_slabs):
                        stage_ref[s, base:base + win, :] = res[:, s * LANES:(s + 1) * LANES]
                    for s in range(n_slabs):
                        col = c * GROUP_WIDTH_A + s * LANES
                        src_ref = stage_ref
                        if outer > 1:
                            sub = win // inner
                            for lo in range(inner):
                                tmp_ref[s, base + lo * sub:base + (lo + 1) * sub, :] = (
                                    stage_ref[s, pl.ds(base + lo, sub, stride=inner), :])
                            src_ref = tmp_ref
                        for lo in range(inner):
                            for hi in range(outer):
                                first = base + lo * (win // inner) + hi if outer > 1 else base + lo
                                rows = src_ref[s, pl.ds(first, NK, stride=outer if outer > 1 else inner), :]
                                dst = base + (hi * inner + lo) * NK
                                o_ref[dst:dst + NK, col:col + LANES] = rows.astype(BF16)

            @pl.when(last_block)
            def _():
                tail = xb[ATTN_BLOCK - win:, :]
                kvt_ref[0:GROUP_WIDTH_A, :] = _dot(tail, w_cols(1)).T
                kvt_ref[GROUP_WIDTH_A:2 * GROUP_WIDTH_A, :] = _dot(tail, w_cols(2)).T


def _qkv_proj(x, w_all):
    s, d = x.shape
    n_blk = s // ATTN_BLOCK
    half = ATTN_BLOCK // 2
    x_half = lambda which: pl.BlockSpec(
        (half, d), lambda n, g: (2 * jnp.minimum(n + jnp.minimum(jnp.maximum(g - which, 0), 1), n_blk - 1) + which, 0))
    kvt_shapes = [(2 * GROUP_WIDTH_A, win) for win, _ in DILATED_GROUPS]
    return pl.pallas_call(
        _qkv_proj_kernel,
        grid=(n_blk, N_GROUPS_A),
        in_specs=[x_half(0), x_half(1),
                  pl.BlockSpec((d, 3 * WIDTH_A), lambda n, g: (0, 0), pipeline_mode=pl.Buffered(1))],
        out_specs=[pl.BlockSpec((None, ATTN_BLOCK, 3 * GROUP_WIDTH_A), lambda n, g: (g, n, 0))]
        + [pl.BlockSpec(shape, lambda n, g: (0, 0)) for shape in kvt_shapes],
        out_shape=[jax.ShapeDtypeStruct((N_GROUPS_A, s, 3 * GROUP_WIDTH_A), BF16)]
        + [jax.ShapeDtypeStruct(shape, F32) for shape in kvt_shapes],
        scratch_shapes=[pltpu.VMEM((ATTN_BLOCK, d), BF16),
                        pltpu.VMEM((GROUP_WIDTH_A // LANES, ATTN_BLOCK, LANES), F32),
                        pltpu.VMEM((GROUP_WIDTH_A // LANES, ATTN_BLOCK, LANES), F32)],
        compiler_params=_params("arbitrary", "arbitrary"),
        name="qkv_proj",
    )(x, x, w_all)


def _attn_units(units, prev_shifts):
    first_head = lax.broadcasted_iota(jnp.int32, (NK, LANES), 1) < HEAD_DIM_A
    row = lax.broadcasted_iota(jnp.int32, (2 * NK, NK), 0) % NK
    col = lax.broadcasted_iota(jnp.int32, (2 * NK, NK), 1)
    cur_ok = col <= row
    prev_ok = {}
    scores = []
    for (q, k_p, k_c, _, _), shift in zip(units, prev_shifts):
        zero = jnp.zeros_like(q)
        q_st = jnp.concatenate([jnp.where(first_head, q, zero), jnp.where(first_head, zero, q)], axis=0)
        if id(shift) not in prev_ok:
            prev_ok[id(shift)] = col >= row + shift
        scores.append((jnp.where(cur_ok, _dot_nt(q_st, k_c), NEG),
                       jnp.where(prev_ok[id(shift)], _dot_nt(q_st, k_p), NEG)))
    maxes = [jnp.max(jnp.maximum(s_c, s_p), axis=-1, keepdims=True) for s_c, s_p in scores]
    probs = [(jnp.exp(s_c - m), jnp.exp(s_p - m)) for (s_c, s_p), m in zip(scores, maxes)]
    dens = [jnp.sum(p_c + p_p, axis=-1, keepdims=True) for p_c, p_p in probs]
    results = []
    for (_, _, _, v_p, v_c), (p_c, p_p), m, den in zip(units, probs, maxes, dens):
        o_st = _dot(p_c.astype(BF16), v_c) + _dot(p_p.astype(BF16), v_p)
        pick = lambda t: jnp.where(first_head, jnp.broadcast_to(t[:NK], (NK, LANES)), jnp.broadcast_to(t[NK:], (NK, LANES)))
        den_sel = pick(den)
        results.append((pick(o_st) / den_sel, pick(m) + jnp.log(den_sel)))
    return results


def _dilated_attn_kernel(qkv0_ref, qkv1_ref, qkv2_ref, o_ref, kv0_ref, kv1_ref, kv2_ref, og_ref, lg_ref):
    n = pl.program_id(0)
    qkv_refs = (qkv0_ref, qkv1_ref, qkv2_ref)
    kv_refs = (kv0_ref, kv1_ref, kv2_ref)
    n_sub = ATTN_BLOCK // NK
    n_hp = GROUP_WIDTH_A // LANES

    @pl.when(n == 0)
    def _():
        for kv_ref in kv_refs:
            kv_ref[...] = jnp.zeros(kv_ref.shape, BF16)

    def sub_block(t, carry, prev_in_block):
        off = pl.multiple_of(t * NK, NK)
        units, shifts, dsts = [], [], []
        for g, (win, dil) in enumerate(DILATED_GROUPS):
            win_blk = t // dil
            res = t - win_blk * dil
            has_prev = (n * (ATTN_BLOCK // win) + win_blk) > 0
            nat = win_blk * win + res
            shift = jnp.where(has_prev, 0, 2 * NK)
            for hp in range(n_hp):
                qs = slice(hp * LANES, (hp + 1) * LANES)
                ks = slice(GROUP_WIDTH_A + hp * LANES, GROUP_WIDTH_A + (hp + 1) * LANES)
                vs = slice(2 * GROUP_WIDTH_A + hp * LANES, 2 * GROUP_WIDTH_A + (hp + 1) * LANES)
                cur = pl.ds(off, NK)
                if prev_in_block[g]:
                    prev = pl.ds(off - win, NK)
                    k_prev, v_prev = qkv_refs[g][prev, ks], qkv_refs[g][prev, vs]
                else:
                    k_prev = kv_refs[g][cur, qs]
                    v_prev = kv_refs[g][cur, GROUP_WIDTH_A + hp * LANES:GROUP_WIDTH_A + (hp + 1) * LANES]
                units.append((qkv_refs[g][cur, qs], k_prev, qkv_refs[g][cur, ks], v_prev, qkv_refs[g][cur, vs]))
                shifts.append(shift)
                dsts.append((g, hp, pl.ds(off, NK) if dil == 1 else pl.ds(nat, NK, stride=dil)))
        for (g, hp, dst), (out, lse) in zip(dsts, _attn_units(units, shifts)):
            og_ref[g, hp, dst, :] = out
            lg_ref[g, hp, dst, :] = lse
        return carry

    bounds = sorted({0, n_sub} | {min(win // NK, n_sub) for win, _ in DILATED_GROUPS})
    for lo, hi in zip(bounds[:-1], bounds[1:]):
        in_block = tuple(lo * NK >= win for win, _ in DILATED_GROUPS)
        lax.fori_loop(lo, hi, functools.partial(sub_block, prev_in_block=in_block), 0)
    for (win, _), qkv_ref, kv_ref in zip(DILATED_GROUPS, qkv_refs, kv_refs):
        kv_ref[...] = qkv_ref[ATTN_BLOCK - win:, GROUP_WIDTH_A:]

    def merge(t, carry):
        off = pl.multiple_of(t * NK, NK)
        for hp in range(GROUP_WIDTH_A // LANES):
            lses = [lg_ref[g, hp, pl.ds(off, NK), :] for g in range(N_GROUPS_A)]
            m = jnp.maximum(jnp.maximum(lses[0], lses[1]), lses[2])
            es = [jnp.exp(l - m) for l in lses]
            num = sum(es[g] * og_ref[g, hp, pl.ds(off, NK), :] for g in range(N_GROUPS_A))
            o_ref[pl.ds(off, NK), hp * LANES:(hp + 1) * LANES] = (num / (es[0] + es[1] + es[2])).astype(BF16)
        return carry

    lax.fori_loop(0, n_sub, merge, 0)


def _dilated_attn(qkv):
    _, s, _ = qkv.shape
    n_blk = s // ATTN_BLOCK
    n_hp = GROUP_WIDTH_A // LANES
    in_specs = [pl.BlockSpec((None, ATTN_BLOCK, 3 * GROUP_WIDTH_A), functools.partial(lambda n, g: (g, n, 0), g=g))
                for g in range(N_GROUPS_A)]
    return pl.pallas_call(
        _dilated_attn_kernel,
        grid=(n_blk,),
        in_specs=in_specs,
        out_specs=pl.BlockSpec((ATTN_BLOCK, GROUP_WIDTH_A), lambda n: (n, 0)),
        out_shape=jax.ShapeDtypeStruct((s, GROUP_WIDTH_A), BF16),
        scratch_shapes=[pltpu.VMEM((win, 2 * GROUP_WIDTH_A), BF16) for win, _ in DILATED_GROUPS]
        + [pltpu.VMEM((N_GROUPS_A, n_hp, ATTN_BLOCK, LANES), F32),
           pltpu.VMEM((N_GROUPS_A, n_hp, ATTN_BLOCK, LANES), F32)],
        compiler_params=_params("arbitrary"),
        name="dilated_attn",
    )(qkv, qkv, qkv)


def _to_heads(row, n_heads, head_dim):
    return jnp.concatenate([row[:, h * head_dim:(h + 1) * head_dim] for h in range(n_heads)], axis=0)


def _from_heads(t):
    return jnp.concatenate([t[h:h + 1] for h in range(t.shape[0])], axis=1)


def _window_cache_attention(c_ref, j, cols_ref, base, dil):
    win = c_ref.shape[-1]
    lc = min(win, SAMPLE_LANE_CHUNK)
    n_chunks = win // lc
    lane = lax.broadcasted_iota(jnp.int32, (1, win), 1)
    wanted = (lane % dil) == 0
    out = []
    for h in range(HEADS_PER_GROUP_A):
        lo = base + h * HEAD_DIM_A
        q, k_new, v_new = (cols_ref[lo + c * GROUP_WIDTH_A:lo + c * GROUP_WIDTH_A + HEAD_DIM_A, :] for c in range(3))
        q_wide = jnp.concatenate([q] * (lc // LANES), axis=1)
        parts = [jnp.sum(c_ref[j, 0, h, :, i * lc:(i + 1) * lc] * q_wide, axis=0, keepdims=True)
                 for i in range(n_chunks)]
        s = jnp.where(wanted, jnp.concatenate(parts, axis=1), NEG)
        s_n = jnp.sum(k_new[:, 0:1] * q[:, 0:1], axis=0, keepdims=True)
        m = jnp.maximum(jnp.max(s, axis=-1, keepdims=True), s_n)
        p = jnp.exp(s - m)
        p_n = jnp.exp(s_n - m)
        den = jnp.sum(p, axis=-1, keepdims=True) + p_n
        acc = p_n * v_new[:, 0:1]
        for i in range(n_chunks):
            acc = acc + jnp.sum(c_ref[j, 1, h, :, i * lc:(i + 1) * lc] * p[:, i * lc:(i + 1) * lc],
                                axis=-1, keepdims=True)
        out.append((acc / den, m + jnp.log(den)))
    return out


def _memory_cache_attention(cm_ref, j, q, s_ref):
    n_heads, head_dim = q.shape
    chunk = SAMPLE_MEM_CHUNK
    n_chunks = cm_ref.shape[1] // chunk
    m = jnp.full((n_heads, head_dim), NEG, F32)
    for c in range(n_chunks):
        pos = slice(c * chunk, (c + 1) * chunk)
        s = jnp.sum(cm_ref[j, pos, 0] * q[None], axis=-1, keepdims=True) * (head_dim ** -0.5)
        s = jnp.broadcast_to(s, (chunk, n_heads, head_dim))
        s_ref[pos] = s
        m = jnp.maximum(m, jnp.max(s, axis=0))
    den = jnp.zeros((n_heads, head_dim), F32)
    acc = jnp.zeros((n_heads, head_dim), F32)
    for c in range(n_chunks):
        pos = slice(c * chunk, (c + 1) * chunk)
        p = jnp.exp(s_ref[pos] - m[None])
        den = den + jnp.sum(p, axis=0)
        acc = acc + jnp.sum(p * cm_ref[j, pos, 1], axis=0)
    return acc / den


def _sample_attn_kernel(x_ref, wqkv_ref, wqm_ref, c0_ref, c1_ref, c2_ref, cm_ref,
                        oa_ref, om_ref, kvn_ref, proj_ref, projt_ref, cols_ref, oat_ref, ms_ref):
    step = pl.program_id(0)
    caches = (c0_ref, c1_ref, c2_ref)
    n_batch, d = x_ref.shape
    qkv_w = N_GROUPS_A * 3 * GROUP_WIDTH_A

    @pl.when(step == 0)
    def _():
        xb = jnp.concatenate([x_ref[...], jnp.zeros((LANES - n_batch, d), F32)], axis=0).astype(BF16)
        for part in range(3):
            h = _dot(xb, wqkv_ref[:, part * WIDTH_A:(part + 1) * WIDTH_A])
            if part == 0:
                h = h * (HEAD_DIM_A ** -0.5)
            for g in range(N_GROUPS_A):
                h_g = h[:, g * GROUP_WIDTH_A:(g + 1) * GROUP_WIDTH_A]
                dst = (3 * g + part) * GROUP_WIDTH_A
                proj_ref[:, dst:dst + GROUP_WIDTH_A] = h_g
                if part > 0:
                    kvn_ref[g, :, (part - 1) * GROUP_WIDTH_A:part * GROUP_WIDTH_A] = h_g[:n_batch]
        proj_ref[:, qkv_w:] = _dot(xb, wqm_ref[:, :WIDTH_M])
        for t in range(qkv_w // LANES):
            projt_ref[t * LANES:(t + 1) * LANES, :] = proj_ref[:, t * LANES:(t + 1) * LANES].T.astype(BF16)
        oat_ref[...] = jnp.zeros(oat_ref.shape, F32)

    def one_row(j, carry):
        b = step * SAMPLE_BATCH_BLOCK + j
        pick_row = (lax.broadcasted_iota(jnp.int32, (LANES, LANES), 0) == b).astype(BF16)
        cols_ref[...] = _dot(projt_ref[...], pick_row)
        per_group = [_window_cache_attention(caches[g], j, cols_ref, g * 3 * GROUP_WIDTH_A, dil)
                     for g, (_, dil) in enumerate(DILATED_GROUPS)]
        this_lane = lax.broadcasted_iota(jnp.int32, (1, LANES), 1) == b
        for h in range(HEADS_PER_GROUP_A):
            outs = [per_group[g][h][0] for g in range(N_GROUPS_A)]
            lses = [per_group[g][h][1] for g in range(N_GROUPS_A)]
            m3 = jnp.maximum(jnp.maximum(lses[0], lses[1]), lses[2])
            es = [jnp.exp(l - m3) for l in lses]
            oa = (es[0] * outs[0] + es[1] * outs[1] + es[2] * outs[2]) / (es[0] + es[1] + es[2])
            rows = slice(h * HEAD_DIM_A, (h + 1) * HEAD_DIM_A)
            oat_ref[rows, :] = jnp.where(this_lane, oa, oat_ref[rows, :])

        qm = _to_heads(proj_ref[pl.ds(b, 1), qkv_w:], N_HEADS_M, HEAD_DIM_M)
        om_ref[pl.ds(b, 1), :] = _from_heads(_memory_cache_attention(cm_ref, j, qm, ms_ref))
        return carry

    lax.fori_loop(0, SAMPLE_BATCH_BLOCK, one_row, 0)

    @pl.when(step == pl.num_programs(0) - 1)
    def _():
        oa_ref[...] = oat_ref[...].T[:n_batch, :]


def _sample_attn(xs, w_all, caches, cache_mem):
    n_batch, d = xs.shape
    assert n_batch <= LANES
    bb = SAMPLE_BATCH_BLOCK
    qkv_w = N_GROUPS_A * 3 * GROUP_WIDTH_A
    qm_col = 3 * WIDTH_A + 2 * WIDTH_B
    assert qm_col % WIDTH_B == 0 and WIDTH_M <= WIDTH_B
    w_specs = [pl.BlockSpec((d, qkv_w), lambda i: (0, 0), pipeline_mode=pl.Buffered(1)),
               pl.BlockSpec((d, WIDTH_B), lambda i: (0, qm_col // WIDTH_B), pipeline_mode=pl.Buffered(1))]
    cache_specs = [pl.BlockSpec((bb,) + c.shape[1:], lambda i: (i, 0, 0, 0, 0)) for c in caches]
    whole = lambda shape: pl.BlockSpec(shape, lambda i: (0,) * len(shape))
    return pl.pallas_call(
        _sample_attn_kernel,
        grid=(n_batch // bb,),
        in_specs=[whole(xs.shape)] + w_specs + cache_specs
        + [pl.BlockSpec((bb,) + cache_mem.shape[1:], lambda i: (i, 0, 0, 0, 0))],
        out_specs=[whole((n_batch, GROUP_WIDTH_A)), whole((n_batch, WIDTH_M)),
                   whole((N_GROUPS_A, n_batch, 2 * GROUP_WIDTH_A))],
        out_shape=[jax.ShapeDtypeStruct((n_batch, GROUP_WIDTH_A), F32),
                   jax.ShapeDtypeStruct((n_batch, WIDTH_M), F32),
                   jax.ShapeDtypeStruct((N_GROUPS_A, n_batch, 2 * GROUP_WIDTH_A), F32)],
        scratch_shapes=[pltpu.VMEM((LANES, qkv_w + WIDTH_M), F32),
                        pltpu.VMEM((qkv_w, LANES), BF16),
                        pltpu.VMEM((qkv_w, LANES), F32),
                        pltpu.VMEM((GROUP_WIDTH_A, LANES), F32),
                        pltpu.VMEM((cache_mem.shape[1], N_HEADS_M, HEAD_DIM_M), F32)],
        compiler_params=_params("arbitrary"),
        name="sample_attn",
    )(xs, w_all, w_all, *caches, cache_mem)


def _mix_kernel(x_ref, oa_ref, mk_ref, mv_ref, xs_ref, oas_ref, oms_ref, w_ref, bgate_ref, lnvg_ref, lnvb_ref,
                ws_ref, bs_ref, wss_ref, bss_ref, wba_ref, wbb_ref, wbm_ref, wout_ref, ln1g_ref, ln1b_ref,
                x1_ref, x1s_ref, vrows_ref, *, alpha):
    shared = (w_ref, bgate_ref, lnvg_ref, lnvb_ref)
    tail = (wba_ref, wbb_ref, wbm_ref, wout_ref, ln1g_ref, ln1b_ref)
    is_sample_step = pl.program_id(0) == pl.num_programs(0) - 1

    @pl.when(jnp.logical_not(is_sample_step))
    def _():
        _mix_body(x_ref, oa_ref, (mk_ref, mv_ref), *shared, ws_ref, bs_ref, *tail, x1_ref, None,
                  sample=False, alpha=alpha)

    @pl.when(is_sample_step)
    def _():
        _mix_body(xs_ref, oas_ref, oms_ref, *shared, wss_ref, bss_ref, *tail, x1s_ref, vrows_ref,
                  sample=True, alpha=alpha)


def _mix_body(x_ref, oa_ref, mem_or_om, w_ref, bgate_ref, lnvg_ref, lnvb_ref, ws_ref, bs_ref,
              wba_ref, wbb_ref, wbm_ref, wout_ref, ln1g_ref, ln1b_ref, x1_ref, vrows_ref, *, sample, alpha):
    if sample:
        om_ref = mem_or_om
    else:
        mk_ref, mv_ref = mem_or_om
    d = x_ref.shape[1]
    rows = x_ref.shape[0]
    x = x_ref[...]
    xb = x.astype(BF16)
    col_u, col_v, col_qm, col_gate = 0, WIDTH_B, 2 * WIDTH_B, 2 * WIDTH_B + WIDTH_M

    h_all = _dot(xb, w_ref[:, 3 * WIDTH_A:])

    def gate(k):
        z = h_all[:, col_gate + k * d:col_gate + (k + 1) * d]
        return 0.5 * jnp.tanh(0.5 * (z + bgate_ref[k:k + 1, :])) + 0.5

    mixed = gate(0) * _dot(oa_ref[...].astype(BF16), wba_ref[...])

    u = jax.nn.gelu(h_all[:, col_u:col_u + WIDTH_B])
    v = _layer_norm(jax.nn.gelu(h_all[:, col_v:col_v + WIDTH_B]), lnvg_ref[...], lnvb_ref[...])
    if sample:
        vrows_ref[...] = v
        spatial = v.astype(BF16).astype(F32) * ws_ref[...].astype(BF16).astype(F32) + bs_ref[...]
    else:
        r_i = lax.broadcasted_iota(jnp.int32, (N_GROUPS_B * CHUNK, CHUNK), 0)
        c_i = lax.broadcasted_iota(jnp.int32, (N_GROUPS_B * CHUNK, CHUNK), 1)
        w_s = jnp.where((r_i % CHUNK) >= c_i, ws_ref[...], 0.0).astype(BF16)
        lane = lax.broadcasted_iota(jnp.int32, (CHUNK, WIDTH_B), 1)
        in_group = [(lane >= gb * GROUP_DIM_B) & (lane < (gb + 1) * GROUP_DIM_B) for gb in range(N_GROUPS_B)]
        vb = v.astype(BF16)
        parts = []
        for ch in range(rows // CHUNK):
            allg = _dot(w_s, vb[ch * CHUNK:(ch + 1) * CHUNK, :])
            sp = bs_ref[...]
            for gb in range(N_GROUPS_B):
                sp = sp + jnp.where(in_group[gb], allg[gb * CHUNK:(gb + 1) * CHUNK, :], 0.0)
            parts.append(sp)
        spatial = jnp.concatenate(parts, axis=0)
    o_b = u * spatial
    mixed = mixed + gate(1) * _dot(o_b.astype(BF16), wbb_ref[...])

    if sample:
        o_m = om_ref[...]
    else:
        qm = h_all[:, col_qm:col_qm + WIDTH_M].astype(BF16)
        head_cols = [slice(h * HEAD_DIM_M, (h + 1) * HEAD_DIM_M) for h in range(N_HEADS_M)]
        scores = [_dot_nt(qm[:, hs], mk_ref[:, hs]) * (HEAD_DIM_M ** -0.5) for hs in head_cols]
        maxes = [jnp.max(s, axis=-1, keepdims=True) for s in scores]
        probs = [jnp.exp(s - m) for s, m in zip(scores, maxes)]
        dens = [jnp.sum(p, axis=-1, keepdims=True) for p in probs]
        o_m = jnp.concatenate([_dot(p.astype(BF16), mv_ref[:, hs]) / den
                               for p, den, hs in zip(probs, dens, head_cols)], axis=-1)
    mixed = mixed + gate(2) * _dot(o_m.astype(BF16), wbm_ref[...])

    _project_add_norm(mixed.astype(BF16), wout_ref, alpha * x, ln1g_ref, ln1b_ref, x1_ref)


def _mix(x, o_a, mk, mv, xs, oa_s, om_s, w_all, b_gate, ln_v_g, ln_v_b, w_s, b_s, w_s_row, b_s_row,
         w_ba, w_bb, w_bm, w_out, ln1_g, ln1_b, *, alpha):
    s, d = x.shape
    n_tiles = s // MIX_ROWS
    tile = lambda width: pl.BlockSpec((MIX_ROWS, width), lambda i: (jnp.minimum(i, n_tiles - 1), 0))
    whole = lambda shape: pl.BlockSpec(shape, lambda i: (0,) * len(shape))
    operands = (x, o_a, mk, mv, xs, oa_s, om_s, w_all, b_gate, ln_v_g, ln_v_b, w_s, b_s, w_s_row, b_s_row,
                w_ba, w_bb, w_bm, w_out, ln1_g, ln1_b)
    n_s = xs.shape[0]
    return pl.pallas_call(
        functools.partial(_mix_kernel, alpha=alpha),
        grid=(n_tiles + 1,),
        in_specs=[tile(d), tile(GROUP_WIDTH_A)] + [_resident(t.shape) for t in operands[2:]],
        out_specs=[tile(d), whole((n_s, d)), whole((n_s, WIDTH_B))],
        out_shape=[jax.ShapeDtypeStruct((s, d), F32), jax.ShapeDtypeStruct((n_s, d), F32),
                   jax.ShapeDtypeStruct((n_s, WIDTH_B), F32)],
        compiler_params=_params("arbitrary"),
        name="mix",
    )(*operands)


def _ffn_kernel(x1_ref, x1s_ref, st0_ref, st1_ref, wup_ref, cw_ref, cb_ref, wdown_ref, g_ref, b_ref,
                y_ref, a_ref, ys_ref, as_ref, abuf_ref, *, alpha):
    consts = (wup_ref, cw_ref, cb_ref, wdown_ref, g_ref, b_ref)
    is_sample_step = pl.program_id(0) == pl.num_programs(0) - 1

    @pl.when(jnp.logical_not(is_sample_step))
    def _():
        _ffn_body(x1_ref, None, None, *consts, y_ref, a_ref, abuf_ref, sample=False, alpha=alpha)

    @pl.when(is_sample_step)
    def _():
        _ffn_body(x1s_ref, st0_ref, st1_ref, *consts, ys_ref, as_ref, None, sample=True, alpha=alpha)


def _ffn_body(x1_ref, st0_ref, st1_ref, wup_ref, cw_ref, cb_ref, wdown_ref, g_ref, b_ref, y_ref, a_ref, abuf_ref,
              *, sample, alpha):
    rows = x1_ref.shape[0]
    d_ff = wdown_ref.shape[0]
    if not sample:
        @pl.when(pl.program_id(0) == 0)
        def _():
            abuf_ref[0:SUBLANES, :] = jnp.zeros((SUBLANES, d_ff), F32)

    x1 = x1_ref[...]
    xb = x1.astype(BF16)
    a = _dot(xb, wup_ref[:, :d_ff])
    val = _dot(xb, wup_ref[:, d_ff:])
    if sample:
        a_ref[...] = a
        a_m2, a_m1 = st0_ref[...], st1_ref[...]
    else:
        abuf_ref[SUBLANES:SUBLANES + rows, :] = a
        a_m1 = abuf_ref[SUBLANES - 1:SUBLANES - 1 + rows, :]
        a_m2 = abuf_ref[SUBLANES - 2:SUBLANES - 2 + rows, :]
    conv = cb_ref[...] + cw_ref[0:1, :] * a_m2 + cw_ref[1:2, :] * a_m1 + cw_ref[2:3, :] * a
    h = jax.nn.gelu(conv) * val
    _project_add_norm(h.astype(BF16), wdown_ref, alpha * x1, g_ref, b_ref, y_ref)
    if not sample:
        tail = abuf_ref[rows:rows + SUBLANES, :]
        abuf_ref[0:SUBLANES, :] = tail
        a_ref[...] = tail


def _ffn(x1, x1_s, state, w_up, conv_w, conv_b, w_down, ln2_g, ln2_b, *, alpha):
    s, d = x1.shape
    n_s = x1_s.shape[0]
    d_ff = w_down.shape[0]
    n_tiles = s // FFN_ROWS
    tile = lambda width: pl.BlockSpec((FFN_ROWS, width), lambda i: (jnp.minimum(i, n_tiles - 1), 0))
    whole = lambda shape: pl.BlockSpec(shape, lambda i: (0,) * len(shape))
    operands = (x1, x1_s, state[:, 0], state[:, 1], w_up, conv_w, conv_b, w_down, ln2_g, ln2_b)
    return pl.pallas_call(
        functools.partial(_ffn_kernel, alpha=alpha),
        grid=(n_tiles + 1,),
        in_specs=[tile(d)] + [_resident(t.shape) for t in operands[1:]],
        out_specs=[tile(d), whole((SUBLANES, d_ff)), whole((n_s, d)), whole((n_s, d_ff))],
        out_shape=[jax.ShapeDtypeStruct((s, d), F32), jax.ShapeDtypeStruct((SUBLANES, d_ff), F32),
                   jax.ShapeDtypeStruct((n_s, d), F32), jax.ShapeDtypeStruct((n_s, d_ff), F32)],
        scratch_shapes=[pltpu.VMEM((FFN_ROWS + SUBLANES, d_ff), F32)],
        compiler_params=_params("arbitrary"),
        name="ffn",
    )(*operands)


def kernel(x_prompt, x_sample, mem_prompt, cache_win128_kv, cache_win512_kv, cache_win2048_kv, cache_mem_kv, state_ffn_conv, w_in, b_gate, ln_v_g, ln_v_b, w_spatial, b_spatial, w_mem_kv, w_branch_a, w_branch_b, w_branch_m, w_out, ln1_g, ln1_b, w_up, conv_w, conv_b, w_down, ln2_g, ln2_b):
    depth = w_in.shape[0]
    batch, seq, d_model = x_prompt.shape
    dec_batch, dec_seq, _ = x_sample.shape
    assert batch == 1 and dec_seq == 1 and seq % ATTN_BLOCK == 0
    alpha = (2.0 * depth) ** 0.25
    win_caches = (cache_win128_kv, cache_win512_kv, cache_win2048_kv)
    for cache, (win, _) in zip(win_caches, DILATED_GROUPS):
        assert cache.shape[2] == win and PAST_LEN >= win

    yp = x_prompt.reshape(seq, d_model)
    ys = x_sample.reshape(dec_batch, d_model)
    win_p = [[] for _ in range(N_GROUPS_A)]
    win_s = [[] for _ in range(N_GROUPS_A)]
    mem_p, conv_p, gmlp_s, conv_s = [], [], [], []
    for l in range(depth):
        w_all = w_in[l].astype(BF16)
        row2 = lambda t: t.reshape(1, -1)
        per_lane = lambda t: jnp.repeat(t, GROUP_DIM_B, axis=-1)
        consts_tail = (w_branch_a[l].astype(BF16), w_branch_b[l].astype(BF16), w_branch_m[l].astype(BF16),
                       w_out[l].astype(BF16), row2(ln1_g[l]), row2(ln1_b[l]))
        ffn_consts = (w_up[l].astype(BF16), conv_w[l], row2(conv_b[l]), w_down[l].astype(BF16),
                      row2(ln2_g[l]), row2(ln2_b[l]))

        mkv = _mem_kv_proj(mem_prompt[0], w_mem_kv[l].astype(BF16))
        mem_p.append(mkv.reshape(1, -1, 2, N_HEADS_M, HEAD_DIM_M))
        qkv, *kv_tails = _qkv_proj(yp, w_all)
        for g, (win, _) in enumerate(DILATED_GROUPS):
            tail = kv_tails[g].reshape(2, HEADS_PER_GROUP_A, HEAD_DIM_A, win)
            win_p[g].append(jnp.transpose(tail, (3, 0, 1, 2))[None])
        o_a = _dilated_attn(qkv)
        caches = [jnp.transpose(c[l], (0, 2, 3, 4, 1)) for c in win_caches]
        oa_s, om_s, kv_new = _sample_attn(ys, w_all, caches, cache_mem_kv[l])
        for g in range(N_GROUPS_A):
            win_s[g].append(kv_new[g].reshape(dec_batch, 1, 2, HEADS_PER_GROUP_A, HEAD_DIM_A))

        x1, x1_s, v_rows = _mix(yp, o_a, mkv[:, :WIDTH_M].astype(BF16), mkv[:, WIDTH_M:].astype(BF16), ys, oa_s, om_s,
                                w_all, b_gate[l], row2(ln_v_g[l]), row2(ln_v_b[l]),
                                w_spatial[l].reshape(N_GROUPS_B * CHUNK, CHUNK), per_lane(b_spatial[l].T),
                                row2(per_lane(w_spatial[l][:, 0, 0])), row2(per_lane(b_spatial[l][:, 0])),
                                *consts_tail, alpha=alpha)
        gmlp_s.append(v_rows.reshape(dec_batch, 1, WIDTH_B))
        yp_next, a_tail, ys_next, a_s = _ffn(x1, x1_s, state_ffn_conv[l], *ffn_consts, alpha=alpha)
        conv_p.append(a_tail[SUBLANES - (CONV_W - 1):][None])
        conv_s.append(jnp.stack([state_ffn_conv[l][:, 1], a_s], axis=1))
        yp, ys = yp_next, ys_next

    return (yp.reshape(batch, seq, d_model), ys.reshape(dec_batch, dec_seq, d_model),
            jnp.stack(win_p[0]), jnp.stack(win_p[1]), jnp.stack(win_p[2]),
            jnp.stack(mem_p), jnp.stack(conv_p),
            jnp.stack(win_s[0]), jnp.stack(win_s[1]), jnp.stack(win_s[2]),
            jnp.stack(gmlp_s), jnp.stack(conv_s))
```

```python
import functools

import jax
import jax.numpy as jnp
from jax import lax
from jax.experimental import pallas as pl
from jax.experimental.pallas import tpu as pltpu

BF16 = jnp.bfloat16
F32 = jnp.float32

HEAD_DIM_A = 64
HEADS_PER_GROUP_A = 4
DILATED_GROUPS = ((128, 1), (512, 4), (2048, 16))
N_GROUPS_A = len(DILATED_GROUPS)
GROUP_WIDTH_A = HEADS_PER_GROUP_A * HEAD_DIM_A
WIDTH_A = N_GROUPS_A * GROUP_WIDTH_A
NK = 128
CHUNK = 128
N_GROUPS_B = 4
WIDTH_B = 768
GROUP_DIM_B = WIDTH_B // N_GROUPS_B
N_HEADS_M = 4
HEAD_DIM_M = 128
WIDTH_M = N_HEADS_M * HEAD_DIM_M
N_BRANCH = 3
CONV_W = 3
LN_EPS = 1e-5
NEG = -1e30
PAST_LEN = 16384

LANES = 128
SUBLANES = 8
VMEM_LIMIT_BYTES = 58 * 1024 * 1024

ATTN_BLOCK = 2048
QKV_ROW_PART = 512
DEINTERLEAVE_STRIDE = 4
MIX_ROWS = 512
FFN_ROWS = 512
NORM_ROW_PARTS = 4
SAMPLE_BATCH_BLOCK = 2
SAMPLE_LANE_CHUNK = 512
SAMPLE_MEM_CHUNK = 32

_NT = (((1,), (1,)), ((), ()))


def _dot(a, b):
    return jnp.dot(a, b, preferred_element_type=F32)


def _dot_nt(a, b):
    return lax.dot_general(a, b, _NT, preferred_element_type=F32)


def _layer_norm(x, g, b):
    mu = jnp.mean(x, axis=-1, keepdims=True)
    var = jnp.mean(jnp.square(x - mu), axis=-1, keepdims=True)
    return (x - mu) * lax.rsqrt(var + LN_EPS) * g + b


def _project_add_norm(lhs, w_ref, residual, g_ref, b_ref, o_ref):
    rows = lhs.shape[0]
    n_parts = NORM_ROW_PARTS if rows % (NORM_ROW_PARTS * 2 * SUBLANES) == 0 else 1
    part = rows // n_parts
    for r0 in range(0, rows, part):
        rs = slice(r0, r0 + part)
        o_ref[rs, :] = _layer_norm(residual[rs] + _dot(lhs[rs], w_ref[...]), g_ref[...], b_ref[...])


def _params(*semantics):
    return pltpu.CompilerParams(dimension_semantics=semantics, vmem_limit_bytes=VMEM_LIMIT_BYTES)


def _resident(shape):
    zeros = (0,) * len(shape)
    return pl.BlockSpec(shape, lambda *_: zeros, pipeline_mode=pl.Buffered(1))


def _mem_kv_kernel(mem_ref, w_ref, o_ref):
    o_ref[...] = _dot(mem_ref[...].astype(BF16), w_ref[...])


def _mem_kv_proj(mem, w):
    n, d = mem.shape
    return pl.pallas_call(
        _mem_kv_kernel,
        out_shape=jax.ShapeDtypeStruct((n, w.shape[1]), F32),
        name="mem_kv_proj",
    )(mem, w)


def _qkv_proj_kernel(xt_ref, xbot_ref, w_ref, o_ref, kt_ref, kvt0_ref, kvt1_ref, kvt2_ref, xb_ref, stage_ref, tmp_ref):
    g_id = pl.program_id(1)

    @pl.when(g_id == 0)
    def _():
        half = ATTN_BLOCK // 2
        xb_ref[0:half, :] = xt_ref[...].astype(BF16)
        xb_ref[half:, :] = xbot_ref[...].astype(BF16)

    xb = xb_ref[...]
    last_block = pl.program_id(0) == pl.num_programs(0) - 1
    for g, (win, dil) in enumerate(DILATED_GROUPS):
        kvt_ref = (kvt0_ref, kvt1_ref, kvt2_ref)[g]

        @pl.when(g_id == g)
        def _(g=g, win=win, dil=dil, kvt_ref=kvt_ref):
            w_cols = lambda c: w_ref[:, c * WIDTH_A + g * GROUP_WIDTH_A:c * WIDTH_A + (g + 1) * GROUP_WIDTH_A]
            for c in range(3):
                w = w_cols(c)
                if c == 0:
                    w = w * jnp.asarray(HEAD_DIM_A ** -0.5, BF16)
                n_slabs = GROUP_WIDTH_A // LANES
                inner = DEINTERLEAVE_STRIDE if dil > DEINTERLEAVE_STRIDE else dil
                outer = dil // inner
                part_rows = win if dil > 1 else QKV_ROW_PART
                for base in range(0, ATTN_BLOCK, part_rows):
                    res = _dot(xb[base:base + part_rows], w)
                    if dil == 1:
                        o_ref[base:base + part_rows, c * GROUP_WIDTH_A:(c + 1) * GROUP_WIDTH_A] = res.astype(BF16)
                        if c == 1:
                            for r0 in range(0, part_rows, NK):
                                kt_ref[(base + r0) // NK] = res[r0:r0 + NK, :].T.astype(BF16)
                        continue
                    for s in range(n_slabs):
                        stage_ref[s, base:base + win, :] = res[:, s * LANES:(s + 1) * LANES]
                    for s in range(n_slabs):
                        col = c * GROUP_WIDTH_A + s * LANES
                        src_ref = stage_ref
                        if outer > 1:
                            sub = win // inner
                            for lo in range(inner):
                                tmp_ref[s, base + lo * sub:base + (lo + 1) * sub, :] = (
                                    stage_ref[s, pl.ds(base + lo, sub, stride=inner), :])
                            src_ref = tmp_ref
                        for lo in range(inner):
                            for hi in range(outer):
                                first = base + lo * (win // inner) + hi if outer > 1 else base + lo
                                rows = src_ref[s, pl.ds(first, NK, stride=outer if outer > 1 else inner), :]
                                dst = base + (hi * inner + lo) * NK
                                o_ref[dst:dst + NK, col:col + LANES] = rows.astype(BF16)
                                if c == 1:
                                    kt_ref[dst // NK, s * LANES:(s + 1) * LANES, :] = rows.T.astype(BF16)

            @pl.when(last_block)
            def _():
                tail = xb[ATTN_BLOCK - win:, :]
                kvt_ref[0:GROUP_WIDTH_A, :] = _dot(tail, w_cols(1)).T
                kvt_ref[GROUP_WIDTH_A:2 * GROUP_WIDTH_A, :] = _dot(tail, w_cols(2)).T


def _qkv_proj(x, w_all):
    s, d = x.shape
    n_blk = s // ATTN_BLOCK
    half = ATTN_BLOCK // 2
    x_half = lambda which: pl.BlockSpec(
        (half, d), lambda n, g: (2 * jnp.minimum(n + jnp.minimum(jnp.maximum(g - which, 0), 1), n_blk - 1) + which, 0))
    kvt_shapes = [(2 * GROUP_WIDTH_A, win) for win, _ in DILATED_GROUPS]
    return pl.pallas_call(
        _qkv_proj_kernel,
        grid=(n_blk, N_GROUPS_A),
        in_specs=[x_half(0), x_half(1),
                  pl.BlockSpec((d, 3 * WIDTH_A), lambda n, g: (0, 0), pipeline_mode=pl.Buffered(1))],
        out_specs=[pl.BlockSpec((None, ATTN_BLOCK, 3 * GROUP_WIDTH_A), lambda n, g: (g, n, 0)),
                   pl.BlockSpec((None, ATTN_BLOCK // NK, GROUP_WIDTH_A, NK), lambda n, g: (g, n, 0, 0))]
        + [pl.BlockSpec(shape, lambda n, g: (0, 0)) for shape in kvt_shapes],
        out_shape=[jax.ShapeDtypeStruct((N_GROUPS_A, s, 3 * GROUP_WIDTH_A), BF16),
                   jax.ShapeDtypeStruct((N_GROUPS_A, s // NK, GROUP_WIDTH_A, NK), BF16)]
        + [jax.ShapeDtypeStruct(shape, F32) for shape in kvt_shapes],
        scratch_shapes=[pltpu.VMEM((ATTN_BLOCK, d), BF16),
                        pltpu.VMEM((GROUP_WIDTH_A // LANES, ATTN_BLOCK, LANES), F32),
                        pltpu.VMEM((GROUP_WIDTH_A // LANES, ATTN_BLOCK, LANES), F32)],
        compiler_params=_params("arbitrary", "arbitrary"),
        name="qkv_proj",
    )(x, x, w_all)


def _attn_units(units, prev_shifts):
    first_head = lax.broadcasted_iota(jnp.int32, (NK, LANES), 1) < HEAD_DIM_A
    row = lax.broadcasted_iota(jnp.int32, (2 * NK, NK), 0) % NK
    col = lax.broadcasted_iota(jnp.int32, (2 * NK, NK), 1)
    cur_ok = col <= row
    prev_ok = {}
    scores = []
    for (q, k_p, k_c, _, _), shift in zip(units, prev_shifts):
        zero = jnp.zeros_like(q)
        q_st = jnp.concatenate([jnp.where(first_head, q, zero), jnp.where(first_head, zero, q)], axis=0)
        if id(shift) not in prev_ok:
            prev_ok[id(shift)] = col >= row + shift
        scores.append((jnp.where(cur_ok, _dot(q_st, k_c), NEG),
                       jnp.where(prev_ok[id(shift)], _dot(q_st, k_p), NEG)))
    maxes = [jnp.max(jnp.maximum(s_c, s_p), axis=-1, keepdims=True) for s_c, s_p in scores]
    probs = [(jnp.exp(s_c - m), jnp.exp(s_p - m)) for (s_c, s_p), m in zip(scores, maxes)]
    dens = [jnp.sum(p_c + p_p, axis=-1, keepdims=True) for p_c, p_p in probs]
    results = []
    for (_, _, _, v_p, v_c), (p_c, p_p), m, den in zip(units, probs, maxes, dens):
        o_st = _dot(p_c.astype(BF16), v_c) + _dot(p_p.astype(BF16), v_p)
        pick = lambda t: jnp.where(first_head, jnp.broadcast_to(t[:NK], (NK, LANES)), jnp.broadcast_to(t[NK:], (NK, LANES)))
        den_sel = pick(den)
        results.append((pick(o_st) / den_sel, pick(m) + jnp.log(den_sel)))
    return results


def _dilated_attn_kernel(qkv0_ref, qkv1_ref, qkv2_ref, kt0_ref, kt1_ref, kt2_ref, o_ref,
                         kv0_ref, kv1_ref, kv2_ref, ktc0_ref, ktc1_ref, ktc2_ref, og_ref, lg_ref):
    n = pl.program_id(0)
    qkv_refs = (qkv0_ref, qkv1_ref, qkv2_ref)
    kt_refs = (kt0_ref, kt1_ref, kt2_ref)
    kv_refs = (kv0_ref, kv1_ref, kv2_ref)
    ktc_refs = (ktc0_ref, ktc1_ref, ktc2_ref)
    n_sub = ATTN_BLOCK // NK
    n_hp = GROUP_WIDTH_A // LANES

    @pl.when(n == 0)
    def _():
        for kv_ref in kv_refs + ktc_refs:
            kv_ref[...] = jnp.zeros(kv_ref.shape, BF16)

    def sub_block(t, carry, prev_in_block):
        off = pl.multiple_of(t * NK, NK)
        units, shifts, dsts = [], [], []
        for g, (win, dil) in enumerate(DILATED_GROUPS):
            win_blk = t // dil
            res = t - win_blk * dil
            has_prev = (n * (ATTN_BLOCK // win) + win_blk) > 0
            nat = win_blk * win + res
            shift = jnp.where(has_prev, 0, 2 * NK)
            for hp in range(n_hp):
                qs = slice(hp * LANES, (hp + 1) * LANES)
                ks = slice(GROUP_WIDTH_A + hp * LANES, GROUP_WIDTH_A + (hp + 1) * LANES)
                vs = slice(2 * GROUP_WIDTH_A + hp * LANES, 2 * GROUP_WIDTH_A + (hp + 1) * LANES)
                cur = pl.ds(off, NK)
                if prev_in_block[g]:
                    prev = pl.ds(off - win, NK)
                    k_prev, v_prev = kt_refs[g][t - win // NK, qs, :], qkv_refs[g][prev, vs]
                else:
                    k_prev = ktc_refs[g][t, qs, :]
                    v_prev = kv_refs[g][cur, GROUP_WIDTH_A + hp * LANES:GROUP_WIDTH_A + (hp + 1) * LANES]
                units.append((qkv_refs[g][cur, qs], k_prev, kt_refs[g][t, qs, :], v_prev, qkv_refs[g][cur, vs]))
                shifts.append(shift)
                dsts.append((g, hp, pl.ds(off, NK) if dil == 1 else pl.ds(nat, NK, stride=dil)))
        for (g, hp, dst), (out, lse) in zip(dsts, _attn_units(units, shifts)):
            og_ref[g, hp, dst, :] = out
            lg_ref[g, hp, dst, :] = lse
        return carry

    bounds = sorted({0, n_sub} | {min(win // NK, n_sub) for win, _ in DILATED_GROUPS})
    for lo, hi in zip(bounds[:-1], bounds[1:]):
        in_block = tuple(lo * NK >= win for win, _ in DILATED_GROUPS)
        lax.fori_loop(lo, hi, functools.partial(sub_block, prev_in_block=in_block), 0)
    for (win, _), qkv_ref, kv_ref in zip(DILATED_GROUPS, qkv_refs, kv_refs):
        kv_ref[...] = qkv_ref[ATTN_BLOCK - win:, GROUP_WIDTH_A:]
    for (win, _), kt_ref, ktc_ref in zip(DILATED_GROUPS, kt_refs, ktc_refs):
        ktc_ref[...] = kt_ref[n_sub - win // NK:, :, :]

    def merge(t, carry):
        off = pl.multiple_of(t * NK, NK)
        for hp in range(GROUP_WIDTH_A // LANES):
            lses = [lg_ref[g, hp, pl.ds(off, NK), :] for g in range(N_GROUPS_A)]
            m = jnp.maximum(jnp.maximum(lses[0], lses[1]), lses[2])
            es = [jnp.exp(l - m) for l in lses]
            num = sum(es[g] * og_ref[g, hp, pl.ds(off, NK), :] for g in range(N_GROUPS_A))
            o_ref[pl.ds(off, NK), hp * LANES:(hp + 1) * LANES] = (num / (es[0] + es[1] + es[2])).astype(BF16)
        return carry

    lax.fori_loop(0, n_sub, merge, 0)


def _dilated_attn(qkv, kt):
    _, s, _ = qkv.shape
    n_blk = s // ATTN_BLOCK
    n_hp = GROUP_WIDTH_A // LANES
    n_sub = ATTN_BLOCK // NK
    in_specs = [pl.BlockSpec((None, ATTN_BLOCK, 3 * GROUP_WIDTH_A), functools.partial(lambda n, g: (g, n, 0), g=g))
                for g in range(N_GROUPS_A)]
    in_specs += [pl.BlockSpec((None, n_sub, GROUP_WIDTH_A, NK), functools.partial(lambda n, g: (g, n, 0, 0), g=g))
                 for g in range(N_GROUPS_A)]
    return pl.pallas_call(
        _dilated_attn_kernel,
        grid=(n_blk,),
        in_specs=in_specs,
        out_specs=pl.BlockSpec((ATTN_BLOCK, GROUP_WIDTH_A), lambda n: (n, 0)),
        out_shape=jax.ShapeDtypeStruct((s, GROUP_WIDTH_A), BF16),
        scratch_shapes=[pltpu.VMEM((win, 2 * GROUP_WIDTH_A), BF16) for win, _ in DILATED_GROUPS]
        + [pltpu.VMEM((win // NK, GROUP_WIDTH_A, NK), BF16) for win, _ in DILATED_GROUPS]
        + [pltpu.VMEM((N_GROUPS_A, n_hp, ATTN_BLOCK, LANES), F32),
           pltpu.VMEM((N_GROUPS_A, n_hp, ATTN_BLOCK, LANES), F32)],
        compiler_params=_params("arbitrary"),
        name="dilated_attn",
    )(qkv, qkv, qkv, kt, kt, kt)


def _to_heads(row, n_heads, head_dim):
    return jnp.concatenate([row[:, h * head_dim:(h + 1) * head_dim] for h in range(n_heads)], axis=0)


def _from_heads(t):
    return jnp.concatenate([t[h:h + 1] for h in range(t.shape[0])], axis=1)


def _window_cache_attention(c_ref, j, cols_ref, base, dil):
    win = c_ref.shape[-1]
    lc = min(win, SAMPLE_LANE_CHUNK)
    n_chunks = win // lc
    lane = lax.broadcasted_iota(jnp.int32, (1, win), 1)
    wanted = (lane % dil) == 0
    out = []
    for h in range(HEADS_PER_GROUP_A):
        lo = base + h * HEAD_DIM_A
        q, k_new, v_new = (cols_ref[lo + c * GROUP_WIDTH_A:lo + c * GROUP_WIDTH_A + HEAD_DIM_A, :] for c in range(3))
        q_wide = jnp.concatenate([q] * (lc // LANES), axis=1)
        parts = [jnp.sum(c_ref[j, 0, h, :, i * lc:(i + 1) * lc] * q_wide, axis=0, keepdims=True)
                 for i in range(n_chunks)]
        s = jnp.where(wanted, jnp.concatenate(parts, axis=1), NEG)
        s_n = jnp.sum(k_new[:, 0:1] * q[:, 0:1], axis=0, keepdims=True)
        m = jnp.maximum(jnp.max(s, axis=-1, keepdims=True), s_n)
        p = jnp.exp(s - m)
        p_n = jnp.exp(s_n - m)
        den = jnp.sum(p, axis=-1, keepdims=True) + p_n
        acc = p_n * v_new[:, 0:1]
        for i in range(n_chunks):
            acc = acc + jnp.sum(c_ref[j, 1, h, :, i * lc:(i + 1) * lc] * p[:, i * lc:(i + 1) * lc],
                                axis=-1, keepdims=True)
        out.append((acc / den, m + jnp.log(den)))
    return out


def _memory_cache_attention(cm_ref, j, q, s_ref):
    n_heads, head_dim = q.shape
    chunk = SAMPLE_MEM_CHUNK
    n_chunks = cm_ref.shape[1] // chunk
    m = jnp.full((n_heads, head_dim), NEG, F32)
    for c in range(n_chunks):
        pos = slice(c * chunk, (c + 1) * chunk)
        s = jnp.sum(cm_ref[j, pos, 0] * q[None], axis=-1, keepdims=True) * (head_dim ** -0.5)
        s = jnp.broadcast_to(s, (chunk, n_heads, head_dim))
        s_ref[pos] = s
        m = jnp.maximum(m, jnp.max(s, axis=0))
    den = jnp.zeros((n_heads, head_dim), F32)
    acc = jnp.zeros((n_heads, head_dim), F32)
    for c in range(n_chunks):
        pos = slice(c * chunk, (c + 1) * chunk)
        p = jnp.exp(s_ref[pos] - m[None])
        den = den + jnp.sum(p, axis=0)
        acc = acc + jnp.sum(p * cm_ref[j, pos, 1], axis=0)
    return acc / den


def _sample_attn_kernel(x_ref, wqkv_ref, wqm_ref, c0_ref, c1_ref, c2_ref, cm_ref,
                        oa_ref, om_ref, kvn_ref, proj_ref, projt_ref, cols_ref, oat_ref, ms_ref):
    step = pl.program_id(0)
    caches = (c0_ref, c1_ref, c2_ref)
    n_batch, d = x_ref.shape
    qkv_w = N_GROUPS_A * 3 * GROUP_WIDTH_A

    @pl.when(step == 0)
    def _():
        xb = jnp.concatenate([x_ref[...], jnp.zeros((LANES - n_batch, d), F32)], axis=0).astype(BF16)
        for part in range(3):
            h = _dot(xb, wqkv_ref[:, part * WIDTH_A:(part + 1) * WIDTH_A])
            if part == 0:
                h = h * (HEAD_DIM_A ** -0.5)
            for g in range(N_GROUPS_A):
                h_g = h[:, g * GROUP_WIDTH_A:(g + 1) * GROUP_WIDTH_A]
                dst = (3 * g + part) * GROUP_WIDTH_A
                proj_ref[:, dst:dst + GROUP_WIDTH_A] = h_g
                if part > 0:
                    kvn_ref[g, :, (part - 1) * GROUP_WIDTH_A:part * GROUP_WIDTH_A] = h_g[:n_batch]
        proj_ref[:, qkv_w:] = _dot(xb, wqm_ref[:, :WIDTH_M])
        for t in range(qkv_w // LANES):
            projt_ref[t * LANES:(t + 1) * LANES, :] = proj_ref[:, t * LANES:(t + 1) * LANES].T.astype(BF16)
        oat_ref[...] = jnp.zeros(oat_ref.shape, F32)

    def one_row(j, carry):
        b = step * SAMPLE_BATCH_BLOCK + j
        pick_row = (lax.broadcasted_iota(jnp.int32, (LANES, LANES), 0) == b).astype(BF16)
        cols_ref[...] = _dot(projt_ref[...], pick_row)
        per_group = [_window_cache_attention(caches[g], j, cols_ref, g * 3 * GROUP_WIDTH_A, dil)
                     for g, (_, dil) in enumerate(DILATED_GROUPS)]
        this_lane = lax.broadcasted_iota(jnp.int32, (1, LANES), 1) == b
        for h in range(HEADS_PER_GROUP_A):
            outs = [per_group[g][h][0] for g in range(N_GROUPS_A)]
            lses = [per_group[g][h][1] for g in range(N_GROUPS_A)]
            m3 = jnp.maximum(jnp.maximum(lses[0], lses[1]), lses[2])
            es = [jnp.exp(l - m3) for l in lses]
            oa = (es[0] * outs[0] + es[1] * outs[1] + es[2] * outs[2]) / (es[0] + es[1] + es[2])
            rows = slice(h * HEAD_DIM_A, (h + 1) * HEAD_DIM_A)
            oat_ref[rows, :] = jnp.where(this_lane, oa, oat_ref[rows, :])

        qm = _to_heads(proj_ref[pl.ds(b, 1), qkv_w:], N_HEADS_M, HEAD_DIM_M)
        om_ref[pl.ds(b, 1), :] = _from_heads(_memory_cache_attention(cm_ref, j, qm, ms_ref))
        return carry

    lax.fori_loop(0, SAMPLE_BATCH_BLOCK, one_row, 0)

    @pl.when(step == pl.num_programs(0) - 1)
    def _():
        oa_ref[...] = oat_ref[...].T[:n_batch, :]


def _sample_attn(xs, w_all, caches, cache_mem):
    n_batch, d = xs.shape
    assert n_batch <= LANES
    bb = SAMPLE_BATCH_BLOCK
    qkv_w = N_GROUPS_A * 3 * GROUP_WIDTH_A
    qm_col = 3 * WIDTH_A + 2 * WIDTH_B
    assert qm_col % WIDTH_B == 0 and WIDTH_M <= WIDTH_B
    w_specs = [pl.BlockSpec((d, qkv_w), lambda i: (0, 0), pipeline_mode=pl.Buffered(1)),
               pl.BlockSpec((d, WIDTH_B), lambda i: (0, qm_col // WIDTH_B), pipeline_mode=pl.Buffered(1))]
    cache_specs = [pl.BlockSpec((bb,) + c.shape[1:], lambda i: (i, 0, 0, 0, 0)) for c in caches]
    whole = lambda shape: pl.BlockSpec(shape, lambda i: (0,) * len(shape))
    return pl.pallas_call(
        _sample_attn_kernel,
        grid=(n_batch // bb,),
        in_specs=[whole(xs.shape)] + w_specs + cache_specs
        + [pl.BlockSpec((bb,) + cache_mem.shape[1:], lambda i: (i, 0, 0, 0, 0))],
        out_specs=[whole((n_batch, GROUP_WIDTH_A)), whole((n_batch, WIDTH_M)),
                   whole((N_GROUPS_A, n_batch, 2 * GROUP_WIDTH_A))],
        out_shape=[jax.ShapeDtypeStruct((n_batch, GROUP_WIDTH_A), F32),
                   jax.ShapeDtypeStruct((n_batch, WIDTH_M), F32),
                   jax.ShapeDtypeStruct((N_GROUPS_A, n_batch, 2 * GROUP_WIDTH_A), F32)],
        scratch_shapes=[pltpu.VMEM((LANES, qkv_w + WIDTH_M), F32),
                        pltpu.VMEM((qkv_w, LANES), BF16),
                        pltpu.VMEM((qkv_w, LANES), F32),
                        pltpu.VMEM((GROUP_WIDTH_A, LANES), F32),
                        pltpu.VMEM((cache_mem.shape[1], N_HEADS_M, HEAD_DIM_M), F32)],
        compiler_params=_params("arbitrary"),
        name="sample_attn",
    )(xs, w_all, w_all, *caches, cache_mem)


def _mix_kernel(x_ref, oa_ref, mk_ref, mv_ref, xs_ref, oas_ref, oms_ref, w_ref, bgate_ref, lnvg_ref, lnvb_ref,
                ws_ref, bs_ref, wss_ref, bss_ref, wba_ref, wbb_ref, wbm_ref, wout_ref, ln1g_ref, ln1b_ref,
                x1_ref, x1s_ref, vrows_ref, *, alpha):
    shared = (w_ref, bgate_ref, lnvg_ref, lnvb_ref)
    tail = (wba_ref, wbb_ref, wbm_ref, wout_ref, ln1g_ref, ln1b_ref)
    is_sample_step = pl.program_id(0) == pl.num_programs(0) - 1

    @pl.when(jnp.logical_not(is_sample_step))
    def _():
        _mix_body(x_ref, oa_ref, (mk_ref, mv_ref), *shared, ws_ref, bs_ref, *tail, x1_ref, None,
                  sample=False, alpha=alpha)

    @pl.when(is_sample_step)
    def _():
        _mix_body(xs_ref, oas_ref, oms_ref, *shared, wss_ref, bss_ref, *tail, x1s_ref, vrows_ref,
                  sample=True, alpha=alpha)


def _mix_body(x_ref, oa_ref, mem_or_om, w_ref, bgate_ref, lnvg_ref, lnvb_ref, ws_ref, bs_ref,
              wba_ref, wbb_ref, wbm_ref, wout_ref, ln1g_ref, ln1b_ref, x1_ref, vrows_ref, *, sample, alpha):
    if sample:
        om_ref = mem_or_om
    else:
        mk_ref, mv_ref = mem_or_om
    d = x_ref.shape[1]
    rows = x_ref.shape[0]
    x = x_ref[...]
    xb = x.astype(BF16)
    col_u, col_v, col_qm, col_gate = 0, WIDTH_B, 2 * WIDTH_B, 2 * WIDTH_B + WIDTH_M

    h_all = _dot(xb, w_ref[:, 3 * WIDTH_A:])

    def gate(k):
        z = h_all[:, col_gate + k * d:col_gate + (k + 1) * d]
        return 0.5 * jnp.tanh(0.5 * (z + bgate_ref[k:k + 1, :])) + 0.5

    mixed = gate(0) * _dot(oa_ref[...].astype(BF16), wba_ref[...])

    u = jax.nn.gelu(h_all[:, col_u:col_u + WIDTH_B])
    v = _layer_norm(jax.nn.gelu(h_all[:, col_v:col_v + WIDTH_B]), lnvg_ref[...], lnvb_ref[...])
    if sample:
        vrows_ref[...] = v
        spatial = v.astype(BF16).astype(F32) * ws_ref[...].astype(BF16).astype(F32) + bs_ref[...]
    else:
        r_i = lax.broadcasted_iota(jnp.int32, (N_GROUPS_B * CHUNK, CHUNK), 0)
        c_i = lax.broadcasted_iota(jnp.int32, (N_GROUPS_B * CHUNK, CHUNK), 1)
        w_s = jnp.where((r_i % CHUNK) >= c_i, ws_ref[...], 0.0).astype(BF16)
        lane = lax.broadcasted_iota(jnp.int32, (CHUNK, WIDTH_B), 1)
        in_group = [(lane >= gb * GROUP_DIM_B) & (lane < (gb + 1) * GROUP_DIM_B) for gb in range(N_GROUPS_B)]
        vb = v.astype(BF16)
        parts = []
        for ch in range(rows // CHUNK):
            allg = _dot(w_s, vb[ch * CHUNK:(ch + 1) * CHUNK, :])
            sp = bs_ref[...]
            for gb in range(N_GROUPS_B):
                sp = sp + jnp.where(in_group[gb], allg[gb * CHUNK:(gb + 1) * CHUNK, :], 0.0)
            parts.append(sp)
        spatial = jnp.concatenate(parts, axis=0)
    o_b = u * spatial
    mixed = mixed + gate(1) * _dot(o_b.astype(BF16), wbb_ref[...])

    if sample:
        o_m = om_ref[...]
    else:
        qm = h_all[:, col_qm:col_qm + WIDTH_M].astype(BF16)
        head_cols = [slice(h * HEAD_DIM_M, (h + 1) * HEAD_DIM_M) for h in range(N_HEADS_M)]
        scores = [_dot_nt(qm[:, hs], mk_ref[:, hs]) * (HEAD_DIM_M ** -0.5) for hs in head_cols]
        maxes = [jnp.max(s, axis=-1, keepdims=True) for s in scores]
        probs = [jnp.exp(s - m) for s, m in zip(scores, maxes)]
        dens = [jnp.sum(p, axis=-1, keepdims=True) for p in probs]
        o_m = jnp.concatenate([_dot(p.astype(BF16), mv_ref[:, hs]) / den
                               for p, den, hs in zip(probs, dens, head_cols)], axis=-1)
    mixed = mixed + gate(2) * _dot(o_m.astype(BF16), wbm_ref[...])

    _project_add_norm(mixed.astype(BF16), wout_ref, alpha * x, ln1g_ref, ln1b_ref, x1_ref)


def _mix(x, o_a, mk, mv, xs, oa_s, om_s, w_all, b_gate, ln_v_g, ln_v_b, w_s, b_s, w_s_row, b_s_row,
         w_ba, w_bb, w_bm, w_out, ln1_g, ln1_b, *, alpha):
    s, d = x.shape
    n_tiles = s // MIX_ROWS
    tile = lambda width: pl.BlockSpec((MIX_ROWS, width), lambda i: (jnp.minimum(i, n_tiles - 1), 0))
    whole = lambda shape: pl.BlockSpec(shape, lambda i: (0,) * len(shape))
    operands = (x, o_a, mk, mv, xs, oa_s, om_s, w_all, b_gate, ln_v_g, ln_v_b, w_s, b_s, w_s_row, b_s_row,
                w_ba, w_bb, w_bm, w_out, ln1_g, ln1_b)
    n_s = xs.shape[0]
    return pl.pallas_call(
        functools.partial(_mix_kernel, alpha=alpha),
        grid=(n_tiles + 1,),
        in_specs=[tile(d), tile(GROUP_WIDTH_A)] + [_resident(t.shape) for t in operands[2:]],
        out_specs=[tile(d), whole((n_s, d)), whole((n_s, WIDTH_B))],
        out_shape=[jax.ShapeDtypeStruct((s, d), F32), jax.ShapeDtypeStruct((n_s, d), F32),
                   jax.ShapeDtypeStruct((n_s, WIDTH_B), F32)],
        compiler_params=_params("arbitrary"),
        name="mix",
    )(*operands)


def _ffn_kernel(x1_ref, x1s_ref, st0_ref, st1_ref, wup_ref, cw_ref, cb_ref, wdown_ref, g_ref, b_ref,
                y_ref, a_ref, ys_ref, as_ref, abuf_ref, *, alpha):
    consts = (wup_ref, cw_ref, cb_ref, wdown_ref, g_ref, b_ref)
    is_sample_step = pl.program_id(0) == pl.num_programs(0) - 1

    @pl.when(jnp.logical_not(is_sample_step))
    def _():
        _ffn_body(x1_ref, None, None, *consts, y_ref, a_ref, abuf_ref, sample=False, alpha=alpha)

    @pl.when(is_sample_step)
    def _():
        _ffn_body(x1s_ref, st0_ref, st1_ref, *consts, ys_ref, as_ref, None, sample=True, alpha=alpha)


def _ffn_body(x1_ref, st0_ref, st1_ref, wup_ref, cw_ref, cb_ref, wdown_ref, g_ref, b_ref, y_ref, a_ref, abuf_ref,
              *, sample, alpha):
    rows = x1_ref.shape[0]
    d_ff = wdown_ref.shape[0]
    if not sample:
        @pl.when(pl.program_id(0) == 0)
        def _():
            abuf_ref[0:SUBLANES, :] = jnp.zeros((SUBLANES, d_ff), F32)

    x1 = x1_ref[...]
    xb = x1.astype(BF16)
    a = _dot(xb, wup_ref[:, :d_ff])
    val = _dot(xb, wup_ref[:, d_ff:])
    if sample:
        a_ref[...] = a
        a_m2, a_m1 = st0_ref[...], st1_ref[...]
    else:
        abuf_ref[SUBLANES:SUBLANES + rows, :] = a
        a_m1 = abuf_ref[SUBLANES - 1:SUBLANES - 1 + rows, :]
        a_m2 = abuf_ref[SUBLANES - 2:SUBLANES - 2 + rows, :]
    conv = cb_ref[...] + cw_ref[0:1, :] * a_m2 + cw_ref[1:2, :] * a_m1 + cw_ref[2:3, :] * a
    h = jax.nn.gelu(conv) * val
    _project_add_norm(h.astype(BF16), wdown_ref, alpha * x1, g_ref, b_ref, y_ref)
    if not sample:
        tail = abuf_ref[rows:rows + SUBLANES, :]
        abuf_ref[0:SUBLANES, :] = tail
        a_ref[...] = tail


def _ffn(x1, x1_s, state, w_up, conv_w, conv_b, w_down, ln2_g, ln2_b, *, alpha):
    s, d = x1.shape
    n_s = x1_s.shape[0]
    d_ff = w_down.shape[0]
    n_tiles = s // FFN_ROWS
    tile = lambda width: pl.BlockSpec((FFN_ROWS, width), lambda i: (jnp.minimum(i, n_tiles - 1), 0))
    whole = lambda shape: pl.BlockSpec(shape, lambda i: (0,) * len(shape))
    operands = (x1, x1_s, state[:, 0], state[:, 1], w_up, conv_w, conv_b, w_down, ln2_g, ln2_b)
    return pl.pallas_call(
        functools.partial(_ffn_kernel, alpha=alpha),
        grid=(n_tiles + 1,),
        in_specs=[tile(d)] + [_resident(t.shape) for t in operands[1:]],
        out_specs=[tile(d), whole((SUBLANES, d_ff)), whole((n_s, d)), whole((n_s, d_ff))],
        out_shape=[jax.ShapeDtypeStruct((s, d), F32), jax.ShapeDtypeStruct((SUBLANES, d_ff), F32),
                   jax.ShapeDtypeStruct((n_s, d), F32), jax.ShapeDtypeStruct((n_s, d_ff), F32)],
        scratch_shapes=[pltpu.VMEM((FFN_ROWS + SUBLANES, d_ff), F32)],
        compiler_params=_params("arbitrary"),
        name="ffn",
    )(*operands)


def kernel(x_prompt, x_sample, mem_prompt, cache_win128_kv, cache_win512_kv, cache_win2048_kv, cache_mem_kv, state_ffn_conv, w_in, b_gate, ln_v_g, ln_v_b, w_spatial, b_spatial, w_mem_kv, w_branch_a, w_branch_b, w_branch_m, w_out, ln1_g, ln1_b, w_up, conv_w, conv_b, w_down, ln2_g, ln2_b):
    depth = w_in.shape[0]
    batch, seq, d_model = x_prompt.shape
    dec_batch, dec_seq, _ = x_sample.shape
    assert batch == 1 and dec_seq == 1 and seq % ATTN_BLOCK == 0
    alpha = (2.0 * depth) ** 0.25
    win_caches = (cache_win128_kv, cache_win512_kv, cache_win2048_kv)
    for cache, (win, _) in zip(win_caches, DILATED_GROUPS):
        assert cache.shape[2] == win and PAST_LEN >= win

    yp = x_prompt.reshape(seq, d_model)
    ys = x_sample.reshape(dec_batch, d_model)
    win_p = [[] for _ in range(N_GROUPS_A)]
    win_s = [[] for _ in range(N_GROUPS_A)]
    mem_p, conv_p, gmlp_s, conv_s = [], [], [], []
    for l in range(depth):
        w_all = w_in[l].astype(BF16)
        row2 = lambda t: t.reshape(1, -1)
        per_lane = lambda t: jnp.repeat(t, GROUP_DIM_B, axis=-1)
        consts_tail = (w_branch_a[l].astype(BF16), w_branch_b[l].astype(BF16), w_branch_m[l].astype(BF16),
                       w_out[l].astype(BF16), row2(ln1_g[l]), row2(ln1_b[l]))
        ffn_consts = (w_up[l].astype(BF16), conv_w[l], row2(conv_b[l]), w_down[l].astype(BF16),
                      row2(ln2_g[l]), row2(ln2_b[l]))

        mkv = _mem_kv_proj(mem_prompt[0], w_mem_kv[l].astype(BF16))
        mem_p.append(mkv.reshape(1, -1, 2, N_HEADS_M, HEAD_DIM_M))
        qkv, kt, *kv_tails = _qkv_proj(yp, w_all)
        for g, (win, _) in enumerate(DILATED_GROUPS):
            tail = kv_tails[g].reshape(2, HEADS_PER_GROUP_A, HEAD_DIM_A, win)
            win_p[g].append(jnp.transpose(tail, (3, 0, 1, 2))[None])
        o_a = _dilated_attn(qkv, kt)
        caches = [jnp.transpose(c[l], (0, 2, 3, 4, 1)) for c in win_caches]
        oa_s, om_s, kv_new = _sample_attn(ys, w_all, caches, cache_mem_kv[l])
        for g in range(N_GROUPS_A):
            win_s[g].append(kv_new[g].reshape(dec_batch, 1, 2, HEADS_PER_GROUP_A, HEAD_DIM_A))

        x1, x1_s, v_rows = _mix(yp, o_a, mkv[:, :WIDTH_M].astype(BF16), mkv[:, WIDTH_M:].astype(BF16), ys, oa_s, om_s,
                                w_all, b_gate[l], row2(ln_v_g[l]), row2(ln_v_b[l]),
                                w_spatial[l].reshape(N_GROUPS_B * CHUNK, CHUNK), per_lane(b_spatial[l].T),
                                row2(per_lane(w_spatial[l][:, 0, 0])), row2(per_lane(b_spatial[l][:, 0])),
                                *consts_tail, alpha=alpha)
        gmlp_s.append(v_rows.reshape(dec_batch, 1, WIDTH_B))
        yp_next, a_tail, ys_next, a_s = _ffn(x1, x1_s, state_ffn_conv[l], *ffn_consts, alpha=alpha)
        conv_p.append(a_tail[SUBLANES - (CONV_W - 1):][None])
        conv_s.append(jnp.stack([state_ffn_conv[l][:, 1], a_s], axis=1))
        yp, ys = yp_next, ys_next

    return (yp.reshape(batch, seq, d_model), ys.reshape(dec_batch, dec_seq, d_model),
            jnp.stack(win_p[0]), jnp.stack(win_p[1]), jnp.stack(win_p[2]),
            jnp.stack(mem_p), jnp.stack(conv_p),
            jnp.stack(win_s[0]), jnp.stack(win_s[1]), jnp.stack(win_s[2]),
            jnp.stack(gmlp_s), jnp.stack(conv_s))
```

```python
import functools

import jax
import jax.numpy as jnp
from jax import lax
from jax.experimental import pallas as pl
from jax.experimental.pallas import tpu as pltpu

BF16 = jnp.bfloat16
F32 = jnp.float32

HEAD_DIM_A = 64
HEADS_PER_GROUP_A = 4
DILATED_GROUPS = ((128, 1), (512, 4), (2048, 16))
N_GROUPS_A = len(DILATED_GROUPS)
GROUP_WIDTH_A = HEADS_PER_GROUP_A * HEAD_DIM_A
WIDTH_A = N_GROUPS_A * GROUP_WIDTH_A
NK = 128
CHUNK = 128
N_GROUPS_B = 4
WIDTH_B = 768
GROUP_DIM_B = WIDTH_B // N_GROUPS_B
N_HEADS_M = 4
HEAD_DIM_M = 128
WIDTH_M = N_HEADS_M * HEAD_DIM_M
N_BRANCH = 3
CONV_W = 3
LN_EPS = 1e-5
NEG = -1e30
PAST_LEN = 16384

LANES = 128
SUBLANES = 8
VMEM_LIMIT_BYTES = 58 * 1024 * 1024

ATTN_BLOCK = 2048
QKV_ROW_PART = 512
DEINTERLEAVE_STRIDE = 4
MIX_ROWS = 512
FFN_ROWS = 512
CAST_STEPS = 16
NORM_ROW_PARTS = 4
SAMPLE_BATCH_BLOCK = 2
SAMPLE_LANE_CHUNK = 512
SAMPLE_MEM_CHUNK = 32

_NT = (((1,), (1,)), ((), ()))


def _dot(a, b):
    return jnp.dot(a, b, preferred_element_type=F32)


def _dot_nt(a, b):
    return lax.dot_general(a, b, _NT, preferred_element_type=F32)


def _layer_norm(x, g, b):
    mu = jnp.mean(x, axis=-1, keepdims=True)
    var = jnp.mean(jnp.square(x - mu), axis=-1, keepdims=True)
    return (x - mu) * lax.rsqrt(var + LN_EPS) * g + b


def _project_add_norm(lhs, w_ref, residual, g_ref, b_ref, o_ref):
    rows = lhs.shape[0]
    n_parts = NORM_ROW_PARTS if rows % (NORM_ROW_PARTS * 2 * SUBLANES) == 0 else 1
    part = rows // n_parts
    for r0 in range(0, rows, part):
        rs = slice(r0, r0 + part)
        o_ref[rs, :] = _layer_norm(residual[rs] + _dot(lhs[rs], w_ref[...]), g_ref[...], b_ref[...])


def _params(*semantics):
    return pltpu.CompilerParams(dimension_semantics=semantics, vmem_limit_bytes=VMEM_LIMIT_BYTES)


def _resident(shape):
    zeros = (0,) * len(shape)
    return pl.BlockSpec(shape, lambda *_: zeros, pipeline_mode=pl.Buffered(1))


def _mem_kv_kernel(mem_ref, w_ref, o_ref):
    o_ref[...] = _dot(mem_ref[...].astype(BF16), w_ref[...])


def _mem_kv_proj(mem, w):
    n, d = mem.shape
    return pl.pallas_call(
        _mem_kv_kernel,
        out_shape=jax.ShapeDtypeStruct((n, w.shape[1]), F32),
        name="mem_kv_proj",
    )(mem, w)


def _qkv_proj_kernel(xt_ref, xbot_ref, w_ref, o_ref, kvt0_ref, kvt1_ref, kvt2_ref, xb_ref, stage_ref, tmp_ref):
    g_id = pl.program_id(1)

    @pl.when(g_id == 0)
    def _():
        half = ATTN_BLOCK // 2
        xb_ref[0:half, :] = xt_ref[...].astype(BF16)
        xb_ref[half:, :] = xbot_ref[...].astype(BF16)

    xb = xb_ref[...]
    last_block = pl.program_id(0) == pl.num_programs(0) - 1
    for g, (win, dil) in enumerate(DILATED_GROUPS):
        kvt_ref = (kvt0_ref, kvt1_ref, kvt2_ref)[g]

        @pl.when(g_id == g)
        def _(g=g, win=win, dil=dil, kvt_ref=kvt_ref):
            w_cols = lambda c: w_ref[:, c * WIDTH_A + g * GROUP_WIDTH_A:c * WIDTH_A + (g + 1) * GROUP_WIDTH_A]
            for c in range(3):
                w = w_cols(c)
                if c == 0:
                    w = w * jnp.asarray(HEAD_DIM_A ** -0.5, BF16)
                n_slabs = GROUP_WIDTH_A // LANES
                inner = DEINTERLEAVE_STRIDE if dil > DEINTERLEAVE_STRIDE else dil
                outer = dil // inner
                part_rows = win if dil > 1 else QKV_ROW_PART
                for base in range(0, ATTN_BLOCK, part_rows):
                    res = _dot(xb[base:base + part_rows], w)
                    if dil == 1:
                        o_ref[base:base + part_rows, c * GROUP_WIDTH_A:(c + 1) * GROUP_WIDTH_A] = res.astype(BF16)
                        continue
                    for s in range(n_slabs):
                        stage_ref[s, base:base + win, :] = res[:, s * LANES:(s + 1) * LANES]
                    for s in range(n_slabs):
                        col = c * GROUP_WIDTH_A + s * LANES
                        src_ref = stage_ref
                        if outer > 1:
                            sub = win // inner
                            for lo in range(inner):
                                tmp_ref[s, base + lo * sub:base + (lo + 1) * sub, :] = (
                                    stage_ref[s, pl.ds(base + lo, sub, stride=inner), :])
                            src_ref = tmp_ref
                        for lo in range(inner):
                            for hi in range(outer):
                                first = base + lo * (win // inner) + hi if outer > 1 else base + lo
                                rows = src_ref[s, pl.ds(first, NK, stride=outer if outer > 1 else inner), :]
                                dst = base + (hi * inner + lo) * NK
                                o_ref[dst:dst + NK, col:col + LANES] = rows.astype(BF16)

            @pl.when(last_block)
            def _():
                tail = xb[ATTN_BLOCK - win:, :]
                kvt_ref[0:GROUP_WIDTH_A, :] = _dot(tail, w_cols(1)).T
                kvt_ref[GROUP_WIDTH_A:2 * GROUP_WIDTH_A, :] = _dot(tail, w_cols(2)).T


def _qkv_proj(x, w_all):
    s, d = x.shape
    n_blk = s // ATTN_BLOCK
    half = ATTN_BLOCK // 2
    x_half = lambda which: pl.BlockSpec(
        (half, d), lambda n, g: (2 * jnp.minimum(n + jnp.minimum(jnp.maximum(g - which, 0), 1), n_blk - 1) + which, 0))
    kvt_shapes = [(2 * GROUP_WIDTH_A, win) for win, _ in DILATED_GROUPS]
    return pl.pallas_call(
        _qkv_proj_kernel,
        grid=(n_blk, N_GROUPS_A),
        in_specs=[x_half(0), x_half(1),
                  pl.BlockSpec((d, 3 * WIDTH_A), lambda n, g: (0, 0), pipeline_mode=pl.Buffered(1))],
        out_specs=[pl.BlockSpec((None, ATTN_BLOCK, 3 * GROUP_WIDTH_A), lambda n, g: (g, n, 0))]
        + [pl.BlockSpec(shape, lambda n, g: (0, 0)) for shape in kvt_shapes],
        out_shape=[jax.ShapeDtypeStruct((N_GROUPS_A, s, 3 * GROUP_WIDTH_A), BF16)]
        + [jax.ShapeDtypeStruct(shape, F32) for shape in kvt_shapes],
        scratch_shapes=[pltpu.VMEM((ATTN_BLOCK, d), BF16),
                        pltpu.VMEM((GROUP_WIDTH_A // LANES, ATTN_BLOCK, LANES), F32),
                        pltpu.VMEM((GROUP_WIDTH_A // LANES, ATTN_BLOCK, LANES), F32)],
        compiler_params=_params("arbitrary", "arbitrary"),
        name="qkv_proj",
    )(x, x, w_all)


def _attn_units(units, prev_shifts):
    first_head = lax.broadcasted_iota(jnp.int32, (NK, LANES), 1) < HEAD_DIM_A
    row = lax.broadcasted_iota(jnp.int32, (2 * NK, NK), 0) % NK
    col = lax.broadcasted_iota(jnp.int32, (2 * NK, NK), 1)
    cur_ok = col <= row
    prev_ok = {}
    scores = []
    for (q, k_p, k_c, _, _), shift in zip(units, prev_shifts):
        zero = jnp.zeros_like(q)
        q_st = jnp.concatenate([jnp.where(first_head, q, zero), jnp.where(first_head, zero, q)], axis=0)
        if id(shift) not in prev_ok:
            prev_ok[id(shift)] = col >= row + shift
        scores.append((jnp.where(cur_ok, _dot_nt(q_st, k_c), NEG),
                       jnp.where(prev_ok[id(shift)], _dot_nt(q_st, k_p), NEG)))
    maxes = [jnp.max(jnp.maximum(s_c, s_p), axis=-1, keepdims=True) for s_c, s_p in scores]
    probs = [(jnp.exp(s_c - m), jnp.exp(s_p - m)) for (s_c, s_p), m in zip(scores, maxes)]
    dens = [jnp.sum(p_c + p_p, axis=-1, keepdims=True) for p_c, p_p in probs]
    results = []
    for (_, _, _, v_p, v_c), (p_c, p_p), m, den in zip(units, probs, maxes, dens):
        o_st = _dot(p_c.astype(BF16), v_c) + _dot(p_p.astype(BF16), v_p)
        pick = lambda t: jnp.where(first_head, jnp.broadcast_to(t[:NK], (NK, LANES)), jnp.broadcast_to(t[NK:], (NK, LANES)))
        den_sel = pick(den)
        results.append((pick(o_st) / den_sel, pick(m) + jnp.log(den_sel)))
    return results


def _dilated_attn_kernel(qkv0_ref, qkv1_ref, qkv2_ref, o_ref, kv0_ref, kv1_ref, kv2_ref, og_ref, lg_ref):
    n = pl.program_id(0)
    qkv_refs = (qkv0_ref, qkv1_ref, qkv2_ref)
    kv_refs = (kv0_ref, kv1_ref, kv2_ref)
    n_sub = ATTN_BLOCK // NK
    n_hp = GROUP_WIDTH_A // LANES

    @pl.when(n == 0)
    def _():
        for kv_ref in kv_refs:
            kv_ref[...] = jnp.zeros(kv_ref.shape, BF16)

    def sub_block(t, carry, prev_in_block):
        off = pl.multiple_of(t * NK, NK)
        units, shifts, dsts = [], [], []
        for g, (win, dil) in enumerate(DILATED_GROUPS):
            win_blk = t // dil
            res = t - win_blk * dil
            has_prev = (n * (ATTN_BLOCK // win) + win_blk) > 0
            nat = win_blk * win + res
            shift = jnp.where(has_prev, 0, 2 * NK)
            for hp in range(n_hp):
                qs = slice(hp * LANES, (hp + 1) * LANES)
                ks = slice(GROUP_WIDTH_A + hp * LANES, GROUP_WIDTH_A + (hp + 1) * LANES)
                vs = slice(2 * GROUP_WIDTH_A + hp * LANES, 2 * GROUP_WIDTH_A + (hp + 1) * LANES)
                cur = pl.ds(off, NK)
                if prev_in_block[g]:
                    prev = pl.ds(off - win, NK)
                    k_prev, v_prev = qkv_refs[g][prev, ks], qkv_refs[g][prev, vs]
                else:
                    k_prev = kv_refs[g][cur, qs]
                    v_prev = kv_refs[g][cur, GROUP_WIDTH_A + hp * LANES:GROUP_WIDTH_A + (hp + 1) * LANES]
                units.append((qkv_refs[g][cur, qs], k_prev, qkv_refs[g][cur, ks], v_prev, qkv_refs[g][cur, vs]))
                shifts.append(shift)
                dsts.append((g, hp, pl.ds(off, NK) if dil == 1 else pl.ds(nat, NK, stride=dil)))
        for (g, hp, dst), (out, lse) in zip(dsts, _attn_units(units, shifts)):
            og_ref[g, hp, dst, :] = out
            lg_ref[g, hp, dst, :] = lse
        return carry

    bounds = sorted({0, n_sub} | {min(win // NK, n_sub) for win, _ in DILATED_GROUPS})
    for lo, hi in zip(bounds[:-1], bounds[1:]):
        in_block = tuple(lo * NK >= win for win, _ in DILATED_GROUPS)
        lax.fori_loop(lo, hi, functools.partial(sub_block, prev_in_block=in_block), 0)
    for (win, _), qkv_ref, kv_ref in zip(DILATED_GROUPS, qkv_refs, kv_refs):
        kv_ref[...] = qkv_ref[ATTN_BLOCK - win:, GROUP_WIDTH_A:]

    def merge(t, carry):
        off = pl.multiple_of(t * NK, NK)
        for hp in range(GROUP_WIDTH_A // LANES):
            lses = [lg_ref[g, hp, pl.ds(off, NK), :] for g in range(N_GROUPS_A)]
            m = jnp.maximum(jnp.maximum(lses[0], lses[1]), lses[2])
            es = [jnp.exp(l - m) for l in lses]
            num = sum(es[g] * og_ref[g, hp, pl.ds(off, NK), :] for g in range(N_GROUPS_A))
            o_ref[pl.ds(off, NK), hp * LANES:(hp + 1) * LANES] = (num / (es[0] + es[1] + es[2])).astype(BF16)
        return carry

    lax.fori_loop(0, n_sub, merge, 0)


def _dilated_attn(qkv):
    _, s, _ = qkv.shape
    n_blk = s // ATTN_BLOCK
    n_hp = GROUP_WIDTH_A // LANES
    in_specs = [pl.BlockSpec((None, ATTN_BLOCK, 3 * GROUP_WIDTH_A), functools.partial(lambda n, g: (g, n, 0), g=g))
                for g in range(N_GROUPS_A)]
    return pl.pallas_call(
        _dilated_attn_kernel,
        grid=(n_blk,),
        in_specs=in_specs,
        out_specs=pl.BlockSpec((ATTN_BLOCK, GROUP_WIDTH_A), lambda n: (n, 0)),
        out_shape=jax.ShapeDtypeStruct((s, GROUP_WIDTH_A), BF16),
        scratch_shapes=[pltpu.VMEM((win, 2 * GROUP_WIDTH_A), BF16) for win, _ in DILATED_GROUPS]
        + [pltpu.VMEM((N_GROUPS_A, n_hp, ATTN_BLOCK, LANES), F32),
           pltpu.VMEM((N_GROUPS_A, n_hp, ATTN_BLOCK, LANES), F32)],
        compiler_params=_params("arbitrary"),
        name="dilated_attn",
    )(qkv, qkv, qkv)


def _to_heads(row, n_heads, head_dim):
    return jnp.concatenate([row[:, h * head_dim:(h + 1) * head_dim] for h in range(n_heads)], axis=0)


def _from_heads(t):
    return jnp.concatenate([t[h:h + 1] for h in range(t.shape[0])], axis=1)


def _window_cache_attention(c_ref, j, cols_ref, base, dil):
    win = c_ref.shape[-1]
    lc = min(win, SAMPLE_LANE_CHUNK)
    n_chunks = win // lc
    lane = lax.broadcasted_iota(jnp.int32, (1, win), 1)
    wanted = (lane % dil) == 0
    out = []
    for h in range(HEADS_PER_GROUP_A):
        lo = base + h * HEAD_DIM_A
        q, k_new, v_new = (cols_ref[lo + c * GROUP_WIDTH_A:lo + c * GROUP_WIDTH_A + HEAD_DIM_A, :] for c in range(3))
        q_wide = jnp.concatenate([q] * (lc // LANES), axis=1)
        parts = [jnp.sum(c_ref[j, 0, h, :, i * lc:(i + 1) * lc] * q_wide, axis=0, keepdims=True)
                 for i in range(n_chunks)]
        s = jnp.where(wanted, jnp.concatenate(parts, axis=1), NEG)
        s_n = jnp.sum(k_new[:, 0:1] * q[:, 0:1], axis=0, keepdims=True)
        m = jnp.maximum(jnp.max(s, axis=-1, keepdims=True), s_n)
        p = jnp.exp(s - m)
        p_n = jnp.exp(s_n - m)
        den = jnp.sum(p, axis=-1, keepdims=True) + p_n
        acc = p_n * v_new[:, 0:1]
        for i in range(n_chunks):
            acc = acc + jnp.sum(c_ref[j, 1, h, :, i * lc:(i + 1) * lc] * p[:, i * lc:(i + 1) * lc],
                                axis=-1, keepdims=True)
        out.append((acc / den, m + jnp.log(den)))
    return out


def _memory_cache_attention(cm_ref, j, q, s_ref):
    n_heads, head_dim = q.shape
    chunk = SAMPLE_MEM_CHUNK
    n_chunks = cm_ref.shape[1] // chunk
    m = jnp.full((n_heads, head_dim), NEG, F32)
    for c in range(n_chunks):
        pos = slice(c * chunk, (c + 1) * chunk)
        s = jnp.sum(cm_ref[j, pos, 0] * q[None], axis=-1, keepdims=True) * (head_dim ** -0.5)
        s = jnp.broadcast_to(s, (chunk, n_heads, head_dim))
        s_ref[pos] = s
        m = jnp.maximum(m, jnp.max(s, axis=0))
    den = jnp.zeros((n_heads, head_dim), F32)
    acc = jnp.zeros((n_heads, head_dim), F32)
    for c in range(n_chunks):
        pos = slice(c * chunk, (c + 1) * chunk)
        p = jnp.exp(s_ref[pos] - m[None])
        den = den + jnp.sum(p, axis=0)
        acc = acc + jnp.sum(p * cm_ref[j, pos, 1], axis=0)
    return acc / den


def _sample_attn_kernel(x_ref, wqkv_ref, wqm_ref, c0_ref, c1_ref, c2_ref, cm_ref,
                        oa_ref, om_ref, kvn_ref, proj_ref, projt_ref, cols_ref, oat_ref, ms_ref):
    step = pl.program_id(0)
    caches = (c0_ref, c1_ref, c2_ref)
    n_batch, d = x_ref.shape
    qkv_w = N_GROUPS_A * 3 * GROUP_WIDTH_A

    @pl.when(step == 0)
    def _():
        xb = jnp.concatenate([x_ref[...], jnp.zeros((LANES - n_batch, d), F32)], axis=0).astype(BF16)
        for part in range(3):
            h = _dot(xb, wqkv_ref[:, part * WIDTH_A:(part + 1) * WIDTH_A])
            if part == 0:
                h = h * (HEAD_DIM_A ** -0.5)
            for g in range(N_GROUPS_A):
                h_g = h[:, g * GROUP_WIDTH_A:(g + 1) * GROUP_WIDTH_A]
                dst = (3 * g + part) * GROUP_WIDTH_A
                proj_ref[:, dst:dst + GROUP_WIDTH_A] = h_g
                if part > 0:
                    kvn_ref[g, :, (part - 1) * GROUP_WIDTH_A:part * GROUP_WIDTH_A] = h_g[:n_batch]
        proj_ref[:, qkv_w:] = _dot(xb, wqm_ref[:, :WIDTH_M])
        for t in range(qkv_w // LANES):
            projt_ref[t * LANES:(t + 1) * LANES, :] = proj_ref[:, t * LANES:(t + 1) * LANES].T.astype(BF16)
        oat_ref[...] = jnp.zeros(oat_ref.shape, F32)

    def one_row(j, carry):
        b = step * SAMPLE_BATCH_BLOCK + j
        pick_row = (lax.broadcasted_iota(jnp.int32, (LANES, LANES), 0) == b).astype(BF16)
        cols_ref[...] = _dot(projt_ref[...], pick_row)
        per_group = [_window_cache_attention(caches[g], j, cols_ref, g * 3 * GROUP_WIDTH_A, dil)
                     for g, (_, dil) in enumerate(DILATED_GROUPS)]
        this_lane = lax.broadcasted_iota(jnp.int32, (1, LANES), 1) == b
        for h in range(HEADS_PER_GROUP_A):
            outs = [per_group[g][h][0] for g in range(N_GROUPS_A)]
            lses = [per_group[g][h][1] for g in range(N_GROUPS_A)]
            m3 = jnp.maximum(jnp.maximum(lses[0], lses[1]), lses[2])
            es = [jnp.exp(l - m3) for l in lses]
            oa = (es[0] * outs[0] + es[1] * outs[1] + es[2] * outs[2]) / (es[0] + es[1] + es[2])
            rows = slice(h * HEAD_DIM_A, (h + 1) * HEAD_DIM_A)
            oat_ref[rows, :] = jnp.where(this_lane, oa, oat_ref[rows, :])

        qm = _to_heads(proj_ref[pl.ds(b, 1), qkv_w:], N_HEADS_M, HEAD_DIM_M)
        om_ref[pl.ds(b, 1), :] = _from_heads(_memory_cache_attention(cm_ref, j, qm, ms_ref))
        return carry

    lax.fori_loop(0, SAMPLE_BATCH_BLOCK, one_row, 0)

    @pl.when(step == pl.num_programs(0) - 1)
    def _():
        oa_ref[...] = oat_ref[...].T[:n_batch, :]


def _sample_attn(xs, w_all, caches, cache_mem):
    n_batch, d = xs.shape
    assert n_batch <= LANES
    bb = SAMPLE_BATCH_BLOCK
    qkv_w = N_GROUPS_A * 3 * GROUP_WIDTH_A
    qm_col = 3 * WIDTH_A + 2 * WIDTH_B
    assert qm_col % WIDTH_B == 0 and WIDTH_M <= WIDTH_B
    w_specs = [pl.BlockSpec((d, qkv_w), lambda i: (0, 0), pipeline_mode=pl.Buffered(1)),
               pl.BlockSpec((d, WIDTH_B), lambda i: (0, qm_col // WIDTH_B), pipeline_mode=pl.Buffered(1))]
    cache_specs = [pl.BlockSpec((bb,) + c.shape[1:], lambda i: (i, 0, 0, 0, 0)) for c in caches]
    whole = lambda shape: pl.BlockSpec(shape, lambda i: (0,) * len(shape))
    return pl.pallas_call(
        _sample_attn_kernel,
        grid=(n_batch // bb,),
        in_specs=[whole(xs.shape)] + w_specs + cache_specs
        + [pl.BlockSpec((bb,) + cache_mem.shape[1:], lambda i: (i, 0, 0, 0, 0))],
        out_specs=[whole((n_batch, GROUP_WIDTH_A)), whole((n_batch, WIDTH_M)),
                   whole((N_GROUPS_A, n_batch, 2 * GROUP_WIDTH_A))],
        out_shape=[jax.ShapeDtypeStruct((n_batch, GROUP_WIDTH_A), F32),
                   jax.ShapeDtypeStruct((n_batch, WIDTH_M), F32),
                   jax.ShapeDtypeStruct((N_GROUPS_A, n_batch, 2 * GROUP_WIDTH_A), F32)],
        scratch_shapes=[pltpu.VMEM((LANES, qkv_w + WIDTH_M), F32),
                        pltpu.VMEM((qkv_w, LANES), BF16),
                        pltpu.VMEM((qkv_w, LANES), F32),
                        pltpu.VMEM((GROUP_WIDTH_A, LANES), F32),
                        pltpu.VMEM((cache_mem.shape[1], N_HEADS_M, HEAD_DIM_M), F32)],
        compiler_params=_params("arbitrary"),
        name="sample_attn",
    )(xs, w_all, w_all, *caches, cache_mem)


def _mix_kernel(x_ref, oa_ref, mk_ref, mv_ref, xs_ref, oas_ref, oms_ref, w_ref, bgate_ref, lnvg_ref, lnvb_ref,
                ws_ref, bs_ref, wss_ref, bss_ref, wba_ref, wbb_ref, wbm_ref, wout_ref, ln1g_ref, ln1b_ref,
                wup_ref, wdown_ref, x1_ref, x1s_ref, vrows_ref, wup_bf_ref, wdown_bf_ref, *, alpha):
    @pl.when(pl.program_id(0) < CAST_STEPS)
    def _():
        wup_bf_ref[...] = wup_ref[...].astype(BF16)
        wdown_bf_ref[...] = wdown_ref[...].astype(BF16)

    shared = (w_ref, bgate_ref, lnvg_ref, lnvb_ref)
    tail = (wba_ref, wbb_ref, wbm_ref, wout_ref, ln1g_ref, ln1b_ref)
    is_sample_step = pl.program_id(0) == pl.num_programs(0) - 1

    @pl.when(jnp.logical_not(is_sample_step))
    def _():
        _mix_body(x_ref, oa_ref, (mk_ref, mv_ref), *shared, ws_ref, bs_ref, *tail, x1_ref, None,
                  sample=False, alpha=alpha)

    @pl.when(is_sample_step)
    def _():
        _mix_body(xs_ref, oas_ref, oms_ref, *shared, wss_ref, bss_ref, *tail, x1s_ref, vrows_ref,
                  sample=True, alpha=alpha)


def _mix_body(x_ref, oa_ref, mem_or_om, w_ref, bgate_ref, lnvg_ref, lnvb_ref, ws_ref, bs_ref,
              wba_ref, wbb_ref, wbm_ref, wout_ref, ln1g_ref, ln1b_ref, x1_ref, vrows_ref, *, sample, alpha):
    if sample:
        om_ref = mem_or_om
    else:
        mk_ref, mv_ref = mem_or_om
    d = x_ref.shape[1]
    rows = x_ref.shape[0]
    x = x_ref[...]
    xb = x.astype(BF16)
    col_u, col_v, col_qm, col_gate = 0, WIDTH_B, 2 * WIDTH_B, 2 * WIDTH_B + WIDTH_M

    h_all = _dot(xb, w_ref[:, 3 * WIDTH_A:])

    def gate(k):
        z = h_all[:, col_gate + k * d:col_gate + (k + 1) * d]
        return 0.5 * jnp.tanh(0.5 * (z + bgate_ref[k:k + 1, :])) + 0.5

    mixed = gate(0) * _dot(oa_ref[...].astype(BF16), wba_ref[...])

    u = jax.nn.gelu(h_all[:, col_u:col_u + WIDTH_B])
    v = _layer_norm(jax.nn.gelu(h_all[:, col_v:col_v + WIDTH_B]), lnvg_ref[...], lnvb_ref[...])
    if sample:
        vrows_ref[...] = v
        spatial = v.astype(BF16).astype(F32) * ws_ref[...].astype(BF16).astype(F32) + bs_ref[...]
    else:
        r_i = lax.broadcasted_iota(jnp.int32, (N_GROUPS_B * CHUNK, CHUNK), 0)
        c_i = lax.broadcasted_iota(jnp.int32, (N_GROUPS_B * CHUNK, CHUNK), 1)
        w_s = jnp.where((r_i % CHUNK) >= c_i, ws_ref[...], 0.0).astype(BF16)
        lane = lax.broadcasted_iota(jnp.int32, (CHUNK, WIDTH_B), 1)
        in_group = [(lane >= gb * GROUP_DIM_B) & (lane < (gb + 1) * GROUP_DIM_B) for gb in range(N_GROUPS_B)]
        vb = v.astype(BF16)
        parts = []
        for ch in range(rows // CHUNK):
            allg = _dot(w_s, vb[ch * CHUNK:(ch + 1) * CHUNK, :])
            sp = bs_ref[...]
            for gb in range(N_GROUPS_B):
                sp = sp + jnp.where(in_group[gb], allg[gb * CHUNK:(gb + 1) * CHUNK, :], 0.0)
            parts.append(sp)
        spatial = jnp.concatenate(parts, axis=0)
    o_b = u * spatial
    mixed = mixed + gate(1) * _dot(o_b.astype(BF16), wbb_ref[...])

    if sample:
        o_m = om_ref[...]
    else:
        qm = h_all[:, col_qm:col_qm + WIDTH_M].astype(BF16)
        head_cols = [slice(h * HEAD_DIM_M, (h + 1) * HEAD_DIM_M) for h in range(N_HEADS_M)]
        scores = [_dot_nt(qm[:, hs], mk_ref[:, hs]) * (HEAD_DIM_M ** -0.5) for hs in head_cols]
        maxes = [jnp.max(s, axis=-1, keepdims=True) for s in scores]
        probs = [jnp.exp(s - m) for s, m in zip(scores, maxes)]
        dens = [jnp.sum(p, axis=-1, keepdims=True) for p in probs]
        o_m = jnp.concatenate([_dot(p.astype(BF16), mv_ref[:, hs]) / den
                               for p, den, hs in zip(probs, dens, head_cols)], axis=-1)
    mixed = mixed + gate(2) * _dot(o_m.astype(BF16), wbm_ref[...])

    _project_add_norm(mixed.astype(BF16), wout_ref, alpha * x, ln1g_ref, ln1b_ref, x1_ref)


def _mix(x, o_a, mk, mv, xs, oa_s, om_s, w_all, b_gate, ln_v_g, ln_v_b, w_s, b_s, w_s_row, b_s_row,
         w_ba, w_bb, w_bm, w_out, ln1_g, ln1_b, w_up, w_down, *, alpha):
    s, d = x.shape
    n_tiles = s // MIX_ROWS
    assert n_tiles >= CAST_STEPS and w_up.shape[0] % (CAST_STEPS * 16) == 0 and w_down.shape[0] % (CAST_STEPS * 16) == 0
    chunk = lambda t: pl.BlockSpec((t.shape[0] // CAST_STEPS, t.shape[1]), lambda i: (jnp.minimum(i, CAST_STEPS - 1), 0))
    tile = lambda width: pl.BlockSpec((MIX_ROWS, width), lambda i: (jnp.minimum(i, n_tiles - 1), 0))
    whole = lambda shape: pl.BlockSpec(shape, lambda i: (0,) * len(shape))
    operands = (x, o_a, mk, mv, xs, oa_s, om_s, w_all, b_gate, ln_v_g, ln_v_b, w_s, b_s, w_s_row, b_s_row,
                w_ba, w_bb, w_bm, w_out, ln1_g, ln1_b)
    n_s = xs.shape[0]
    return pl.pallas_call(
        functools.partial(_mix_kernel, alpha=alpha),
        grid=(n_tiles + 1,),
        in_specs=[tile(d), tile(GROUP_WIDTH_A)] + [_resident(t.shape) for t in operands[2:]] + [chunk(w_up), chunk(w_down)],
        out_specs=[tile(d), whole((n_s, d)), whole((n_s, WIDTH_B)), chunk(w_up), chunk(w_down)],
        out_shape=[jax.ShapeDtypeStruct((s, d), F32), jax.ShapeDtypeStruct((n_s, d), F32),
                   jax.ShapeDtypeStruct((n_s, WIDTH_B), F32),
                   jax.ShapeDtypeStruct(w_up.shape, BF16), jax.ShapeDtypeStruct(w_down.shape, BF16)],
        compiler_params=_params("arbitrary"),
        name="mix",
    )(*operands, w_up, w_down)


def _ffn_kernel(x1_ref, x1s_ref, st0_ref, st1_ref, wup_ref, cw_ref, cb_ref, wdown_ref, g_ref, b_ref,
                y_ref, a_ref, ys_ref, as_ref, abuf_ref, *, alpha):
    consts = (wup_ref, cw_ref, cb_ref, wdown_ref, g_ref, b_ref)
    is_sample_step = pl.program_id(0) == pl.num_programs(0) - 1

    @pl.when(jnp.logical_not(is_sample_step))
    def _():
        _ffn_body(x1_ref, None, None, *consts, y_ref, a_ref, abuf_ref, sample=False, alpha=alpha)

    @pl.when(is_sample_step)
    def _():
        _ffn_body(x1s_ref, st0_ref, st1_ref, *consts, ys_ref, as_ref, None, sample=True, alpha=alpha)


def _ffn_body(x1_ref, st0_ref, st1_ref, wup_ref, cw_ref, cb_ref, wdown_ref, g_ref, b_ref, y_ref, a_ref, abuf_ref,
              *, sample, alpha):
    rows = x1_ref.shape[0]
    d_ff = wdown_ref.shape[0]
    if not sample:
        @pl.when(pl.program_id(0) == 0)
        def _():
            abuf_ref[0:SUBLANES, :] = jnp.zeros((SUBLANES, d_ff), F32)

    x1 = x1_ref[...]
    xb = x1.astype(BF16)
    a = _dot(xb, wup_ref[:, :d_ff])
    val = _dot(xb, wup_ref[:, d_ff:])
    if sample:
        a_ref[...] = a
        a_m2, a_m1 = st0_ref[...], st1_ref[...]
    else:
        abuf_ref[SUBLANES:SUBLANES + rows, :] = a
        a_m1 = abuf_ref[SUBLANES - 1:SUBLANES - 1 + rows, :]
        a_m2 = abuf_ref[SUBLANES - 2:SUBLANES - 2 + rows, :]
    conv = cb_ref[...] + cw_ref[0:1, :] * a_m2 + cw_ref[1:2, :] * a_m1 + cw_ref[2:3, :] * a
    h = jax.nn.gelu(conv) * val
    _project_add_norm(h.astype(BF16), wdown_ref, alpha * x1, g_ref, b_ref, y_ref)
    if not sample:
        tail = abuf_ref[rows:rows + SUBLANES, :]
        abuf_ref[0:SUBLANES, :] = tail
        a_ref[...] = tail


def _ffn(x1, x1_s, state, w_up, conv_w, conv_b, w_down, ln2_g, ln2_b, *, alpha):
    s, d = x1.shape
    n_s = x1_s.shape[0]
    d_ff = w_down.shape[0]
    n_tiles = s // FFN_ROWS
    tile = lambda width: pl.BlockSpec((FFN_ROWS, width), lambda i: (jnp.minimum(i, n_tiles - 1), 0))
    whole = lambda shape: pl.BlockSpec(shape, lambda i: (0,) * len(shape))
    operands = (x1, x1_s, state[:, 0], state[:, 1], w_up, conv_w, conv_b, w_down, ln2_g, ln2_b)
    return pl.pallas_call(
        functools.partial(_ffn_kernel, alpha=alpha),
        grid=(n_tiles + 1,),
        in_specs=[tile(d)] + [_resident(t.shape) for t in operands[1:]],
        out_specs=[tile(d), whole((SUBLANES, d_ff)), whole((n_s, d)), whole((n_s, d_ff))],
        out_shape=[jax.ShapeDtypeStruct((s, d), F32), jax.ShapeDtypeStruct((SUBLANES, d_ff), F32),
                   jax.ShapeDtypeStruct((n_s, d), F32), jax.ShapeDtypeStruct((n_s, d_ff), F32)],
        scratch_shapes=[pltpu.VMEM((FFN_ROWS + SUBLANES, d_ff), F32)],
        compiler_params=_params("arbitrary"),
        name="ffn",
    )(*operands)


def kernel(x_prompt, x_sample, mem_prompt, cache_win128_kv, cache_win512_kv, cache_win2048_kv, cache_mem_kv, state_ffn_conv, w_in, b_gate, ln_v_g, ln_v_b, w_spatial, b_spatial, w_mem_kv, w_branch_a, w_branch_b, w_branch_m, w_out, ln1_g, ln1_b, w_up, conv_w, conv_b, w_down, ln2_g, ln2_b):
    depth = w_in.shape[0]
    batch, seq, d_model = x_prompt.shape
    dec_batch, dec_seq, _ = x_sample.shape
    assert batch == 1 and dec_seq == 1 and seq % ATTN_BLOCK == 0
    alpha = (2.0 * depth) ** 0.25
    win_caches = (cache_win128_kv, cache_win512_kv, cache_win2048_kv)
    for cache, (win, _) in zip(win_caches, DILATED_GROUPS):
        assert cache.shape[2] == win and PAST_LEN >= win

    yp = x_prompt.reshape(seq, d_model)
    ys = x_sample.reshape(dec_batch, d_model)
    win_p = [[] for _ in range(N_GROUPS_A)]
    win_s = [[] for _ in range(N_GROUPS_A)]
    mem_p, conv_p, gmlp_s, conv_s = [], [], [], []
    for l in range(depth):
        w_all = w_in[l].astype(BF16)
        row2 = lambda t: t.reshape(1, -1)
        per_lane = lambda t: jnp.repeat(t, GROUP_DIM_B, axis=-1)
        consts_tail = (w_branch_a[l].astype(BF16), w_branch_b[l].astype(BF16), w_branch_m[l].astype(BF16),
                       w_out[l].astype(BF16), row2(ln1_g[l]), row2(ln1_b[l]))

        mkv = _mem_kv_proj(mem_prompt[0], w_mem_kv[l].astype(BF16))
        mem_p.append(mkv.reshape(1, -1, 2, N_HEADS_M, HEAD_DIM_M))
        qkv, *kv_tails = _qkv_proj(yp, w_all)
        for g, (win, _) in enumerate(DILATED_GROUPS):
            tail = kv_tails[g].reshape(2, HEADS_PER_GROUP_A, HEAD_DIM_A, win)
            win_p[g].append(jnp.transpose(tail, (3, 0, 1, 2))[None])
        o_a = _dilated_attn(qkv)
        caches = [jnp.transpose(c[l], (0, 2, 3, 4, 1)) for c in win_caches]
        oa_s, om_s, kv_new = _sample_attn(ys, w_all, caches, cache_mem_kv[l])
        for g in range(N_GROUPS_A):
            win_s[g].append(kv_new[g].reshape(dec_batch, 1, 2, HEADS_PER_GROUP_A, HEAD_DIM_A))

        x1, x1_s, v_rows, w_up_bf, w_down_bf = _mix(yp, o_a, mkv[:, :WIDTH_M].astype(BF16), mkv[:, WIDTH_M:].astype(BF16), ys, oa_s, om_s,
                                w_all, b_gate[l], row2(ln_v_g[l]), row2(ln_v_b[l]),
                                w_spatial[l].reshape(N_GROUPS_B * CHUNK, CHUNK), per_lane(b_spatial[l].T),
                                row2(per_lane(w_spatial[l][:, 0, 0])), row2(per_lane(b_spatial[l][:, 0])),
                                *consts_tail, w_up[l], w_down[l], alpha=alpha)
        ffn_consts = (w_up_bf, conv_w[l], row2(conv_b[l]), w_down_bf, row2(ln2_g[l]), row2(ln2_b[l]))
        gmlp_s.append(v_rows.reshape(dec_batch, 1, WIDTH_B))
        yp_next, a_tail, ys_next, a_s = _ffn(x1, x1_s, state_ffn_conv[l], *ffn_consts, alpha=alpha)
        conv_p.append(a_tail[SUBLANES - (CONV_W - 1):][None])
        conv_s.append(jnp.stack([state_ffn_conv[l][:, 1], a_s], axis=1))
        yp, ys = yp_next, ys_next

    return (yp.reshape(batch, seq, d_model), ys.reshape(dec_batch, dec_seq, d_model),
            jnp.stack(win_p[0]), jnp.stack(win_p[1]), jnp.stack(win_p[2]),
            jnp.stack(mem_p), jnp.stack(conv_p),
            jnp.stack(win_s[0]), jnp.stack(win_s[1]), jnp.stack(win_s[2]),
            jnp.stack(gmlp_s), jnp.stack(conv_s))
```

```python
import functools

import jax
import jax.numpy as jnp
from jax import lax
from jax.experimental import pallas as pl
from jax.experimental.pallas import tpu as pltpu

BF16 = jnp.bfloat16
F32 = jnp.float32

HEAD_DIM_A = 64
HEADS_PER_GROUP_A = 4
DILATED_GROUPS = ((128, 1), (512, 4), (2048, 16))
N_GROUPS_A = len(DILATED_GROUPS)
GROUP_WIDTH_A = HEADS_PER_GROUP_A * HEAD_DIM_A
WIDTH_A = N_GROUPS_A * GROUP_WIDTH_A
NK = 128
CHUNK = 128
N_GROUPS_B = 4
WIDTH_B = 768
GROUP_DIM_B = WIDTH_B // N_GROUPS_B
N_HEADS_M = 4
HEAD_DIM_M = 128
WIDTH_M = N_HEADS_M * HEAD_DIM_M
N_BRANCH = 3
CONV_W = 3
LN_EPS = 1e-5
NEG = -1e30
PAST_LEN = 16384

LANES = 128
SUBLANES = 8
VMEM_LIMIT_BYTES = 58 * 1024 * 1024

ATTN_BLOCK = 2048
QKV_ROW_PART = 512
DEINTERLEAVE_STRIDE = 4
MIX_ROWS = 512
FFN_ROWS = 512
CAST_STEPS = 16
NORM_ROW_PARTS = 4
SAMPLE_BATCH_BLOCK = 2
SAMPLE_LANE_CHUNK = 512
SAMPLE_MEM_CHUNK = 32

_NT = (((1,), (1,)), ((), ()))


def _dot(a, b):
    return jnp.dot(a, b, preferred_element_type=F32)


def _dot_nt(a, b):
    return lax.dot_general(a, b, _NT, preferred_element_type=F32)


def _layer_norm(x, g, b):
    mu = jnp.mean(x, axis=-1, keepdims=True)
    var = jnp.mean(jnp.square(x - mu), axis=-1, keepdims=True)
    return (x - mu) * lax.rsqrt(var + LN_EPS) * g + b


def _project_add_norm(lhs, w_ref, residual, g_ref, b_ref, o_ref):
    rows = lhs.shape[0]
    n_parts = NORM_ROW_PARTS if rows % (NORM_ROW_PARTS * 2 * SUBLANES) == 0 else 1
    part = rows // n_parts
    for r0 in range(0, rows, part):
        rs = slice(r0, r0 + part)
        o_ref[rs, :] = _layer_norm(residual[rs] + _dot(lhs[rs], w_ref[...]), g_ref[...], b_ref[...])


def _params(*semantics):
    return pltpu.CompilerParams(dimension_semantics=semantics, vmem_limit_bytes=VMEM_LIMIT_BYTES)


def _resident(shape):
    zeros = (0,) * len(shape)
    return pl.BlockSpec(shape, lambda *_: zeros, pipeline_mode=pl.Buffered(1))


def _mem_kv_kernel(mem_ref, w_ref, o_ref):
    o_ref[...] = _dot(mem_ref[...].astype(BF16), w_ref[...])


def _mem_kv_proj(mem, w):
    n, d = mem.shape
    return pl.pallas_call(
        _mem_kv_kernel,
        out_shape=jax.ShapeDtypeStruct((n, w.shape[1]), F32),
        name="mem_kv_proj",
    )(mem, w)


def _qkv_proj_kernel(xt_ref, xbot_ref, w_ref, o_ref, kvt0_ref, kvt1_ref, kvt2_ref, xb_ref, stage_ref, tmp_ref):
    g_id = pl.program_id(1)

    @pl.when(g_id == 0)
    def _():
        half = ATTN_BLOCK // 2
        xb_ref[0:half, :] = xt_ref[...].astype(BF16)
        xb_ref[half:, :] = xbot_ref[...].astype(BF16)

    xb = xb_ref[...]
    last_block = pl.program_id(0) == pl.num_programs(0) - 1
    for g, (win, dil) in enumerate(DILATED_GROUPS):
        kvt_ref = (kvt0_ref, kvt1_ref, kvt2_ref)[g]

        @pl.when(g_id == g)
        def _(g=g, win=win, dil=dil, kvt_ref=kvt_ref):
            w_cols = lambda c: w_ref[:, c * WIDTH_A + g * GROUP_WIDTH_A:c * WIDTH_A + (g + 1) * GROUP_WIDTH_A]
            for c in range(3):
                w = w_cols(c)
                if c == 0:
                    w = w * jnp.asarray(HEAD_DIM_A ** -0.5, BF16)
                n_slabs = GROUP_WIDTH_A // LANES
                inner = DEINTERLEAVE_STRIDE if dil > DEINTERLEAVE_STRIDE else dil
                outer = dil // inner
                part_rows = win if dil > 1 else QKV_ROW_PART
                for base in range(0, ATTN_BLOCK, part_rows):
                    res = _dot(xb[base:base + part_rows], w)
                    if dil == 1:
                        o_ref[base:base + part_rows, c * GROUP_WIDTH_A:(c + 1) * GROUP_WIDTH_A] = res.astype(BF16)
                        continue
                    for s in range(n_slabs):
                        stage_ref[s, base:base + win, :] = res[:, s * LANES:(s + 1) * LANES]
                    for s in range(n_slabs):
                        col = c * GROUP_WIDTH_A + s * LANES
                        src_ref = stage_ref
                        if outer > 1:
                            sub = win // inner
                            for lo in range(inner):
                                tmp_ref[s, base + lo * sub:base + (lo + 1) * sub, :] = (
                                    stage_ref[s, pl.ds(base + lo, sub, stride=inner), :])
                            src_ref = tmp_ref
                        for lo in range(inner):
                            for hi in range(outer):
                                first = base + lo * (win // inner) + hi if outer > 1 else base + lo
                                rows = src_ref[s, pl.ds(first, NK, stride=outer if outer > 1 else inner), :]
                                dst = base + (hi * inner + lo) * NK
                                o_ref[dst:dst + NK, col:col + LANES] = rows.astype(BF16)

            @pl.when(last_block)
            def _():
                tail = xb[ATTN_BLOCK - win:, :]
                kvt_ref[0:GROUP_WIDTH_A, :] = _dot(tail, w_cols(1)).T
                kvt_ref[GROUP_WIDTH_A:2 * GROUP_WIDTH_A, :] = _dot(tail, w_cols(2)).T


def _qkv_proj(x, w_all):
    s, d = x.shape
    n_blk = s // ATTN_BLOCK
    half = ATTN_BLOCK // 2
    x_half = lambda which: pl.BlockSpec(
        (half, d), lambda n, g: (2 * jnp.minimum(n + jnp.minimum(jnp.maximum(g - which, 0), 1), n_blk - 1) + which, 0))
    kvt_shapes = [(2 * GROUP_WIDTH_A, win) for win, _ in DILATED_GROUPS]
    return pl.pallas_call(
        _qkv_proj_kernel,
        grid=(n_blk, N_GROUPS_A),
        in_specs=[x_half(0), x_half(1),
                  pl.BlockSpec((d, 3 * WIDTH_A), lambda n, g: (0, 0), pipeline_mode=pl.Buffered(1))],
        out_specs=[pl.BlockSpec((None, ATTN_BLOCK, 3 * GROUP_WIDTH_A), lambda n, g: (g, n, 0))]
        + [pl.BlockSpec(shape, lambda n, g: (0, 0)) for shape in kvt_shapes],
        out_shape=[jax.ShapeDtypeStruct((N_GROUPS_A, s, 3 * GROUP_WIDTH_A), BF16)]
        + [jax.ShapeDtypeStruct(shape, F32) for shape in kvt_shapes],
        scratch_shapes=[pltpu.VMEM((ATTN_BLOCK, d), BF16),
                        pltpu.VMEM((GROUP_WIDTH_A // LANES, ATTN_BLOCK, LANES), F32),
                        pltpu.VMEM((GROUP_WIDTH_A // LANES, ATTN_BLOCK, LANES), F32)],
        compiler_params=_params("arbitrary", "arbitrary"),
        name="qkv_proj",
    )(x, x, w_all)


def _attn_units(units, prev_shifts):
    first_head = lax.broadcasted_iota(jnp.int32, (NK, LANES), 1) < HEAD_DIM_A
    row = lax.broadcasted_iota(jnp.int32, (2 * NK, NK), 0) % NK
    col = lax.broadcasted_iota(jnp.int32, (2 * NK, NK), 1)
    cur_ok = col <= row
    prev_ok = {}
    scores = []
    for (q, k_p, k_c, _, _), shift in zip(units, prev_shifts):
        zero = jnp.zeros_like(q)
        q_st = jnp.concatenate([jnp.where(first_head, q, zero), jnp.where(first_head, zero, q)], axis=0)
        if id(shift) not in prev_ok:
            prev_ok[id(shift)] = col >= row + shift
        scores.append((jnp.where(cur_ok, _dot_nt(q_st, k_c), NEG),
                       jnp.where(prev_ok[id(shift)], _dot_nt(q_st, k_p), NEG)))
    maxes = [jnp.max(jnp.maximum(s_c, s_p), axis=-1, keepdims=True) for s_c, s_p in scores]
    probs = [(jnp.exp(s_c - m), jnp.exp(s_p - m)) for (s_c, s_p), m in zip(scores, maxes)]
    dens = [jnp.sum(p_c + p_p, axis=-1, keepdims=True) for p_c, p_p in probs]
    results = []
    for (_, _, _, v_p, v_c), (p_c, p_p), m, den in zip(units, probs, maxes, dens):
        o_st = _dot(p_c.astype(BF16), v_c) + _dot(p_p.astype(BF16), v_p)
        pick = lambda t: jnp.where(first_head, jnp.broadcast_to(t[:NK], (NK, LANES)), jnp.broadcast_to(t[NK:], (NK, LANES)))
        den_sel = pick(den)
        results.append((pick(o_st) / den_sel, pick(m) + jnp.log(den_sel)))
    return results


def _dilated_attn_kernel(qkv0_ref, qkv1_ref, qkv2_ref, w_ref, o_ref, w_bf_ref, kv0_ref, kv1_ref, kv2_ref,
                         og_ref, lg_ref):
    n = pl.program_id(0)
    w_bf_ref[...] = w_ref[...].astype(BF16)
    qkv_refs = (qkv0_ref, qkv1_ref, qkv2_ref)
    kv_refs = (kv0_ref, kv1_ref, kv2_ref)
    n_sub = ATTN_BLOCK // NK
    n_hp = GROUP_WIDTH_A // LANES

    @pl.when(n == 0)
    def _():
        for kv_ref in kv_refs:
            kv_ref[...] = jnp.zeros(kv_ref.shape, BF16)

    def sub_block(t, carry, prev_in_block):
        off = pl.multiple_of(t * NK, NK)
        units, shifts, dsts = [], [], []
        for g, (win, dil) in enumerate(DILATED_GROUPS):
            win_blk = t // dil
            res = t - win_blk * dil
            has_prev = (n * (ATTN_BLOCK // win) + win_blk) > 0
            nat = win_blk * win + res
            shift = jnp.where(has_prev, 0, 2 * NK)
            for hp in range(n_hp):
                qs = slice(hp * LANES, (hp + 1) * LANES)
                ks = slice(GROUP_WIDTH_A + hp * LANES, GROUP_WIDTH_A + (hp + 1) * LANES)
                vs = slice(2 * GROUP_WIDTH_A + hp * LANES, 2 * GROUP_WIDTH_A + (hp + 1) * LANES)
                cur = pl.ds(off, NK)
                if prev_in_block[g]:
                    prev = pl.ds(off - win, NK)
                    k_prev, v_prev = qkv_refs[g][prev, ks], qkv_refs[g][prev, vs]
                else:
                    k_prev = kv_refs[g][cur, qs]
                    v_prev = kv_refs[g][cur, GROUP_WIDTH_A + hp * LANES:GROUP_WIDTH_A + (hp + 1) * LANES]
                units.append((qkv_refs[g][cur, qs], k_prev, qkv_refs[g][cur, ks], v_prev, qkv_refs[g][cur, vs]))
                shifts.append(shift)
                dsts.append((g, hp, pl.ds(off, NK) if dil == 1 else pl.ds(nat, NK, stride=dil)))
        for (g, hp, dst), (out, lse) in zip(dsts, _attn_units(units, shifts)):
            og_ref[g, hp, dst, :] = out
            lg_ref[g, hp, dst, :] = lse
        return carry

    bounds = sorted({0, n_sub} | {min(win // NK, n_sub) for win, _ in DILATED_GROUPS})
    for lo, hi in zip(bounds[:-1], bounds[1:]):
        in_block = tuple(lo * NK >= win for win, _ in DILATED_GROUPS)
        lax.fori_loop(lo, hi, functools.partial(sub_block, prev_in_block=in_block), 0)
    for (win, _), qkv_ref, kv_ref in zip(DILATED_GROUPS, qkv_refs, kv_refs):
        kv_ref[...] = qkv_ref[ATTN_BLOCK - win:, GROUP_WIDTH_A:]

    def merge(t, carry):
        off = pl.multiple_of(t * NK, NK)
        for hp in range(GROUP_WIDTH_A // LANES):
            lses = [lg_ref[g, hp, pl.ds(off, NK), :] for g in range(N_GROUPS_A)]
            m = jnp.maximum(jnp.maximum(lses[0], lses[1]), lses[2])
            es = [jnp.exp(l - m) for l in lses]
            num = sum(es[g] * og_ref[g, hp, pl.ds(off, NK), :] for g in range(N_GROUPS_A))
            o_ref[pl.ds(off, NK), hp * LANES:(hp + 1) * LANES] = (num / (es[0] + es[1] + es[2])).astype(BF16)
        return carry

    lax.fori_loop(0, n_sub, merge, 0)


def _dilated_attn(qkv, w):
    _, s, _ = qkv.shape
    n_blk = s // ATTN_BLOCK
    n_hp = GROUP_WIDTH_A // LANES
    assert w.shape[0] % (n_blk * 16) == 0
    w_chunk = pl.BlockSpec((w.shape[0] // n_blk, w.shape[1]), lambda n: (n, 0))
    in_specs = [pl.BlockSpec((None, ATTN_BLOCK, 3 * GROUP_WIDTH_A), functools.partial(lambda n, g: (g, n, 0), g=g))
                for g in range(N_GROUPS_A)]
    return pl.pallas_call(
        _dilated_attn_kernel,
        grid=(n_blk,),
        in_specs=in_specs + [w_chunk],
        out_specs=[pl.BlockSpec((ATTN_BLOCK, GROUP_WIDTH_A), lambda n: (n, 0)), w_chunk],
        out_shape=[jax.ShapeDtypeStruct((s, GROUP_WIDTH_A), BF16), jax.ShapeDtypeStruct(w.shape, BF16)],
        scratch_shapes=[pltpu.VMEM((win, 2 * GROUP_WIDTH_A), BF16) for win, _ in DILATED_GROUPS]
        + [pltpu.VMEM((N_GROUPS_A, n_hp, ATTN_BLOCK, LANES), F32),
           pltpu.VMEM((N_GROUPS_A, n_hp, ATTN_BLOCK, LANES), F32)],
        compiler_params=_params("arbitrary"),
        name="dilated_attn",
    )(qkv, qkv, qkv, w)


def _to_heads(row, n_heads, head_dim):
    return jnp.concatenate([row[:, h * head_dim:(h + 1) * head_dim] for h in range(n_heads)], axis=0)


def _from_heads(t):
    return jnp.concatenate([t[h:h + 1] for h in range(t.shape[0])], axis=1)


def _window_cache_attention(c_ref, j, cols_ref, base, dil):
    win = c_ref.shape[-1]
    lc = min(win, SAMPLE_LANE_CHUNK)
    n_chunks = win // lc
    lane = lax.broadcasted_iota(jnp.int32, (1, win), 1)
    wanted = (lane % dil) == 0
    out = []
    for h in range(HEADS_PER_GROUP_A):
        lo = base + h * HEAD_DIM_A
        q, k_new, v_new = (cols_ref[lo + c * GROUP_WIDTH_A:lo + c * GROUP_WIDTH_A + HEAD_DIM_A, :] for c in range(3))
        q_wide = jnp.concatenate([q] * (lc // LANES), axis=1)
        parts = [jnp.sum(c_ref[j, 0, h, :, i * lc:(i + 1) * lc] * q_wide, axis=0, keepdims=True)
                 for i in range(n_chunks)]
        s = jnp.where(wanted, jnp.concatenate(parts, axis=1), NEG)
        s_n = jnp.sum(k_new[:, 0:1] * q[:, 0:1], axis=0, keepdims=True)
        m = jnp.maximum(jnp.max(s, axis=-1, keepdims=True), s_n)
        p = jnp.exp(s - m)
        p_n = jnp.exp(s_n - m)
        den = jnp.sum(p, axis=-1, keepdims=True) + p_n
        acc = p_n * v_new[:, 0:1]
        for i in range(n_chunks):
            acc = acc + jnp.sum(c_ref[j, 1, h, :, i * lc:(i + 1) * lc] * p[:, i * lc:(i + 1) * lc],
                                axis=-1, keepdims=True)
        out.append((acc / den, m + jnp.log(den)))
    return out


def _memory_cache_attention(cm_ref, j, q, s_ref):
    n_heads, head_dim = q.shape
    chunk = SAMPLE_MEM_CHUNK
    n_chunks = cm_ref.shape[1] // chunk
    m = jnp.full((n_heads, head_dim), NEG, F32)
    for c in range(n_chunks):
        pos = slice(c * chunk, (c + 1) * chunk)
        s = jnp.sum(cm_ref[j, pos, 0] * q[None], axis=-1, keepdims=True) * (head_dim ** -0.5)
        s = jnp.broadcast_to(s, (chunk, n_heads, head_dim))
        s_ref[pos] = s
        m = jnp.maximum(m, jnp.max(s, axis=0))
    den = jnp.zeros((n_heads, head_dim), F32)
    acc = jnp.zeros((n_heads, head_dim), F32)
    for c in range(n_chunks):
        pos = slice(c * chunk, (c + 1) * chunk)
        p = jnp.exp(s_ref[pos] - m[None])
        den = den + jnp.sum(p, axis=0)
        acc = acc + jnp.sum(p * cm_ref[j, pos, 1], axis=0)
    return acc / den


def _sample_attn_kernel(x_ref, wqkv_ref, wqm_ref, c0_ref, c1_ref, c2_ref, cm_ref,
                        oa_ref, om_ref, kvn_ref, proj_ref, projt_ref, cols_ref, oat_ref, ms_ref):
    step = pl.program_id(0)
    caches = (c0_ref, c1_ref, c2_ref)
    n_batch, d = x_ref.shape
    qkv_w = N_GROUPS_A * 3 * GROUP_WIDTH_A

    @pl.when(step == 0)
    def _():
        xb = jnp.concatenate([x_ref[...], jnp.zeros((LANES - n_batch, d), F32)], axis=0).astype(BF16)
        for part in range(3):
            h = _dot(xb, wqkv_ref[:, part * WIDTH_A:(part + 1) * WIDTH_A])
            if part == 0:
                h = h * (HEAD_DIM_A ** -0.5)
            for g in range(N_GROUPS_A):
                h_g = h[:, g * GROUP_WIDTH_A:(g + 1) * GROUP_WIDTH_A]
                dst = (3 * g + part) * GROUP_WIDTH_A
                proj_ref[:, dst:dst + GROUP_WIDTH_A] = h_g
                if part > 0:
                    kvn_ref[g, :, (part - 1) * GROUP_WIDTH_A:part * GROUP_WIDTH_A] = h_g[:n_batch]
        proj_ref[:, qkv_w:] = _dot(xb, wqm_ref[:, :WIDTH_M])
        for t in range(qkv_w // LANES):
            projt_ref[t * LANES:(t + 1) * LANES, :] = proj_ref[:, t * LANES:(t + 1) * LANES].T.astype(BF16)
        oat_ref[...] = jnp.zeros(oat_ref.shape, F32)

    def one_row(j, carry):
        b = step * SAMPLE_BATCH_BLOCK + j
        pick_row = (lax.broadcasted_iota(jnp.int32, (LANES, LANES), 0) == b).astype(BF16)
        cols_ref[...] = _dot(projt_ref[...], pick_row)
        per_group = [_window_cache_attention(caches[g], j, cols_ref, g * 3 * GROUP_WIDTH_A, dil)
                     for g, (_, dil) in enumerate(DILATED_GROUPS)]
        this_lane = lax.broadcasted_iota(jnp.int32, (1, LANES), 1) == b
        for h in range(HEADS_PER_GROUP_A):
            outs = [per_group[g][h][0] for g in range(N_GROUPS_A)]
            lses = [per_group[g][h][1] for g in range(N_GROUPS_A)]
            m3 = jnp.maximum(jnp.maximum(lses[0], lses[1]), lses[2])
            es = [jnp.exp(l - m3) for l in lses]
            oa = (es[0] * outs[0] + es[1] * outs[1] + es[2] * outs[2]) / (es[0] + es[1] + es[2])
            rows = slice(h * HEAD_DIM_A, (h + 1) * HEAD_DIM_A)
            oat_ref[rows, :] = jnp.where(this_lane, oa, oat_ref[rows, :])

        qm = _to_heads(proj_ref[pl.ds(b, 1), qkv_w:], N_HEADS_M, HEAD_DIM_M)
        om_ref[pl.ds(b, 1), :] = _from_heads(_memory_cache_attention(cm_ref, j, qm, ms_ref))
        return carry

    lax.fori_loop(0, SAMPLE_BATCH_BLOCK, one_row, 0)

    @pl.when(step == pl.num_programs(0) - 1)
    def _():
        oa_ref[...] = oat_ref[...].T[:n_batch, :]


def _sample_attn(xs, w_all, caches, cache_mem):
    n_batch, d = xs.shape
    assert n_batch <= LANES
    bb = SAMPLE_BATCH_BLOCK
    qkv_w = N_GROUPS_A * 3 * GROUP_WIDTH_A
    qm_col = 3 * WIDTH_A + 2 * WIDTH_B
    assert qm_col % WIDTH_B == 0 and WIDTH_M <= WIDTH_B
    w_specs = [pl.BlockSpec((d, qkv_w), lambda i: (0, 0), pipeline_mode=pl.Buffered(1)),
               pl.BlockSpec((d, WIDTH_B), lambda i: (0, qm_col // WIDTH_B), pipeline_mode=pl.Buffered(1))]
    cache_specs = [pl.BlockSpec((bb,) + c.shape[1:], lambda i: (i, 0, 0, 0, 0)) for c in caches]
    whole = lambda shape: pl.BlockSpec(shape, lambda i: (0,) * len(shape))
    return pl.pallas_call(
        _sample_attn_kernel,
        grid=(n_batch // bb,),
        in_specs=[whole(xs.shape)] + w_specs + cache_specs
        + [pl.BlockSpec((bb,) + cache_mem.shape[1:], lambda i: (i, 0, 0, 0, 0))],
        out_specs=[whole((n_batch, GROUP_WIDTH_A)), whole((n_batch, WIDTH_M)),
                   whole((N_GROUPS_A, n_batch, 2 * GROUP_WIDTH_A))],
        out_shape=[jax.ShapeDtypeStruct((n_batch, GROUP_WIDTH_A), F32),
                   jax.ShapeDtypeStruct((n_batch, WIDTH_M), F32),
                   jax.ShapeDtypeStruct((N_GROUPS_A, n_batch, 2 * GROUP_WIDTH_A), F32)],
        scratch_shapes=[pltpu.VMEM((LANES, qkv_w + WIDTH_M), F32),
                        pltpu.VMEM((qkv_w, LANES), BF16),
                        pltpu.VMEM((qkv_w, LANES), F32),
                        pltpu.VMEM((GROUP_WIDTH_A, LANES), F32),
                        pltpu.VMEM((cache_mem.shape[1], N_HEADS_M, HEAD_DIM_M), F32)],
        compiler_params=_params("arbitrary"),
        name="sample_attn",
    )(xs, w_all, w_all, *caches, cache_mem)


def _mix_kernel(x_ref, oa_ref, mk_ref, mv_ref, xs_ref, oas_ref, oms_ref, w_ref, bgate_ref, lnvg_ref, lnvb_ref,
                ws_ref, bs_ref, wss_ref, bss_ref, wba_ref, wbb_ref, wbm_ref, wout_ref, ln1g_ref, ln1b_ref,
                wup_ref, wdown_ref, x1_ref, x1s_ref, vrows_ref, wup_bf_ref, wdown_bf_ref, *, alpha):
    @pl.when(pl.program_id(0) < CAST_STEPS)
    def _():
        wup_bf_ref[...] = wup_ref[...].astype(BF16)
        wdown_bf_ref[...] = wdown_ref[...].astype(BF16)

    shared = (w_ref, bgate_ref, lnvg_ref, lnvb_ref)
    tail = (wba_ref, wbb_ref, wbm_ref, wout_ref, ln1g_ref, ln1b_ref)
    is_sample_step = pl.program_id(0) == pl.num_programs(0) - 1

    @pl.when(jnp.logical_not(is_sample_step))
    def _():
        _mix_body(x_ref, oa_ref, (mk_ref, mv_ref), *shared, ws_ref, bs_ref, *tail, x1_ref, None,
                  sample=False, alpha=alpha)

    @pl.when(is_sample_step)
    def _():
        _mix_body(xs_ref, oas_ref, oms_ref, *shared, wss_ref, bss_ref, *tail, x1s_ref, vrows_ref,
                  sample=True, alpha=alpha)


def _mix_body(x_ref, oa_ref, mem_or_om, w_ref, bgate_ref, lnvg_ref, lnvb_ref, ws_ref, bs_ref,
              wba_ref, wbb_ref, wbm_ref, wout_ref, ln1g_ref, ln1b_ref, x1_ref, vrows_ref, *, sample, alpha):
    if sample:
        om_ref = mem_or_om
    else:
        mk_ref, mv_ref = mem_or_om
    d = x_ref.shape[1]
    rows = x_ref.shape[0]
    x = x_ref[...]
    xb = x.astype(BF16)
    col_u, col_v, col_qm, col_gate = 0, WIDTH_B, 2 * WIDTH_B, 2 * WIDTH_B + WIDTH_M

    h_all = _dot(xb, w_ref[:, 3 * WIDTH_A:])

    def gate(k):
        z = h_all[:, col_gate + k * d:col_gate + (k + 1) * d]
        return 0.5 * jnp.tanh(0.5 * (z + bgate_ref[k:k + 1, :])) + 0.5

    mixed = gate(0) * _dot(oa_ref[...].astype(BF16), wba_ref[...])

    u = jax.nn.gelu(h_all[:, col_u:col_u + WIDTH_B])
    v = _layer_norm(jax.nn.gelu(h_all[:, col_v:col_v + WIDTH_B]), lnvg_ref[...], lnvb_ref[...])
    if sample:
        vrows_ref[...] = v
        spatial = v.astype(BF16).astype(F32) * ws_ref[...].astype(BF16).astype(F32) + bs_ref[...]
    else:
        r_i = lax.broadcasted_iota(jnp.int32, (N_GROUPS_B * CHUNK, CHUNK), 0)
        c_i = lax.broadcasted_iota(jnp.int32, (N_GROUPS_B * CHUNK, CHUNK), 1)
        w_s = jnp.where((r_i % CHUNK) >= c_i, ws_ref[...], 0.0).astype(BF16)
        lane = lax.broadcasted_iota(jnp.int32, (CHUNK, WIDTH_B), 1)
        in_group = [(lane >= gb * GROUP_DIM_B) & (lane < (gb + 1) * GROUP_DIM_B) for gb in range(N_GROUPS_B)]
        vb = v.astype(BF16)
        parts = []
        for ch in range(rows // CHUNK):
            allg = _dot(w_s, vb[ch * CHUNK:(ch + 1) * CHUNK, :])
            sp = bs_ref[...]
            for gb in range(N_GROUPS_B):
                sp = sp + jnp.where(in_group[gb], allg[gb * CHUNK:(gb + 1) * CHUNK, :], 0.0)
            parts.append(sp)
        spatial = jnp.concatenate(parts, axis=0)
    o_b = u * spatial
    mixed = mixed + gate(1) * _dot(o_b.astype(BF16), wbb_ref[...])

    if sample:
        o_m = om_ref[...]
    else:
        qm = h_all[:, col_qm:col_qm + WIDTH_M].astype(BF16)
        head_cols = [slice(h * HEAD_DIM_M, (h + 1) * HEAD_DIM_M) for h in range(N_HEADS_M)]
        scores = [_dot_nt(qm[:, hs], mk_ref[:, hs]) * (HEAD_DIM_M ** -0.5) for hs in head_cols]
        maxes = [jnp.max(s, axis=-1, keepdims=True) for s in scores]
        probs = [jnp.exp(s - m) for s, m in zip(scores, maxes)]
        dens = [jnp.sum(p, axis=-1, keepdims=True) for p in probs]
        o_m = jnp.concatenate([_dot(p.astype(BF16), mv_ref[:, hs]) / den
                               for p, den, hs in zip(probs, dens, head_cols)], axis=-1)
    mixed = mixed + gate(2) * _dot(o_m.astype(BF16), wbm_ref[...])

    _project_add_norm(mixed.astype(BF16), wout_ref, alpha * x, ln1g_ref, ln1b_ref, x1_ref)


def _mix(x, o_a, mk, mv, xs, oa_s, om_s, w_all, b_gate, ln_v_g, ln_v_b, w_s, b_s, w_s_row, b_s_row,
         w_ba, w_bb, w_bm, w_out, ln1_g, ln1_b, w_up, w_down, *, alpha):
    s, d = x.shape
    n_tiles = s // MIX_ROWS
    assert n_tiles >= CAST_STEPS and w_up.shape[0] % (CAST_STEPS * 16) == 0 and w_down.shape[0] % (CAST_STEPS * 16) == 0
    chunk = lambda t: pl.BlockSpec((t.shape[0] // CAST_STEPS, t.shape[1]), lambda i: (jnp.minimum(i, CAST_STEPS - 1), 0))
    tile = lambda width: pl.BlockSpec((MIX_ROWS, width), lambda i: (jnp.minimum(i, n_tiles - 1), 0))
    whole = lambda shape: pl.BlockSpec(shape, lambda i: (0,) * len(shape))
    operands = (x, o_a, mk, mv, xs, oa_s, om_s, w_all, b_gate, ln_v_g, ln_v_b, w_s, b_s, w_s_row, b_s_row,
                w_ba, w_bb, w_bm, w_out, ln1_g, ln1_b)
    n_s = xs.shape[0]
    return pl.pallas_call(
        functools.partial(_mix_kernel, alpha=alpha),
        grid=(n_tiles + 1,),
        in_specs=[tile(d), tile(GROUP_WIDTH_A)] + [_resident(t.shape) for t in operands[2:]] + [chunk(w_up), chunk(w_down)],
        out_specs=[tile(d), whole((n_s, d)), whole((n_s, WIDTH_B)), chunk(w_up), chunk(w_down)],
        out_shape=[jax.ShapeDtypeStruct((s, d), F32), jax.ShapeDtypeStruct((n_s, d), F32),
                   jax.ShapeDtypeStruct((n_s, WIDTH_B), F32),
                   jax.ShapeDtypeStruct(w_up.shape, BF16), jax.ShapeDtypeStruct(w_down.shape, BF16)],
        compiler_params=_params("arbitrary"),
        name="mix",
    )(*operands, w_up, w_down)


def _ffn_kernel(x1_ref, x1s_ref, st0_ref, st1_ref, wup_ref, cw_ref, cb_ref, wdown_ref, g_ref, b_ref,
                y_ref, a_ref, ys_ref, as_ref, abuf_ref, *, alpha):
    consts = (wup_ref, cw_ref, cb_ref, wdown_ref, g_ref, b_ref)
    is_sample_step = pl.program_id(0) == pl.num_programs(0) - 1

    @pl.when(jnp.logical_not(is_sample_step))
    def _():
        _ffn_body(x1_ref, None, None, *consts, y_ref, a_ref, abuf_ref, sample=False, alpha=alpha)

    @pl.when(is_sample_step)
    def _():
        _ffn_body(x1s_ref, st0_ref, st1_ref, *consts, ys_ref, as_ref, None, sample=True, alpha=alpha)


def _ffn_body(x1_ref, st0_ref, st1_ref, wup_ref, cw_ref, cb_ref, wdown_ref, g_ref, b_ref, y_ref, a_ref, abuf_ref,
              *, sample, alpha):
    rows = x1_ref.shape[0]
    d_ff = wdown_ref.shape[0]
    if not sample:
        @pl.when(pl.program_id(0) == 0)
        def _():
            abuf_ref[0:SUBLANES, :] = jnp.zeros((SUBLANES, d_ff), F32)

    x1 = x1_ref[...]
    xb = x1.astype(BF16)
    a = _dot(xb, wup_ref[:, :d_ff])
    val = _dot(xb, wup_ref[:, d_ff:])
    if sample:
        a_ref[...] = a
        a_m2, a_m1 = st0_ref[...], st1_ref[...]
    else:
        abuf_ref[SUBLANES:SUBLANES + rows, :] = a
        a_m1 = abuf_ref[SUBLANES - 1:SUBLANES - 1 + rows, :]
        a_m2 = abuf_ref[SUBLANES - 2:SUBLANES - 2 + rows, :]
    conv = cb_ref[...] + cw_ref[0:1, :] * a_m2 + cw_ref[1:2, :] * a_m1 + cw_ref[2:3, :] * a
    h = jax.nn.gelu(conv) * val
    _project_add_norm(h.astype(BF16), wdown_ref, alpha * x1, g_ref, b_ref, y_ref)
    if not sample:
        tail = abuf_ref[rows:rows + SUBLANES, :]
        abuf_ref[0:SUBLANES, :] = tail
        a_ref[...] = tail


def _ffn(x1, x1_s, state, w_up, conv_w, conv_b, w_down, ln2_g, ln2_b, *, alpha):
    s, d = x1.shape
    n_s = x1_s.shape[0]
    d_ff = w_down.shape[0]
    n_tiles = s // FFN_ROWS
    tile = lambda width: pl.BlockSpec((FFN_ROWS, width), lambda i: (jnp.minimum(i, n_tiles - 1), 0))
    whole = lambda shape: pl.BlockSpec(shape, lambda i: (0,) * len(shape))
    operands = (x1, x1_s, state[:, 0], state[:, 1], w_up, conv_w, conv_b, w_down, ln2_g, ln2_b)
    return pl.pallas_call(
        functools.partial(_ffn_kernel, alpha=alpha),
        grid=(n_tiles + 1,),
        in_specs=[tile(d)] + [_resident(t.shape) for t in operands[1:]],
        out_specs=[tile(d), whole((SUBLANES, d_ff)), whole((n_s, d)), whole((n_s, d_ff))],
        out_shape=[jax.ShapeDtypeStruct((s, d), F32), jax.ShapeDtypeStruct((SUBLANES, d_ff), F32),
                   jax.ShapeDtypeStruct((n_s, d), F32), jax.ShapeDtypeStruct((n_s, d_ff), F32)],
        scratch_shapes=[pltpu.VMEM((FFN_ROWS + SUBLANES, d_ff), F32)],
        compiler_params=_params("arbitrary"),
        name="ffn",
    )(*operands)


def kernel(x_prompt, x_sample, mem_prompt, cache_win128_kv, cache_win512_kv, cache_win2048_kv, cache_mem_kv, state_ffn_conv, w_in, b_gate, ln_v_g, ln_v_b, w_spatial, b_spatial, w_mem_kv, w_branch_a, w_branch_b, w_branch_m, w_out, ln1_g, ln1_b, w_up, conv_w, conv_b, w_down, ln2_g, ln2_b):
    depth = w_in.shape[0]
    batch, seq, d_model = x_prompt.shape
    dec_batch, dec_seq, _ = x_sample.shape
    assert batch == 1 and dec_seq == 1 and seq % ATTN_BLOCK == 0
    alpha = (2.0 * depth) ** 0.25
    win_caches = (cache_win128_kv, cache_win512_kv, cache_win2048_kv)
    for cache, (win, _) in zip(win_caches, DILATED_GROUPS):
        assert cache.shape[2] == win and PAST_LEN >= win

    yp = x_prompt.reshape(seq, d_model)
    ys = x_sample.reshape(dec_batch, d_model)
    win_p = [[] for _ in range(N_GROUPS_A)]
    win_s = [[] for _ in range(N_GROUPS_A)]
    mem_p, conv_p, gmlp_s, conv_s = [], [], [], []
    for l in range(depth):
        w_qkv = w_in[l][:, :3 * WIDTH_A].astype(BF16)
        row2 = lambda t: t.reshape(1, -1)
        per_lane = lambda t: jnp.repeat(t, GROUP_DIM_B, axis=-1)
        consts_tail = (w_branch_a[l].astype(BF16), w_branch_b[l].astype(BF16), w_branch_m[l].astype(BF16),
                       w_out[l].astype(BF16), row2(ln1_g[l]), row2(ln1_b[l]))

        mkv = _mem_kv_proj(mem_prompt[0], w_mem_kv[l].astype(BF16))
        mem_p.append(mkv.reshape(1, -1, 2, N_HEADS_M, HEAD_DIM_M))
        qkv, *kv_tails = _qkv_proj(yp, w_qkv)
        for g, (win, _) in enumerate(DILATED_GROUPS):
            tail = kv_tails[g].reshape(2, HEADS_PER_GROUP_A, HEAD_DIM_A, win)
            win_p[g].append(jnp.transpose(tail, (3, 0, 1, 2))[None])
        o_a, w_all = _dilated_attn(qkv, w_in[l])
        caches = [jnp.transpose(c[l], (0, 2, 3, 4, 1)) for c in win_caches]
        oa_s, om_s, kv_new = _sample_attn(ys, w_all, caches, cache_mem_kv[l])
        for g in range(N_GROUPS_A):
            win_s[g].append(kv_new[g].reshape(dec_batch, 1, 2, HEADS_PER_GROUP_A, HEAD_DIM_A))

        x1, x1_s, v_rows, w_up_bf, w_down_bf = _mix(yp, o_a, mkv[:, :WIDTH_M].astype(BF16), mkv[:, WIDTH_M:].astype(BF16), ys, oa_s, om_s,
                                w_all, b_gate[l], row2(ln_v_g[l]), row2(ln_v_b[l]),
                                w_spatial[l].reshape(N_GROUPS_B * CHUNK, CHUNK), per_lane(b_spatial[l].T),
                                row2(per_lane(w_spatial[l][:, 0, 0])), row2(per_lane(b_spatial[l][:, 0])),
                                *consts_tail, w_up[l], w_down[l], alpha=alpha)
        ffn_consts = (w_up_bf, conv_w[l], row2(conv_b[l]), w_down_bf, row2(ln2_g[l]), row2(ln2_b[l]))
        gmlp_s.append(v_rows.reshape(dec_batch, 1, WIDTH_B))
        yp_next, a_tail, ys_next, a_s = _ffn(x1, x1_s, state_ffn_conv[l], *ffn_consts, alpha=alpha)
        conv_p.append(a_tail[SUBLANES - (CONV_W - 1):][None])
        conv_s.append(jnp.stack([state_ffn_conv[l][:, 1], a_s], axis=1))
        yp, ys = yp_next, ys_next

    return (yp.reshape(batch, seq, d_model), ys.reshape(dec_batch, dec_seq, d_model),
            jnp.stack(win_p[0]), jnp.stack(win_p[1]), jnp.stack(win_p[2]),
            jnp.stack(mem_p), jnp.stack(conv_p),
            jnp.stack(win_s[0]), jnp.stack(win_s[1]), jnp.stack(win_s[2]),
            jnp.stack(gmlp_s), jnp.stack(conv_s))
```
